```python
import jax, jax.numpy as jnp
from jax import lax
import numpy as np

D_MODEL = 4096
BATCH = 8
SEQ = 4096
DEPTH = 1

HEAD_DIM = 128
D_MIX = D_MODEL
W_A = D_MIX // 2
W_B = D_MIX - W_A
N_GROUPS_A = W_A // HEAD_DIM
N_GROUPS_B = W_B // HEAD_DIM
CONV_A = 3
CONV_B = 31
D_IN = 4 * W_A + 3 * W_B
SPLITS = (W_A, 2 * W_A, 3 * W_A, 4 * W_A, 4 * W_A + W_B, 4 * W_A + 2 * W_B)
EPS = 1e-6

kernel_name = "hymba_style_conv_hybrid_adaln"


def _rmsnorm(x, g):
    xf = x.astype(jnp.float32)
    y = xf * lax.rsqrt(jnp.mean(xf * xf, axis=-1, keepdims=True) + EPS)
    return (y * g.astype(jnp.float32)).astype(x.dtype)


def _layernorm(x, g, b):
    xf = x.astype(jnp.float32)
    mu = jnp.mean(xf, axis=-1, keepdims=True)
    xc = xf - mu
    var = jnp.mean(xc * xc, axis=-1, keepdims=True)
    y = xc * lax.rsqrt(var + EPS) * g.astype(jnp.float32) + b.astype(jnp.float32)
    return y.astype(x.dtype)


def _causal_depthwise_conv(u, w):
    k, ch = w.shape
    return lax.conv_general_dilated(
        u, w[:, None, :].astype(u.dtype),
        window_strides=(1,), padding=((k - 1, 0),),
        dimension_numbers=("NWC", "WIO", "NWC"),
        feature_group_count=ch)


def _fwd_setup_inputs(seed: int = 0) -> dict:
    key = jax.random.key(seed)
    ks = jax.random.split(key, 14)
    f32 = jnp.float32
    x = jax.random.normal(ks[0], (BATCH, SEQ, D_MODEL), f32)
    c = jax.random.normal(ks[1], (BATCH, D_MODEL), f32)
    norm_g = 1.0 + 0.01 * jax.random.normal(ks[2], (DEPTH, D_MODEL), f32)
    w_ada = 0.5 * D_MODEL ** -0.5 * jax.random.normal(ks[3], (DEPTH, D_MODEL, 3 * D_MODEL), f32)
    b_ada = 0.01 * jax.random.normal(ks[4], (DEPTH, 3 * D_MODEL), f32)
    w_in = D_MODEL ** -0.5 * jax.random.normal(ks[5], (DEPTH, D_MODEL, D_IN), f32)
    conv_a_w = CONV_A ** -0.5 * jax.random.normal(ks[6], (DEPTH, CONV_A, W_A), f32)
    conv_b_w = CONV_B ** -0.5 * jax.random.normal(ks[7], (DEPTH, CONV_B, W_B), f32)
    conv_b_b = 0.01 * jax.random.normal(ks[8], (DEPTH, W_B), f32)
    ln_b_g = 1.0 + 0.01 * jax.random.normal(ks[9], (DEPTH, W_B), f32)
    ln_b_b = 0.01 * jax.random.normal(ks[10], (DEPTH, W_B), f32)
    w_out = D_MIX ** -0.5 * jax.random.normal(ks[11], (DEPTH, D_MIX, D_MODEL), f32)
    final_g = 1.0 + 0.01 * jax.random.normal(ks[12], (D_MODEL,), f32)
    return {"x": x, "c": c, "norm_g": norm_g, "w_ada": w_ada, "b_ada": b_ada,
            "w_in": w_in, "conv_a_w": conv_a_w, "conv_b_w": conv_b_w,
            "conv_b_b": conv_b_b, "ln_b_g": ln_b_g, "ln_b_b": ln_b_b,
            "w_out": w_out, "final_g": final_g}


def _fwd_reference(x, c, norm_g, w_ada, b_ada, w_in, conv_a_w, conv_b_w, conv_b_b,
              ln_b_g, ln_b_b, w_out, final_g):
    c_act = jax.nn.silu(c)
    for l in range(DEPTH):
        mod = c_act @ w_ada[l] + b_ada[l]
        shift, scale, gate = jnp.split(mod, 3, axis=-1)
        h = _rmsnorm(x, norm_g[l]) * (1.0 + scale[:, None, :]) + shift[:, None, :]

        proj = jnp.einsum("bsd,de->bse", h, w_in[l])
        a_b, a_c, a_x, a_z, b_v, b_g, b_z = jnp.split(proj, SPLITS, axis=-1)

        y_a = a_b * _causal_depthwise_conv(a_c * a_x, conv_a_w[l]) * jax.nn.silu(a_z)

        u = b_v * jax.nn.sigmoid(b_g)
        u = _causal_depthwise_conv(u, conv_b_w[l]) + conv_b_b[l]
        y_b = jax.nn.silu(_layernorm(u, ln_b_g[l], ln_b_b[l])) * jax.nn.silu(b_z)

        y = jnp.concatenate([y_a, y_b], axis=-1)
        x = x + gate[:, None, :] * jnp.einsum("bse,ed->bsd", y, w_out[l])
    return _rmsnorm(x, final_g)


import jax as _jax
import jax.numpy as _jnp

TWIN_FORMAT = 'train_step'
FWD_PARAMS = ['x', 'c', 'norm_g', 'w_ada', 'b_ada', 'w_in', 'conv_a_w', 'conv_b_w', 'conv_b_b', 'ln_b_g', 'ln_b_b', 'w_out', 'final_g']
TWIN_WEIGHTS = ['norm_g', 'w_ada', 'b_ada', 'w_in', 'conv_a_w', 'conv_b_w', 'conv_b_b', 'ln_b_g', 'ln_b_b', 'w_out', 'final_g']
TWIN_DIFF_INPUT = 'x'
TWIN_INPUTS = ['x', 'c', 'norm_g', 'w_ada', 'b_ada', 'w_in', 'conv_a_w', 'conv_b_w', 'conv_b_b', 'ln_b_g', 'ln_b_b', 'w_out', 'final_g', 'loss_target', 'm_norm_g', 'm_w_ada', 'm_b_ada', 'm_w_in', 'm_conv_a_w', 'm_conv_b_w', 'm_conv_b_b', 'm_ln_b_g', 'm_ln_b_b', 'm_w_out', 'm_final_g', 'v_norm_g', 'v_w_ada', 'v_b_ada', 'v_w_in', 'v_conv_a_w', 'v_conv_b_w', 'v_conv_b_b', 'v_ln_b_g', 'v_ln_b_b', 'v_w_out', 'v_final_g']
TWIN_OUTPUTS = ['loss', 'grad_x', 'grad_norm_g', 'grad_w_ada', 'grad_b_ada', 'grad_w_in', 'grad_conv_a_w', 'grad_conv_b_w', 'grad_conv_b_b', 'grad_ln_b_g', 'grad_ln_b_b', 'grad_w_out', 'grad_final_g', 'delta_norm_g', 'delta_w_ada', 'delta_b_ada', 'delta_w_in', 'delta_conv_a_w', 'delta_conv_b_w', 'delta_conv_b_b', 'delta_ln_b_g', 'delta_ln_b_b', 'delta_w_out', 'delta_final_g', 'new_m_norm_g', 'new_m_w_ada', 'new_m_b_ada', 'new_m_w_in', 'new_m_conv_a_w', 'new_m_conv_b_w', 'new_m_conv_b_b', 'new_m_ln_b_g', 'new_m_ln_b_b', 'new_m_w_out', 'new_m_final_g', 'new_v_norm_g', 'new_v_w_ada', 'new_v_b_ada', 'new_v_w_in', 'new_v_conv_a_w', 'new_v_conv_b_w', 'new_v_conv_b_b', 'new_v_ln_b_g', 'new_v_ln_b_b', 'new_v_w_out', 'new_v_final_g']
TWIN_LEAF_KINDS = {'loss': 'loss', 'grad_x': 'grad_x', 'grad_norm_g': 'grad_w', 'grad_w_ada': 'grad_w', 'grad_b_ada': 'grad_w', 'grad_w_in': 'grad_w', 'grad_conv_a_w': 'grad_w', 'grad_conv_b_w': 'grad_w', 'grad_conv_b_b': 'grad_w', 'grad_ln_b_g': 'grad_w', 'grad_ln_b_b': 'grad_w', 'grad_w_out': 'grad_w', 'grad_final_g': 'grad_w', 'delta_norm_g': 'delta_w', 'delta_w_ada': 'delta_w', 'delta_b_ada': 'delta_w', 'delta_w_in': 'delta_w', 'delta_conv_a_w': 'delta_w', 'delta_conv_b_w': 'delta_w', 'delta_conv_b_b': 'delta_w', 'delta_ln_b_g': 'delta_w', 'delta_ln_b_b': 'delta_w', 'delta_w_out': 'delta_w', 'delta_final_g': 'delta_w', 'new_m_norm_g': 'new_m', 'new_m_w_ada': 'new_m', 'new_m_b_ada': 'new_m', 'new_m_w_in': 'new_m', 'new_m_conv_a_w': 'new_m', 'new_m_conv_b_w': 'new_m', 'new_m_conv_b_b': 'new_m', 'new_m_ln_b_g': 'new_m', 'new_m_ln_b_b': 'new_m', 'new_m_w_out': 'new_m', 'new_m_final_g': 'new_m', 'new_v_norm_g': 'new_v', 'new_v_w_ada': 'new_v', 'new_v_b_ada': 'new_v', 'new_v_w_in': 'new_v', 'new_v_conv_a_w': 'new_v', 'new_v_conv_b_w': 'new_v', 'new_v_conv_b_b': 'new_v', 'new_v_ln_b_g': 'new_v', 'new_v_ln_b_b': 'new_v', 'new_v_w_out': 'new_v', 'new_v_final_g': 'new_v'}


def _forward(args):
    return _fwd_reference(*[args[k] for k in FWD_PARAMS])


def _output_shape():
    out = _jax.eval_shape(lambda: _forward(_fwd_setup_inputs(0)))
    return out.shape, out.dtype

N_MICROBATCH = 1
ADAM_LR = 0.001
ADAM_B1 = 0.9
ADAM_B2 = 0.999
ADAM_EPS = 1e-08
ADAM_WD = 0.01
ADAM_STEP = 10
PER_EXAMPLE_BATCH_AXIS = {'x': 0, 'c': 0, 'loss_target': 0}
SHARED_INPUTS = []
_WEIGHT_DTYPES = {'norm_g': _jnp.float32, 'w_ada': _jnp.float32, 'b_ada': _jnp.float32, 'w_in': _jnp.float32, 'conv_a_w': _jnp.float32, 'conv_b_w': _jnp.float32, 'conv_b_b': _jnp.float32, 'ln_b_g': _jnp.float32, 'ln_b_b': _jnp.float32, 'w_out': _jnp.float32, 'final_g': _jnp.float32}
MOMENT_SCALE = {'norm_g': 1.643274e-02, 'w_ada': 1.260091e-02, 'b_ada': 2.079272e-02, 'w_in': 9.023125e-03, 'conv_a_w': 1.133448e-02, 'conv_b_w': 5.482004e-03, 'conv_b_b': 1.008158e-02, 'ln_b_g': 6.634758e-03, 'ln_b_b': 5.468035e-03, 'w_out': 8.655635e-03, 'final_g': 7.988019e+00}


def _to_microbatches(a, axis):
    t = _jnp.moveaxis(a, axis, 0)
    t = t.reshape((N_MICROBATCH, t.shape[0] // N_MICROBATCH) + t.shape[1:])
    return _jnp.moveaxis(t, 1, axis + 1)


def setup_inputs(seed: int = 0) -> dict:
    inp = _fwd_setup_inputs(seed)
    key = _jax.random.fold_in(_jax.random.key(seed), 7919)
    shape, _ = _output_shape()
    out = dict(inp)
    out["loss_target"] = _jax.random.normal(_jax.random.fold_in(key, 0), shape, _jnp.float32)
    for i, name in enumerate(TWIN_WEIGHTS):
        w = inp[name].astype(_jnp.float32)
        if MOMENT_SCALE is None:
            s = _jnp.sqrt(_jnp.mean(_jnp.square(w)) + 1e-30)
        else:
            s = MOMENT_SCALE[name]
        km, kv = _jax.random.split(_jax.random.fold_in(key, i + 1))
        out[name] = w
        out["m_" + name] = s * _jax.random.normal(km, w.shape, _jnp.float32)
        out["v_" + name] = (s * s) * _jax.random.uniform(kv, w.shape, _jnp.float32, 0.5, 1.5)
    if N_MICROBATCH > 1:
        for name, axis in PER_EXAMPLE_BATCH_AXIS.items():
            out[name] = _to_microbatches(out[name], axis)
    return {'x': out['x'], 'c': out['c'], 'norm_g': out['norm_g'], 'w_ada': out['w_ada'], 'b_ada': out['b_ada'], 'w_in': out['w_in'], 'conv_a_w': out['conv_a_w'], 'conv_b_w': out['conv_b_w'], 'conv_b_b': out['conv_b_b'], 'ln_b_g': out['ln_b_g'], 'ln_b_b': out['ln_b_b'], 'w_out': out['w_out'], 'final_g': out['final_g'], 'loss_target': out['loss_target'], 'm_norm_g': out['m_norm_g'], 'm_w_ada': out['m_w_ada'], 'm_b_ada': out['m_b_ada'], 'm_w_in': out['m_w_in'], 'm_conv_a_w': out['m_conv_a_w'], 'm_conv_b_w': out['m_conv_b_w'], 'm_conv_b_b': out['m_conv_b_b'], 'm_ln_b_g': out['m_ln_b_g'], 'm_ln_b_b': out['m_ln_b_b'], 'm_w_out': out['m_w_out'], 'm_final_g': out['m_final_g'], 'v_norm_g': out['v_norm_g'], 'v_w_ada': out['v_w_ada'], 'v_b_ada': out['v_b_ada'], 'v_w_in': out['v_w_in'], 'v_conv_a_w': out['v_conv_a_w'], 'v_conv_b_w': out['v_conv_b_w'], 'v_conv_b_b': out['v_conv_b_b'], 'v_ln_b_g': out['v_ln_b_g'], 'v_ln_b_b': out['v_ln_b_b'], 'v_w_out': out['v_w_out'], 'v_final_g': out['v_final_g']}


def _loss(weights, diff, rest, loss_target):
    with _jax.named_scope("forward"):
        args = {**rest, TWIN_DIFF_INPUT: diff, **{k: w.astype(_WEIGHT_DTYPES[k]) for k, w in weights.items()}}
        y = _forward(args)
    with _jax.named_scope("loss_head"):
        err = _jnp.square(y.astype(_jnp.float32) - loss_target)
        return 0.5 * _jnp.sum(_jnp.mean(err, axis=-1)) if err.ndim else 0.5 * err


def _adamw(w, g, m, v):
    m = ADAM_B1 * m + (1.0 - ADAM_B1) * g
    v = ADAM_B2 * v + (1.0 - ADAM_B2) * _jnp.square(g)
    m_hat = m / (1.0 - ADAM_B1 ** ADAM_STEP)
    v_hat = v / (1.0 - ADAM_B2 ** ADAM_STEP)
    delta = -ADAM_LR * (m_hat / (_jnp.sqrt(v_hat) + ADAM_EPS) + ADAM_WD * w)
    return delta, m, v


def reference(x, c, norm_g, w_ada, b_ada, w_in, conv_a_w, conv_b_w, conv_b_b, ln_b_g, ln_b_b, w_out, final_g, loss_target, m_norm_g, m_w_ada, m_b_ada, m_w_in, m_conv_a_w, m_conv_b_w, m_conv_b_b, m_ln_b_g, m_ln_b_b, m_w_out, m_final_g, v_norm_g, v_w_ada, v_b_ada, v_w_in, v_conv_a_w, v_conv_b_w, v_conv_b_b, v_ln_b_g, v_ln_b_b, v_w_out, v_final_g):
    given = dict(x=x, c=c, norm_g=norm_g, w_ada=w_ada, b_ada=b_ada, w_in=w_in, conv_a_w=conv_a_w, conv_b_w=conv_b_w, conv_b_b=conv_b_b, ln_b_g=ln_b_g, ln_b_b=ln_b_b, w_out=w_out, final_g=final_g, loss_target=loss_target, m_norm_g=m_norm_g, m_w_ada=m_w_ada, m_b_ada=m_b_ada, m_w_in=m_w_in, m_conv_a_w=m_conv_a_w, m_conv_b_w=m_conv_b_w, m_conv_b_b=m_conv_b_b, m_ln_b_g=m_ln_b_g, m_ln_b_b=m_ln_b_b, m_w_out=m_w_out, m_final_g=m_final_g, v_norm_g=v_norm_g, v_w_ada=v_w_ada, v_b_ada=v_b_ada, v_w_in=v_w_in, v_conv_a_w=v_conv_a_w, v_conv_b_w=v_conv_b_w, v_conv_b_b=v_conv_b_b, v_ln_b_g=v_ln_b_g, v_ln_b_b=v_ln_b_b, v_w_out=v_w_out, v_final_g=v_final_g)
    weights = {n: given[n] for n in TWIN_WEIGHTS}
    shared = {n: given[n] for n in SHARED_INPUTS}
    per_example = {n: given[n] for n in ['x', 'c']}
    grad_fn = _jax.value_and_grad(_loss, argnums=(0, 1))

    def one_microbatch(ex, loss_target):
        ex = dict(ex)
        diff = ex.pop(TWIN_DIFF_INPUT)
        return grad_fn(weights, diff, {**shared, **ex}, loss_target)

    if N_MICROBATCH == 1:
        loss, (grad_w, grad_x) = one_microbatch(per_example, given["loss_target"])
    else:
        def body(carry, xs):
            loss_sum, grad_sum = carry
            l_k, (gw_k, gx_k) = one_microbatch(xs[0], xs[1])
            with _jax.named_scope("update"):
                return (loss_sum + l_k, _jax.tree.map(_jnp.add, grad_sum, gw_k)), gx_k

        init = (_jnp.zeros((), _jnp.float32), _jax.tree.map(_jnp.zeros_like, weights))
        (loss, grad_w), grad_x = _jax.lax.scan(body, init, (per_example, given["loss_target"]))
    with _jax.named_scope("update"):
        delta_w, new_m, new_v = {}, {}, {}
        for n in TWIN_WEIGHTS:
            delta_w[n], new_m[n], new_v[n] = _adamw(weights[n], grad_w[n], given["m_" + n], given["v_" + n])
    return (loss, grad_x, *[grad_w[n] for n in TWIN_WEIGHTS], *[delta_w[n] for n in TWIN_WEIGHTS],
            *[new_m[n] for n in TWIN_WEIGHTS], *[new_v[n] for n in TWIN_WEIGHTS])
```

```python
import functools

import jax
import jax.numpy as jnp
from jax import lax
from jax.experimental import pallas as pl
from jax.experimental.pallas import tpu as pltpu

F32 = jnp.float32
BF16 = jnp.bfloat16
EPS = 1e-6
N_CHIPS = 4
N_DEV = 8
TAPS_A = 3
TAPS_B = 31
HALO_A = 8
HALO_B = 32
ADAM_LR = 0.001
ADAM_B1 = 0.9
ADAM_B2 = 0.999
ADAM_EPS = 1e-08
ADAM_WD = 0.01
ADAM_STEP = 10
VMEM_LIMIT = 56 * 1024 * 1024
MESH = pl.DeviceIdType.MESH
ANY = pl.BlockSpec(memory_space=pl.ANY)
VMEM = pl.BlockSpec(memory_space=pltpu.VMEM)


def _params(sem=None):
    return pltpu.CompilerParams(dimension_semantics=sem, vmem_limit_bytes=VMEM_LIMIT)


def _sigmoid(v):
    return jax.nn.sigmoid(v)


def _position():
    return lax.axis_index("x"), lax.axis_index("y"), lax.axis_index("c")


def _rcopy(src, dst, ssem, rsem, dev):
    return pltpu.make_async_remote_copy(src_ref=src, dst_ref=dst, send_sem=ssem, recv_sem=rsem,
                                        device_id=dev, device_id_type=MESH)


def _other_chips(x, y):
    chips = [(1 - x, y), (x, 1 - y), (1 - x, 1 - y)]
    return chips, [2 * cx + cy for cx, cy in chips]


def _cast_bf16(a, rows, name):
    m, n = a.shape

    def body(a_ref, o_ref):
        o_ref[...] = a_ref[...].astype(BF16)

    return pl.pallas_call(
        body, name=name, grid=(m // rows,),
        in_specs=[pl.BlockSpec((rows, n), lambda i: (i, 0))],
        out_specs=pl.BlockSpec((rows, n), lambda i: (i, 0)),
        out_shape=jax.ShapeDtypeStruct((m, n), BF16),
        compiler_params=_params(("parallel",)),
    )(a)


def _matmul(a, b, *, grid, a_spec, b_spec, o_spec, out_shape, dims, name):
    nk = grid[2]

    def body(a_ref, b_ref, o_ref, *acc):
        p = lax.dot_general(a_ref[...], b_ref[...], (dims, ((), ())), preferred_element_type=F32)
        if nk == 1:
            o_ref[...] = p.astype(o_ref.dtype)
        else:
            acc_ref, = acc
            k = pl.program_id(2)

            @pl.when(k == 0)
            def _():
                acc_ref[...] = p

            @pl.when(k > 0)
            def _():
                acc_ref[...] += p

            @pl.when(k == nk - 1)
            def _():
                o_ref[...] = acc_ref[...].astype(o_ref.dtype)

    block = [d for d in o_spec.block_shape if d is not None]
    scratch = [pltpu.VMEM(tuple(block), F32)] if nk > 1 else []
    return pl.pallas_call(
        body, name=name, grid=grid, in_specs=[a_spec, b_spec], out_specs=o_spec,
        out_shape=out_shape, scratch_shapes=scratch,
        compiler_params=_params(("parallel", "parallel", "arbitrary")),
    )(a, b)


def _adam_math(w, g, m, v):
    m = ADAM_B1 * m + (1.0 - ADAM_B1) * g
    v = ADAM_B2 * v + (1.0 - ADAM_B2) * (g * g)
    m_hat = m / (1.0 - ADAM_B1 ** ADAM_STEP)
    v_hat = v / (1.0 - ADAM_B2 ** ADAM_STEP)
    delta = -ADAM_LR * (m_hat / (jnp.sqrt(v_hat) + ADAM_EPS) + ADAM_WD * w)
    return delta, m, v


def _adam(w, g, m, v, rows, name):
    r, n = w.shape

    def body(w_ref, g_ref, m_ref, v_ref, d_ref, mo_ref, vo_ref):
        d, mo, vo = _adam_math(w_ref[...], g_ref[...], m_ref[...], v_ref[...])
        d_ref[...] = d
        mo_ref[...] = mo
        vo_ref[...] = vo

    spec = pl.BlockSpec((rows, n), lambda i: (i, 0))
    shape = jax.ShapeDtypeStruct((r, n), F32)
    return pl.pallas_call(
        body, name=name, grid=(r // rows,), in_specs=[spec] * 4, out_specs=[spec] * 3,
        out_shape=[shape] * 3, compiler_params=_params(("parallel",)),
    )(w, g, m, v)


def _adam_ada(c_cols, dmod, w, m, v, rows, name):
    r, n = w.shape

    def body(c_ref, dm_ref, w_ref, m_ref, v_ref, g_ref, d_ref, mo_ref, vo_ref):
        cv = c_ref[...]
        c_act = cv * _sigmoid(cv)
        g = c_act[:, 0:1] * dm_ref[0:1, :]
        for b in range(1, N_DEV):
            g = g + c_act[:, b:b + 1] * dm_ref[b:b + 1, :]
        d, mo, vo = _adam_math(w_ref[...], g, m_ref[...], v_ref[...])
        g_ref[...] = g
        d_ref[...] = d
        mo_ref[...] = mo
        vo_ref[...] = vo

    spec = pl.BlockSpec((rows, n), lambda i: (i, 0))
    shape = jax.ShapeDtypeStruct((r, n), F32)
    return pl.pallas_call(
        body, name=name, grid=(r // rows,),
        in_specs=[pl.BlockSpec((rows, N_DEV), lambda i: (i, 0)), pl.BlockSpec((N_DEV, n), lambda i: (0, 0)),
                  spec, spec, spec],
        out_specs=[spec] * 4, out_shape=[shape] * 4, compiler_params=_params(("parallel",)),
    )(c_cols, dmod, w, m, v)


def _gather_weights(win, wout, c8, cw):
    d, ns = win.shape
    r4, dm = wout.shape
    h, ho = d // 2, r4 // 2

    def body(win_ref, wout_ref, c8_ref, cw_ref, winf_ref, woutf_ref, call_ref, cwall_ref, ssem, rsem, lsem):
        x, y, c = _position()
        chip = 2 * x + y
        me = 4 * x + 2 * y + c
        sib = (x, y, 1 - c)
        chips, cidx = _other_chips(x, y)

        def win_rows(slot, half):
            return winf_ref.at[slot, pl.ds(half * h, h), :]

        def wout_rows(slot, half):
            return woutf_ref.at[slot, pl.ds(half * ho, ho), :]

        own = [pltpu.make_async_copy(win_ref, winf_ref.at[chip], lsem.at[0]),
               pltpu.make_async_copy(wout_ref, woutf_ref.at[chip], lsem.at[1]),
               pltpu.make_async_copy(c8_ref, call_ref.at[me], lsem.at[2]),
               pltpu.make_async_copy(cw_ref, cwall_ref.at[chip], lsem.at[3])]
        for cp in own:
            cp.start()
        sends = []
        for k, (cx, cy) in enumerate(chips):
            dev = (cx, cy, c)
            sends.append(_rcopy(win_ref.at[pl.ds(c * h, h), :], win_rows(chip, c), ssem.at[k], rsem.at[k], dev))
            sends.append(_rcopy(wout_ref.at[pl.ds(c * ho, ho), :], wout_rows(chip, c), ssem.at[3 + k], rsem.at[3 + k], dev))
            sends.append(_rcopy(cw_ref, cwall_ref.at[chip], ssem.at[6 + k], rsem.at[6 + k], dev))
        for mask in range(1, N_DEV):
            fx, fy, fc = (mask >> 2) & 1, (mask >> 1) & 1, mask & 1
            dev = (1 - x if fx else x, 1 - y if fy else y, 1 - c if fc else c)
            sends.append(_rcopy(c8_ref, call_ref.at[me], ssem.at[8 + mask], rsem.at[8 + mask], dev))
        for cp in sends:
            cp.start()
        for k in range(3):
            rows = win_rows(cidx[k], c)
            _rcopy(rows, rows, ssem.at[k], rsem.at[k], sib).wait_recv()
            fw = _rcopy(rows, rows, ssem.at[16 + k], rsem.at[16 + k], sib)
            fw.start()
            sends.append(fw)
            rows = wout_rows(cidx[k], c)
            _rcopy(rows, rows, ssem.at[3 + k], rsem.at[3 + k], sib).wait_recv()
            fw = _rcopy(rows, rows, ssem.at[19 + k], rsem.at[19 + k], sib)
            fw.start()
            sends.append(fw)
        for k in range(3):
            rows = win_rows(cidx[k], 1 - c)
            _rcopy(rows, rows, ssem.at[16 + k], rsem.at[16 + k], sib).wait_recv()
            rows = wout_rows(cidx[k], 1 - c)
            _rcopy(rows, rows, ssem.at[19 + k], rsem.at[19 + k], sib).wait_recv()
            slot = cwall_ref.at[cidx[k]]
            _rcopy(slot, slot, ssem.at[6 + k], rsem.at[6 + k], sib).wait_recv()
        for mask in range(1, N_DEV):
            slot = call_ref.at[jnp.bitwise_xor(me, mask)]
            _rcopy(slot, slot, ssem.at[8 + mask], rsem.at[8 + mask], sib).wait_recv()
        for cp in sends:
            cp.wait_send()
        for cp in own:
            cp.wait()

    return pl.pallas_call(
        body, name="gather_weights",
        in_specs=[ANY, ANY, VMEM, VMEM], out_specs=[ANY, ANY, VMEM, VMEM],
        out_shape=[jax.ShapeDtypeStruct((N_CHIPS, d, ns), BF16), jax.ShapeDtypeStruct((N_CHIPS, r4, dm), BF16),
                   jax.ShapeDtypeStruct((N_DEV,) + c8.shape, F32), jax.ShapeDtypeStruct((N_CHIPS,) + cw.shape, F32)],
        scratch_shapes=[pltpu.SemaphoreType.DMA((24,)), pltpu.SemaphoreType.DMA((24,)), pltpu.SemaphoreType.DMA((4,))],
        compiler_params=pltpu.CompilerParams(vmem_limit_bytes=VMEM_LIMIT),
    )(win, wout, c8, cw)


def _exchange_mod(mod_part):
    def body(mp_ref, out_ref, ssem, rsem, lsem):
        x, y, c = _position()
        chip = 2 * x + y
        chips, cidx = _other_chips(x, y)
        own = pltpu.make_async_copy(mp_ref, out_ref.at[chip], lsem)
        own.start()
        sends = [_rcopy(mp_ref, out_ref.at[chip], ssem.at[k], rsem.at[k], (cx, cy, c))
                 for k, (cx, cy) in enumerate(chips)]
        for cp in sends:
            cp.start()
        for k in range(3):
            slot = out_ref.at[cidx[k]]
            _rcopy(slot, slot, ssem.at[k], rsem.at[k], (x, y, c)).wait_recv()
        for cp in sends:
            cp.wait_send()
        own.wait()

    return pl.pallas_call(
        body, name="exchange_mod", in_specs=[VMEM], out_specs=VMEM,
        out_shape=jax.ShapeDtypeStruct((N_CHIPS,) + mod_part.shape, F32),
        scratch_shapes=[pltpu.SemaphoreType.DMA((3,)), pltpu.SemaphoreType.DMA((3,)), pltpu.SemaphoreType.DMA],
    )(mod_part)


def _gather_small(pack):
    rows, n = pack.shape

    def body(p_ref, sum_ref, all_ref, ssem, rsem, lsem):
        x, y, c = _position()
        me = 4 * x + 2 * y + c
        own = pltpu.make_async_copy(p_ref, all_ref.at[me], lsem)
        own.start()
        sends = []
        for mask in range(1, N_DEV):
            fx, fy, fc = (mask >> 2) & 1, (mask >> 1) & 1, mask & 1
            dev = (1 - x if fx else x, 1 - y if fy else y, 1 - c if fc else c)
            sends.append(_rcopy(p_ref, all_ref.at[me], ssem.at[mask - 1], rsem.at[mask - 1], dev))
        for cp in sends:
            cp.start()
        for mask in range(1, N_DEV):
            slot = all_ref.at[jnp.bitwise_xor(me, mask)]
            _rcopy(slot, slot, ssem.at[mask - 1], rsem.at[mask - 1], (x, y, c)).wait_recv()
        for cp in sends:
            cp.wait_send()
        own.wait()
        acc = all_ref[0]
        for k in range(1, N_DEV):
            acc = acc + all_ref[k]
        sum_ref[...] = acc

    return pl.pallas_call(
        body, name="gather_small", in_specs=[VMEM], out_specs=[VMEM, VMEM],
        out_shape=[jax.ShapeDtypeStruct((rows, n), F32), jax.ShapeDtypeStruct((N_DEV, rows, n), F32)],
        scratch_shapes=[pltpu.SemaphoreType.DMA((7,)), pltpu.SemaphoreType.DMA((7,)), pltpu.SemaphoreType.DMA],
        compiler_params=pltpu.CompilerParams(vmem_limit_bytes=VMEM_LIMIT),
    )(pack)


def _swap_halves(ga, gb):
    ha, hb = ga.shape[1] // 2, gb.shape[1] // 2

    def body(ga_ref, gb_ref, ra_ref, rb_ref, ssem, rsem):
        x, y, c = _position()
        sib = (x, y, 1 - c)
        cps = [_rcopy(ga_ref.at[:, pl.ds((1 - c) * ha, ha), :], ra_ref, ssem.at[0], rsem.at[0], sib),
               _rcopy(gb_ref.at[:, pl.ds((1 - c) * hb, hb), :], rb_ref, ssem.at[1], rsem.at[1], sib)]
        for cp in cps:
            cp.start()
        for cp in cps:
            cp.wait()

    return pl.pallas_call(
        body, name="swap_halves", in_specs=[ANY, ANY], out_specs=[ANY, ANY],
        out_shape=[jax.ShapeDtypeStruct((ga.shape[0], ha, ga.shape[2]), F32),
                   jax.ShapeDtypeStruct((gb.shape[0], hb, gb.shape[2]), F32)],
        scratch_shapes=[pltpu.SemaphoreType.DMA((2,)), pltpu.SemaphoreType.DMA((2,))],
    )(ga, gb)


def _send_chip_partials(qa, qb):
    def body(qa_ref, qb_ref, ra_ref, rb_ref, ssem, rsem):
        x, y, c = _position()
        chips, cidx = _other_chips(x, y)
        cps = []
        for k, (cx, cy) in enumerate(chips):
            dev = (cx, cy, c)
            cps.append(_rcopy(qa_ref.at[cidx[k]], ra_ref.at[k], ssem.at[k], rsem.at[k], dev))
            cps.append(_rcopy(qb_ref.at[cidx[k]], rb_ref.at[k], ssem.at[3 + k], rsem.at[3 + k], dev))
        for cp in cps:
            cp.start()
        for cp in cps:
            cp.wait()

    return pl.pallas_call(
        body, name="send_chip_partials", in_specs=[ANY, ANY], out_specs=[ANY, ANY],
        out_shape=[jax.ShapeDtypeStruct((3,) + qa.shape[1:], BF16), jax.ShapeDtypeStruct((3,) + qb.shape[1:], BF16)],
        scratch_shapes=[pltpu.SemaphoreType.DMA((6,)), pltpu.SemaphoreType.DMA((6,))],
    )(qa, qb)


def _share_halves(ga, gb):
    ha, hb = ga.shape[0] // 2, gb.shape[0] // 2

    def body(ga_in, gb_in, ga_ref, gb_ref, ssem, rsem):
        x, y, c = _position()
        sib = (x, y, 1 - c)
        ra = ga_ref.at[pl.ds(c * ha, ha), :]
        rb = gb_ref.at[pl.ds(c * hb, hb), :]
        cps = [_rcopy(ra, ra, ssem.at[0], rsem.at[0], sib), _rcopy(rb, rb, ssem.at[1], rsem.at[1], sib)]
        for cp in cps:
            cp.start()
        for cp in cps:
            cp.wait_send()
        ra = ga_ref.at[pl.ds((1 - c) * ha, ha), :]
        rb = gb_ref.at[pl.ds((1 - c) * hb, hb), :]
        _rcopy(ra, ra, ssem.at[0], rsem.at[0], sib).wait_recv()
        _rcopy(rb, rb, ssem.at[1], rsem.at[1], sib).wait_recv()

    return pl.pallas_call(
        body, name="share_halves", in_specs=[ANY, ANY], out_specs=[ANY, ANY],
        out_shape=[jax.ShapeDtypeStruct(ga.shape, F32), jax.ShapeDtypeStruct(gb.shape, F32)],
        input_output_aliases={0: 0, 1: 1},
        scratch_shapes=[pltpu.SemaphoreType.DMA((2,)), pltpu.SemaphoreType.DMA((2,))],
    )(ga, gb)


def _chip_partial(pos, g, recv, rows, name):
    ns, full, n = g.shape
    h = full // 2
    nb = h // rows

    def body(pos_ref, g_ref, r_ref, o_ref):
        o_ref[...] = (g_ref[...] + r_ref[...]).astype(BF16)

    return pl.pallas_call(
        body, name=name,
        grid_spec=pltpu.PrefetchScalarGridSpec(
            num_scalar_prefetch=1, grid=(ns, nb),
            in_specs=[pl.BlockSpec((None, rows, n), lambda s, i, p: (s, p[1] * nb + i, 0)),
                      pl.BlockSpec((None, rows, n), lambda s, i, p: (s, i, 0))],
            out_specs=pl.BlockSpec((None, rows, n), lambda s, i, p: (s, i, 0))),
        out_shape=jax.ShapeDtypeStruct((ns, h, n), BF16),
        compiler_params=_params(("parallel", "parallel")),
    )(pos, g, recv)


def _final_half(pos, g, recv_a, recv_b, rows, name):
    ns, full, n = g.shape
    h = full // 2
    nb = h // rows

    def body(pos_ref, g_ref, ra_ref, rb_ref, o_ref):
        acc = g_ref[...] + ra_ref[...]
        for k in range(3):
            acc = acc + rb_ref[k].astype(F32)
        o_ref[...] = acc

    return pl.pallas_call(
        body, name=name,
        grid_spec=pltpu.PrefetchScalarGridSpec(
            num_scalar_prefetch=1, grid=(nb,),
            in_specs=[pl.BlockSpec((None, rows, n), lambda i, p: (p[0], p[1] * nb + i, 0)),
                      pl.BlockSpec((None, rows, n), lambda i, p: (p[0], i, 0)),
                      pl.BlockSpec((3, rows, n), lambda i, p: (0, i, 0))],
            out_specs=pl.BlockSpec((rows, n), lambda i, p: (p[1] * nb + i, 0))),
        out_shape=jax.ShapeDtypeStruct((full, n), F32),
        compiler_params=_params(("parallel",)),
    )(pos, g, recv_a, recv_b)


def _modulation(c_rows, w_ada, b_ada, cols, name):
    d, n = w_ada.shape
    rows = c_rows.shape[0]

    def body(c_ref, w_ref, b_ref, o_ref):
        cv = c_ref[...]
        c_act = (cv * _sigmoid(cv)).astype(BF16)
        o_ref[...] = jnp.dot(c_act, w_ref[...].astype(BF16), preferred_element_type=F32) + b_ref[...]

    return pl.pallas_call(
        body, name=name, grid=(n // cols,),
        in_specs=[pl.BlockSpec((rows, d), lambda j: (0, 0)), pl.BlockSpec((d, cols), lambda j: (0, j)),
                  pl.BlockSpec((1, cols), lambda j: (0, j))],
        out_specs=pl.BlockSpec((rows, cols), lambda j: (0, j)),
        out_shape=jax.ShapeDtypeStruct((rows, n), F32),
        compiler_params=_params(("parallel",)),
    )(c_rows, w_ada, b_ada)


def _prenorm(x, norm_g, scale, shift, rows):
    s, d = x.shape

    def body(x_ref, g_ref, sc_ref, sh_ref, h_ref, r_ref):
        xv = x_ref[...]
        r = lax.rsqrt(jnp.mean(xv * xv, axis=-1, keepdims=True) + EPS)
        h = (xv * r * g_ref[...]) * (1.0 + sc_ref[...]) + sh_ref[...]
        h_ref[...] = h.astype(BF16)
        r_ref[...] = r

    vec = pl.BlockSpec((1, d), lambda i: (0, 0))
    return pl.pallas_call(
        body, name="prenorm", grid=(s // rows,),
        in_specs=[pl.BlockSpec((rows, d), lambda i: (i, 0)), vec, vec, vec],
        out_specs=[pl.BlockSpec((rows, d), lambda i: (i, 0)), pl.BlockSpec((rows, 1), lambda i: (i, 0))],
        out_shape=[jax.ShapeDtypeStruct((s, d), BF16), jax.ShapeDtypeStruct((s, 1), F32)],
        compiler_params=_params(("parallel",)),
    )(x, norm_g, scale, shift)


def _mixer_a_fwd(proj, conv_w, wa, rows, cols):
    s = proj.shape[0]
    ncb = wa // cols

    def body(ab_ref, ac_ref, ax_ref, az_ref, w_ref, y_ref, qbuf):
        t = pl.program_id(1)

        @pl.when(t == 0)
        def _():
            qbuf[0:HALO_A, :] = jnp.zeros((HALO_A, cols), F32)

        q = ac_ref[...] * ax_ref[...]
        qbuf[HALO_A:HALO_A + rows, :] = q
        conv = w_ref[2:3, :] * q
        for k in range(TAPS_A - 1):
            off = HALO_A - (TAPS_A - 1) + k
            conv = conv + w_ref[k:k + 1, :] * qbuf[off:off + rows, :]
        zv = az_ref[...]
        y_ref[...] = (ab_ref[...] * conv * (zv * _sigmoid(zv))).astype(BF16)
        qbuf[0:HALO_A, :] = qbuf[rows:rows + HALO_A, :]

    def sec(k):
        return pl.BlockSpec((rows, cols), lambda cb, t, k=k: (t, k * ncb + cb))

    return pl.pallas_call(
        body, name="mixer_a_fwd", grid=(ncb, s // rows),
        in_specs=[sec(0), sec(1), sec(2), sec(3), pl.BlockSpec((HALO_A, cols), lambda cb, t: (0, cb))],
        out_specs=pl.BlockSpec((rows, cols), lambda cb, t: (t, cb)),
        out_shape=jax.ShapeDtypeStruct((s, 2 * wa), BF16),
        scratch_shapes=[pltpu.VMEM((HALO_A + rows, cols), F32)],
        compiler_params=_params(("parallel", "arbitrary")),
    )(proj, proj, proj, proj, conv_w)


def _shifted_back(dst, src, lo, hi):
    for n in range(8):
        dst[n, lo:hi, :] = src[lo - n:hi - n, :]


def _shifted_fwd(dst, src, lo, hi):
    for n in range(8):
        dst[n, lo:hi, :] = src[lo + n:hi + n, :]


def _mixer_b_conv_fwd(proj, conv_w, conv_b, wa, rows, cols, chunk):
    s = proj.shape[0]
    wb = conv_w.shape[1]
    ncb = wb // cols
    sec0 = 4 * wa // cols

    def body(bv_ref, bg_ref, w_ref, b_ref, u0_ref, u_ref, ubuf, sh):
        t = pl.program_id(1)

        @pl.when(t == 0)
        def _():
            ubuf[0:HALO_B, :] = jnp.zeros((HALO_B, cols), F32)

        u0 = bv_ref[...] * _sigmoid(bg_ref[...])
        u0_ref[...] = u0
        ubuf[HALO_B:HALO_B + rows, :] = u0
        _shifted_back(sh, ubuf, 8, HALO_B + rows)

        def row_chunk(rc, carry):
            base = pl.multiple_of(rc * chunk, chunk)
            acc = jnp.zeros((chunk, cols), F32)
            for k in range(TAPS_B):
                mq, n = divmod(TAPS_B - 1 - k, 8)
                acc = acc + w_ref[k:k + 1, :] * sh[n, pl.ds(HALO_B - 8 * mq + base, chunk), :]
            u_ref[pl.ds(base, chunk), :] = acc + b_ref[...]
            return carry

        lax.fori_loop(0, rows // chunk, row_chunk, 0)
        ubuf[0:HALO_B, :] = ubuf[rows:rows + HALO_B, :]

    return pl.pallas_call(
        body, name="mixer_b_conv_fwd", grid=(ncb, s // rows),
        in_specs=[pl.BlockSpec((rows, cols), lambda cb, t: (t, sec0 + cb)),
                  pl.BlockSpec((rows, cols), lambda cb, t: (t, sec0 + ncb + cb)),
                  pl.BlockSpec((HALO_B, cols), lambda cb, t: (0, cb)),
                  pl.BlockSpec((1, cols), lambda cb, t: (0, cb))],
        out_specs=[pl.BlockSpec((rows, cols), lambda cb, t: (t, cb))] * 2,
        out_shape=[jax.ShapeDtypeStruct((s, wb), F32)] * 2,
        scratch_shapes=[pltpu.VMEM((HALO_B + rows, cols), F32), pltpu.VMEM((8, HALO_B + rows, cols), F32)],
        compiler_params=_params(("parallel", "arbitrary")),
    )(proj, proj, conv_w, conv_b)


def _layernorm_stats(u):
    mu = jnp.mean(u, axis=-1, keepdims=True)
    xc = u - mu
    var = jnp.mean(xc * xc, axis=-1, keepdims=True)
    return xc * lax.rsqrt(var + EPS), lax.rsqrt(var + EPS)


def _mixer_b_gate_fwd(y, u, proj, ln_g, ln_b, wa, rows):
    s, wb = u.shape
    sec_z = (4 * wa + 2 * wb) // wb

    def body(y_in, u_ref, bz_ref, g_ref, b_ref, y_ref):
        uh, _ = _layernorm_stats(u_ref[...])
        ln = uh * g_ref[...] + b_ref[...]
        zv = bz_ref[...]
        y_ref[...] = ((ln * _sigmoid(ln)) * (zv * _sigmoid(zv))).astype(BF16)

    vec = pl.BlockSpec((1, wb), lambda i: (0, 0))
    return pl.pallas_call(
        body, name="mixer_b_gate_fwd", grid=(s // rows,),
        in_specs=[ANY, pl.BlockSpec((rows, wb), lambda i: (i, 0)), pl.BlockSpec((rows, wb), lambda i: (i, sec_z)),
                  vec, vec],
        out_specs=pl.BlockSpec((rows, wb), lambda i: (i, wa // wb)),
        out_shape=jax.ShapeDtypeStruct(y.shape, BF16), input_output_aliases={0: 0},
        compiler_params=_params(("parallel",)),
    )(y, u, proj, ln_g, ln_b)


def _loss_head(x, o, target, gate, final_g, rows):
    s, d = x.shape

    def body(x_ref, o_ref, t_ref, gate_ref, fg_ref, dx2_ref, do_ref, loss_ref, gfg_ref, dgate_ref):
        i = pl.program_id(0)
        ov = o_ref[...]
        x2 = x_ref[...] + gate_ref[...] * ov
        r2 = lax.rsqrt(jnp.mean(x2 * x2, axis=-1, keepdims=True) + EPS)
        xn2 = x2 * r2
        diff = xn2 * fg_ref[...] - t_ref[...]
        dout = diff * (1.0 / d)
        dxn2 = dout * fg_ref[...]
        dx2 = r2 * (dxn2 - xn2 * jnp.mean(dxn2 * xn2, axis=-1, keepdims=True))
        dx2_ref[...] = dx2
        do_ref[...] = (gate_ref[...] * dx2).astype(BF16)
        loss_part = 0.5 * jnp.sum(jnp.mean(diff * diff, axis=-1, keepdims=True), axis=0, keepdims=True)
        gfg_part = jnp.sum(dout * xn2, axis=0, keepdims=True)
        dgate_part = jnp.sum(dx2 * ov, axis=0, keepdims=True)

        @pl.when(i == 0)
        def _():
            loss_ref[...] = jnp.zeros_like(loss_ref)
            gfg_ref[...] = jnp.zeros_like(gfg_ref)
            dgate_ref[...] = jnp.zeros_like(dgate_ref)

        loss_ref[...] += jnp.broadcast_to(loss_part, loss_ref.shape)
        gfg_ref[...] += gfg_part
        dgate_ref[...] += dgate_part

    blk = pl.BlockSpec((rows, d), lambda i: (i, 0))
    vec = pl.BlockSpec((1, d), lambda i: (0, 0))
    return pl.pallas_call(
        body, name="loss_head", grid=(s // rows,),
        in_specs=[blk, blk, blk, vec, vec],
        out_specs=[blk, blk, pl.BlockSpec((1, 128), lambda i: (0, 0)), vec, vec],
        out_shape=[jax.ShapeDtypeStruct((s, d), F32), jax.ShapeDtypeStruct((s, d), BF16),
                   jax.ShapeDtypeStruct((1, 128), F32), jax.ShapeDtypeStruct((1, d), F32),
                   jax.ShapeDtypeStruct((1, d), F32)],
        compiler_params=_params(("arbitrary",)),
    )(x, o, target, gate, final_g)


def _mixer_a_bwd(proj, dy, conv_w, wa, din, rows):
    s = proj.shape[0]
    nt = s // rows
    per8 = rows // HALO_A

    def body(ab_ref, ac_ref, ax_ref, az_ref, hc_ref, hx_ref, dy_ref, w_ref, dp_ref, dw_ref, qbuf, dbuf):
        i = pl.program_id(0)

        @pl.when(i == 0)
        def _():
            dbuf[rows:rows + HALO_A, :] = jnp.zeros((HALO_A, wa), F32)
            dw_ref[...] = jnp.zeros_like(dw_ref)

        keep = jnp.where(i == nt - 1, 0.0, 1.0)
        qbuf[0:HALO_A, :] = hc_ref[...] * hx_ref[...] * keep
        acv, axv = ac_ref[...], ax_ref[...]
        q = acv * axv
        qbuf[HALO_A:HALO_A + rows, :] = q
        conv = w_ref[2:3, :] * q
        for k in range(TAPS_A - 1):
            off = HALO_A - (TAPS_A - 1) + k
            conv = conv + w_ref[k:k + 1, :] * qbuf[off:off + rows, :]
        zv, abv, dyv = az_ref[...], ab_ref[...], dy_ref[...]
        sg = _sigmoid(zv)
        sz = zv * sg
        dp_ref[:, 0:wa] = (dyv * conv * sz).astype(BF16)
        dp_ref[:, 3 * wa:4 * wa] = (dyv * abv * conv * (sg * (1.0 + zv * (1.0 - sg)))).astype(BF16)
        dconv = dyv * abv * sz
        dbuf[0:rows, :] = dconv
        dq = w_ref[2:3, :] * dconv
        for k in range(TAPS_A - 1):
            off = TAPS_A - 1 - k
            dq = dq + w_ref[k:k + 1, :] * dbuf[off:off + rows, :]
        dp_ref[:, wa:2 * wa] = (dq * axv).astype(BF16)
        dp_ref[:, 2 * wa:3 * wa] = (dq * acv).astype(BF16)
        for k in range(TAPS_A):
            off = HALO_A - (TAPS_A - 1) + k
            dw_ref[k:k + 1, :] += jnp.sum(dconv * qbuf[off:off + rows, :], axis=0, keepdims=True)
        dbuf[rows:rows + HALO_A, :] = dbuf[0:HALO_A, :]

    def sec(k):
        return pl.BlockSpec((rows, wa), lambda i, k=k: (nt - 1 - i, k))

    def halo(k):
        return pl.BlockSpec((HALO_A, wa), lambda i, k=k: (jnp.maximum((nt - 1 - i) * per8 - 1, 0), k))

    return pl.pallas_call(
        body, name="mixer_a_bwd", grid=(nt,),
        in_specs=[sec(0), sec(1), sec(2), sec(3), halo(1), halo(2),
                  pl.BlockSpec((rows, wa), lambda i: (nt - 1 - i, 0)),
                  pl.BlockSpec((HALO_A, wa), lambda i: (0, 0))],
        out_specs=[pl.BlockSpec((rows, 4 * wa), lambda i: (nt - 1 - i, 0)),
                   pl.BlockSpec((HALO_A, wa), lambda i: (0, 0))],
        out_shape=[jax.ShapeDtypeStruct((s, din), BF16), jax.ShapeDtypeStruct((HALO_A, wa), F32)],
        scratch_shapes=[pltpu.VMEM((HALO_A + rows, wa), F32), pltpu.VMEM((rows + HALO_A, wa), F32)],
        compiler_params=_params(("arbitrary",)),
    )(proj, proj, proj, proj, proj, proj, dy, conv_w)


def _mixer_b_gate_bwd(dproj, dy, u, proj, ln_g, ln_b, wa, rows):
    s, wb = u.shape
    sec_z = (4 * wa + 2 * wb) // wb

    def body(dp_in, dy_ref, u_ref, bz_ref, g_ref, b_ref, dp_ref, du_ref, dg_ref, db_ref, dcb_ref):
        i = pl.program_id(0)
        uh, rs = _layernorm_stats(u_ref[...])
        ln = uh * g_ref[...] + b_ref[...]
        sl = _sigmoid(ln)
        zv = bz_ref[...]
        sg = _sigmoid(zv)
        dyv = dy_ref[...]
        dp_ref[...] = (dyv * (ln * sl) * (sg * (1.0 + zv * (1.0 - sg)))).astype(BF16)
        dln = dyv * (zv * sg) * (sl * (1.0 + ln * (1.0 - sl)))
        duh = dln * g_ref[...]
        du = rs * (duh - jnp.mean(duh, axis=-1, keepdims=True) - uh * jnp.mean(duh * uh, axis=-1, keepdims=True))
        du_ref[...] = du

        @pl.when(i == 0)
        def _():
            dg_ref[...] = jnp.zeros_like(dg_ref)
            db_ref[...] = jnp.zeros_like(db_ref)
            dcb_ref[...] = jnp.zeros_like(dcb_ref)

        dg_ref[...] += jnp.sum(dln * uh, axis=0, keepdims=True)
        db_ref[...] += jnp.sum(dln, axis=0, keepdims=True)
        dcb_ref[...] += jnp.sum(du, axis=0, keepdims=True)

    blk = pl.BlockSpec((rows, wb), lambda i: (i, 0))
    vec = pl.BlockSpec((1, wb), lambda i: (0, 0))
    vshape = jax.ShapeDtypeStruct((1, wb), F32)
    return pl.pallas_call(
        body, name="mixer_b_gate_bwd", grid=(s // rows,),
        in_specs=[ANY, pl.BlockSpec((rows, wb), lambda i: (i, wa // wb)), blk,
                  pl.BlockSpec((rows, wb), lambda i: (i, sec_z)), vec, vec],
        out_specs=[pl.BlockSpec((rows, wb), lambda i: (i, sec_z)), blk, vec, vec, vec],
        out_shape=[jax.ShapeDtypeStruct(dproj.shape, BF16), jax.ShapeDtypeStruct((s, wb), F32), vshape, vshape, vshape],
        input_output_aliases={0: 0},
        compiler_params=_params(("arbitrary",)),
    )(dproj, dy, u, proj, ln_g, ln_b)


def _mixer_b_conv_bwd(dproj, du, u0, proj, conv_w, wa, rows, lanes, chunk):
    s, wb = du.shape
    nt = s // rows
    per32 = rows // HALO_B
    sec_v = 4 * wa // wb
    nlc = wb // lanes
    nrc = rows // chunk

    def body(dp_in, du_ref, u0_ref, h0_ref, bv_ref, bg_ref, w_ref, dp_ref, dw_ref, ubuf, dbuf, sh, shf, carry):
        i = pl.program_id(0)

        @pl.when(i == 0)
        def _():
            carry[...] = jnp.zeros_like(carry)
            dw_ref[...] = jnp.zeros_like(dw_ref)

        keep = jnp.where(i == nt - 1, 0.0, 1.0)
        for lc in range(nlc):
            cs = slice(lc * lanes, (lc + 1) * lanes)
            ubuf[0:HALO_B, :] = h0_ref[:, cs] * keep
            ubuf[HALO_B:HALO_B + rows, :] = u0_ref[:, cs]
            dbuf[0:rows, :] = du_ref[:, cs]
            dbuf[rows:rows + HALO_B, :] = carry[:, cs]
            _shifted_back(sh, ubuf, 8, HALO_B + rows)
            _shifted_fwd(shf, dbuf, 0, rows + HALO_B - 8)

            def row_chunk(rc, c0):
                base = pl.multiple_of(rc * chunk, chunk)
                acc = jnp.zeros((chunk, lanes), F32)
                for k in range(TAPS_B):
                    mq, n = divmod(TAPS_B - 1 - k, 8)
                    acc = acc + w_ref[k:k + 1, cs] * shf[n, pl.ds(base + 8 * mq, chunk), :]
                sg = _sigmoid(bg_ref[pl.ds(base, chunk), cs])
                bv = bv_ref[pl.ds(base, chunk), cs]
                dp_ref[pl.ds(base, chunk), cs] = (acc * sg).astype(BF16)
                dp_ref[pl.ds(base, chunk), lc * lanes + wb:(lc + 1) * lanes + wb] = (
                    acc * bv * sg * (1.0 - sg)).astype(BF16)
                return c0

            lax.fori_loop(0, nrc, row_chunk, 0)

            for k in range(TAPS_B):
                mq, n = divmod(TAPS_B - 1 - k, 8)

                def tap_rows(rc, acc, mq=mq, n=n):
                    base = pl.multiple_of(rc * chunk, chunk)
                    prod = dbuf[pl.ds(base, chunk), :] * sh[n, pl.ds(HALO_B - 8 * mq + base, chunk), :]
                    return acc + jnp.sum(prod.reshape(chunk // 8, 8, lanes), axis=0)

                part = lax.fori_loop(0, nrc, tap_rows, jnp.zeros((8, lanes), F32))
                dw_ref[k:k + 1, cs] += jnp.sum(part, axis=0, keepdims=True)
            carry[:, cs] = dbuf[0:HALO_B, :]

    def rev(cols_blk):
        return pl.BlockSpec((rows, wb), lambda i, cb=cols_blk: (nt - 1 - i, cb))

    return pl.pallas_call(
        body, name="mixer_b_conv_bwd", grid=(nt,),
        in_specs=[ANY, rev(0), rev(0),
                  pl.BlockSpec((HALO_B, wb), lambda i: (jnp.maximum((nt - 1 - i) * per32 - 1, 0), 0)),
                  rev(sec_v), rev(sec_v + 1), pl.BlockSpec((HALO_B, wb), lambda i: (0, 0))],
        out_specs=[pl.BlockSpec((rows, 2 * wb), lambda i: (nt - 1 - i, sec_v // 2)),
                   pl.BlockSpec((HALO_B, wb), lambda i: (0, 0))],
        out_shape=[jax.ShapeDtypeStruct(dproj.shape, BF16), jax.ShapeDtypeStruct((HALO_B, wb), F32)],
        input_output_aliases={0: 0},
        scratch_shapes=[pltpu.VMEM((HALO_B + rows, lanes), F32), pltpu.VMEM((rows + HALO_B, lanes), F32),
                        pltpu.VMEM((8, HALO_B + rows, lanes), F32), pltpu.VMEM((8, rows + HALO_B, lanes), F32),
                        pltpu.VMEM((HALO_B, wb), F32)],
        compiler_params=_params(("arbitrary",)),
    )(dproj, du, u0, u0, proj, proj, conv_w)


def _prenorm_bwd(x, r, dh, dx2, norm_g, scale, rows):
    s, d = x.shape

    def body(x_ref, r_ref, dh_ref, dx2_ref, g_ref, sc_ref, gx_ref, dsh_ref, dsc_ref, dg_ref):
        i = pl.program_id(0)
        rv = r_ref[...]
        xn = x_ref[...] * rv
        dhv = dh_ref[...]
        one_sc = 1.0 + sc_ref[...]
        dxn = dhv * one_sc * g_ref[...]
        gx_ref[...] = dx2_ref[...] + rv * (dxn - xn * jnp.mean(dxn * xn, axis=-1, keepdims=True))

        @pl.when(i == 0)
        def _():
            dsh_ref[...] = jnp.zeros_like(dsh_ref)
            dsc_ref[...] = jnp.zeros_like(dsc_ref)
            dg_ref[...] = jnp.zeros_like(dg_ref)

        dsh_ref[...] += jnp.sum(dhv, axis=0, keepdims=True)
        dsc_ref[...] += jnp.sum(dhv * (xn * g_ref[...]), axis=0, keepdims=True)
        dg_ref[...] += jnp.sum(dhv * one_sc * xn, axis=0, keepdims=True)

    blk = pl.BlockSpec((rows, d), lambda i: (i, 0))
    vec = pl.BlockSpec((1, d), lambda i: (0, 0))
    vshape = jax.ShapeDtypeStruct((1, d), F32)
    return pl.pallas_call(
        body, name="prenorm_bwd", grid=(s // rows,),
        in_specs=[blk, pl.BlockSpec((rows, 1), lambda i: (i, 0)), blk, blk, vec, vec],
        out_specs=[blk, vec, vec, vec],
        out_shape=[jax.ShapeDtypeStruct((s, d), F32), vshape, vshape, vshape],
        compiler_params=_params(("arbitrary",)),
    )(x, r, dh, dx2, norm_g, scale)


def _pad_rows(a, rows):
    return jnp.pad(a, ((0, rows - a.shape[0]), (0, 0)))


def _tile(n, want):
    t = min(n, want)
    while n % t:
        t -= 1
    return t


def kernel(x, c, norm_g, w_ada, b_ada, w_in, conv_a_w, conv_b_w, conv_b_b, ln_b_g, ln_b_b, w_out, final_g, loss_target, m_norm_g, m_w_ada, m_b_ada, m_w_in, m_conv_a_w, m_conv_b_w, m_conv_b_b, m_ln_b_g, m_ln_b_b, m_w_out, m_final_g, v_norm_g, v_w_ada, v_b_ada, v_w_in, v_conv_a_w, v_conv_b_w, v_conv_b_b, v_ln_b_g, v_ln_b_b, v_w_out, v_final_g):
    s, d = x.shape[1], x.shape[2]
    wa = conv_b_b.shape[-1]
    dmix = 2 * wa
    ns = w_in.shape[-1]
    din = N_CHIPS * ns
    r4 = w_out.shape[1]
    na = w_ada.shape[-1]
    wsh = conv_a_w.shape[-1]
    px, py, pc = _position()
    chip = 2 * px + py
    me = 4 * px + 2 * py + pc
    pos = jnp.stack([chip, pc]).astype(jnp.int32)
    x2d = x.reshape(s, d)
    target = loss_target.reshape(s, d)

    win_bf = _cast_bf16(w_in[0], _tile(d, 512), "cast_w_in")
    wout_bf = _cast_bf16(w_out[0], _tile(r4, 512), "cast_w_out")
    c8 = jnp.broadcast_to(c, (8, d))
    cw = jnp.concatenate([_pad_rows(conv_a_w[0], HALO_A), _pad_rows(conv_b_w[0], HALO_B)], axis=0)
    win_full, wout_full, c_all, cw_all = _gather_weights(win_bf, wout_bf, c8, cw)
    c_rows = c_all[:, 0, :]
    cw_full = jnp.transpose(cw_all, (1, 0, 2)).reshape(HALO_A + HALO_B, wa)
    conv_a_full, conv_b_full = cw_full[:HALO_A], cw_full[HALO_A:]

    b_ada_sh = lax.dynamic_slice(b_ada, (0, chip * na), (1, na))
    mod_part = _modulation(_pad_rows(c_rows, 2 * N_DEV), w_ada[0], b_ada_sh, _tile(na, 512), "modulation")[:N_DEV]
    mod_all = _exchange_mod(mod_part)
    mod = lax.dynamic_index_in_dim(mod_all, me, axis=1, keepdims=False).reshape(1, 3 * d)
    shift, scale, gate = mod[:, :d], mod[:, d:2 * d], mod[:, 2 * d:]

    h, r = _prenorm(x2d, norm_g, scale, shift, _tile(s, 256))
    bm = _tile(s, 1024)
    bn = _tile(ns, 896)
    nb = ns // bn
    proj = _matmul(
        h, win_full, grid=(s // bm, N_CHIPS * nb, 1),
        a_spec=pl.BlockSpec((bm, d), lambda i, j, k: (i, 0)),
        b_spec=pl.BlockSpec((None, d, bn), lambda i, j, k: (j // nb, 0, j % nb)),
        o_spec=pl.BlockSpec((bm, bn), lambda i, j, k: (i, j)),
        out_shape=jax.ShapeDtypeStruct((s, din), F32), dims=((1,), (0,)), name="proj")
    y = _mixer_a_fwd(proj, conv_a_full, wa, _tile(s, 512), _tile(wa, 512))
    u0, u = _mixer_b_conv_fwd(proj, conv_b_full, conv_b_b, wa, _tile(s, 512), _tile(wa, 256), 64)
    y = _mixer_b_gate_fwd(y, u, proj, ln_b_g, ln_b_b, wa, _tile(s, 256))
    wout2d = wout_full.reshape(dmix, d)
    bd = _tile(d, 1024)
    o = _matmul(
        y, wout2d, grid=(s // bm, d // bd, 1),
        a_spec=pl.BlockSpec((bm, dmix), lambda i, j, k: (i, 0)),
        b_spec=pl.BlockSpec((dmix, bd), lambda i, j, k: (0, j)),
        o_spec=pl.BlockSpec((bm, bd), lambda i, j, k: (i, j)),
        out_shape=jax.ShapeDtypeStruct((s, d), F32), dims=((1,), (0,)), name="out_proj")
    dx2, do, loss_p, gfg_p, dgate_p = _loss_head(x2d, o, target, gate, final_g.reshape(1, d), _tile(s, 128))

    be = _tile(dmix, 1024)
    dy = _matmul(
        do, wout2d, grid=(s // bm, dmix // be, 1),
        a_spec=pl.BlockSpec((bm, d), lambda i, j, k: (i, 0)),
        b_spec=pl.BlockSpec((be, d), lambda i, j, k: (j, 0)),
        o_spec=pl.BlockSpec((bm, be), lambda i, j, k: (i, j)),
        out_shape=jax.ShapeDtypeStruct((s, dmix), F32), dims=((1,), (1,)), name="dy")
    g_wout = _matmul(
        y, do, grid=(dmix // be, d // bd, 1),
        a_spec=pl.BlockSpec((s, be), lambda i, j, k: (0, i)),
        b_spec=pl.BlockSpec((s, bd), lambda i, j, k: (0, j)),
        o_spec=pl.BlockSpec((be, bd), lambda i, j, k: (i, j)),
        out_shape=jax.ShapeDtypeStruct((dmix, d), F32), dims=((0,), (0,)), name="grad_w_out")
    dproj, dwa_p = _mixer_a_bwd(proj, dy, conv_a_full, wa, din, _tile(s, 128))
    dproj, du, dlng_p, dlnb_p, dcb_p = _mixer_b_gate_bwd(dproj, dy, u, proj, ln_b_g, ln_b_b, wa, _tile(s, 128))
    dproj, dwb_p = _mixer_b_conv_bwd(dproj, du, u0, proj, conv_b_full, wa, _tile(s, 256), _tile(wa, 256), 64)
    bk = _tile(ns, 1792)
    nkb = ns // bk
    dh = _matmul(
        dproj, win_full, grid=(s // bm, d // bd, N_CHIPS * nkb),
        a_spec=pl.BlockSpec((bm, bk), lambda i, j, k: (i, k)),
        b_spec=pl.BlockSpec((None, bd, bk), lambda i, j, k: (k // nkb, j, k % nkb)),
        o_spec=pl.BlockSpec((bm, bd), lambda i, j, k: (i, j)),
        out_shape=jax.ShapeDtypeStruct((s, d), F32), dims=((1,), (1,)), name="dh")
    g_win = _matmul(
        h, dproj, grid=(d // bd, N_CHIPS * nb, 1),
        a_spec=pl.BlockSpec((s, bd), lambda i, j, k: (0, i)),
        b_spec=pl.BlockSpec((s, bn), lambda i, j, k: (0, j)),
        o_spec=pl.BlockSpec((None, bd, bn), lambda i, j, k: (j // nb, i, j % nb)),
        out_shape=jax.ShapeDtypeStruct((N_CHIPS, d, ns), F32), dims=((0,), (0,)), name="grad_w_in")
    grad_x, dshift_p, dscale_p, gng_p = _prenorm_bwd(x2d, r, dh, dx2, norm_g, scale, _tile(s, 128))

    def rows_of(v):
        return _pad_rows(v.reshape(-1, wa), 8 * ((v.size // wa + 7) // 8))

    dmod = jnp.concatenate([dshift_p, dscale_p, dgate_p], axis=1)
    parts = [gng_p, dmod, dwa_p, dwb_p, dcb_p, dlng_p, dlnb_p, gfg_p,
             jnp.broadcast_to(loss_p[:, :1], (1, wa))]
    starts, packed = [], []
    for p in parts:
        starts.append(sum(q.shape[0] for q in packed))
        packed.append(rows_of(p) if p.shape[0] == 1 else p)
    small_sum, small_all = _gather_small(jnp.concatenate(packed, axis=0))

    def summed(k, rows):
        return small_sum[starts[k]:starts[k] + rows]

    grad_norm_g = summed(0, d // wa).reshape(1, d)
    grad_b_ada = summed(1, 3 * d // wa).reshape(1, 3 * d)
    grad_conv_a_full = summed(2, TAPS_A)
    grad_conv_b_full = summed(3, TAPS_B)
    grad_conv_b_b = summed(4, 1)
    grad_ln_b_g = summed(5, 1)
    grad_ln_b_b = summed(6, 1)
    grad_final_g = summed(7, d // wa).reshape(d)
    loss = summed(8, 1)[0, 0]
    grad_conv_a_w = lax.dynamic_slice(grad_conv_a_full, (0, chip * wsh), (TAPS_A, wsh))
    grad_conv_b_w = lax.dynamic_slice(grad_conv_b_full, (0, chip * wsh), (TAPS_B, wsh))
    dmod_all = small_all[:, starts[1]:starts[1] + 3 * d // wa, :].reshape(N_DEV, 3 * d)
    dmod_sh = lax.dynamic_slice(dmod_all, (0, chip * na), (N_DEV, na))

    g_wout3 = g_wout.reshape(N_CHIPS, r4, d)
    ra_in, ra_out = _swap_halves(g_win, g_wout3)
    q_in = _chip_partial(pos, g_win, ra_in, _tile(d // 2, 256), "chip_partial_w_in")
    q_out = _chip_partial(pos, g_wout3, ra_out, _tile(r4 // 2, 256), "chip_partial_w_out")
    rb_in, rb_out = _send_chip_partials(q_in, q_out)
    gh_in = _final_half(pos, g_win, ra_in, rb_in, _tile(d // 2, 256), "final_half_w_in")
    gh_out = _final_half(pos, g_wout3, ra_out, rb_out, _tile(r4 // 2, 256), "final_half_w_out")
    grad_w_in, grad_w_out = _share_halves(gh_in, gh_out)

    d_win, nm_win, nv_win = _adam(w_in[0], grad_w_in, m_w_in[0], v_w_in[0], _tile(d, 128), "adam_w_in")
    d_wout, nm_wout, nv_wout = _adam(w_out[0], grad_w_out, m_w_out[0], v_w_out[0], _tile(r4, 128), "adam_w_out")
    grad_w_ada, d_wada, nm_wada, nv_wada = _adam_ada(c_rows.T, dmod_sh, w_ada[0], m_w_ada[0], v_w_ada[0],
                                                     _tile(d, 128), "adam_w_ada")

    def small_adam(w, g, m, v, name):
        shape = w.shape
        w2 = w.reshape(-1, shape[-1])
        out = _adam(w2, g.reshape(w2.shape), m.reshape(w2.shape), v.reshape(w2.shape), w2.shape[0], name)
        return [o_.reshape(shape) for o_ in out]

    small = {
        "norm_g": small_adam(norm_g, grad_norm_g, m_norm_g, v_norm_g, "adam_norm_g"),
        "b_ada": small_adam(b_ada, grad_b_ada, m_b_ada, v_b_ada, "adam_b_ada"),
        "conv_a_w": small_adam(conv_a_w, grad_conv_a_w, m_conv_a_w, v_conv_a_w, "adam_conv_a_w"),
        "conv_b_w": small_adam(conv_b_w, grad_conv_b_w, m_conv_b_w, v_conv_b_w, "adam_conv_b_w"),
        "conv_b_b": small_adam(conv_b_b, grad_conv_b_b, m_conv_b_b, v_conv_b_b, "adam_conv_b_b"),
        "ln_b_g": small_adam(ln_b_g, grad_ln_b_g, m_ln_b_g, v_ln_b_g, "adam_ln_b_g"),
        "ln_b_b": small_adam(ln_b_b, grad_ln_b_b, m_ln_b_b, v_ln_b_b, "adam_ln_b_b"),
        "final_g": small_adam(final_g.reshape(1, d), grad_final_g, m_final_g.reshape(1, d),
                              v_final_g.reshape(1, d), "adam_final_g"),
    }
    small["final_g"] = [o_.reshape(d) for o_ in small["final_g"]]
    big = {
        "w_ada": [a[None] for a in (d_wada, nm_wada, nv_wada)],
        "w_in": [a[None] for a in (d_win, nm_win, nv_win)],
        "w_out": [a[None] for a in (d_wout, nm_wout, nv_wout)],
    }
    upd = {**small, **big}
    order = ["norm_g", "w_ada", "b_ada", "w_in", "conv_a_w", "conv_b_w", "conv_b_b", "ln_b_g", "ln_b_b",
             "w_out", "final_g"]
    grads = {
        "norm_g": grad_norm_g, "w_ada": grad_w_ada[None], "b_ada": grad_b_ada, "w_in": grad_w_in[None],
        "conv_a_w": grad_conv_a_w[None], "conv_b_w": grad_conv_b_w[None], "conv_b_b": grad_conv_b_b,
        "ln_b_g": grad_ln_b_g, "ln_b_b": grad_ln_b_b, "w_out": grad_w_out[None], "final_g": grad_final_g,
    }
    return (loss, grad_x.reshape(1, s, d), *[grads[n] for n in order], *[upd[n][0] for n in order],
            *[upd[n][1] for n in order], *[upd[n][2] for n in order])
```

```python
import functools

import jax
import jax.numpy as jnp
from jax import lax
from jax.experimental import pallas as pl
from jax.experimental.pallas import tpu as pltpu

F32 = jnp.float32
BF16 = jnp.bfloat16
EPS = 1e-6
N_CHIPS = 4
N_DEV = 8
TAPS_A = 3
TAPS_B = 31
HALO_A = 8
HALO_B = 32
ADAM_LR = 0.001
ADAM_B1 = 0.9
ADAM_B2 = 0.999
ADAM_EPS = 1e-08
ADAM_WD = 0.01
ADAM_STEP = 10
VMEM_LIMIT = 56 * 1024 * 1024
MESH = pl.DeviceIdType.MESH
ANY = pl.BlockSpec(memory_space=pl.ANY)
VMEM = pl.BlockSpec(memory_space=pltpu.VMEM)
HBM_SPEC = pl.BlockSpec(memory_space=pltpu.HBM)
SEM_SPEC = pl.BlockSpec(memory_space=pltpu.SEMAPHORE)
EFFECT = pltpu.SideEffectType.DATAFLOW_SIDE_EFFECTING


def _params(sem=None):
    return pltpu.CompilerParams(dimension_semantics=sem, vmem_limit_bytes=VMEM_LIMIT)


def _sigmoid(v):
    return jax.nn.sigmoid(v)


def _position():
    return lax.axis_index("x"), lax.axis_index("y"), lax.axis_index("c")


def _rcopy(src, dst, ssem, rsem, dev):
    return pltpu.make_async_remote_copy(src_ref=src, dst_ref=dst, send_sem=ssem, recv_sem=rsem,
                                        device_id=dev, device_id_type=MESH)


def _other_chips(x, y):
    chips = [(1 - x, y), (x, 1 - y), (1 - x, 1 - y)]
    return chips, [2 * cx + cy for cx, cy in chips]


def _cast_bf16(a, rows, name):
    m, n = a.shape

    def body(a_ref, o_ref):
        o_ref[...] = a_ref[...].astype(BF16)

    return pl.pallas_call(
        body, name=name, grid=(m // rows,),
        in_specs=[pl.BlockSpec((rows, n), lambda i: (i, 0))],
        out_specs=pl.BlockSpec((rows, n), lambda i: (i, 0)),
        out_shape=jax.ShapeDtypeStruct((m, n), BF16),
        compiler_params=_params(("parallel",)),
    )(a)


def _cast_halves(a, rows, name):
    m, n = a.shape
    hc = n // 2

    def body(a_ref, o_ref):
        o_ref[...] = a_ref[...].astype(BF16)

    return pl.pallas_call(
        body, name=name, grid=(2, m // rows),
        in_specs=[pl.BlockSpec((rows, hc), lambda hf, i: (i, hf))],
        out_specs=pl.BlockSpec((None, rows, hc), lambda hf, i: (hf, i, 0)),
        out_shape=jax.ShapeDtypeStruct((2, m, hc), BF16),
        compiler_params=_params(("parallel", "parallel")),
    )(a)


def _proj_piece(colblk, h, w, proj, din, bm, bn, name, half=None):
    s, d = h.shape
    nj = w.shape[-1] // bn
    if half is None:
        w_spec = pl.BlockSpec((d, bn), lambda i, j, cb: (0, j))
    else:
        w_spec = pl.BlockSpec((None, d, bn), lambda i, j, cb: (half, 0, j))

    def body(cb_ref, h_ref, w_ref, *rest):
        rest[-1][...] = jnp.dot(h_ref[...], w_ref[...], preferred_element_type=F32)

    args, extra, alias = [colblk, h, w], [], {}
    if proj is not None:
        args, extra, alias = args + [proj], [ANY], {3: 0}
    return pl.pallas_call(
        body, name=name,
        grid_spec=pltpu.PrefetchScalarGridSpec(
            num_scalar_prefetch=1, grid=(s // bm, nj),
            in_specs=[pl.BlockSpec((bm, d), lambda i, j, cb: (i, 0)), w_spec] + extra,
            out_specs=pl.BlockSpec((bm, bn), lambda i, j, cb: (i, cb[0] + j))),
        out_shape=jax.ShapeDtypeStruct((s, din), F32), input_output_aliases=alias,
        compiler_params=_params(("parallel", "parallel")),
    )(*args)


def _grad_slot(slot, h, dproj, after, ns, bd, bn, name):
    s, d = h.shape
    nb = ns // bn

    def body(slot_ref, h_ref, dp_ref, *rest):
        rest[-1][...] = lax.dot_general(h_ref[...], dp_ref[...], (((0,), (0,)), ((), ())),
                                        preferred_element_type=F32)

    return pl.pallas_call(
        body, name=name,
        grid_spec=pltpu.PrefetchScalarGridSpec(
            num_scalar_prefetch=1, grid=(d // bd, nb),
            in_specs=[pl.BlockSpec((s, bd), lambda i, j, sl: (0, i)),
                      pl.BlockSpec((s, bn), lambda i, j, sl: (0, sl[0] * nb + j))] + [ANY] * len(after),
            out_specs=pl.BlockSpec((bd, bn), lambda i, j, sl: (i, j))),
        out_shape=jax.ShapeDtypeStruct((d, ns), F32),
        compiler_params=_params(("parallel", "parallel")),
    )(slot, h, dproj, *after)


def _matmul(a, b, *, grid, a_spec, b_spec, o_spec, out_shape, dims, name, after=()):
    nk = grid[2]
    n_after = len(after)

    def body(a_ref, b_ref, *rest):
        o_ref, acc = rest[n_after], rest[n_after + 1:]
        p = lax.dot_general(a_ref[...], b_ref[...], (dims, ((), ())), preferred_element_type=F32)
        if nk == 1:
            o_ref[...] = p.astype(o_ref.dtype)
        else:
            acc_ref, = acc
            k = pl.program_id(2)

            @pl.when(k == 0)
            def _():
                acc_ref[...] = p

            @pl.when(k > 0)
            def _():
                acc_ref[...] += p

            @pl.when(k == nk - 1)
            def _():
                o_ref[...] = acc_ref[...].astype(o_ref.dtype)

    block = [d for d in o_spec.block_shape if d is not None]
    scratch = [pltpu.VMEM(tuple(block), F32)] if nk > 1 else []
    return pl.pallas_call(
        body, name=name, grid=grid, in_specs=[a_spec, b_spec] + [ANY] * n_after, out_specs=o_spec,
        out_shape=out_shape, scratch_shapes=scratch,
        compiler_params=_params(("parallel", "parallel", "arbitrary")),
    )(a, b, *after)


def _adam_math(w, g, m, v):
    m = ADAM_B1 * m + (1.0 - ADAM_B1) * g
    v = ADAM_B2 * v + (1.0 - ADAM_B2) * (g * g)
    m_hat = m / (1.0 - ADAM_B1 ** ADAM_STEP)
    v_hat = v / (1.0 - ADAM_B2 ** ADAM_STEP)
    delta = -ADAM_LR * (m_hat / (jnp.sqrt(v_hat) + ADAM_EPS) + ADAM_WD * w)
    return delta, m, v


def _adam(w, g, m, v, rows, name):
    r, n = w.shape

    def body(w_ref, g_ref, m_ref, v_ref, d_ref, mo_ref, vo_ref):
        d, mo, vo = _adam_math(w_ref[...], g_ref[...], m_ref[...], v_ref[...])
        d_ref[...] = d
        mo_ref[...] = mo
        vo_ref[...] = vo

    spec = pl.BlockSpec((rows, n), lambda i: (i, 0))
    shape = jax.ShapeDtypeStruct((r, n), F32)
    return pl.pallas_call(
        body, name=name, grid=(r // rows,), in_specs=[spec] * 4, out_specs=[spec] * 3,
        out_shape=[shape] * 3, compiler_params=_params(("parallel",)),
    )(w, g, m, v)


def _adam_ada(c_cols, dmod, w, m, v, rows, name):
    r, n = w.shape

    def body(c_ref, dm_ref, w_ref, m_ref, v_ref, g_ref, d_ref, mo_ref, vo_ref):
        cv = c_ref[...]
        c_act = cv * _sigmoid(cv)
        g = c_act[:, 0:1] * dm_ref[0:1, :]
        for b in range(1, N_DEV):
            g = g + c_act[:, b:b + 1] * dm_ref[b:b + 1, :]
        d, mo, vo = _adam_math(w_ref[...], g, m_ref[...], v_ref[...])
        g_ref[...] = g
        d_ref[...] = d
        mo_ref[...] = mo
        vo_ref[...] = vo

    spec = pl.BlockSpec((rows, n), lambda i: (i, 0))
    shape = jax.ShapeDtypeStruct((r, n), F32)
    return pl.pallas_call(
        body, name=name, grid=(r // rows,),
        in_specs=[pl.BlockSpec((rows, N_DEV), lambda i: (i, 0)), pl.BlockSpec((N_DEV, n), lambda i: (0, 0)),
                  spec, spec, spec],
        out_specs=[spec] * 4, out_shape=[shape] * 4, compiler_params=_params(("parallel",)),
    )(c_cols, dmod, w, m, v)


def _start_copies(name, plan, n, bufs):
    nb = len(bufs)

    def body(*refs):
        sems = refs[nb:nb + 2 * n]
        for k, (src, dst, dev) in enumerate(plan(refs[:nb])):
            _rcopy(src, dst, sems[2 * k], sems[2 * k + 1], dev).start()
        refs[-1][...] = jnp.zeros((8, 128), F32)

    outs = pl.pallas_call(
        body, name=name,
        out_shape=[pltpu.SemaphoreType.DMA(())] * (2 * n) + [pltpu.HBM(a.shape, a.dtype) for a in bufs]
        + [jax.ShapeDtypeStruct((8, 128), F32)],
        in_specs=[HBM_SPEC] * nb, out_specs=[SEM_SPEC] * (2 * n) + [HBM_SPEC] * nb + [VMEM],
        input_output_aliases={i: 2 * n + i for i in range(nb)},
        compiler_params=pltpu.CompilerParams(has_side_effects=EFFECT),
    )(*[pltpu.with_memory_space_constraint(a, pltpu.HBM) for a in bufs])
    return list(outs[:2 * n]), list(outs[2 * n:2 * n + nb]), outs[-1]


def _wait_copies(name, plan, bufs, sems, after=(), send=True, recv=True):
    nb, nsem = len(bufs), len(sems)

    def body(*refs):
        s = refs[nb:nb + nsem]
        for k, (src, dst, dev) in enumerate(plan(refs[:nb])):
            cp = _rcopy(src, dst, s[2 * k], s[2 * k + 1], dev)
            if send:
                cp.wait_send()
            if recv:
                cp.wait_recv()

    outs = pl.pallas_call(
        body, name=name, out_shape=[pltpu.HBM(a.shape, a.dtype) for a in bufs],
        in_specs=[HBM_SPEC] * nb + [SEM_SPEC] * nsem + [ANY] * len(after), out_specs=[HBM_SPEC] * nb,
        input_output_aliases={i: i for i in range(nb)},
        compiler_params=pltpu.CompilerParams(has_side_effects=EFFECT),
    )(*bufs, *sems, *after)
    return list(outs)


def _to_sibling(views):
    def plan(b):
        x, y, c = _position()
        return [(view(b[2 * k], c), b[2 * k + 1], (x, y, 1 - c)) for k, view in enumerate(views)]
    return plan


def _to_chip(k):
    def plan(b):
        x, y, c = _position()
        cx, cy = _other_chips(x, y)[0][k]
        return [(b[0], b[1], (cx, cy, c))]
    return plan


def _slots_to_chips(b):
    x, y, c = _position()
    chips, cidx = _other_chips(x, y)
    return [(b[0].at[cidx[k]], b[1 + k], (cx, cy, c)) for k, (cx, cy) in enumerate(chips)]


def _halves_to_sibling(b):
    x, y, c = _position()
    views = [r.at[pl.ds(c * (r.shape[0] // 2), r.shape[0] // 2), :] for r in b]
    return [(v, v, (x, y, 1 - c)) for v in views]


def _landed(b):
    x, y, c = _position()
    return [(ref, ref, (x, y, c)) for ref in b]


def _assemble(name, pieces, out_shape, index_of):
    n = len(pieces)

    def body(*refs):
        out_ref, sem = refs[n], refs[n + 1]
        x, y, c = _position()
        _, cidx = _other_chips(x, y)
        cps = [pltpu.make_async_copy(refs[k], out_ref.at[index_of(k, 2 * x + y, c, cidx)], sem.at[k]) for k in range(n)]
        for cp in cps:
            cp.start()
        for cp in cps:
            cp.wait()

    return pl.pallas_call(
        body, name=name, in_specs=[ANY] * n, out_specs=ANY, out_shape=out_shape,
        scratch_shapes=[pltpu.SemaphoreType.DMA((n,))],
    )(*pieces)


def _gather_cond(c8, cw):
    def body(c8_ref, cw_ref, call_ref, cwall_ref, ssem, rsem, lsem):
        x, y, c = _position()
        chip = 2 * x + y
        me = 4 * x + 2 * y + c
        chips, cidx = _other_chips(x, y)
        own = [pltpu.make_async_copy(c8_ref, call_ref.at[me], lsem.at[0]),
               pltpu.make_async_copy(cw_ref, cwall_ref.at[chip], lsem.at[1])]
        for cp in own:
            cp.start()
        sends = [_rcopy(cw_ref, cwall_ref.at[chip], ssem.at[k], rsem.at[k], (cx, cy, c))
                 for k, (cx, cy) in enumerate(chips)]
        for mask in range(1, N_DEV):
            fx, fy, fc = (mask >> 2) & 1, (mask >> 1) & 1, mask & 1
            dev = (1 - x if fx else x, 1 - y if fy else y, 1 - c if fc else c)
            sends.append(_rcopy(c8_ref, call_ref.at[me], ssem.at[2 + mask], rsem.at[2 + mask], dev))
        for cp in sends:
            cp.start()
        for k in range(3):
            slot = cwall_ref.at[cidx[k]]
            _rcopy(slot, slot, ssem.at[k], rsem.at[k], (x, y, c)).wait_recv()
        for mask in range(1, N_DEV):
            slot = call_ref.at[jnp.bitwise_xor(me, mask)]
            _rcopy(slot, slot, ssem.at[2 + mask], rsem.at[2 + mask], (x, y, c)).wait_recv()
        for cp in sends:
            cp.wait_send()
        for cp in own:
            cp.wait()

    return pl.pallas_call(
        body, name="gather_cond", in_specs=[VMEM, VMEM], out_specs=[VMEM, VMEM],
        out_shape=[jax.ShapeDtypeStruct((N_DEV,) + c8.shape, F32), jax.ShapeDtypeStruct((N_CHIPS,) + cw.shape, F32)],
        scratch_shapes=[pltpu.SemaphoreType.DMA((10,)), pltpu.SemaphoreType.DMA((10,)), pltpu.SemaphoreType.DMA((2,))],
    )(c8, cw)


def _exchange_mod(mod_part):
    def body(mp_ref, out_ref, ssem, rsem, lsem):
        x, y, c = _position()
        chip = 2 * x + y
        chips, cidx = _other_chips(x, y)
        own = pltpu.make_async_copy(mp_ref, out_ref.at[chip], lsem)
        own.start()
        sends = [_rcopy(mp_ref, out_ref.at[chip], ssem.at[k], rsem.at[k], (cx, cy, c))
                 for k, (cx, cy) in enumerate(chips)]
        for cp in sends:
            cp.start()
        for k in range(3):
            slot = out_ref.at[cidx[k]]
            _rcopy(slot, slot, ssem.at[k], rsem.at[k], (x, y, c)).wait_recv()
        for cp in sends:
            cp.wait_send()
        own.wait()

    return pl.pallas_call(
        body, name="exchange_mod", in_specs=[VMEM], out_specs=VMEM,
        out_shape=jax.ShapeDtypeStruct((N_CHIPS,) + mod_part.shape, F32),
        scratch_shapes=[pltpu.SemaphoreType.DMA((3,)), pltpu.SemaphoreType.DMA((3,)), pltpu.SemaphoreType.DMA],
    )(mod_part)


def _gather_small(pack):
    rows, n = pack.shape

    def body(p_ref, sum_ref, all_ref, ssem, rsem, lsem):
        x, y, c = _position()
        me = 4 * x + 2 * y + c
        own = pltpu.make_async_copy(p_ref, all_ref.at[me], lsem)
        own.start()
        sends = []
        for mask in range(1, N_DEV):
            fx, fy, fc = (mask >> 2) & 1, (mask >> 1) & 1, mask & 1
            dev = (1 - x if fx else x, 1 - y if fy else y, 1 - c if fc else c)
            sends.append(_rcopy(p_ref, all_ref.at[me], ssem.at[mask - 1], rsem.at[mask - 1], dev))
        for cp in sends:
            cp.start()
        for mask in range(1, N_DEV):
            slot = all_ref.at[jnp.bitwise_xor(me, mask)]
            _rcopy(slot, slot, ssem.at[mask - 1], rsem.at[mask - 1], (x, y, c)).wait_recv()
        for cp in sends:
            cp.wait_send()
        own.wait()
        acc = all_ref[0]
        for k in range(1, N_DEV):
            acc = acc + all_ref[k]
        sum_ref[...] = acc

    return pl.pallas_call(
        body, name="gather_small", in_specs=[VMEM], out_specs=[VMEM, VMEM],
        out_shape=[jax.ShapeDtypeStruct((rows, n), F32), jax.ShapeDtypeStruct((N_DEV, rows, n), F32)],
        scratch_shapes=[pltpu.SemaphoreType.DMA((7,)), pltpu.SemaphoreType.DMA((7,)), pltpu.SemaphoreType.DMA],
        compiler_params=pltpu.CompilerParams(vmem_limit_bytes=VMEM_LIMIT),
    )(pack)


def _chip_partial(pos, g, recv, rows, name):
    ns, full, n = g.shape
    h = full // 2
    nb = h // rows

    def body(pos_ref, g_ref, r_ref, o_ref):
        o_ref[...] = (g_ref[...] + r_ref[...]).astype(BF16)

    return pl.pallas_call(
        body, name=name,
        grid_spec=pltpu.PrefetchScalarGridSpec(
            num_scalar_prefetch=1, grid=(ns, nb),
            in_specs=[pl.BlockSpec((None, rows, n), lambda s, i, p: (s, p[1] * nb + i, 0)),
                      pl.BlockSpec((None, rows, n), lambda s, i, p: (s, i, 0))],
            out_specs=pl.BlockSpec((None, rows, n), lambda s, i, p: (s, i, 0))),
        out_shape=jax.ShapeDtypeStruct((ns, h, n), BF16),
        compiler_params=_params(("parallel", "parallel")),
    )(pos, g, recv)


def _final_half(pos, g, recv_a, recv_b, rows, name):
    ns, full, n = g.shape
    h = full // 2
    nb = h // rows

    def body(pos_ref, g_ref, ra_ref, rb0_ref, rb1_ref, rb2_ref, o_ref):
        acc = g_ref[...] + ra_ref[...]
        for rb_ref in (rb0_ref, rb1_ref, rb2_ref):
            acc = acc + rb_ref[...].astype(F32)
        o_ref[...] = acc

    part = pl.BlockSpec((rows, n), lambda i, p: (i, 0))
    return pl.pallas_call(
        body, name=name,
        grid_spec=pltpu.PrefetchScalarGridSpec(
            num_scalar_prefetch=1, grid=(nb,),
            in_specs=[pl.BlockSpec((None, rows, n), lambda i, p: (p[0], p[1] * nb + i, 0)),
                      pl.BlockSpec((None, rows, n), lambda i, p: (p[0], i, 0)), part, part, part],
            out_specs=pl.BlockSpec((rows, n), lambda i, p: (p[1] * nb + i, 0))),
        out_shape=jax.ShapeDtypeStruct((full, n), F32),
        compiler_params=_params(("parallel",)),
    )(pos, g, recv_a, *recv_b)


def _modulation(c_rows, w_ada, b_ada, cols, name):
    d, n = w_ada.shape
    rows = c_rows.shape[0]

    def body(c_ref, w_ref, b_ref, o_ref):
        cv = c_ref[...]
        c_act = (cv * _sigmoid(cv)).astype(BF16)
        o_ref[...] = jnp.dot(c_act, w_ref[...].astype(BF16), preferred_element_type=F32) + b_ref[...]

    return pl.pallas_call(
        body, name=name, grid=(n // cols,),
        in_specs=[pl.BlockSpec((rows, d), lambda j: (0, 0)), pl.BlockSpec((d, cols), lambda j: (0, j)),
                  pl.BlockSpec((1, cols), lambda j: (0, j))],
        out_specs=pl.BlockSpec((rows, cols), lambda j: (0, j)),
        out_shape=jax.ShapeDtypeStruct((rows, n), F32),
        compiler_params=_params(("parallel",)),
    )(c_rows, w_ada, b_ada)


def _prenorm(x, norm_g, scale, shift, rows):
    s, d = x.shape

    def body(x_ref, g_ref, sc_ref, sh_ref, h_ref, r_ref):
        xv = x_ref[...]
        r = lax.rsqrt(jnp.mean(xv * xv, axis=-1, keepdims=True) + EPS)
        h = (xv * r * g_ref[...]) * (1.0 + sc_ref[...]) + sh_ref[...]
        h_ref[...] = h.astype(BF16)
        r_ref[...] = r

    vec = pl.BlockSpec((1, d), lambda i: (0, 0))
    return pl.pallas_call(
        body, name="prenorm", grid=(s // rows,),
        in_specs=[pl.BlockSpec((rows, d), lambda i: (i, 0)), vec, vec, vec],
        out_specs=[pl.BlockSpec((rows, d), lambda i: (i, 0)), pl.BlockSpec((rows, 1), lambda i: (i, 0))],
        out_shape=[jax.ShapeDtypeStruct((s, d), BF16), jax.ShapeDtypeStruct((s, 1), F32)],
        compiler_params=_params(("parallel",)),
    )(x, norm_g, scale, shift)


def _mixer_a_fwd(proj, conv_w, wa, rows, cols):
    s = proj.shape[0]
    ncb = wa // cols

    def body(ab_ref, ac_ref, ax_ref, az_ref, w_ref, y_ref, qbuf):
        t = pl.program_id(1)

        @pl.when(t == 0)
        def _():
            qbuf[0:HALO_A, :] = jnp.zeros((HALO_A, cols), F32)

        q = ac_ref[...] * ax_ref[...]
        qbuf[HALO_A:HALO_A + rows, :] = q
        conv = w_ref[2:3, :] * q
        for k in range(TAPS_A - 1):
            off = HALO_A - (TAPS_A - 1) + k
            conv = conv + w_ref[k:k + 1, :] * qbuf[off:off + rows, :]
        zv = az_ref[...]
        y_ref[...] = (ab_ref[...] * conv * (zv * _sigmoid(zv))).astype(BF16)
        qbuf[0:HALO_A, :] = qbuf[rows:rows + HALO_A, :]

    def sec(k):
        return pl.BlockSpec((rows, cols), lambda cb, t, k=k: (t, k * ncb + cb))

    return pl.pallas_call(
        body, name="mixer_a_fwd", grid=(ncb, s // rows),
        in_specs=[sec(0), sec(1), sec(2), sec(3), pl.BlockSpec((HALO_A, cols), lambda cb, t: (0, cb))],
        out_specs=pl.BlockSpec((rows, cols), lambda cb, t: (t, cb)),
        out_shape=jax.ShapeDtypeStruct((s, 2 * wa), BF16),
        scratch_shapes=[pltpu.VMEM((HALO_A + rows, cols), F32)],
        compiler_params=_params(("parallel", "arbitrary")),
    )(proj, proj, proj, proj, conv_w)


def _shifted_back(dst, src, lo, hi):
    for n in range(8):
        dst[n, lo:hi, :] = src[lo - n:hi - n, :]


def _shifted_fwd(dst, src, lo, hi):
    for n in range(8):
        dst[n, lo:hi, :] = src[lo + n:hi + n, :]


def _mixer_b_conv_fwd(proj, conv_w, conv_b, wa, rows, cols, chunk):
    s = proj.shape[0]
    wb = conv_w.shape[1]
    ncb = wb // cols
    sec0 = 4 * wa // cols

    def body(bv_ref, bg_ref, w_ref, b_ref, u0_ref, u_ref, ubuf, sh):
        t = pl.program_id(1)

        @pl.when(t == 0)
        def _():
            ubuf[0:HALO_B, :] = jnp.zeros((HALO_B, cols), F32)

        u0 = bv_ref[...] * _sigmoid(bg_ref[...])
        u0_ref[...] = u0
        ubuf[HALO_B:HALO_B + rows, :] = u0
        _shifted_back(sh, ubuf, 8, HALO_B + rows)

        def row_chunk(rc, carry):
            base = pl.multiple_of(rc * chunk, chunk)
            acc = jnp.zeros((chunk, cols), F32)
            for k in range(TAPS_B):
                mq, n = divmod(TAPS_B - 1 - k, 8)
                acc = acc + w_ref[k:k + 1, :] * sh[n, pl.ds(HALO_B - 8 * mq + base, chunk), :]
            u_ref[pl.ds(base, chunk), :] = acc + b_ref[...]
            return carry

        lax.fori_loop(0, rows // chunk, row_chunk, 0)
        ubuf[0:HALO_B, :] = ubuf[rows:rows + HALO_B, :]

    return pl.pallas_call(
        body, name="mixer_b_conv_fwd", grid=(ncb, s // rows),
        in_specs=[pl.BlockSpec((rows, cols), lambda cb, t: (t, sec0 + cb)),
                  pl.BlockSpec((rows, cols), lambda cb, t: (t, sec0 + ncb + cb)),
                  pl.BlockSpec((HALO_B, cols), lambda cb, t: (0, cb)),
                  pl.BlockSpec((1, cols), lambda cb, t: (0, cb))],
        out_specs=[pl.BlockSpec((rows, cols), lambda cb, t: (t, cb))] * 2,
        out_shape=[jax.ShapeDtypeStruct((s, wb), F32)] * 2,
        scratch_shapes=[pltpu.VMEM((HALO_B + rows, cols), F32), pltpu.VMEM((8, HALO_B + rows, cols), F32)],
        compiler_params=_params(("parallel", "arbitrary")),
    )(proj, proj, conv_w, conv_b)


def _layernorm_stats(u):
    mu = jnp.mean(u, axis=-1, keepdims=True)
    xc = u - mu
    var = jnp.mean(xc * xc, axis=-1, keepdims=True)
    return xc * lax.rsqrt(var + EPS), lax.rsqrt(var + EPS)


def _mixer_b_gate_fwd(y, u, proj, ln_g, ln_b, wa, rows):
    s, wb = u.shape
    sec_z = (4 * wa + 2 * wb) // wb

    def body(y_in, u_ref, bz_ref, g_ref, b_ref, y_ref):
        uh, _ = _layernorm_stats(u_ref[...])
        ln = uh * g_ref[...] + b_ref[...]
        zv = bz_ref[...]
        y_ref[...] = ((ln * _sigmoid(ln)) * (zv * _sigmoid(zv))).astype(BF16)

    vec = pl.BlockSpec((1, wb), lambda i: (0, 0))
    return pl.pallas_call(
        body, name="mixer_b_gate_fwd", grid=(s // rows,),
        in_specs=[ANY, pl.BlockSpec((rows, wb), lambda i: (i, 0)), pl.BlockSpec((rows, wb), lambda i: (i, sec_z)),
                  vec, vec],
        out_specs=pl.BlockSpec((rows, wb), lambda i: (i, wa // wb)),
        out_shape=jax.ShapeDtypeStruct(y.shape, BF16), input_output_aliases={0: 0},
        compiler_params=_params(("parallel",)),
    )(y, u, proj, ln_g, ln_b)


def _loss_head(x, o, target, gate, final_g, rows):
    s, d = x.shape

    def body(x_ref, o_ref, t_ref, gate_ref, fg_ref, dx2_ref, do_ref, loss_ref, gfg_ref, dgate_ref):
        i = pl.program_id(0)
        ov = o_ref[...]
        x2 = x_ref[...] + gate_ref[...] * ov
        r2 = lax.rsqrt(jnp.mean(x2 * x2, axis=-1, keepdims=True) + EPS)
        xn2 = x2 * r2
        diff = xn2 * fg_ref[...] - t_ref[...]
        dout = diff * (1.0 / d)
        dxn2 = dout * fg_ref[...]
        dx2 = r2 * (dxn2 - xn2 * jnp.mean(dxn2 * xn2, axis=-1, keepdims=True))
        dx2_ref[...] = dx2
        do_ref[...] = (gate_ref[...] * dx2).astype(BF16)
        loss_part = 0.5 * jnp.sum(jnp.mean(diff * diff, axis=-1, keepdims=True), axis=0, keepdims=True)
        gfg_part = jnp.sum(dout * xn2, axis=0, keepdims=True)
        dgate_part = jnp.sum(dx2 * ov, axis=0, keepdims=True)

        @pl.when(i == 0)
        def _():
            loss_ref[...] = jnp.zeros_like(loss_ref)
            gfg_ref[...] = jnp.zeros_like(gfg_ref)
            dgate_ref[...] = jnp.zeros_like(dgate_ref)

        loss_ref[...] += jnp.broadcast_to(loss_part, loss_ref.shape)
        gfg_ref[...] += gfg_part
        dgate_ref[...] += dgate_part

    blk = pl.BlockSpec((rows, d), lambda i: (i, 0))
    vec = pl.BlockSpec((1, d), lambda i: (0, 0))
    return pl.pallas_call(
        body, name="loss_head", grid=(s // rows,),
        in_specs=[blk, blk, blk, vec, vec],
        out_specs=[blk, blk, pl.BlockSpec((1, 128), lambda i: (0, 0)), vec, vec],
        out_shape=[jax.ShapeDtypeStruct((s, d), F32), jax.ShapeDtypeStruct((s, d), BF16),
                   jax.ShapeDtypeStruct((1, 128), F32), jax.ShapeDtypeStruct((1, d), F32),
                   jax.ShapeDtypeStruct((1, d), F32)],
        compiler_params=_params(("arbitrary",)),
    )(x, o, target, gate, final_g)


def _mixer_a_bwd(proj, dy, conv_w, wa, din, rows):
    s = proj.shape[0]
    nt = s // rows
    per8 = rows // HALO_A

    def body(ab_ref, ac_ref, ax_ref, az_ref, hc_ref, hx_ref, dy_ref, w_ref, dp_ref, dw_ref, qbuf, dbuf):
        i = pl.program_id(0)

        @pl.when(i == 0)
        def _():
            dbuf[rows:rows + HALO_A, :] = jnp.zeros((HALO_A, wa), F32)
            dw_ref[...] = jnp.zeros_like(dw_ref)

        keep = jnp.where(i == nt - 1, 0.0, 1.0)
        qbuf[0:HALO_A, :] = hc_ref[...] * hx_ref[...] * keep
        acv, axv = ac_ref[...], ax_ref[...]
        q = acv * axv
        qbuf[HALO_A:HALO_A + rows, :] = q
        conv = w_ref[2:3, :] * q
        for k in range(TAPS_A - 1):
            off = HALO_A - (TAPS_A - 1) + k
            conv = conv + w_ref[k:k + 1, :] * qbuf[off:off + rows, :]
        zv, abv, dyv = az_ref[...], ab_ref[...], dy_ref[...]
        sg = _sigmoid(zv)
        sz = zv * sg
        dp_ref[:, 0:wa] = (dyv * conv * sz).astype(BF16)
        dp_ref[:, 3 * wa:4 * wa] = (dyv * abv * conv * (sg * (1.0 + zv * (1.0 - sg)))).astype(BF16)
        dconv = dyv * abv * sz
        dbuf[0:rows, :] = dconv
        dq = w_ref[2:3, :] * dconv
        for k in range(TAPS_A - 1):
            off = TAPS_A - 1 - k
            dq = dq + w_ref[k:k + 1, :] * dbuf[off:off + rows, :]
        dp_ref[:, wa:2 * wa] = (dq * axv).astype(BF16)
        dp_ref[:, 2 * wa:3 * wa] = (dq * acv).astype(BF16)
        for k in range(TAPS_A):
            off = HALO_A - (TAPS_A - 1) + k
            dw_ref[k:k + 1, :] += jnp.sum(dconv * qbuf[off:off + rows, :], axis=0, keepdims=True)
        dbuf[rows:rows + HALO_A, :] = dbuf[0:HALO_A, :]

    def sec(k):
        return pl.BlockSpec((rows, wa), lambda i, k=k: (nt - 1 - i, k))

    def halo(k):
        return pl.BlockSpec((HALO_A, wa), lambda i, k=k: (jnp.maximum((nt - 1 - i) * per8 - 1, 0), k))

    return pl.pallas_call(
        body, name="mixer_a_bwd", grid=(nt,),
        in_specs=[sec(0), sec(1), sec(2), sec(3), halo(1), halo(2),
                  pl.BlockSpec((rows, wa), lambda i: (nt - 1 - i, 0)),
                  pl.BlockSpec((HALO_A, wa), lambda i: (0, 0))],
        out_specs=[pl.BlockSpec((rows, 4 * wa), lambda i: (nt - 1 - i, 0)),
                   pl.BlockSpec((HALO_A, wa), lambda i: (0, 0))],
        out_shape=[jax.ShapeDtypeStruct((s, din), BF16), jax.ShapeDtypeStruct((HALO_A, wa), F32)],
        scratch_shapes=[pltpu.VMEM((HALO_A + rows, wa), F32), pltpu.VMEM((rows + HALO_A, wa), F32)],
        compiler_params=_params(("arbitrary",)),
    )(proj, proj, proj, proj, proj, proj, dy, conv_w)


def _mixer_b_gate_bwd(dproj, dy, u, proj, ln_g, ln_b, wa, rows):
    s, wb = u.shape
    sec_z = (4 * wa + 2 * wb) // wb

    def body(dp_in, dy_ref, u_ref, bz_ref, g_ref, b_ref, dp_ref, du_ref, dg_ref, db_ref, dcb_ref):
        i = pl.program_id(0)
        uh, rs = _layernorm_stats(u_ref[...])
        ln = uh * g_ref[...] + b_ref[...]
        sl = _sigmoid(ln)
        zv = bz_ref[...]
        sg = _sigmoid(zv)
        dyv = dy_ref[...]
        dp_ref[...] = (dyv * (ln * sl) * (sg * (1.0 + zv * (1.0 - sg)))).astype(BF16)
        dln = dyv * (zv * sg) * (sl * (1.0 + ln * (1.0 - sl)))
        duh = dln * g_ref[...]
        du = rs * (duh - jnp.mean(duh, axis=-1, keepdims=True) - uh * jnp.mean(duh * uh, axis=-1, keepdims=True))
        du_ref[...] = du

        @pl.when(i == 0)
        def _():
            dg_ref[...] = jnp.zeros_like(dg_ref)
            db_ref[...] = jnp.zeros_like(db_ref)
            dcb_ref[...] = jnp.zeros_like(dcb_ref)

        dg_ref[...] += jnp.sum(dln * uh, axis=0, keepdims=True)
        db_ref[...] += jnp.sum(dln, axis=0, keepdims=True)
        dcb_ref[...] += jnp.sum(du, axis=0, keepdims=True)

    blk = pl.BlockSpec((rows, wb), lambda i: (i, 0))
    vec = pl.BlockSpec((1, wb), lambda i: (0, 0))
    vshape = jax.ShapeDtypeStruct((1, wb), F32)
    return pl.pallas_call(
        body, name="mixer_b_gate_bwd", grid=(s // rows,),
        in_specs=[ANY, pl.BlockSpec((rows, wb), lambda i: (i, wa // wb)), blk,
                  pl.BlockSpec((rows, wb), lambda i: (i, sec_z)), vec, vec],
        out_specs=[pl.BlockSpec((rows, wb), lambda i: (i, sec_z)), blk, vec, vec, vec],
        out_shape=[jax.ShapeDtypeStruct(dproj.shape, BF16), jax.ShapeDtypeStruct((s, wb), F32), vshape, vshape, vshape],
        input_output_aliases={0: 0},
        compiler_params=_params(("arbitrary",)),
    )(dproj, dy, u, proj, ln_g, ln_b)


def _mixer_b_conv_bwd(dproj, du, u0, proj, conv_w, wa, rows, lanes, chunk):
    s, wb = du.shape
    nt = s // rows
    per32 = rows // HALO_B
    sec_v = 4 * wa // wb
    nlc = wb // lanes
    nrc = rows // chunk

    def body(dp_in, du_ref, u0_ref, h0_ref, bv_ref, bg_ref, w_ref, dp_ref, dw_ref, ubuf, dbuf, sh, shf, carry):
        i = pl.program_id(0)

        @pl.when(i == 0)
        def _():
            carry[...] = jnp.zeros_like(carry)
            dw_ref[...] = jnp.zeros_like(dw_ref)

        keep = jnp.where(i == nt - 1, 0.0, 1.0)
        for lc in range(nlc):
            cs = slice(lc * lanes, (lc + 1) * lanes)
            ubuf[0:HALO_B, :] = h0_ref[:, cs] * keep
            ubuf[HALO_B:HALO_B + rows, :] = u0_ref[:, cs]
            dbuf[0:rows, :] = du_ref[:, cs]
            dbuf[rows:rows + HALO_B, :] = carry[:, cs]
            _shifted_back(sh, ubuf, 8, HALO_B + rows)
            _shifted_fwd(shf, dbuf, 0, rows + HALO_B - 8)

            def row_chunk(rc, c0):
                base = pl.multiple_of(rc * chunk, chunk)
                acc = jnp.zeros((chunk, lanes), F32)
                for k in range(TAPS_B):
                    mq, n = divmod(TAPS_B - 1 - k, 8)
                    acc = acc + w_ref[k:k + 1, cs] * shf[n, pl.ds(base + 8 * mq, chunk), :]
                sg = _sigmoid(bg_ref[pl.ds(base, chunk), cs])
                bv = bv_ref[pl.ds(base, chunk), cs]
                dp_ref[pl.ds(base, chunk), cs] = (acc * sg).astype(BF16)
                dp_ref[pl.ds(base, chunk), lc * lanes + wb:(lc + 1) * lanes + wb] = (
                    acc * bv * sg * (1.0 - sg)).astype(BF16)
                return c0

            lax.fori_loop(0, nrc, row_chunk, 0)

            for k in range(TAPS_B):
                mq, n = divmod(TAPS_B - 1 - k, 8)

                def tap_rows(rc, acc, mq=mq, n=n):
                    base = pl.multiple_of(rc * chunk, chunk)
                    prod = dbuf[pl.ds(base, chunk), :] * sh[n, pl.ds(HALO_B - 8 * mq + base, chunk), :]
                    return acc + jnp.sum(prod.reshape(chunk // 8, 8, lanes), axis=0)

                part = lax.fori_loop(0, nrc, tap_rows, jnp.zeros((8, lanes), F32))
                dw_ref[k:k + 1, cs] += jnp.sum(part, axis=0, keepdims=True)
            carry[:, cs] = dbuf[0:HALO_B, :]

    def rev(cols_blk):
        return pl.BlockSpec((rows, wb), lambda i, cb=cols_blk: (nt - 1 - i, cb))

    return pl.pallas_call(
        body, name="mixer_b_conv_bwd", grid=(nt,),
        in_specs=[ANY, rev(0), rev(0),
                  pl.BlockSpec((HALO_B, wb), lambda i: (jnp.maximum((nt - 1 - i) * per32 - 1, 0), 0)),
                  rev(sec_v), rev(sec_v + 1), pl.BlockSpec((HALO_B, wb), lambda i: (0, 0))],
        out_specs=[pl.BlockSpec((rows, 2 * wb), lambda i: (nt - 1 - i, sec_v // 2)),
                   pl.BlockSpec((HALO_B, wb), lambda i: (0, 0))],
        out_shape=[jax.ShapeDtypeStruct(dproj.shape, BF16), jax.ShapeDtypeStruct((HALO_B, wb), F32)],
        input_output_aliases={0: 0},
        scratch_shapes=[pltpu.VMEM((HALO_B + rows, lanes), F32), pltpu.VMEM((rows + HALO_B, lanes), F32),
                        pltpu.VMEM((8, HALO_B + rows, lanes), F32), pltpu.VMEM((8, rows + HALO_B, lanes), F32),
                        pltpu.VMEM((HALO_B, wb), F32)],
        compiler_params=_params(("arbitrary",)),
    )(dproj, du, u0, u0, proj, proj, conv_w)


def _prenorm_bwd(x, r, dh, dx2, norm_g, scale, rows):
    s, d = x.shape

    def body(x_ref, r_ref, dh_ref, dx2_ref, g_ref, sc_ref, gx_ref, dsh_ref, dsc_ref, dg_ref):
        i = pl.program_id(0)
        rv = r_ref[...]
        xn = x_ref[...] * rv
        dhv = dh_ref[...]
        one_sc = 1.0 + sc_ref[...]
        dxn = dhv * one_sc * g_ref[...]
        gx_ref[...] = dx2_ref[...] + rv * (dxn - xn * jnp.mean(dxn * xn, axis=-1, keepdims=True))

        @pl.when(i == 0)
        def _():
            dsh_ref[...] = jnp.zeros_like(dsh_ref)
            dsc_ref[...] = jnp.zeros_like(dsc_ref)
            dg_ref[...] = jnp.zeros_like(dg_ref)

        dsh_ref[...] += jnp.sum(dhv, axis=0, keepdims=True)
        dsc_ref[...] += jnp.sum(dhv * (xn * g_ref[...]), axis=0, keepdims=True)
        dg_ref[...] += jnp.sum(dhv * one_sc * xn, axis=0, keepdims=True)

    blk = pl.BlockSpec((rows, d), lambda i: (i, 0))
    vec = pl.BlockSpec((1, d), lambda i: (0, 0))
    vshape = jax.ShapeDtypeStruct((1, d), F32)
    return pl.pallas_call(
        body, name="prenorm_bwd", grid=(s // rows,),
        in_specs=[blk, pl.BlockSpec((rows, 1), lambda i: (i, 0)), blk, blk, vec, vec],
        out_specs=[blk, vec, vec, vec],
        out_shape=[jax.ShapeDtypeStruct((s, d), F32), vshape, vshape, vshape],
        compiler_params=_params(("arbitrary",)),
    )(x, r, dh, dx2, norm_g, scale)


def _pad_rows(a, rows):
    return jnp.pad(a, ((0, rows - a.shape[0]), (0, 0)))


def _tile(n, want):
    t = min(n, want)
    while n % t:
        t -= 1
    return t


def kernel(x, c, norm_g, w_ada, b_ada, w_in, conv_a_w, conv_b_w, conv_b_b, ln_b_g, ln_b_b, w_out, final_g, loss_target, m_norm_g, m_w_ada, m_b_ada, m_w_in, m_conv_a_w, m_conv_b_w, m_conv_b_b, m_ln_b_g, m_ln_b_b, m_w_out, m_final_g, v_norm_g, v_w_ada, v_b_ada, v_w_in, v_conv_a_w, v_conv_b_w, v_conv_b_b, v_ln_b_g, v_ln_b_b, v_w_out, v_final_g):
    s, d = x.shape[1], x.shape[2]
    wa = conv_b_b.shape[-1]
    dmix = 2 * wa
    ns = w_in.shape[-1]
    din = N_CHIPS * ns
    r4 = w_out.shape[1]
    na = w_ada.shape[-1]
    wsh = conv_a_w.shape[-1]
    px, py, pc = _position()
    chip = 2 * px + py
    me = 4 * px + 2 * py + pc
    pos = jnp.stack([chip, pc]).astype(jnp.int32)
    x2d = x.reshape(s, d)
    target = loss_target.reshape(s, d)

    hc, ho, hrow = ns // 2, r4 // 2, d // 2
    _, cidx = _other_chips(px, py)
    win2 = _cast_halves(w_in[0], _tile(d, 512), "cast_w_in")
    wout_bf = _cast_bf16(w_out[0], _tile(r4, 512), "cast_w_out")

    def gather_plan(b):
        x, y, cc = _position()
        chips, _ = _other_chips(x, y)
        return ([(b[0].at[cc], b[2 + k], (cx, cy, cc)) for k, (cx, cy) in enumerate(chips)]
                + [(b[1].at[pl.ds(cc * ho, ho), :], b[5 + k], (cx, cy, cc)) for k, (cx, cy) in enumerate(chips)])

    def gather_sent(b):
        return [(src, src, dev) for src, _, dev in gather_plan(list(b) + [None] * 6)]

    g_sems, g_bufs, g_tok = _start_copies(
        "gather_start", gather_plan, 6,
        [win2, wout_bf] + [lax.empty((d, hc), BF16) for _ in range(3)] + [lax.empty((ho, d), BF16) for _ in range(3)])
    win2, wout_bf, li, lo = g_bufs[0], g_bufs[1], g_bufs[2:5], g_bufs[5:8]
    whole = _to_sibling([lambda ref, cc: ref])

    c8 = jnp.broadcast_to(c, (8, d)) + g_tok[0, 0]
    cw = jnp.concatenate([_pad_rows(conv_a_w[0], HALO_A), _pad_rows(conv_b_w[0], HALO_B)], axis=0)
    c_all, cw_all = _gather_cond(c8, cw)
    c_rows = c_all[:, 0, :]
    cw_full = jnp.transpose(cw_all, (1, 0, 2)).reshape(HALO_A + HALO_B, wa)
    conv_a_full, conv_b_full = cw_full[:HALO_A], cw_full[HALO_A:]

    b_ada_sh = lax.dynamic_slice(b_ada, (0, chip * na), (1, na))
    mod_part = _modulation(_pad_rows(c_rows, 2 * N_DEV), w_ada[0], b_ada_sh, _tile(na, 512), "modulation")[:N_DEV]
    mod_all = _exchange_mod(mod_part)
    mod = lax.dynamic_index_in_dim(mod_all, me, axis=1, keepdims=False).reshape(1, 3 * d)
    shift, scale, gate = mod[:, :d], mod[:, d:2 * d], mod[:, 2 * d:]

    h, r = _prenorm(x2d, norm_g, scale, shift, _tile(s, 256))
    bm = _tile(s, 1024)
    bn = _tile(hc, 896)
    nh = hc // bn

    def colblk(slot, half):
        return jnp.reshape((2 * slot + half) * nh, (1,)).astype(jnp.int32)

    def piece(slot, half, w, proj_in, name, which=None):
        return _proj_piece(colblk(slot, half), h, w, proj_in, din, bm, bn, name, half=which)

    def arrive(k, after):
        got, = _wait_copies(f"gather_wait{k}", _landed, [li[k]], g_sems[2 * k:2 * k + 2], after=after, send=False)
        sems, (got, passed), _ = _start_copies(f"pass_on{k}", whole, 1, [got, lax.empty((d, hc), BF16)])
        return got, passed, sems

    proj = piece(chip, 0, win2, None, "proj_own0", which=0)
    proj = piece(chip, 1, win2, proj, "proj_own1", which=1)
    li0, ld0, f0 = arrive(0, [proj])
    proj = piece(cidx[0], pc, li0, proj, "proj_0a")
    li1, ld1, f1 = arrive(1, [proj])
    proj = piece(cidx[1], pc, li1, proj, "proj_1a")
    li0, ld0 = _wait_copies("pass_wait0", whole, [li0, ld0], f0, after=[proj])
    proj = piece(cidx[0], 1 - pc, ld0, proj, "proj_0b")
    li1, ld1 = _wait_copies("pass_wait1", whole, [li1, ld1], f1, after=[proj])
    proj = piece(cidx[1], 1 - pc, ld1, proj, "proj_1b")
    li2, ld2, f2 = arrive(2, [proj])
    proj = piece(cidx[2], pc, li2, proj, "proj_2a")
    li2, ld2 = _wait_copies("pass_wait2", whole, [li2, ld2], f2, after=[proj])
    proj = piece(cidx[2], 1 - pc, ld2, proj, "proj_2b")

    lo = _wait_copies("gather_wait_out", _landed, lo, g_sems[6:12], after=[proj], send=False)
    o_sems, o_bufs, o_tok = _start_copies(
        "pass_on_out", _to_sibling([lambda ref, cc: ref] * 3), 3,
        [b for k in range(3) for b in (lo[k], lax.empty((ho, d), BF16))])
    win2, wout_bf = _wait_copies("gather_wait_sent", gather_sent, [win2, wout_bf], g_sems, after=[o_tok], recv=False)

    def slot_index(k, chip_, cc, others):
        if k == 0:
            return pl.ds(2 * chip_, 2)
        return 2 * others[(k - 1) % 3] + (cc if k <= 3 else 1 - cc)

    win_full = _assemble("assemble_w_in", [win2, li0, li1, li2, ld0, ld1, ld2],
                         jax.ShapeDtypeStruct((2 * N_CHIPS, d, hc), BF16), slot_index)
    y = _mixer_a_fwd(proj, conv_a_full, wa, _tile(s, 512), _tile(wa, 512))
    u0, u = _mixer_b_conv_fwd(proj, conv_b_full, conv_b_b, wa, _tile(s, 512), _tile(wa, 256), 64)
    y = _mixer_b_gate_fwd(y, u, proj, ln_b_g, ln_b_b, wa, _tile(s, 256))
    o_bufs = _wait_copies("pass_wait_out", _to_sibling([lambda ref, cc: ref] * 3), o_bufs, o_sems, after=[y])
    wout_full = _assemble("assemble_w_out", [wout_bf.reshape(2, ho, d)] + o_bufs[0::2] + o_bufs[1::2],
                          jax.ShapeDtypeStruct((2 * N_CHIPS, ho, d), BF16), slot_index)
    wout2d = wout_full.reshape(dmix, d)
    bd = _tile(d, 1024)
    o = _matmul(
        y, wout2d, grid=(s // bm, d // bd, 1),
        a_spec=pl.BlockSpec((bm, dmix), lambda i, j, k: (i, 0)),
        b_spec=pl.BlockSpec((dmix, bd), lambda i, j, k: (0, j)),
        o_spec=pl.BlockSpec((bm, bd), lambda i, j, k: (i, j)),
        out_shape=jax.ShapeDtypeStruct((s, d), F32), dims=((1,), (0,)), name="out_proj")
    dx2, do, loss_p, gfg_p, dgate_p = _loss_head(x2d, o, target, gate, final_g.reshape(1, d), _tile(s, 128))

    be = _tile(dmix, 1024)
    g_wout = _matmul(
        y, do, grid=(dmix // be, d // bd, 1),
        a_spec=pl.BlockSpec((s, be), lambda i, j, k: (0, i)),
        b_spec=pl.BlockSpec((s, bd), lambda i, j, k: (0, j)),
        o_spec=pl.BlockSpec((be, bd), lambda i, j, k: (i, j)),
        out_shape=jax.ShapeDtypeStruct((dmix, d), F32), dims=((0,), (0,)), name="grad_w_out")
    swap_out = _to_sibling([lambda ref, cc: ref.at[:, pl.ds((1 - cc) * ho, ho), :]])
    so_sems, (g_wout3, ra_out), so_tok = _start_copies(
        "swap_out_start", swap_out, 1, [g_wout.reshape(N_CHIPS, r4, d), lax.empty((N_CHIPS, ho, d), F32)])
    dy = _matmul(
        do, wout2d, grid=(s // bm, dmix // be, 1),
        a_spec=pl.BlockSpec((bm, d), lambda i, j, k: (i, 0)),
        b_spec=pl.BlockSpec((be, d), lambda i, j, k: (j, 0)),
        o_spec=pl.BlockSpec((bm, be), lambda i, j, k: (i, j)),
        out_shape=jax.ShapeDtypeStruct((s, dmix), F32), dims=((1,), (1,)), name="dy", after=[so_tok])
    g_wout3, ra_out = _wait_copies("swap_out_wait", swap_out, [g_wout3, ra_out], so_sems, after=[dy])
    q_out = _chip_partial(pos, g_wout3, ra_out, _tile(ho, 256), "chip_partial_w_out")
    po_sems, po_bufs, po_tok = _start_copies(
        "send_out_start", _slots_to_chips, 3, [q_out] + [lax.empty((ho, d), BF16) for _ in range(3)])
    dproj, dwa_p = _mixer_a_bwd(proj, dy, conv_a_full + po_tok[0, 0], wa, din, _tile(s, 128))
    dproj, du, dlng_p, dlnb_p, dcb_p = _mixer_b_gate_bwd(dproj, dy, u, proj, ln_b_g, ln_b_b, wa, _tile(s, 128))
    dproj, dwb_p = _mixer_b_conv_bwd(dproj, du, u0, proj, conv_b_full, wa, _tile(s, 256), _tile(wa, 256), 64)

    bn2 = _tile(ns, 896)
    swap_in = _to_sibling([lambda ref, cc: ref.at[pl.ds((1 - cc) * hrow, hrow), :]])
    slots = [cidx[0], cidx[1], cidx[2], chip]
    core_only = jnp.stack([0 * pc, pc]).astype(jnp.int32)
    g, ra, sw, q, rb, snd = [None] * 4, [None] * 4, [None] * 4, [None] * 3, [None] * 3, [None] * 3
    after = []
    for k in range(4):
        g[k] = _grad_slot(jnp.reshape(slots[k], (1,)).astype(jnp.int32), h, dproj, after, ns, bd, bn2,
                          f"grad_w_in{k}")
        sw[k], (g[k], ra[k]), tok = _start_copies(f"swap_in_start{k}", swap_in, 1,
                                                  [g[k], lax.empty((hrow, ns), F32)])
        after = [tok]
        if k >= 1:
            j = k - 1
            g[j], ra[j] = _wait_copies(f"swap_in_wait{j}", swap_in, [g[j], ra[j]], sw[j], after=[g[k]])
            part = _chip_partial(core_only, g[j][None], ra[j][None], _tile(hrow, 256), f"chip_partial_w_in{j}")
            snd[j], (q[j], rb[j]), tok2 = _start_copies(f"send_in_start{j}", _to_chip(j), 1,
                                                        [part.reshape(hrow, ns), lax.empty((hrow, ns), BF16)])
            after = [tok, tok2]
    dh = _matmul(
        dproj, win_full, grid=(s // bm, d // bd, 2 * N_CHIPS),
        a_spec=pl.BlockSpec((bm, hc), lambda i, j, k: (i, k)),
        b_spec=pl.BlockSpec((None, bd, hc), lambda i, j, k: (k, j, 0)),
        o_spec=pl.BlockSpec((bm, bd), lambda i, j, k: (i, j)),
        out_shape=jax.ShapeDtypeStruct((s, d), F32), dims=((1,), (1,)), name="dh", after=after)
    grad_x, dshift_p, dscale_p, gng_p = _prenorm_bwd(x2d, r, dh, dx2, norm_g, scale, _tile(s, 128))

    def rows_of(v):
        return _pad_rows(v.reshape(-1, wa), 8 * ((v.size // wa + 7) // 8))

    dmod = jnp.concatenate([dshift_p, dscale_p, dgate_p], axis=1)
    parts = [gng_p, dmod, dwa_p, dwb_p, dcb_p, dlng_p, dlnb_p, gfg_p,
             jnp.broadcast_to(loss_p[:, :1], (1, wa))]
    starts, packed = [], []
    for p in parts:
        starts.append(sum(q.shape[0] for q in packed))
        packed.append(rows_of(p) if p.shape[0] == 1 else p)
    small_sum, small_all = _gather_small(jnp.concatenate(packed, axis=0))

    def summed(k, rows):
        return small_sum[starts[k]:starts[k] + rows]

    grad_norm_g = summed(0, d // wa).reshape(1, d)
    grad_b_ada = summed(1, 3 * d // wa).reshape(1, 3 * d)
    grad_conv_a_full = summed(2, TAPS_A)
    grad_conv_b_full = summed(3, TAPS_B)
    grad_conv_b_b = summed(4, 1)
    grad_ln_b_g = summed(5, 1)
    grad_ln_b_b = summed(6, 1)
    grad_final_g = summed(7, d // wa).reshape(d)
    loss = summed(8, 1)[0, 0]
    grad_conv_a_w = lax.dynamic_slice(grad_conv_a_full, (0, chip * wsh), (TAPS_A, wsh))
    grad_conv_b_w = lax.dynamic_slice(grad_conv_b_full, (0, chip * wsh), (TAPS_B, wsh))
    dmod_all = small_all[:, starts[1]:starts[1] + 3 * d // wa, :].reshape(N_DEV, 3 * d)
    dmod_sh = lax.dynamic_slice(dmod_all, (0, chip * na), (N_DEV, na))

    def pairs_to_chips(b):
        x, y, cc = _position()
        chips, _ = _other_chips(x, y)
        return [(b[2 * k], b[2 * k + 1], (cx, cy, cc)) for k, (cx, cy) in enumerate(chips)]

    po_bufs = _wait_copies("send_out_wait", _slots_to_chips, po_bufs, po_sems, after=[small_sum])
    gh_out = _final_half(pos, g_wout3, ra_out, po_bufs[1:], _tile(ho, 256), "final_half_w_out")
    g[3], ra[3] = _wait_copies("swap_in_wait3", swap_in, [g[3], ra[3]], sw[3], after=[small_sum])
    in_bufs = _wait_copies("send_in_wait", pairs_to_chips, [b for k in range(3) for b in (q[k], rb[k])],
                           snd[0] + snd[1] + snd[2], after=[small_sum])
    gh_in = _final_half(core_only, g[3][None], ra[3][None], in_bufs[1::2], _tile(hrow, 256), "final_half_w_in")
    sh_sems, sh_bufs, sh_tok = _start_copies("share_start", _halves_to_sibling, 2, [gh_in, gh_out])

    grad_w_ada, d_wada, nm_wada, nv_wada = _adam_ada(c_rows.T, dmod_sh + sh_tok[0, 0], w_ada[0], m_w_ada[0],
                                                     v_w_ada[0], _tile(d, 128), "adam_w_ada")
    grad_w_in, grad_w_out = _wait_copies("share_wait", _halves_to_sibling, sh_bufs, sh_sems, after=[d_wada])
    d_win, nm_win, nv_win = _adam(w_in[0], grad_w_in, m_w_in[0], v_w_in[0], _tile(d, 128), "adam_w_in")
    d_wout, nm_wout, nv_wout = _adam(w_out[0], grad_w_out, m_w_out[0], v_w_out[0], _tile(r4, 128), "adam_w_out")

    def small_adam(w, g, m, v, name):
        shape = w.shape
        w2 = w.reshape(-1, shape[-1])
        out = _adam(w2, g.reshape(w2.shape), m.reshape(w2.shape), v.reshape(w2.shape), w2.shape[0], name)
        return [o_.reshape(shape) for o_ in out]

    small = {
        "norm_g": small_adam(norm_g, grad_norm_g, m_norm_g, v_norm_g, "adam_norm_g"),
        "b_ada": small_adam(b_ada, grad_b_ada, m_b_ada, v_b_ada, "adam_b_ada"),
        "conv_a_w": small_adam(conv_a_w, grad_conv_a_w, m_conv_a_w, v_conv_a_w, "adam_conv_a_w"),
        "conv_b_w": small_adam(conv_b_w, grad_conv_b_w, m_conv_b_w, v_conv_b_w, "adam_conv_b_w"),
        "conv_b_b": small_adam(conv_b_b, grad_conv_b_b, m_conv_b_b, v_conv_b_b, "adam_conv_b_b"),
        "ln_b_g": small_adam(ln_b_g, grad_ln_b_g, m_ln_b_g, v_ln_b_g, "adam_ln_b_g"),
        "ln_b_b": small_adam(ln_b_b, grad_ln_b_b, m_ln_b_b, v_ln_b_b, "adam_ln_b_b"),
        "final_g": small_adam(final_g.reshape(1, d), grad_final_g, m_final_g.reshape(1, d),
                              v_final_g.reshape(1, d), "adam_final_g"),
    }
    small["final_g"] = [o_.reshape(d) for o_ in small["final_g"]]
    big = {
        "w_ada": [a[None] for a in (d_wada, nm_wada, nv_wada)],
        "w_in": [a[None] for a in (d_win, nm_win, nv_win)],
        "w_out": [a[None] for a in (d_wout, nm_wout, nv_wout)],
    }
    upd = {**small, **big}
    order = ["norm_g", "w_ada", "b_ada", "w_in", "conv_a_w", "conv_b_w", "conv_b_b", "ln_b_g", "ln_b_b",
             "w_out", "final_g"]
    grads = {
        "norm_g": grad_norm_g, "w_ada": grad_w_ada[None], "b_ada": grad_b_ada, "w_in": grad_w_in[None],
        "conv_a_w": grad_conv_a_w[None], "conv_b_w": grad_conv_b_w[None], "conv_b_b": grad_conv_b_b,
        "ln_b_g": grad_ln_b_g, "ln_b_b": grad_ln_b_b, "w_out": grad_w_out[None], "final_g": grad_final_g,
    }
    return (loss, grad_x.reshape(1, s, d), *[grads[n] for n in order], *[upd[n][0] for n in order],
            *[upd[n][1] for n in order], *[upd[n][2] for n in order])
```

```python
import functools

import jax
import jax.numpy as jnp
from jax import lax
from jax.experimental import pallas as pl
from jax.experimental.pallas import tpu as pltpu

F32 = jnp.float32
BF16 = jnp.bfloat16
EPS = 1e-6
N_CHIPS = 4
N_DEV = 8
TAPS_A = 3
TAPS_B = 31
HALO_A = 8
HALO_B = 32
ADAM_LR = 0.001
ADAM_B1 = 0.9
ADAM_B2 = 0.999
ADAM_EPS = 1e-08
ADAM_WD = 0.01
ADAM_STEP = 10
VMEM_LIMIT = 56 * 1024 * 1024
MESH = pl.DeviceIdType.MESH
ANY = pl.BlockSpec(memory_space=pl.ANY)
VMEM = pl.BlockSpec(memory_space=pltpu.VMEM)
HBM_SPEC = pl.BlockSpec(memory_space=pltpu.HBM)
SEM_SPEC = pl.BlockSpec(memory_space=pltpu.SEMAPHORE)
EFFECT = pltpu.SideEffectType.DATAFLOW_SIDE_EFFECTING


def _params(sem=None):
    return pltpu.CompilerParams(dimension_semantics=sem, vmem_limit_bytes=VMEM_LIMIT)


def _sigmoid(v):
    return jax.nn.sigmoid(v)


def _position():
    return lax.axis_index("x"), lax.axis_index("y"), lax.axis_index("c")


def _rcopy(src, dst, ssem, rsem, dev):
    return pltpu.make_async_remote_copy(src_ref=src, dst_ref=dst, send_sem=ssem, recv_sem=rsem,
                                        device_id=dev, device_id_type=MESH)


def _other_chips(x, y):
    chips = [(1 - x, y), (x, 1 - y), (1 - x, 1 - y)]
    return chips, [2 * cx + cy for cx, cy in chips]


def _cast_bf16(a, rows, name):
    m, n = a.shape

    def body(a_ref, o_ref):
        o_ref[...] = a_ref[...].astype(BF16)

    return pl.pallas_call(
        body, name=name, grid=(m // rows,),
        in_specs=[pl.BlockSpec((rows, n), lambda i: (i, 0))],
        out_specs=pl.BlockSpec((rows, n), lambda i: (i, 0)),
        out_shape=jax.ShapeDtypeStruct((m, n), BF16),
        compiler_params=_params(("parallel",)),
    )(a)


def _cast_halves(a, rows, name):
    m, n = a.shape
    hc = n // 2

    def body(a_ref, o_ref):
        o_ref[...] = a_ref[...].astype(BF16)

    return pl.pallas_call(
        body, name=name, grid=(2, m // rows),
        in_specs=[pl.BlockSpec((rows, hc), lambda hf, i: (i, hf))],
        out_specs=pl.BlockSpec((None, rows, hc), lambda hf, i: (hf, i, 0)),
        out_shape=jax.ShapeDtypeStruct((2, m, hc), BF16),
        compiler_params=_params(("parallel", "parallel")),
    )(a)


def _proj_piece(where, h, w, proj, w_all, din, n_pieces, bm, bn, name, half=None):
    s, d = h.shape
    n = w.shape[-1]
    nj = n // bn
    if half is None:
        w_spec = pl.BlockSpec((d, bn), lambda j, i, p: (0, j))
    else:
        w_spec = pl.BlockSpec((None, d, bn), lambda j, i, p: (half, 0, j))

    def body(p_ref, h_ref, w_ref, *rest):
        o_ref, wall_ref = rest[-2], rest[-1]
        o_ref[...] = jnp.dot(h_ref[...], w_ref[...], preferred_element_type=F32)

        @pl.when(pl.program_id(1) == 0)
        def _():
            wall_ref[...] = w_ref[...]

    args, extra, alias = [where, h, w], [], {}
    if proj is not None:
        args, extra, alias = args + [proj, w_all], [ANY, ANY], {3: 0, 4: 1}
    return pl.pallas_call(
        body, name=name,
        grid_spec=pltpu.PrefetchScalarGridSpec(
            num_scalar_prefetch=1, grid=(nj, s // bm),
            in_specs=[pl.BlockSpec((bm, d), lambda j, i, p: (i, 0)), w_spec] + extra,
            out_specs=[pl.BlockSpec((bm, bn), lambda j, i, p: (i, p[0] * nj + j)),
                       pl.BlockSpec((None, d, bn), lambda j, i, p: (p[0], 0, j))]),
        out_shape=[jax.ShapeDtypeStruct((s, din), F32), jax.ShapeDtypeStruct((n_pieces, d, n), BF16)],
        input_output_aliases=alias,
        compiler_params=_params(("parallel", "arbitrary")),
    )(*args)


def _grad_slot(slot, h, dproj, after, ns, bd, bn, name):
    s, d = h.shape
    nb = ns // bn

    def body(slot_ref, h_ref, dp_ref, *rest):
        rest[-1][...] = lax.dot_general(h_ref[...], dp_ref[...], (((0,), (0,)), ((), ())),
                                        preferred_element_type=F32)

    return pl.pallas_call(
        body, name=name,
        grid_spec=pltpu.PrefetchScalarGridSpec(
            num_scalar_prefetch=1, grid=(d // bd, nb),
            in_specs=[pl.BlockSpec((s, bd), lambda i, j, sl: (0, i)),
                      pl.BlockSpec((s, bn), lambda i, j, sl: (0, sl[0] * nb + j))] + [ANY] * len(after),
            out_specs=pl.BlockSpec((bd, bn), lambda i, j, sl: (i, j))),
        out_shape=jax.ShapeDtypeStruct((d, ns), F32),
        compiler_params=_params(("parallel", "parallel")),
    )(slot, h, dproj, *after)


def _matmul(a, b, *, grid, a_spec, b_spec, o_spec, out_shape, dims, name, after=()):
    nk = grid[2]
    n_after = len(after)

    def body(a_ref, b_ref, *rest):
        o_ref, acc = rest[n_after], rest[n_after + 1:]
        p = lax.dot_general(a_ref[...], b_ref[...], (dims, ((), ())), preferred_element_type=F32)
        if nk == 1:
            o_ref[...] = p.astype(o_ref.dtype)
        else:
            acc_ref, = acc
            k = pl.program_id(2)

            @pl.when(k == 0)
            def _():
                acc_ref[...] = p

            @pl.when(k > 0)
            def _():
                acc_ref[...] += p

            @pl.when(k == nk - 1)
            def _():
                o_ref[...] = acc_ref[...].astype(o_ref.dtype)

    block = [d for d in o_spec.block_shape if d is not None]
    scratch = [pltpu.VMEM(tuple(block), F32)] if nk > 1 else []
    return pl.pallas_call(
        body, name=name, grid=grid, in_specs=[a_spec, b_spec] + [ANY] * n_after, out_specs=o_spec,
        out_shape=out_shape, scratch_shapes=scratch,
        compiler_params=_params(("parallel", "parallel", "arbitrary")),
    )(a, b, *after)


def _adam_math(w, g, m, v):
    m = ADAM_B1 * m + (1.0 - ADAM_B1) * g
    v = ADAM_B2 * v + (1.0 - ADAM_B2) * (g * g)
    m_hat = m / (1.0 - ADAM_B1 ** ADAM_STEP)
    v_hat = v / (1.0 - ADAM_B2 ** ADAM_STEP)
    delta = -ADAM_LR * (m_hat / (jnp.sqrt(v_hat) + ADAM_EPS) + ADAM_WD * w)
    return delta, m, v


def _adam(w, g, m, v, rows, name):
    r, n = w.shape

    def body(w_ref, g_ref, m_ref, v_ref, d_ref, mo_ref, vo_ref):
        d, mo, vo = _adam_math(w_ref[...], g_ref[...], m_ref[...], v_ref[...])
        d_ref[...] = d
        mo_ref[...] = mo
        vo_ref[...] = vo

    spec = pl.BlockSpec((rows, n), lambda i: (i, 0))
    shape = jax.ShapeDtypeStruct((r, n), F32)
    return pl.pallas_call(
        body, name=name, grid=(r // rows,), in_specs=[spec] * 4, out_specs=[spec] * 3,
        out_shape=[shape] * 3, compiler_params=_params(("parallel",)),
    )(w, g, m, v)


def _adam_ada(c_cols, dmod, w, m, v, rows, name):
    r, n = w.shape

    def body(c_ref, dm_ref, w_ref, m_ref, v_ref, g_ref, d_ref, mo_ref, vo_ref):
        cv = c_ref[...]
        c_act = cv * _sigmoid(cv)
        g = c_act[:, 0:1] * dm_ref[0:1, :]
        for b in range(1, N_DEV):
            g = g + c_act[:, b:b + 1] * dm_ref[b:b + 1, :]
        d, mo, vo = _adam_math(w_ref[...], g, m_ref[...], v_ref[...])
        g_ref[...] = g
        d_ref[...] = d
        mo_ref[...] = mo
        vo_ref[...] = vo

    spec = pl.BlockSpec((rows, n), lambda i: (i, 0))
    shape = jax.ShapeDtypeStruct((r, n), F32)
    return pl.pallas_call(
        body, name=name, grid=(r // rows,),
        in_specs=[pl.BlockSpec((rows, N_DEV), lambda i: (i, 0)), pl.BlockSpec((N_DEV, n), lambda i: (0, 0)),
                  spec, spec, spec],
        out_specs=[spec] * 4, out_shape=[shape] * 4, compiler_params=_params(("parallel",)),
    )(c_cols, dmod, w, m, v)


def _start_copies(name, plan, n, bufs, after=()):
    nb, na = len(bufs), len(after)

    def body(*refs):
        sems = refs[nb + na:nb + na + 2 * n]
        for k, (src, dst, dev) in enumerate(plan(refs[:nb])):
            _rcopy(src, dst, sems[2 * k], sems[2 * k + 1], dev).start()
        refs[-1][...] = jnp.zeros((8, 128), F32)

    outs = pl.pallas_call(
        body, name=name,
        out_shape=[pltpu.SemaphoreType.DMA(())] * (2 * n) + [pltpu.HBM(a.shape, a.dtype) for a in bufs]
        + [jax.ShapeDtypeStruct((8, 128), F32)],
        in_specs=[HBM_SPEC] * nb + [ANY] * na, out_specs=[SEM_SPEC] * (2 * n) + [HBM_SPEC] * nb + [VMEM],
        input_output_aliases={i: 2 * n + i for i in range(nb)},
        compiler_params=pltpu.CompilerParams(has_side_effects=EFFECT),
    )(*[pltpu.with_memory_space_constraint(a, pltpu.HBM) for a in bufs], *after)
    return list(outs[:2 * n]), list(outs[2 * n:2 * n + nb]), outs[-1]


def _wait_copies(name, plan, bufs, sems, after=(), send=True, recv=True):
    nb, nsem = len(bufs), len(sems)

    def body(*refs):
        s = refs[nb:nb + nsem]
        for k, (src, dst, dev) in enumerate(plan(refs[:nb])):
            cp = _rcopy(src, dst, s[2 * k], s[2 * k + 1], dev)
            if send:
                cp.wait_send()
            if recv:
                cp.wait_recv()

    outs = pl.pallas_call(
        body, name=name, out_shape=[pltpu.HBM(a.shape, a.dtype) for a in bufs],
        in_specs=[HBM_SPEC] * nb + [SEM_SPEC] * nsem + [ANY] * len(after), out_specs=[HBM_SPEC] * nb,
        input_output_aliases={i: i for i in range(nb)},
        compiler_params=pltpu.CompilerParams(has_side_effects=EFFECT),
    )(*bufs, *sems, *after)
    return list(outs)


def _to_sibling(views):
    def plan(b):
        x, y, c = _position()
        return [(view(b[2 * k], c), b[2 * k + 1], (x, y, 1 - c)) for k, view in enumerate(views)]
    return plan


def _to_chip(k):
    def plan(b):
        x, y, c = _position()
        cx, cy = _other_chips(x, y)[0][k]
        return [(b[0], b[1], (cx, cy, c))]
    return plan


def _slots_to_chips(b):
    x, y, c = _position()
    chips, cidx = _other_chips(x, y)
    return [(b[0].at[cidx[k]], b[1 + k], (cx, cy, c)) for k, (cx, cy) in enumerate(chips)]


def _halves_to_sibling(b):
    x, y, c = _position()
    views = [r.at[pl.ds(c * (r.shape[0] // 2), r.shape[0] // 2), :] for r in b]
    return [(v, v, (x, y, 1 - c)) for v in views]


def _landed(b):
    x, y, c = _position()
    return [(ref, ref, (x, y, c)) for ref in b]


def _assemble(name, pieces, out_shape, index_of):
    n = len(pieces)

    def body(*refs):
        out_ref, sem = refs[n], refs[n + 1]
        x, y, c = _position()
        _, cidx = _other_chips(x, y)
        cps = [pltpu.make_async_copy(refs[k], out_ref.at[index_of(k, 2 * x + y, c, cidx)], sem.at[k]) for k in range(n)]
        for cp in cps:
            cp.start()
        for cp in cps:
            cp.wait()

    return pl.pallas_call(
        body, name=name, in_specs=[VMEM] * n, out_specs=ANY, out_shape=out_shape,
        scratch_shapes=[pltpu.SemaphoreType.DMA((n,))],
        compiler_params=pltpu.CompilerParams(vmem_limit_bytes=VMEM_LIMIT),
    )(*pieces)


def _gather_cond(c8, cw):
    def body(c8_ref, cw_ref, call_ref, cwall_ref, ssem, rsem, lsem):
        x, y, c = _position()
        chip = 2 * x + y
        me = 4 * x + 2 * y + c
        chips, cidx = _other_chips(x, y)
        own = [pltpu.make_async_copy(c8_ref, call_ref.at[me], lsem.at[0]),
               pltpu.make_async_copy(cw_ref, cwall_ref.at[chip], lsem.at[1])]
        for cp in own:
            cp.start()
        sends = [_rcopy(cw_ref, cwall_ref.at[chip], ssem.at[k], rsem.at[k], (cx, cy, c))
                 for k, (cx, cy) in enumerate(chips)]
        for mask in range(1, N_DEV):
            fx, fy, fc = (mask >> 2) & 1, (mask >> 1) & 1, mask & 1
            dev = (1 - x if fx else x, 1 - y if fy else y, 1 - c if fc else c)
            sends.append(_rcopy(c8_ref, call_ref.at[me], ssem.at[2 + mask], rsem.at[2 + mask], dev))
        for cp in sends:
            cp.start()
        for k in range(3):
            slot = cwall_ref.at[cidx[k]]
            _rcopy(slot, slot, ssem.at[k], rsem.at[k], (x, y, c)).wait_recv()
        for mask in range(1, N_DEV):
            slot = call_ref.at[jnp.bitwise_xor(me, mask)]
            _rcopy(slot, slot, ssem.at[2 + mask], rsem.at[2 + mask], (x, y, c)).wait_recv()
        for cp in sends:
            cp.wait_send()
        for cp in own:
            cp.wait()

    return pl.pallas_call(
        body, name="gather_cond", in_specs=[VMEM, VMEM], out_specs=[VMEM, VMEM],
        out_shape=[jax.ShapeDtypeStruct((N_DEV,) + c8.shape, F32), jax.ShapeDtypeStruct((N_CHIPS,) + cw.shape, F32)],
        scratch_shapes=[pltpu.SemaphoreType.DMA((10,)), pltpu.SemaphoreType.DMA((10,)), pltpu.SemaphoreType.DMA((2,))],
    )(c8, cw)


def _exchange_mod(mod_part):
    def body(mp_ref, out_ref, ssem, rsem, lsem):
        x, y, c = _position()
        chip = 2 * x + y
        chips, cidx = _other_chips(x, y)
        own = pltpu.make_async_copy(mp_ref, out_ref.at[chip], lsem)
        own.start()
        sends = [_rcopy(mp_ref, out_ref.at[chip], ssem.at[k], rsem.at[k], (cx, cy, c))
                 for k, (cx, cy) in enumerate(chips)]
        for cp in sends:
            cp.start()
        for k in range(3):
            slot = out_ref.at[cidx[k]]
            _rcopy(slot, slot, ssem.at[k], rsem.at[k], (x, y, c)).wait_recv()
        for cp in sends:
            cp.wait_send()
        own.wait()

    return pl.pallas_call(
        body, name="exchange_mod", in_specs=[VMEM], out_specs=VMEM,
        out_shape=jax.ShapeDtypeStruct((N_CHIPS,) + mod_part.shape, F32),
        scratch_shapes=[pltpu.SemaphoreType.DMA((3,)), pltpu.SemaphoreType.DMA((3,)), pltpu.SemaphoreType.DMA],
    )(mod_part)


def _gather_small(pack):
    rows, n = pack.shape

    def body(p_ref, sum_ref, all_ref, ssem, rsem, lsem):
        x, y, c = _position()
        me = 4 * x + 2 * y + c
        own = pltpu.make_async_copy(p_ref, all_ref.at[me], lsem)
        own.start()
        sends = []
        for mask in range(1, N_DEV):
            fx, fy, fc = (mask >> 2) & 1, (mask >> 1) & 1, mask & 1
            dev = (1 - x if fx else x, 1 - y if fy else y, 1 - c if fc else c)
            sends.append(_rcopy(p_ref, all_ref.at[me], ssem.at[mask - 1], rsem.at[mask - 1], dev))
        for cp in sends:
            cp.start()
        for mask in range(1, N_DEV):
            slot = all_ref.at[jnp.bitwise_xor(me, mask)]
            _rcopy(slot, slot, ssem.at[mask - 1], rsem.at[mask - 1], (x, y, c)).wait_recv()
        for cp in sends:
            cp.wait_send()
        own.wait()
        acc = all_ref[0]
        for k in range(1, N_DEV):
            acc = acc + all_ref[k]
        sum_ref[...] = acc

    return pl.pallas_call(
        body, name="gather_small", in_specs=[VMEM], out_specs=[VMEM, VMEM],
        out_shape=[jax.ShapeDtypeStruct((rows, n), F32), jax.ShapeDtypeStruct((N_DEV, rows, n), F32)],
        scratch_shapes=[pltpu.SemaphoreType.DMA((7,)), pltpu.SemaphoreType.DMA((7,)), pltpu.SemaphoreType.DMA],
        compiler_params=pltpu.CompilerParams(vmem_limit_bytes=VMEM_LIMIT),
    )(pack)


def _chip_partial(pos, g, recv, rows, name):
    ns, full, n = g.shape
    h = full // 2
    nb = h // rows

    def body(pos_ref, g_ref, r_ref, o_ref):
        o_ref[...] = (g_ref[...] + r_ref[...]).astype(BF16)

    return pl.pallas_call(
        body, name=name,
        grid_spec=pltpu.PrefetchScalarGridSpec(
            num_scalar_prefetch=1, grid=(ns, nb),
            in_specs=[pl.BlockSpec((None, rows, n), lambda s, i, p: (s, p[1] * nb + i, 0)),
                      pl.BlockSpec((None, rows, n), lambda s, i, p: (s, i, 0))],
            out_specs=pl.BlockSpec((None, rows, n), lambda s, i, p: (s, i, 0))),
        out_shape=jax.ShapeDtypeStruct((ns, h, n), BF16),
        compiler_params=_params(("parallel", "parallel")),
    )(pos, g, recv)


def _final_half(pos, g, recv_a, recv_b, rows, name):
    ns, full, n = g.shape
    h = full // 2
    nb = h // rows

    def body(pos_ref, g_ref, ra_ref, rb0_ref, rb1_ref, rb2_ref, o_ref):
        acc = g_ref[...] + ra_ref[...]
        for rb_ref in (rb0_ref, rb1_ref, rb2_ref):
            acc = acc + rb_ref[...].astype(F32)
        o_ref[...] = acc

    part = pl.BlockSpec((rows, n), lambda i, p: (i, 0))
    return pl.pallas_call(
        body, name=name,
        grid_spec=pltpu.PrefetchScalarGridSpec(
            num_scalar_prefetch=1, grid=(nb,),
            in_specs=[pl.BlockSpec((None, rows, n), lambda i, p: (p[0], p[1] * nb + i, 0)),
                      pl.BlockSpec((None, rows, n), lambda i, p: (p[0], i, 0)), part, part, part],
            out_specs=pl.BlockSpec((rows, n), lambda i, p: (p[1] * nb + i, 0))),
        out_shape=jax.ShapeDtypeStruct((full, n), F32),
        compiler_params=_params(("parallel",)),
    )(pos, g, recv_a, *recv_b)


def _modulation(c_rows, w_ada, b_ada, cols, name):
    d, n = w_ada.shape
    rows = c_rows.shape[0]

    def body(c_ref, w_ref, b_ref, o_ref):
        cv = c_ref[...]
        c_act = (cv * _sigmoid(cv)).astype(BF16)
        o_ref[...] = jnp.dot(c_act, w_ref[...].astype(BF16), preferred_element_type=F32) + b_ref[...]

    return pl.pallas_call(
        body, name=name, grid=(n // cols,),
        in_specs=[pl.BlockSpec((rows, d), lambda j: (0, 0)), pl.BlockSpec((d, cols), lambda j: (0, j)),
                  pl.BlockSpec((1, cols), lambda j: (0, j))],
        out_specs=pl.BlockSpec((rows, cols), lambda j: (0, j)),
        out_shape=jax.ShapeDtypeStruct((rows, n), F32),
        compiler_params=_params(("parallel",)),
    )(c_rows, w_ada, b_ada)


def _prenorm(x, norm_g, scale, shift, rows):
    s, d = x.shape

    def body(x_ref, g_ref, sc_ref, sh_ref, h_ref, r_ref):
        xv = x_ref[...]
        r = lax.rsqrt(jnp.mean(xv * xv, axis=-1, keepdims=True) + EPS)
        h = (xv * r * g_ref[...]) * (1.0 + sc_ref[...]) + sh_ref[...]
        h_ref[...] = h.astype(BF16)
        r_ref[...] = r

    vec = pl.BlockSpec((1, d), lambda i: (0, 0))
    return pl.pallas_call(
        body, name="prenorm", grid=(s // rows,),
        in_specs=[pl.BlockSpec((rows, d), lambda i: (i, 0)), vec, vec, vec],
        out_specs=[pl.BlockSpec((rows, d), lambda i: (i, 0)), pl.BlockSpec((rows, 1), lambda i: (i, 0))],
        out_shape=[jax.ShapeDtypeStruct((s, d), BF16), jax.ShapeDtypeStruct((s, 1), F32)],
        compiler_params=_params(("parallel",)),
    )(x, norm_g, scale, shift)


def _mixer_a_fwd(proj, conv_w, wa, rows, cols):
    s = proj.shape[0]
    ncb = wa // cols

    def body(ab_ref, ac_ref, ax_ref, az_ref, w_ref, y_ref, qbuf):
        t = pl.program_id(1)

        @pl.when(t == 0)
        def _():
            qbuf[0:HALO_A, :] = jnp.zeros((HALO_A, cols), F32)

        q = ac_ref[...] * ax_ref[...]
        qbuf[HALO_A:HALO_A + rows, :] = q
        conv = w_ref[2:3, :] * q
        for k in range(TAPS_A - 1):
            off = HALO_A - (TAPS_A - 1) + k
            conv = conv + w_ref[k:k + 1, :] * qbuf[off:off + rows, :]
        zv = az_ref[...]
        y_ref[...] = (ab_ref[...] * conv * (zv * _sigmoid(zv))).astype(BF16)
        qbuf[0:HALO_A, :] = qbuf[rows:rows + HALO_A, :]

    def sec(k):
        return pl.BlockSpec((rows, cols), lambda cb, t, k=k: (t, k * ncb + cb))

    return pl.pallas_call(
        body, name="mixer_a_fwd", grid=(ncb, s // rows),
        in_specs=[sec(0), sec(1), sec(2), sec(3), pl.BlockSpec((HALO_A, cols), lambda cb, t: (0, cb))],
        out_specs=pl.BlockSpec((rows, cols), lambda cb, t: (t, cb)),
        out_shape=jax.ShapeDtypeStruct((s, 2 * wa), BF16),
        scratch_shapes=[pltpu.VMEM((HALO_A + rows, cols), F32)],
        compiler_params=_params(("parallel", "arbitrary")),
    )(proj, proj, proj, proj, conv_w)


def _shifted_back(dst, src, lo, hi):
    for n in range(8):
        dst[n, lo:hi, :] = src[lo - n:hi - n, :]


def _shifted_fwd(dst, src, lo, hi):
    for n in range(8):
        dst[n, lo:hi, :] = src[lo + n:hi + n, :]


def _mixer_b_conv_fwd(proj, conv_w, conv_b, wa, rows, cols, chunk):
    s = proj.shape[0]
    wb = conv_w.shape[1]
    ncb = wb // cols
    sec0 = 4 * wa // cols

    def body(bv_ref, bg_ref, w_ref, b_ref, u0_ref, u_ref, ubuf, sh):
        t = pl.program_id(1)

        @pl.when(t == 0)
        def _():
            ubuf[0:HALO_B, :] = jnp.zeros((HALO_B, cols), F32)

        u0 = bv_ref[...] * _sigmoid(bg_ref[...])
        u0_ref[...] = u0
        ubuf[HALO_B:HALO_B + rows, :] = u0
        _shifted_back(sh, ubuf, 8, HALO_B + rows)

        def row_chunk(rc, carry):
            base = pl.multiple_of(rc * chunk, chunk)
            acc = jnp.zeros((chunk, cols), F32)
            for k in range(TAPS_B):
                mq, n = divmod(TAPS_B - 1 - k, 8)
                acc = acc + w_ref[k:k + 1, :] * sh[n, pl.ds(HALO_B - 8 * mq + base, chunk), :]
            u_ref[pl.ds(base, chunk), :] = acc + b_ref[...]
            return carry

        lax.fori_loop(0, rows // chunk, row_chunk, 0)
        ubuf[0:HALO_B, :] = ubuf[rows:rows + HALO_B, :]

    return pl.pallas_call(
        body, name="mixer_b_conv_fwd", grid=(ncb, s // rows),
        in_specs=[pl.BlockSpec((rows, cols), lambda cb, t: (t, sec0 + cb)),
                  pl.BlockSpec((rows, cols), lambda cb, t: (t, sec0 + ncb + cb)),
                  pl.BlockSpec((HALO_B, cols), lambda cb, t: (0, cb)),
                  pl.BlockSpec((1, cols), lambda cb, t: (0, cb))],
        out_specs=[pl.BlockSpec((rows, cols), lambda cb, t: (t, cb))] * 2,
        out_shape=[jax.ShapeDtypeStruct((s, wb), F32)] * 2,
        scratch_shapes=[pltpu.VMEM((HALO_B + rows, cols), F32), pltpu.VMEM((8, HALO_B + rows, cols), F32)],
        compiler_params=_params(("parallel", "arbitrary")),
    )(proj, proj, conv_w, conv_b)


def _layernorm_stats(u):
    mu = jnp.mean(u, axis=-1, keepdims=True)
    xc = u - mu
    var = jnp.mean(xc * xc, axis=-1, keepdims=True)
    return xc * lax.rsqrt(var + EPS), lax.rsqrt(var + EPS)


def _mixer_b_gate_fwd(y, u, proj, ln_g, ln_b, wa, rows):
    s, wb = u.shape
    sec_z = (4 * wa + 2 * wb) // wb

    def body(y_in, u_ref, bz_ref, g_ref, b_ref, y_ref):
        uh, _ = _layernorm_stats(u_ref[...])
        ln = uh * g_ref[...] + b_ref[...]
        zv = bz_ref[...]
        y_ref[...] = ((ln * _sigmoid(ln)) * (zv * _sigmoid(zv))).astype(BF16)

    vec = pl.BlockSpec((1, wb), lambda i: (0, 0))
    return pl.pallas_call(
        body, name="mixer_b_gate_fwd", grid=(s // rows,),
        in_specs=[ANY, pl.BlockSpec((rows, wb), lambda i: (i, 0)), pl.BlockSpec((rows, wb), lambda i: (i, sec_z)),
                  vec, vec],
        out_specs=pl.BlockSpec((rows, wb), lambda i: (i, wa // wb)),
        out_shape=jax.ShapeDtypeStruct(y.shape, BF16), input_output_aliases={0: 0},
        compiler_params=_params(("parallel",)),
    )(y, u, proj, ln_g, ln_b)


def _loss_head(x, o, target, gate, final_g, rows):
    s, d = x.shape

    def body(x_ref, o_ref, t_ref, gate_ref, fg_ref, dx2_ref, do_ref, loss_ref, gfg_ref, dgate_ref):
        i = pl.program_id(0)
        ov = o_ref[...]
        x2 = x_ref[...] + gate_ref[...] * ov
        r2 = lax.rsqrt(jnp.mean(x2 * x2, axis=-1, keepdims=True) + EPS)
        xn2 = x2 * r2
        diff = xn2 * fg_ref[...] - t_ref[...]
        dout = diff * (1.0 / d)
        dxn2 = dout * fg_ref[...]
        dx2 = r2 * (dxn2 - xn2 * jnp.mean(dxn2 * xn2, axis=-1, keepdims=True))
        dx2_ref[...] = dx2
        do_ref[...] = (gate_ref[...] * dx2).astype(BF16)
        loss_part = 0.5 * jnp.sum(jnp.mean(diff * diff, axis=-1, keepdims=True), axis=0, keepdims=True)
        gfg_part = jnp.sum(dout * xn2, axis=0, keepdims=True)
        dgate_part = jnp.sum(dx2 * ov, axis=0, keepdims=True)

        @pl.when(i == 0)
        def _():
            loss_ref[...] = jnp.zeros_like(loss_ref)
            gfg_ref[...] = jnp.zeros_like(gfg_ref)
            dgate_ref[...] = jnp.zeros_like(dgate_ref)

        loss_ref[...] += jnp.broadcast_to(loss_part, loss_ref.shape)
        gfg_ref[...] += gfg_part
        dgate_ref[...] += dgate_part

    blk = pl.BlockSpec((rows, d), lambda i: (i, 0))
    vec = pl.BlockSpec((1, d), lambda i: (0, 0))
    return pl.pallas_call(
        body, name="loss_head", grid=(s // rows,),
        in_specs=[blk, blk, blk, vec, vec],
        out_specs=[blk, blk, pl.BlockSpec((1, 128), lambda i: (0, 0)), vec, vec],
        out_shape=[jax.ShapeDtypeStruct((s, d), F32), jax.ShapeDtypeStruct((s, d), BF16),
                   jax.ShapeDtypeStruct((1, 128), F32), jax.ShapeDtypeStruct((1, d), F32),
                   jax.ShapeDtypeStruct((1, d), F32)],
        compiler_params=_params(("arbitrary",)),
    )(x, o, target, gate, final_g)


def _mixer_a_bwd(proj, dy, conv_w, wa, din, rows):
    s = proj.shape[0]
    nt = s // rows
    per8 = rows // HALO_A

    def body(ab_ref, ac_ref, ax_ref, az_ref, hc_ref, hx_ref, dy_ref, w_ref, dp_ref, dw_ref, qbuf, dbuf):
        i = pl.program_id(0)

        @pl.when(i == 0)
        def _():
            dbuf[rows:rows + HALO_A, :] = jnp.zeros((HALO_A, wa), F32)
            dw_ref[...] = jnp.zeros_like(dw_ref)

        keep = jnp.where(i == nt - 1, 0.0, 1.0)
        qbuf[0:HALO_A, :] = hc_ref[...] * hx_ref[...] * keep
        acv, axv = ac_ref[...], ax_ref[...]
        q = acv * axv
        qbuf[HALO_A:HALO_A + rows, :] = q
        conv = w_ref[2:3, :] * q
        for k in range(TAPS_A - 1):
            off = HALO_A - (TAPS_A - 1) + k
            conv = conv + w_ref[k:k + 1, :] * qbuf[off:off + rows, :]
        zv, abv, dyv = az_ref[...], ab_ref[...], dy_ref[...]
        sg = _sigmoid(zv)
        sz = zv * sg
        dp_ref[:, 0:wa] = (dyv * conv * sz).astype(BF16)
        dp_ref[:, 3 * wa:4 * wa] = (dyv * abv * conv * (sg * (1.0 + zv * (1.0 - sg)))).astype(BF16)
        dconv = dyv * abv * sz
        dbuf[0:rows, :] = dconv
        dq = w_ref[2:3, :] * dconv
        for k in range(TAPS_A - 1):
            off = TAPS_A - 1 - k
            dq = dq + w_ref[k:k + 1, :] * dbuf[off:off + rows, :]
        dp_ref[:, wa:2 * wa] = (dq * axv).astype(BF16)
        dp_ref[:, 2 * wa:3 * wa] = (dq * acv).astype(BF16)
        for k in range(TAPS_A):
            off = HALO_A - (TAPS_A - 1) + k
            dw_ref[k:k + 1, :] += jnp.sum(dconv * qbuf[off:off + rows, :], axis=0, keepdims=True)
        dbuf[rows:rows + HALO_A, :] = dbuf[0:HALO_A, :]

    def sec(k):
        return pl.BlockSpec((rows, wa), lambda i, k=k: (nt - 1 - i, k))

    def halo(k):
        return pl.BlockSpec((HALO_A, wa), lambda i, k=k: (jnp.maximum((nt - 1 - i) * per8 - 1, 0), k))

    return pl.pallas_call(
        body, name="mixer_a_bwd", grid=(nt,),
        in_specs=[sec(0), sec(1), sec(2), sec(3), halo(1), halo(2),
                  pl.BlockSpec((rows, wa), lambda i: (nt - 1 - i, 0)),
                  pl.BlockSpec((HALO_A, wa), lambda i: (0, 0))],
        out_specs=[pl.BlockSpec((rows, 4 * wa), lambda i: (nt - 1 - i, 0)),
                   pl.BlockSpec((HALO_A, wa), lambda i: (0, 0))],
        out_shape=[jax.ShapeDtypeStruct((s, din), BF16), jax.ShapeDtypeStruct((HALO_A, wa), F32)],
        scratch_shapes=[pltpu.VMEM((HALO_A + rows, wa), F32), pltpu.VMEM((rows + HALO_A, wa), F32)],
        compiler_params=_params(("arbitrary",)),
    )(proj, proj, proj, proj, proj, proj, dy, conv_w)


def _mixer_b_gate_bwd(dproj, dy, u, proj, ln_g, ln_b, wa, rows):
    s, wb = u.shape
    sec_z = (4 * wa + 2 * wb) // wb

    def body(dp_in, dy_ref, u_ref, bz_ref, g_ref, b_ref, dp_ref, du_ref, dg_ref, db_ref, dcb_ref):
        i = pl.program_id(0)
        uh, rs = _layernorm_stats(u_ref[...])
        ln = uh * g_ref[...] + b_ref[...]
        sl = _sigmoid(ln)
        zv = bz_ref[...]
        sg = _sigmoid(zv)
        dyv = dy_ref[...]
        dp_ref[...] = (dyv * (ln * sl) * (sg * (1.0 + zv * (1.0 - sg)))).astype(BF16)
        dln = dyv * (zv * sg) * (sl * (1.0 + ln * (1.0 - sl)))
        duh = dln * g_ref[...]
        du = rs * (duh - jnp.mean(duh, axis=-1, keepdims=True) - uh * jnp.mean(duh * uh, axis=-1, keepdims=True))
        du_ref[...] = du

        @pl.when(i == 0)
        def _():
            dg_ref[...] = jnp.zeros_like(dg_ref)
            db_ref[...] = jnp.zeros_like(db_ref)
            dcb_ref[...] = jnp.zeros_like(dcb_ref)

        dg_ref[...] += jnp.sum(dln * uh, axis=0, keepdims=True)
        db_ref[...] += jnp.sum(dln, axis=0, keepdims=True)
        dcb_ref[...] += jnp.sum(du, axis=0, keepdims=True)

    blk = pl.BlockSpec((rows, wb), lambda i: (i, 0))
    vec = pl.BlockSpec((1, wb), lambda i: (0, 0))
    vshape = jax.ShapeDtypeStruct((1, wb), F32)
    return pl.pallas_call(
        body, name="mixer_b_gate_bwd", grid=(s // rows,),
        in_specs=[ANY, pl.BlockSpec((rows, wb), lambda i: (i, wa // wb)), blk,
                  pl.BlockSpec((rows, wb), lambda i: (i, sec_z)), vec, vec],
        out_specs=[pl.BlockSpec((rows, wb), lambda i: (i, sec_z)), blk, vec, vec, vec],
        out_shape=[jax.ShapeDtypeStruct(dproj.shape, BF16), jax.ShapeDtypeStruct((s, wb), F32), vshape, vshape, vshape],
        input_output_aliases={0: 0},
        compiler_params=_params(("arbitrary",)),
    )(dproj, dy, u, proj, ln_g, ln_b)


def _mixer_b_conv_bwd(dproj, du, u0, proj, conv_w, wa, rows, lanes, chunk):
    s, wb = du.shape
    nt = s // rows
    per32 = rows // HALO_B
    sec_v = 4 * wa // wb
    nlc = wb // lanes
    nrc = rows // chunk

    def body(dp_in, du_ref, u0_ref, h0_ref, bv_ref, bg_ref, w_ref, dp_ref, dw_ref, ubuf, dbuf, sh, shf, carry):
        i = pl.program_id(0)

        @pl.when(i == 0)
        def _():
            carry[...] = jnp.zeros_like(carry)
            dw_ref[...] = jnp.zeros_like(dw_ref)

        keep = jnp.where(i == nt - 1, 0.0, 1.0)
        for lc in range(nlc):
            cs = slice(lc * lanes, (lc + 1) * lanes)
            ubuf[0:HALO_B, :] = h0_ref[:, cs] * keep
            ubuf[HALO_B:HALO_B + rows, :] = u0_ref[:, cs]
            dbuf[0:rows, :] = du_ref[:, cs]
            dbuf[rows:rows + HALO_B, :] = carry[:, cs]
            _shifted_back(sh, ubuf, 8, HALO_B + rows)
            _shifted_fwd(shf, dbuf, 0, rows + HALO_B - 8)

            def row_chunk(rc, c0):
                base = pl.multiple_of(rc * chunk, chunk)
                acc = jnp.zeros((chunk, lanes), F32)
                for k in range(TAPS_B):
                    mq, n = divmod(TAPS_B - 1 - k, 8)
                    acc = acc + w_ref[k:k + 1, cs] * shf[n, pl.ds(base + 8 * mq, chunk), :]
                sg = _sigmoid(bg_ref[pl.ds(base, chunk), cs])
                bv = bv_ref[pl.ds(base, chunk), cs]
                dp_ref[pl.ds(base, chunk), cs] = (acc * sg).astype(BF16)
                dp_ref[pl.ds(base, chunk), lc * lanes + wb:(lc + 1) * lanes + wb] = (
                    acc * bv * sg * (1.0 - sg)).astype(BF16)
                return c0

            lax.fori_loop(0, nrc, row_chunk, 0)

            for k in range(TAPS_B):
                mq, n = divmod(TAPS_B - 1 - k, 8)

                def tap_rows(rc, acc, mq=mq, n=n):
                    base = pl.multiple_of(rc * chunk, chunk)
                    prod = dbuf[pl.ds(base, chunk), :] * sh[n, pl.ds(HALO_B - 8 * mq + base, chunk), :]
                    return acc + jnp.sum(prod.reshape(chunk // 8, 8, lanes), axis=0)

                part = lax.fori_loop(0, nrc, tap_rows, jnp.zeros((8, lanes), F32))
                dw_ref[k:k + 1, cs] += jnp.sum(part, axis=0, keepdims=True)
            carry[:, cs] = dbuf[0:HALO_B, :]

    def rev(cols_blk):
        return pl.BlockSpec((rows, wb), lambda i, cb=cols_blk: (nt - 1 - i, cb))

    return pl.pallas_call(
        body, name="mixer_b_conv_bwd", grid=(nt,),
        in_specs=[ANY, rev(0), rev(0),
                  pl.BlockSpec((HALO_B, wb), lambda i: (jnp.maximum((nt - 1 - i) * per32 - 1, 0), 0)),
                  rev(sec_v), rev(sec_v + 1), pl.BlockSpec((HALO_B, wb), lambda i: (0, 0))],
        out_specs=[pl.BlockSpec((rows, 2 * wb), lambda i: (nt - 1 - i, sec_v // 2)),
                   pl.BlockSpec((HALO_B, wb), lambda i: (0, 0))],
        out_shape=[jax.ShapeDtypeStruct(dproj.shape, BF16), jax.ShapeDtypeStruct((HALO_B, wb), F32)],
        input_output_aliases={0: 0},
        scratch_shapes=[pltpu.VMEM((HALO_B + rows, lanes), F32), pltpu.VMEM((rows + HALO_B, lanes), F32),
                        pltpu.VMEM((8, HALO_B + rows, lanes), F32), pltpu.VMEM((8, rows + HALO_B, lanes), F32),
                        pltpu.VMEM((HALO_B, wb), F32)],
        compiler_params=_params(("arbitrary",)),
    )(dproj, du, u0, u0, proj, proj, conv_w)


def _prenorm_bwd(x, r, dh, dx2, norm_g, scale, rows):
    s, d = x.shape

    def body(x_ref, r_ref, dh_ref, dx2_ref, g_ref, sc_ref, gx_ref, dsh_ref, dsc_ref, dg_ref):
        i = pl.program_id(0)
        rv = r_ref[...]
        xn = x_ref[...] * rv
        dhv = dh_ref[...]
        one_sc = 1.0 + sc_ref[...]
        dxn = dhv * one_sc * g_ref[...]
        gx_ref[...] = dx2_ref[...] + rv * (dxn - xn * jnp.mean(dxn * xn, axis=-1, keepdims=True))

        @pl.when(i == 0)
        def _():
            dsh_ref[...] = jnp.zeros_like(dsh_ref)
            dsc_ref[...] = jnp.zeros_like(dsc_ref)
            dg_ref[...] = jnp.zeros_like(dg_ref)

        dsh_ref[...] += jnp.sum(dhv, axis=0, keepdims=True)
        dsc_ref[...] += jnp.sum(dhv * (xn * g_ref[...]), axis=0, keepdims=True)
        dg_ref[...] += jnp.sum(dhv * one_sc * xn, axis=0, keepdims=True)

    blk = pl.BlockSpec((rows, d), lambda i: (i, 0))
    vec = pl.BlockSpec((1, d), lambda i: (0, 0))
    vshape = jax.ShapeDtypeStruct((1, d), F32)
    return pl.pallas_call(
        body, name="prenorm_bwd", grid=(s // rows,),
        in_specs=[blk, pl.BlockSpec((rows, 1), lambda i: (i, 0)), blk, blk, vec, vec],
        out_specs=[blk, vec, vec, vec],
        out_shape=[jax.ShapeDtypeStruct((s, d), F32), vshape, vshape, vshape],
        compiler_params=_params(("arbitrary",)),
    )(x, r, dh, dx2, norm_g, scale)


def _pad_rows(a, rows):
    return jnp.pad(a, ((0, rows - a.shape[0]), (0, 0)))


def _tile(n, want):
    t = min(n, want)
    while n % t:
        t -= 1
    return t


def kernel(x, c, norm_g, w_ada, b_ada, w_in, conv_a_w, conv_b_w, conv_b_b, ln_b_g, ln_b_b, w_out, final_g, loss_target, m_norm_g, m_w_ada, m_b_ada, m_w_in, m_conv_a_w, m_conv_b_w, m_conv_b_b, m_ln_b_g, m_ln_b_b, m_w_out, m_final_g, v_norm_g, v_w_ada, v_b_ada, v_w_in, v_conv_a_w, v_conv_b_w, v_conv_b_b, v_ln_b_g, v_ln_b_b, v_w_out, v_final_g):
    s, d = x.shape[1], x.shape[2]
    wa = conv_b_b.shape[-1]
    dmix = 2 * wa
    ns = w_in.shape[-1]
    din = N_CHIPS * ns
    r4 = w_out.shape[1]
    na = w_ada.shape[-1]
    wsh = conv_a_w.shape[-1]
    px, py, pc = _position()
    chip = 2 * px + py
    me = 4 * px + 2 * py + pc
    pos = jnp.stack([chip, pc]).astype(jnp.int32)
    x2d = x.reshape(s, d)
    target = loss_target.reshape(s, d)

    hc, ho, hrow = ns // 2, r4 // 2, d // 2
    _, cidx = _other_chips(px, py)
    win2 = _cast_halves(w_in[0], _tile(d, 512), "cast_w_in")
    wout_bf = _cast_bf16(w_out[0], _tile(r4, 512), "cast_w_out")

    def gather_plan(b):
        x, y, cc = _position()
        chips, _ = _other_chips(x, y)
        return ([(b[0].at[cc], b[2 + k], (cx, cy, cc)) for k, (cx, cy) in enumerate(chips)]
                + [(b[1].at[pl.ds(cc * ho, ho), :], b[5 + k], (cx, cy, cc)) for k, (cx, cy) in enumerate(chips)])

    def gather_sent(b):
        return [(src, src, dev) for src, _, dev in gather_plan(list(b) + [None] * 6)]

    whole = _to_sibling([lambda ref, cc: ref])

    c8 = jnp.broadcast_to(c, (8, d))
    cw = jnp.concatenate([_pad_rows(conv_a_w[0], HALO_A), _pad_rows(conv_b_w[0], HALO_B)], axis=0)
    c_all, cw_all = _gather_cond(c8, cw)
    c_rows = c_all[:, 0, :]
    cw_full = jnp.transpose(cw_all, (1, 0, 2)).reshape(HALO_A + HALO_B, wa)
    conv_a_full, conv_b_full = cw_full[:HALO_A], cw_full[HALO_A:]

    b_ada_sh = lax.dynamic_slice(b_ada, (0, chip * na), (1, na))
    mod_part = _modulation(_pad_rows(c_rows, 2 * N_DEV), w_ada[0], b_ada_sh, _tile(na, 512), "modulation")[:N_DEV]
    mod_all = _exchange_mod(mod_part)
    mod = lax.dynamic_index_in_dim(mod_all, me, axis=1, keepdims=False).reshape(1, 3 * d)
    shift, scale, gate = mod[:, :d], mod[:, d:2 * d], mod[:, 2 * d:]

    g_sems, g_bufs, g_tok = _start_copies(
        "gather_start", gather_plan, 6,
        [win2, wout_bf] + [lax.empty((d, hc), BF16) for _ in range(3)] + [lax.empty((ho, d), BF16) for _ in range(3)],
        after=[mod_all])
    win2, wout_bf, li, lo = g_bufs[0], g_bufs[1], g_bufs[2:5], g_bufs[5:8]

    h, r = _prenorm(x2d, norm_g, scale, shift + g_tok[0, 0], _tile(s, 256))
    bm = _tile(s, 1024)
    bn = _tile(hc, 896)
    pieces = [None, None]

    def piece(slot, half, w, name, which=None):
        where = jnp.reshape(2 * slot + half, (1,)).astype(jnp.int32)
        pieces[:] = _proj_piece(where, h, w, pieces[0], pieces[1], din, 2 * N_CHIPS, _tile(s, 512), bn, name,
                                half=which)
        return pieces[0]

    def arrive(k, after):
        got, = _wait_copies(f"gather_wait{k}", _landed, [li[k]], g_sems[2 * k:2 * k + 2], after=after, send=False)
        sems, (got, passed), _ = _start_copies(f"pass_on{k}", whole, 1, [got, lax.empty((d, hc), BF16)])
        return got, passed, sems

    proj = piece(chip, 0, win2, "proj_own0", which=0)
    proj = piece(chip, 1, win2, "proj_own1", which=1)
    li0, ld0, f0 = arrive(0, [proj])
    proj = piece(cidx[0], pc, li0, "proj_0a")
    li1, ld1, f1 = arrive(1, [proj])
    proj = piece(cidx[1], pc, li1, "proj_1a")
    li0, ld0 = _wait_copies("pass_wait0", whole, [li0, ld0], f0, after=[proj])
    proj = piece(cidx[0], 1 - pc, ld0, "proj_0b")
    li1, ld1 = _wait_copies("pass_wait1", whole, [li1, ld1], f1, after=[proj])
    proj = piece(cidx[1], 1 - pc, ld1, "proj_1b")
    li2, ld2, f2 = arrive(2, [proj])
    proj = piece(cidx[2], pc, li2, "proj_2a")
    li2, ld2 = _wait_copies("pass_wait2", whole, [li2, ld2], f2, after=[proj])
    proj = piece(cidx[2], 1 - pc, ld2, "proj_2b")
    win_full = pieces[1]

    lo = _wait_copies("gather_wait_out", _landed, lo, g_sems[6:12], after=[proj], send=False)
    o_sems, o_bufs, o_tok = _start_copies(
        "pass_on_out", _to_sibling([lambda ref, cc: ref] * 3), 3,
        [b for k in range(3) for b in (lo[k], lax.empty((ho, d), BF16))])
    win2, wout_bf = _wait_copies("gather_wait_sent", gather_sent, [win2, wout_bf], g_sems, after=[o_tok], recv=False)

    def slot_index(k, chip_, cc, others):
        if k == 0:
            return pl.ds(2 * chip_, 2)
        return 2 * others[(k - 1) % 3] + (cc if k <= 3 else 1 - cc)

    y = _mixer_a_fwd(proj, conv_a_full, wa, _tile(s, 512), _tile(wa, 512))
    u0, u = _mixer_b_conv_fwd(proj, conv_b_full, conv_b_b, wa, _tile(s, 512), _tile(wa, 256), 64)
    y = _mixer_b_gate_fwd(y, u, proj, ln_b_g, ln_b_b, wa, _tile(s, 256))
    o_bufs = _wait_copies("pass_wait_out", _to_sibling([lambda ref, cc: ref] * 3), o_bufs, o_sems, after=[y])
    wout_full = _assemble("assemble_w_out", [wout_bf.reshape(2, ho, d)] + o_bufs[0::2] + o_bufs[1::2],
                          jax.ShapeDtypeStruct((2 * N_CHIPS, ho, d), BF16), slot_index)
    wout2d = wout_full.reshape(dmix, d)
    bd = _tile(d, 1024)
    o = _matmul(
        y, wout2d, grid=(s // bm, d // bd, 1),
        a_spec=pl.BlockSpec((bm, dmix), lambda i, j, k: (i, 0)),
        b_spec=pl.BlockSpec((dmix, bd), lambda i, j, k: (0, j)),
        o_spec=pl.BlockSpec((bm, bd), lambda i, j, k: (i, j)),
        out_shape=jax.ShapeDtypeStruct((s, d), F32), dims=((1,), (0,)), name="out_proj")
    dx2, do, loss_p, gfg_p, dgate_p = _loss_head(x2d, o, target, gate, final_g.reshape(1, d), _tile(s, 128))

    be = _tile(dmix, 1024)
    g_wout = _matmul(
        y, do, grid=(dmix // be, d // bd, 1),
        a_spec=pl.BlockSpec((s, be), lambda i, j, k: (0, i)),
        b_spec=pl.BlockSpec((s, bd), lambda i, j, k: (0, j)),
        o_spec=pl.BlockSpec((be, bd), lambda i, j, k: (i, j)),
        out_shape=jax.ShapeDtypeStruct((dmix, d), F32), dims=((0,), (0,)), name="grad_w_out")
    swap_out = _to_sibling([lambda ref, cc: ref.at[:, pl.ds((1 - cc) * ho, ho), :]])
    so_sems, (g_wout3, ra_out), so_tok = _start_copies(
        "swap_out_start", swap_out, 1, [g_wout.reshape(N_CHIPS, r4, d), lax.empty((N_CHIPS, ho, d), F32)])
    dy = _matmul(
        do, wout2d, grid=(s // bm, dmix // be, 1),
        a_spec=pl.BlockSpec((bm, d), lambda i, j, k: (i, 0)),
        b_spec=pl.BlockSpec((be, d), lambda i, j, k: (j, 0)),
        o_spec=pl.BlockSpec((bm, be), lambda i, j, k: (i, j)),
        out_shape=jax.ShapeDtypeStruct((s, dmix), F32), dims=((1,), (1,)), name="dy", after=[so_tok])
    g_wout3, ra_out = _wait_copies("swap_out_wait", swap_out, [g_wout3, ra_out], so_sems, after=[dy])
    q_out = _chip_partial(pos, g_wout3, ra_out, _tile(ho, 256), "chip_partial_w_out")
    po_sems, po_bufs, po_tok = _start_copies(
        "send_out_start", _slots_to_chips, 3, [q_out] + [lax.empty((ho, d), BF16) for _ in range(3)])
    dproj, dwa_p = _mixer_a_bwd(proj, dy, conv_a_full + po_tok[0, 0], wa, din, _tile(s, 128))
    dproj, du, dlng_p, dlnb_p, dcb_p = _mixer_b_gate_bwd(dproj, dy, u, proj, ln_b_g, ln_b_b, wa, _tile(s, 128))
    dproj, dwb_p = _mixer_b_conv_bwd(dproj, du, u0, proj, conv_b_full, wa, _tile(s, 256), _tile(wa, 256), 64)

    bn2 = _tile(ns, 896)
    swap_in = _to_sibling([lambda ref, cc: ref.at[pl.ds((1 - cc) * hrow, hrow), :]])
    slots = [cidx[0], cidx[1], cidx[2], chip]
    core_only = jnp.stack([0 * pc, pc]).astype(jnp.int32)
    g, ra, sw, q, rb, snd = [None] * 4, [None] * 4, [None] * 4, [None] * 3, [None] * 3, [None] * 3
    after = []
    for k in range(4):
        g[k] = _grad_slot(jnp.reshape(slots[k], (1,)).astype(jnp.int32), h, dproj, after, ns, bd, bn2,
                          f"grad_w_in{k}")
        sw[k], (g[k], ra[k]), tok = _start_copies(f"swap_in_start{k}", swap_in, 1,
                                                  [g[k], lax.empty((hrow, ns), F32)])
        after = [tok]
        if k >= 1:
            j = k - 1
            g[j], ra[j] = _wait_copies(f"swap_in_wait{j}", swap_in, [g[j], ra[j]], sw[j], after=[g[k]])
            part = _chip_partial(core_only, g[j][None], ra[j][None], _tile(hrow, 256), f"chip_partial_w_in{j}")
            snd[j], (q[j], rb[j]), tok2 = _start_copies(f"send_in_start{j}", _to_chip(j), 1,
                                                        [part.reshape(hrow, ns), lax.empty((hrow, ns), BF16)])
            after = [tok, tok2]
    dh = _matmul(
        dproj, win_full, grid=(s // bm, d // bd, 2 * N_CHIPS),
        a_spec=pl.BlockSpec((bm, hc), lambda i, j, k: (i, k)),
        b_spec=pl.BlockSpec((None, bd, hc), lambda i, j, k: (k, j, 0)),
        o_spec=pl.BlockSpec((bm, bd), lambda i, j, k: (i, j)),
        out_shape=jax.ShapeDtypeStruct((s, d), F32), dims=((1,), (1,)), name="dh", after=after)
    grad_x, dshift_p, dscale_p, gng_p = _prenorm_bwd(x2d, r, dh, dx2, norm_g, scale, _tile(s, 128))

    def rows_of(v):
        return _pad_rows(v.reshape(-1, wa), 8 * ((v.size // wa + 7) // 8))

    dmod = jnp.concatenate([dshift_p, dscale_p, dgate_p], axis=1)
    parts = [gng_p, dmod, dwa_p, dwb_p, dcb_p, dlng_p, dlnb_p, gfg_p,
             jnp.broadcast_to(loss_p[:, :1], (1, wa))]
    starts, packed = [], []
    for p in parts:
        starts.append(sum(q.shape[0] for q in packed))
        packed.append(rows_of(p) if p.shape[0] == 1 else p)
    small_sum, small_all = _gather_small(jnp.concatenate(packed, axis=0))

    def summed(k, rows):
        return small_sum[starts[k]:starts[k] + rows]

    grad_norm_g = summed(0, d // wa).reshape(1, d)
    grad_b_ada = summed(1, 3 * d // wa).reshape(1, 3 * d)
    grad_conv_a_full = summed(2, TAPS_A)
    grad_conv_b_full = summed(3, TAPS_B)
    grad_conv_b_b = summed(4, 1)
    grad_ln_b_g = summed(5, 1)
    grad_ln_b_b = summed(6, 1)
    grad_final_g = summed(7, d // wa).reshape(d)
    loss = summed(8, 1)[0, 0]
    grad_conv_a_w = lax.dynamic_slice(grad_conv_a_full, (0, chip * wsh), (TAPS_A, wsh))
    grad_conv_b_w = lax.dynamic_slice(grad_conv_b_full, (0, chip * wsh), (TAPS_B, wsh))
    dmod_all = small_all[:, starts[1]:starts[1] + 3 * d // wa, :].reshape(N_DEV, 3 * d)
    dmod_sh = lax.dynamic_slice(dmod_all, (0, chip * na), (N_DEV, na))

    def pairs_to_chips(b):
        x, y, cc = _position()
        chips, _ = _other_chips(x, y)
        return [(b[2 * k], b[2 * k + 1], (cx, cy, cc)) for k, (cx, cy) in enumerate(chips)]

    po_bufs = _wait_copies("send_out_wait", _slots_to_chips, po_bufs, po_sems, after=[small_sum])
    gh_out = _final_half(pos, g_wout3, ra_out, po_bufs[1:], _tile(ho, 256), "final_half_w_out")
    g[3], ra[3] = _wait_copies("swap_in_wait3", swap_in, [g[3], ra[3]], sw[3], after=[small_sum])
    in_bufs = _wait_copies("send_in_wait", pairs_to_chips, [b for k in range(3) for b in (q[k], rb[k])],
                           snd[0] + snd[1] + snd[2], after=[small_sum])
    gh_in = _final_half(core_only, g[3][None], ra[3][None], in_bufs[1::2], _tile(hrow, 256), "final_half_w_in")
    sh_sems, sh_bufs, sh_tok = _start_copies("share_start", _halves_to_sibling, 2, [gh_in, gh_out])

    grad_w_ada, d_wada, nm_wada, nv_wada = _adam_ada(c_rows.T, dmod_sh + sh_tok[0, 0], w_ada[0], m_w_ada[0],
                                                     v_w_ada[0], _tile(d, 128), "adam_w_ada")
    grad_w_in, grad_w_out = _wait_copies("share_wait", _halves_to_sibling, sh_bufs, sh_sems, after=[d_wada])
    d_win, nm_win, nv_win = _adam(w_in[0], grad_w_in, m_w_in[0], v_w_in[0], _tile(d, 128), "adam_w_in")
    d_wout, nm_wout, nv_wout = _adam(w_out[0], grad_w_out, m_w_out[0], v_w_out[0], _tile(r4, 128), "adam_w_out")

    def small_adam(w, g, m, v, name):
        shape = w.shape
        w2 = w.reshape(-1, shape[-1])
        out = _adam(w2, g.reshape(w2.shape), m.reshape(w2.shape), v.reshape(w2.shape), w2.shape[0], name)
        return [o_.reshape(shape) for o_ in out]

    small = {
        "norm_g": small_adam(norm_g, grad_norm_g, m_norm_g, v_norm_g, "adam_norm_g"),
        "b_ada": small_adam(b_ada, grad_b_ada, m_b_ada, v_b_ada, "adam_b_ada"),
        "conv_a_w": small_adam(conv_a_w, grad_conv_a_w, m_conv_a_w, v_conv_a_w, "adam_conv_a_w"),
        "conv_b_w": small_adam(conv_b_w, grad_conv_b_w, m_conv_b_w, v_conv_b_w, "adam_conv_b_w"),
        "conv_b_b": small_adam(conv_b_b, grad_conv_b_b, m_conv_b_b, v_conv_b_b, "adam_conv_b_b"),
        "ln_b_g": small_adam(ln_b_g, grad_ln_b_g, m_ln_b_g, v_ln_b_g, "adam_ln_b_g"),
        "ln_b_b": small_adam(ln_b_b, grad_ln_b_b, m_ln_b_b, v_ln_b_b, "adam_ln_b_b"),
        "final_g": small_adam(final_g.reshape(1, d), grad_final_g, m_final_g.reshape(1, d),
                              v_final_g.reshape(1, d), "adam_final_g"),
    }
    small["final_g"] = [o_.reshape(d) for o_ in small["final_g"]]
    big = {
        "w_ada": [a[None] for a in (d_wada, nm_wada, nv_wada)],
        "w_in": [a[None] for a in (d_win, nm_win, nv_win)],
        "w_out": [a[None] for a in (d_wout, nm_wout, nv_wout)],
    }
    upd = {**small, **big}
    order = ["norm_g", "w_ada", "b_ada", "w_in", "conv_a_w", "conv_b_w", "conv_b_b", "ln_b_g", "ln_b_b",
             "w_out", "final_g"]
    grads = {
        "norm_g": grad_norm_g, "w_ada": grad_w_ada[None], "b_ada": grad_b_ada, "w_in": grad_w_in[None],
        "conv_a_w": grad_conv_a_w[None], "conv_b_w": grad_conv_b_w[None], "conv_b_b": grad_conv_b_b,
        "ln_b_g": grad_ln_b_g, "ln_b_b": grad_ln_b_b, "w_out": grad_w_out[None], "final_g": grad_final_g,
    }
    return (loss, grad_x.reshape(1, s, d), *[grads[n] for n in order], *[upd[n][0] for n in order],
            *[upd[n][1] for n in order], *[upd[n][2] for n in order])
```

```python
import functools

import jax
import jax.numpy as jnp
from jax import lax
from jax.experimental import pallas as pl
from jax.experimental.pallas import tpu as pltpu

F32 = jnp.float32
BF16 = jnp.bfloat16
EPS = 1e-6
N_CHIPS = 4
N_DEV = 8
TAPS_A = 3
TAPS_B = 31
HALO_A = 8
HALO_B = 32
LANES = 128
SUBLANES = 8
ADAM_LR = 0.001
ADAM_B1 = 0.9
ADAM_B2 = 0.999
ADAM_EPS = 1e-08
ADAM_WD = 0.01
ADAM_STEP = 10
VMEM_LIMIT = 56 * 1024 * 1024
MESH = pl.DeviceIdType.MESH
ANY = pl.BlockSpec(memory_space=pl.ANY)
VMEM = pl.BlockSpec(memory_space=pltpu.VMEM)
HBM_SPEC = pl.BlockSpec(memory_space=pltpu.HBM)
SEM_SPEC = pl.BlockSpec(memory_space=pltpu.SEMAPHORE)
EFFECT = pltpu.SideEffectType.DATAFLOW_SIDE_EFFECTING


def _params(sem=None):
    return pltpu.CompilerParams(dimension_semantics=sem, vmem_limit_bytes=VMEM_LIMIT)


def _sigmoid(v):
    return jax.nn.sigmoid(v)


def _position():
    return lax.axis_index("x"), lax.axis_index("y"), lax.axis_index("c")


def _rcopy(src, dst, ssem, rsem, dev):
    return pltpu.make_async_remote_copy(src_ref=src, dst_ref=dst, send_sem=ssem, recv_sem=rsem,
                                        device_id=dev, device_id_type=MESH)


def _other_chips(x, y):
    chips = [(1 - x, y), (x, 1 - y), (1 - x, 1 - y)]
    return chips, [2 * cx + cy for cx, cy in chips]


def _cast_bf16(a, rows, name):
    m, n = a.shape

    def body(a_ref, o_ref):
        o_ref[...] = a_ref[...].astype(BF16)

    return pl.pallas_call(
        body, name=name, grid=(m // rows,),
        in_specs=[pl.BlockSpec((rows, n), lambda i: (i, 0))],
        out_specs=pl.BlockSpec((rows, n), lambda i: (i, 0)),
        out_shape=jax.ShapeDtypeStruct((m, n), BF16),
        compiler_params=_params(("parallel",)),
    )(a)


def _cast_halves(a, rows, name):
    m, n = a.shape
    hc = n // 2

    def body(a_ref, o_ref):
        o_ref[...] = a_ref[...].astype(BF16)

    return pl.pallas_call(
        body, name=name, grid=(2, m // rows),
        in_specs=[pl.BlockSpec((rows, hc), lambda hf, i: (i, hf))],
        out_specs=pl.BlockSpec((None, rows, hc), lambda hf, i: (hf, i, 0)),
        out_shape=jax.ShapeDtypeStruct((2, m, hc), BF16),
        compiler_params=_params(("parallel", "parallel")),
    )(a)


def _proj_piece(where, h, w, proj, w_all, din, n_pieces, bm, bn, name, half=None):
    s, d = h.shape
    n = w.shape[-1]
    nj = n // bn
    if half is None:
        w_spec = pl.BlockSpec((d, bn), lambda j, i, p: (0, j), pipeline_mode=pl.Buffered(1))
    else:
        w_spec = pl.BlockSpec((None, d, bn), lambda j, i, p: (half, 0, j), pipeline_mode=pl.Buffered(1))

    def body(p_ref, h_ref, w_ref, *rest):
        o_ref, wall_ref = rest[-2], rest[-1]
        o_ref[...] = jnp.dot(h_ref[...], w_ref[...], preferred_element_type=F32)

        @pl.when(pl.program_id(1) == 0)
        def _():
            wall_ref[...] = w_ref[...]

    args, extra, alias = [where, h, w], [], {}
    if proj is not None:
        args, extra, alias = args + [proj, w_all], [ANY, ANY], {3: 0, 4: 1}
    return pl.pallas_call(
        body, name=name,
        grid_spec=pltpu.PrefetchScalarGridSpec(
            num_scalar_prefetch=1, grid=(nj, s // bm),
            in_specs=[pl.BlockSpec((bm, d), lambda j, i, p: (i, 0)), w_spec] + extra,
            out_specs=[pl.BlockSpec((bm, bn), lambda j, i, p: (i, p[0] * nj + j)),
                       pl.BlockSpec((None, d, bn), lambda j, i, p: (p[0], 0, j))]),
        out_shape=[jax.ShapeDtypeStruct((s, din), F32), jax.ShapeDtypeStruct((n_pieces, d, n), BF16)],
        input_output_aliases=alias,
        compiler_params=_params(("parallel", "arbitrary")),
    )(*args)


def _grad_slot(slot, h, dproj, after, ns, bd, bn, name):
    s, d = h.shape
    nb = ns // bn

    def body(slot_ref, h_ref, dp_ref, *rest):
        rest[-1][...] = lax.dot_general(h_ref[...], dp_ref[...], (((0,), (0,)), ((), ())),
                                        preferred_element_type=F32)

    return pl.pallas_call(
        body, name=name,
        grid_spec=pltpu.PrefetchScalarGridSpec(
            num_scalar_prefetch=1, grid=(d // bd, nb),
            in_specs=[pl.BlockSpec((s, bd), lambda i, j, sl: (0, i)),
                      pl.BlockSpec((s, bn), lambda i, j, sl: (0, sl[0] * nb + j))] + [ANY] * len(after),
            out_specs=pl.BlockSpec((bd, bn), lambda i, j, sl: (i, j))),
        out_shape=jax.ShapeDtypeStruct((d, ns), F32),
        compiler_params=_params(("parallel", "parallel")),
    )(slot, h, dproj, *after)


def _matmul(a, b, *, grid, a_spec, b_spec, o_spec, out_shape, dims, name, after=()):
    nk = grid[2]
    n_after = len(after)

    def body(a_ref, b_ref, *rest):
        o_ref, acc = rest[n_after], rest[n_after + 1:]
        p = lax.dot_general(a_ref[...], b_ref[...], (dims, ((), ())), preferred_element_type=F32)
        if nk == 1:
            o_ref[...] = p.astype(o_ref.dtype)
        else:
            acc_ref, = acc
            k = pl.program_id(2)

            @pl.when(k == 0)
            def _():
                acc_ref[...] = p

            @pl.when(k > 0)
            def _():
                acc_ref[...] += p

            @pl.when(k == nk - 1)
            def _():
                o_ref[...] = acc_ref[...].astype(o_ref.dtype)

    block = [d for d in o_spec.block_shape if d is not None]
    scratch = [pltpu.VMEM(tuple(block), F32)] if nk > 1 else []
    return pl.pallas_call(
        body, name=name, grid=grid, in_specs=[a_spec, b_spec] + [ANY] * n_after, out_specs=o_spec,
        out_shape=out_shape, scratch_shapes=scratch,
        compiler_params=_params(("parallel", "parallel", "arbitrary")),
    )(a, b, *after)


def _adam_math(w, g, m, v):
    m = ADAM_B1 * m + (1.0 - ADAM_B1) * g
    v = ADAM_B2 * v + (1.0 - ADAM_B2) * (g * g)
    m_hat = m / (1.0 - ADAM_B1 ** ADAM_STEP)
    v_hat = v / (1.0 - ADAM_B2 ** ADAM_STEP)
    delta = -ADAM_LR * (m_hat / (jnp.sqrt(v_hat) + ADAM_EPS) + ADAM_WD * w)
    return delta, m, v


def _adam(w, g, m, v, rows, name):
    r, n = w.shape

    def body(w_ref, g_ref, m_ref, v_ref, d_ref, mo_ref, vo_ref):
        d, mo, vo = _adam_math(w_ref[...], g_ref[...], m_ref[...], v_ref[...])
        d_ref[...] = d
        mo_ref[...] = mo
        vo_ref[...] = vo

    spec = pl.BlockSpec((rows, n), lambda i: (i, 0))
    shape = jax.ShapeDtypeStruct((r, n), F32)
    return pl.pallas_call(
        body, name=name, grid=(r // rows,), in_specs=[spec] * 4, out_specs=[spec] * 3,
        out_shape=[shape] * 3, compiler_params=_params(("parallel",)),
    )(w, g, m, v)


def _adam_ada(c_cols, dmod, w, m, v, rows, name):
    r, n = w.shape

    def body(c_ref, dm_ref, w_ref, m_ref, v_ref, g_ref, d_ref, mo_ref, vo_ref):
        cv = c_ref[...]
        c_act = cv * _sigmoid(cv)
        g = c_act[:, 0:1] * dm_ref[0:1, :]
        for b in range(1, N_DEV):
            g = g + c_act[:, b:b + 1] * dm_ref[b:b + 1, :]
        d, mo, vo = _adam_math(w_ref[...], g, m_ref[...], v_ref[...])
        g_ref[...] = g
        d_ref[...] = d
        mo_ref[...] = mo
        vo_ref[...] = vo

    spec = pl.BlockSpec((rows, n), lambda i: (i, 0))
    shape = jax.ShapeDtypeStruct((r, n), F32)
    return pl.pallas_call(
        body, name=name, grid=(r // rows,),
        in_specs=[pl.BlockSpec((rows, N_DEV), lambda i: (i, 0)), pl.BlockSpec((N_DEV, n), lambda i: (0, 0)),
                  spec, spec, spec],
        out_specs=[spec] * 4, out_shape=[shape] * 4, compiler_params=_params(("parallel",)),
    )(c_cols, dmod, w, m, v)


def _start_copies(name, plan, n, bufs, after=()):
    nb, na = len(bufs), len(after)

    def body(*refs):
        sems = refs[nb + na:nb + na + 2 * n]
        for k, (src, dst, dev) in enumerate(plan(refs[:nb])):
            _rcopy(src, dst, sems[2 * k], sems[2 * k + 1], dev).start()
        refs[-1][...] = jnp.zeros((8, 128), F32)

    outs = pl.pallas_call(
        body, name=name,
        out_shape=[pltpu.SemaphoreType.DMA(())] * (2 * n) + [pltpu.HBM(a.shape, a.dtype) for a in bufs]
        + [jax.ShapeDtypeStruct((8, 128), F32)],
        in_specs=[HBM_SPEC] * nb + [ANY] * na, out_specs=[SEM_SPEC] * (2 * n) + [HBM_SPEC] * nb + [VMEM],
        input_output_aliases={i: 2 * n + i for i in range(nb)},
        compiler_params=pltpu.CompilerParams(has_side_effects=EFFECT),
    )(*[pltpu.with_memory_space_constraint(a, pltpu.HBM) for a in bufs], *after)
    return list(outs[:2 * n]), list(outs[2 * n:2 * n + nb]), outs[-1]


def _wait_copies(name, plan, bufs, sems, after=(), send=True, recv=True):
    nb, nsem = len(bufs), len(sems)

    def body(*refs):
        s = refs[nb:nb + nsem]
        for k, (src, dst, dev) in enumerate(plan(refs[:nb])):
            cp = _rcopy(src, dst, s[2 * k], s[2 * k + 1], dev)
            if send:
                cp.wait_send()
            if recv:
                cp.wait_recv()

    outs = pl.pallas_call(
        body, name=name, out_shape=[pltpu.HBM(a.shape, a.dtype) for a in bufs],
        in_specs=[HBM_SPEC] * nb + [SEM_SPEC] * nsem + [ANY] * len(after), out_specs=[HBM_SPEC] * nb,
        input_output_aliases={i: i for i in range(nb)},
        compiler_params=pltpu.CompilerParams(has_side_effects=EFFECT),
    )(*bufs, *sems, *after)
    return list(outs)


def _to_sibling(views):
    def plan(b):
        x, y, c = _position()
        return [(view(b[2 * k], c), b[2 * k + 1], (x, y, 1 - c)) for k, view in enumerate(views)]
    return plan


def _to_chip(k):
    def plan(b):
        x, y, c = _position()
        cx, cy = _other_chips(x, y)[0][k]
        return [(b[0], b[1], (cx, cy, c))]
    return plan


def _slots_to_chips(b):
    x, y, c = _position()
    chips, cidx = _other_chips(x, y)
    return [(b[0].at[cidx[k]], b[1 + k], (cx, cy, c)) for k, (cx, cy) in enumerate(chips)]


def _halves_to_sibling(b):
    x, y, c = _position()
    views = [r.at[pl.ds(c * (r.shape[0] // 2), r.shape[0] // 2), :] for r in b]
    return [(v, v, (x, y, 1 - c)) for v in views]


def _landed(b):
    x, y, c = _position()
    return [(ref, ref, (x, y, c)) for ref in b]


def _assemble(name, pieces, out_shape, index_of):
    n = len(pieces)

    def body(*refs):
        out_ref, sem = refs[n], refs[n + 1]
        x, y, c = _position()
        _, cidx = _other_chips(x, y)
        cps = [pltpu.make_async_copy(refs[k], out_ref.at[index_of(k, 2 * x + y, c, cidx)], sem.at[k]) for k in range(n)]
        for cp in cps:
            cp.start()
        for cp in cps:
            cp.wait()

    return pl.pallas_call(
        body, name=name, in_specs=[VMEM] * n, out_specs=ANY, out_shape=out_shape,
        scratch_shapes=[pltpu.SemaphoreType.DMA((n,))],
        compiler_params=pltpu.CompilerParams(vmem_limit_bytes=VMEM_LIMIT),
    )(*pieces)


def _gather_cond(c8, cw):
    def body(c8_ref, cw_ref, call_ref, cwall_ref, ssem, rsem, lsem):
        x, y, c = _position()
        chip = 2 * x + y
        me = 4 * x + 2 * y + c
        chips, cidx = _other_chips(x, y)
        own = [pltpu.make_async_copy(c8_ref, call_ref.at[me], lsem.at[0]),
               pltpu.make_async_copy(cw_ref, cwall_ref.at[chip], lsem.at[1])]
        for cp in own:
            cp.start()
        sends = [_rcopy(cw_ref, cwall_ref.at[chip], ssem.at[k], rsem.at[k], (cx, cy, c))
                 for k, (cx, cy) in enumerate(chips)]
        for mask in range(1, N_DEV):
            fx, fy, fc = (mask >> 2) & 1, (mask >> 1) & 1, mask & 1
            dev = (1 - x if fx else x, 1 - y if fy else y, 1 - c if fc else c)
            sends.append(_rcopy(c8_ref, call_ref.at[me], ssem.at[2 + mask], rsem.at[2 + mask], dev))
        for cp in sends:
            cp.start()
        for k in range(3):
            slot = cwall_ref.at[cidx[k]]
            _rcopy(slot, slot, ssem.at[k], rsem.at[k], (x, y, c)).wait_recv()
        for mask in range(1, N_DEV):
            slot = call_ref.at[jnp.bitwise_xor(me, mask)]
            _rcopy(slot, slot, ssem.at[2 + mask], rsem.at[2 + mask], (x, y, c)).wait_recv()
        for cp in sends:
            cp.wait_send()
        for cp in own:
            cp.wait()

    return pl.pallas_call(
        body, name="gather_cond", in_specs=[VMEM, VMEM], out_specs=[VMEM, VMEM],
        out_shape=[jax.ShapeDtypeStruct((N_DEV,) + c8.shape, F32), jax.ShapeDtypeStruct((N_CHIPS,) + cw.shape, F32)],
        scratch_shapes=[pltpu.SemaphoreType.DMA((10,)), pltpu.SemaphoreType.DMA((10,)), pltpu.SemaphoreType.DMA((2,))],
    )(c8, cw)


def _exchange_mod(mod_part):
    def body(mp_ref, out_ref, ssem, rsem, lsem):
        x, y, c = _position()
        chip = 2 * x + y
        chips, cidx = _other_chips(x, y)
        own = pltpu.make_async_copy(mp_ref, out_ref.at[chip], lsem)
        own.start()
        sends = [_rcopy(mp_ref, out_ref.at[chip], ssem.at[k], rsem.at[k], (cx, cy, c))
                 for k, (cx, cy) in enumerate(chips)]
        for cp in sends:
            cp.start()
        for k in range(3):
            slot = out_ref.at[cidx[k]]
            _rcopy(slot, slot, ssem.at[k], rsem.at[k], (x, y, c)).wait_recv()
        for cp in sends:
            cp.wait_send()
        own.wait()

    return pl.pallas_call(
        body, name="exchange_mod", in_specs=[VMEM], out_specs=VMEM,
        out_shape=jax.ShapeDtypeStruct((N_CHIPS,) + mod_part.shape, F32),
        scratch_shapes=[pltpu.SemaphoreType.DMA((3,)), pltpu.SemaphoreType.DMA((3,)), pltpu.SemaphoreType.DMA],
    )(mod_part)


def _gather_small(pack):
    rows, n = pack.shape

    def body(p_ref, sum_ref, all_ref, ssem, rsem, lsem):
        x, y, c = _position()
        me = 4 * x + 2 * y + c
        own = pltpu.make_async_copy(p_ref, all_ref.at[me], lsem)
        own.start()
        sends = []
        for mask in range(1, N_DEV):
            fx, fy, fc = (mask >> 2) & 1, (mask >> 1) & 1, mask & 1
            dev = (1 - x if fx else x, 1 - y if fy else y, 1 - c if fc else c)
            sends.append(_rcopy(p_ref, all_ref.at[me], ssem.at[mask - 1], rsem.at[mask - 1], dev))
        for cp in sends:
            cp.start()
        for mask in range(1, N_DEV):
            slot = all_ref.at[jnp.bitwise_xor(me, mask)]
            _rcopy(slot, slot, ssem.at[mask - 1], rsem.at[mask - 1], (x, y, c)).wait_recv()
        for cp in sends:
            cp.wait_send()
        own.wait()
        acc = all_ref[0]
        for k in range(1, N_DEV):
            acc = acc + all_ref[k]
        sum_ref[...] = acc

    return pl.pallas_call(
        body, name="gather_small", in_specs=[VMEM], out_specs=[VMEM, VMEM],
        out_shape=[jax.ShapeDtypeStruct((rows, n), F32), jax.ShapeDtypeStruct((N_DEV, rows, n), F32)],
        scratch_shapes=[pltpu.SemaphoreType.DMA((7,)), pltpu.SemaphoreType.DMA((7,)), pltpu.SemaphoreType.DMA],
        compiler_params=pltpu.CompilerParams(vmem_limit_bytes=VMEM_LIMIT),
    )(pack)


def _chip_partial(pos, g, recv, rows, name):
    ns, full, n = g.shape
    h = full // 2
    nb = h // rows

    def body(pos_ref, g_ref, r_ref, o_ref):
        o_ref[...] = (g_ref[...] + r_ref[...]).astype(BF16)

    return pl.pallas_call(
        body, name=name,
        grid_spec=pltpu.PrefetchScalarGridSpec(
            num_scalar_prefetch=1, grid=(ns, nb),
            in_specs=[pl.BlockSpec((None, rows, n), lambda s, i, p: (s, p[1] * nb + i, 0)),
                      pl.BlockSpec((None, rows, n), lambda s, i, p: (s, i, 0))],
            out_specs=pl.BlockSpec((None, rows, n), lambda s, i, p: (s, i, 0))),
        out_shape=jax.ShapeDtypeStruct((ns, h, n), BF16),
        compiler_params=_params(("parallel", "parallel")),
    )(pos, g, recv)


def _final_half(pos, g, recv_a, recv_b, rows, name):
    ns, full, n = g.shape
    h = full // 2
    nb = h // rows

    def body(pos_ref, g_ref, ra_ref, rb0_ref, rb1_ref, rb2_ref, o_ref):
        acc = g_ref[...] + ra_ref[...]
        for rb_ref in (rb0_ref, rb1_ref, rb2_ref):
            acc = acc + rb_ref[...].astype(F32)
        o_ref[...] = acc

    part = pl.BlockSpec((rows, n), lambda i, p: (i, 0))
    return pl.pallas_call(
        body, name=name,
        grid_spec=pltpu.PrefetchScalarGridSpec(
            num_scalar_prefetch=1, grid=(nb,),
            in_specs=[pl.BlockSpec((None, rows, n), lambda i, p: (p[0], p[1] * nb + i, 0)),
                      pl.BlockSpec((None, rows, n), lambda i, p: (p[0], i, 0)), part, part, part],
            out_specs=pl.BlockSpec((rows, n), lambda i, p: (p[1] * nb + i, 0))),
        out_shape=jax.ShapeDtypeStruct((full, n), F32),
        compiler_params=_params(("parallel",)),
    )(pos, g, recv_a, *recv_b)


def _modulation(c_rows, w_ada, b_ada, cols, name):
    d, n = w_ada.shape
    rows = c_rows.shape[0]

    def body(c_ref, w_ref, b_ref, o_ref):
        cv = c_ref[...]
        c_act = (cv * _sigmoid(cv)).astype(BF16)
        o_ref[...] = jnp.dot(c_act, w_ref[...].astype(BF16), preferred_element_type=F32) + b_ref[...]

    return pl.pallas_call(
        body, name=name, grid=(n // cols,),
        in_specs=[pl.BlockSpec((rows, d), lambda j: (0, 0)), pl.BlockSpec((d, cols), lambda j: (0, j)),
                  pl.BlockSpec((1, cols), lambda j: (0, j))],
        out_specs=pl.BlockSpec((rows, cols), lambda j: (0, j)),
        out_shape=jax.ShapeDtypeStruct((rows, n), F32),
        compiler_params=_params(("parallel",)),
    )(c_rows, w_ada, b_ada)


def _prenorm(x, norm_g, scale, shift, rows):
    s, d = x.shape

    def body(x_ref, g_ref, sc_ref, sh_ref, h_ref, r_ref):
        xv = x_ref[...]
        r = lax.rsqrt(jnp.mean(xv * xv, axis=-1, keepdims=True) + EPS)
        h = (xv * r * g_ref[...]) * (1.0 + sc_ref[...]) + sh_ref[...]
        h_ref[...] = h.astype(BF16)
        r_ref[...] = r

    vec = pl.BlockSpec((1, d), lambda i: (0, 0))
    return pl.pallas_call(
        body, name="prenorm", grid=(s // rows,),
        in_specs=[pl.BlockSpec((rows, d), lambda i: (i, 0)), vec, vec, vec],
        out_specs=[pl.BlockSpec((rows, d), lambda i: (i, 0)), pl.BlockSpec((rows, 1), lambda i: (i, 0))],
        out_shape=[jax.ShapeDtypeStruct((s, d), BF16), jax.ShapeDtypeStruct((s, 1), F32)],
        compiler_params=_params(("parallel",)),
    )(x, norm_g, scale, shift)


def _mixer_a_fwd(proj, conv_w, wa, rows, cols):
    s = proj.shape[0]
    ncb = wa // cols

    def body(ab_ref, ac_ref, ax_ref, az_ref, w_ref, y_ref, qbuf):
        t = pl.program_id(1)

        @pl.when(t == 0)
        def _():
            qbuf[0:HALO_A, :] = jnp.zeros((HALO_A, cols), F32)

        q = ac_ref[...] * ax_ref[...]
        qbuf[HALO_A:HALO_A + rows, :] = q
        conv = w_ref[2:3, :] * q
        for k in range(TAPS_A - 1):
            off = HALO_A - (TAPS_A - 1) + k
            conv = conv + w_ref[k:k + 1, :] * qbuf[off:off + rows, :]
        zv = az_ref[...]
        y_ref[...] = (ab_ref[...] * conv * (zv * _sigmoid(zv))).astype(BF16)
        qbuf[0:HALO_A, :] = qbuf[rows:rows + HALO_A, :]

    def sec(k):
        return pl.BlockSpec((rows, cols), lambda cb, t, k=k: (t, k * ncb + cb))

    return pl.pallas_call(
        body, name="mixer_a_fwd", grid=(ncb, s // rows),
        in_specs=[sec(0), sec(1), sec(2), sec(3), pl.BlockSpec((HALO_A, cols), lambda cb, t: (0, cb))],
        out_specs=pl.BlockSpec((rows, cols), lambda cb, t: (t, cb)),
        out_shape=jax.ShapeDtypeStruct((s, 2 * wa), BF16),
        scratch_shapes=[pltpu.VMEM((HALO_A + rows, cols), F32)],
        compiler_params=_params(("parallel", "arbitrary")),
    )(proj, proj, proj, proj, conv_w)


def _shifted_back(dst, src, lo, hi, cs):
    for n in range(8):
        dst[n, lo:hi, :] = src[lo - n:hi - n, cs]


def _shifted_fwd(dst, src, lo, hi, cs):
    for n in range(8):
        dst[n, lo:hi, :] = src[lo + n:hi + n, cs]


def _mixer_b_conv_fwd(proj, conv_w, conv_b, wa, rows, cols, chunk):
    s = proj.shape[0]
    wb = conv_w.shape[1]
    ncb = wb // cols
    sec0 = 4 * wa // cols

    def body(bv_ref, bg_ref, w_ref, b_ref, u0_ref, u_ref, ubuf, sh):
        t = pl.program_id(1)

        @pl.when(t == 0)
        def _():
            ubuf[0:HALO_B, :] = jnp.zeros((HALO_B, cols), F32)

        u0 = bv_ref[...] * _sigmoid(bg_ref[...])
        u0_ref[...] = u0
        ubuf[HALO_B:HALO_B + rows, :] = u0
        for lc in range(cols // LANES):
            cs = slice(lc * LANES, (lc + 1) * LANES)
            _shifted_back(sh, ubuf, 8, HALO_B + rows, cs)
            taps = [w_ref[k:k + 1, cs] for k in range(TAPS_B)]
            bias = b_ref[:, cs]

            def row_chunk(rc, carry, cs=cs, taps=taps, bias=bias):
                base = pl.multiple_of(rc * chunk, chunk)
                acc = jnp.zeros((chunk, LANES), F32)
                for k in range(TAPS_B):
                    mq, n = divmod(TAPS_B - 1 - k, 8)
                    acc = acc + taps[k] * sh[n, pl.ds(HALO_B - 8 * mq + base, chunk), :]
                u_ref[pl.ds(base, chunk), cs] = acc + bias
                return carry

            lax.fori_loop(0, rows // chunk, row_chunk, 0)
        ubuf[0:HALO_B, :] = ubuf[rows:rows + HALO_B, :]

    return pl.pallas_call(
        body, name="mixer_b_conv_fwd", grid=(ncb, s // rows),
        in_specs=[pl.BlockSpec((rows, cols), lambda cb, t: (t, sec0 + cb)),
                  pl.BlockSpec((rows, cols), lambda cb, t: (t, sec0 + ncb + cb)),
                  pl.BlockSpec((HALO_B, cols), lambda cb, t: (0, cb)),
                  pl.BlockSpec((1, cols), lambda cb, t: (0, cb))],
        out_specs=[pl.BlockSpec((rows, cols), lambda cb, t: (t, cb))] * 2,
        out_shape=[jax.ShapeDtypeStruct((s, wb), F32)] * 2,
        scratch_shapes=[pltpu.VMEM((HALO_B + rows, cols), F32), pltpu.VMEM((8, HALO_B + rows, LANES), F32)],
        compiler_params=_params(("parallel", "arbitrary")),
    )(proj, proj, conv_w, conv_b)


def _layernorm_stats(u):
    mu = jnp.mean(u, axis=-1, keepdims=True)
    xc = u - mu
    var = jnp.mean(xc * xc, axis=-1, keepdims=True)
    return xc * lax.rsqrt(var + EPS), lax.rsqrt(var + EPS)


def _mixer_b_gate_fwd(y, u, proj, ln_g, ln_b, wa, rows):
    s, wb = u.shape
    sec_z = (4 * wa + 2 * wb) // wb

    def body(y_in, u_ref, bz_ref, g_ref, b_ref, y_ref):
        uh, _ = _layernorm_stats(u_ref[...])
        ln = uh * g_ref[...] + b_ref[...]
        zv = bz_ref[...]
        y_ref[...] = ((ln * _sigmoid(ln)) * (zv * _sigmoid(zv))).astype(BF16)

    vec = pl.BlockSpec((1, wb), lambda i: (0, 0))
    return pl.pallas_call(
        body, name="mixer_b_gate_fwd", grid=(s // rows,),
        in_specs=[ANY, pl.BlockSpec((rows, wb), lambda i: (i, 0)), pl.BlockSpec((rows, wb), lambda i: (i, sec_z)),
                  vec, vec],
        out_specs=pl.BlockSpec((rows, wb), lambda i: (i, wa // wb)),
        out_shape=jax.ShapeDtypeStruct(y.shape, BF16), input_output_aliases={0: 0},
        compiler_params=_params(("parallel",)),
    )(y, u, proj, ln_g, ln_b)


def _loss_head(x, o, target, gate, final_g, rows):
    s, d = x.shape

    def body(x_ref, o_ref, t_ref, gate_ref, fg_ref, dx2_ref, do_ref, loss_ref, gfg_ref, dgate_ref):
        i = pl.program_id(0)
        ov = o_ref[...]
        x2 = x_ref[...] + gate_ref[...] * ov
        r2 = lax.rsqrt(jnp.mean(x2 * x2, axis=-1, keepdims=True) + EPS)
        xn2 = x2 * r2
        diff = xn2 * fg_ref[...] - t_ref[...]
        dout = diff * (1.0 / d)
        dxn2 = dout * fg_ref[...]
        dx2 = r2 * (dxn2 - xn2 * jnp.mean(dxn2 * xn2, axis=-1, keepdims=True))
        dx2_ref[...] = dx2
        do_ref[...] = (gate_ref[...] * dx2).astype(BF16)
        loss_part = 0.5 * jnp.sum(jnp.mean(diff * diff, axis=-1, keepdims=True), axis=0, keepdims=True)
        gfg_part = jnp.sum(dout * xn2, axis=0, keepdims=True)
        dgate_part = jnp.sum(dx2 * ov, axis=0, keepdims=True)

        @pl.when(i == 0)
        def _():
            loss_ref[...] = jnp.zeros_like(loss_ref)
            gfg_ref[...] = jnp.zeros_like(gfg_ref)
            dgate_ref[...] = jnp.zeros_like(dgate_ref)

        loss_ref[...] += jnp.broadcast_to(loss_part, loss_ref.shape)
        gfg_ref[...] += gfg_part
        dgate_ref[...] += dgate_part

    blk = pl.BlockSpec((rows, d), lambda i: (i, 0))
    vec = pl.BlockSpec((1, d), lambda i: (0, 0))
    return pl.pallas_call(
        body, name="loss_head", grid=(s // rows,),
        in_specs=[blk, blk, blk, vec, vec],
        out_specs=[blk, blk, pl.BlockSpec((1, 128), lambda i: (0, 0)), vec, vec],
        out_shape=[jax.ShapeDtypeStruct((s, d), F32), jax.ShapeDtypeStruct((s, d), BF16),
                   jax.ShapeDtypeStruct((1, 128), F32), jax.ShapeDtypeStruct((1, d), F32),
                   jax.ShapeDtypeStruct((1, d), F32)],
        compiler_params=_params(("arbitrary",)),
    )(x, o, target, gate, final_g)


def _mixer_a_bwd(proj, dy, conv_w, wa, din, rows):
    s = proj.shape[0]
    nt = s // rows
    per8 = rows // HALO_A

    def body(ab_ref, ac_ref, ax_ref, az_ref, hc_ref, hx_ref, dy_ref, w_ref, dp_ref, dw_ref, qbuf, dbuf):
        i = pl.program_id(0)

        @pl.when(i == 0)
        def _():
            dbuf[rows:rows + HALO_A, :] = jnp.zeros((HALO_A, wa), F32)
            dw_ref[...] = jnp.zeros_like(dw_ref)

        keep = jnp.where(i == nt - 1, 0.0, 1.0)
        qbuf[0:HALO_A, :] = hc_ref[...] * hx_ref[...] * keep
        acv, axv = ac_ref[...], ax_ref[...]
        q = acv * axv
        qbuf[HALO_A:HALO_A + rows, :] = q
        conv = w_ref[2:3, :] * q
        for k in range(TAPS_A - 1):
            off = HALO_A - (TAPS_A - 1) + k
            conv = conv + w_ref[k:k + 1, :] * qbuf[off:off + rows, :]
        zv, abv, dyv = az_ref[...], ab_ref[...], dy_ref[...]
        sg = _sigmoid(zv)
        sz = zv * sg
        dp_ref[:, 0:wa] = (dyv * conv * sz).astype(BF16)
        dp_ref[:, 3 * wa:4 * wa] = (dyv * abv * conv * (sg * (1.0 + zv * (1.0 - sg)))).astype(BF16)
        dconv = dyv * abv * sz
        dbuf[0:rows, :] = dconv
        dq = w_ref[2:3, :] * dconv
        for k in range(TAPS_A - 1):
            off = TAPS_A - 1 - k
            dq = dq + w_ref[k:k + 1, :] * dbuf[off:off + rows, :]
        dp_ref[:, wa:2 * wa] = (dq * axv).astype(BF16)
        dp_ref[:, 2 * wa:3 * wa] = (dq * acv).astype(BF16)
        for k in range(TAPS_A):
            off = HALO_A - (TAPS_A - 1) + k
            dw_ref[k:k + 1, :] += jnp.sum(dconv * qbuf[off:off + rows, :], axis=0, keepdims=True)
        dbuf[rows:rows + HALO_A, :] = dbuf[0:HALO_A, :]

    def sec(k):
        return pl.BlockSpec((rows, wa), lambda i, k=k: (nt - 1 - i, k))

    def halo(k):
        return pl.BlockSpec((HALO_A, wa), lambda i, k=k: (jnp.maximum((nt - 1 - i) * per8 - 1, 0), k))

    return pl.pallas_call(
        body, name="mixer_a_bwd", grid=(nt,),
        in_specs=[sec(0), sec(1), sec(2), sec(3), halo(1), halo(2),
                  pl.BlockSpec((rows, wa), lambda i: (nt - 1 - i, 0)),
                  pl.BlockSpec((HALO_A, wa), lambda i: (0, 0))],
        out_specs=[pl.BlockSpec((rows, 4 * wa), lambda i: (nt - 1 - i, 0)),
                   pl.BlockSpec((HALO_A, wa), lambda i: (0, 0))],
        out_shape=[jax.ShapeDtypeStruct((s, din), BF16), jax.ShapeDtypeStruct((HALO_A, wa), F32)],
        scratch_shapes=[pltpu.VMEM((HALO_A + rows, wa), F32), pltpu.VMEM((rows + HALO_A, wa), F32)],
        compiler_params=_params(("arbitrary",)),
    )(proj, proj, proj, proj, proj, proj, dy, conv_w)


def _mixer_b_gate_bwd(dproj, dy, u, proj, ln_g, ln_b, wa, rows):
    s, wb = u.shape
    sec_z = (4 * wa + 2 * wb) // wb

    def body(dp_in, dy_ref, u_ref, bz_ref, g_ref, b_ref, dp_ref, du_ref, dg_ref, db_ref, dcb_ref):
        i = pl.program_id(0)
        uh, rs = _layernorm_stats(u_ref[...])
        ln = uh * g_ref[...] + b_ref[...]
        sl = _sigmoid(ln)
        zv = bz_ref[...]
        sg = _sigmoid(zv)
        dyv = dy_ref[...]
        dp_ref[...] = (dyv * (ln * sl) * (sg * (1.0 + zv * (1.0 - sg)))).astype(BF16)
        dln = dyv * (zv * sg) * (sl * (1.0 + ln * (1.0 - sl)))
        duh = dln * g_ref[...]
        du = rs * (duh - jnp.mean(duh, axis=-1, keepdims=True) - uh * jnp.mean(duh * uh, axis=-1, keepdims=True))
        du_ref[...] = du

        @pl.when(i == 0)
        def _():
            dg_ref[...] = jnp.zeros_like(dg_ref)
            db_ref[...] = jnp.zeros_like(db_ref)
            dcb_ref[...] = jnp.zeros_like(dcb_ref)

        dg_ref[...] += jnp.sum(dln * uh, axis=0, keepdims=True)
        db_ref[...] += jnp.sum(dln, axis=0, keepdims=True)
        dcb_ref[...] += jnp.sum(du, axis=0, keepdims=True)

    blk = pl.BlockSpec((rows, wb), lambda i: (i, 0))
    vec = pl.BlockSpec((1, wb), lambda i: (0, 0))
    vshape = jax.ShapeDtypeStruct((1, wb), F32)
    return pl.pallas_call(
        body, name="mixer_b_gate_bwd", grid=(s // rows,),
        in_specs=[ANY, pl.BlockSpec((rows, wb), lambda i: (i, wa // wb)), blk,
                  pl.BlockSpec((rows, wb), lambda i: (i, sec_z)), vec, vec],
        out_specs=[pl.BlockSpec((rows, wb), lambda i: (i, sec_z)), blk, vec, vec, vec],
        out_shape=[jax.ShapeDtypeStruct(dproj.shape, BF16), jax.ShapeDtypeStruct((s, wb), F32), vshape, vshape, vshape],
        input_output_aliases={0: 0},
        compiler_params=_params(("arbitrary",)),
    )(dproj, dy, u, proj, ln_g, ln_b)


def _mixer_b_conv_bwd(dproj, du, u0, proj, conv_w, wa, rows, chunk):
    s, wb = du.shape
    nt = s // rows
    per32 = rows // HALO_B
    sec_v = 4 * wa // wb
    nrc = rows // chunk

    def body(dp_in, du_ref, u0_ref, h0_ref, bv_ref, bg_ref, w_ref, dp_ref, dw_ref, ubuf, dbuf, sh, shf, dwacc):
        i = pl.program_id(0)

        @pl.when(i == 0)
        def _():
            dbuf[rows:rows + HALO_B, :] = jnp.zeros((HALO_B, wb), F32)
            dwacc[...] = jnp.zeros_like(dwacc)

        ubuf[0:HALO_B, :] = h0_ref[...] * jnp.where(i == nt - 1, 0.0, 1.0)
        ubuf[HALO_B:HALO_B + rows, :] = u0_ref[...]
        dbuf[0:rows, :] = du_ref[...]
        for lc in range(wb // LANES):
            cs = slice(lc * LANES, (lc + 1) * LANES)
            _shifted_back(sh, ubuf, 8, HALO_B + rows, cs)
            _shifted_fwd(shf, dbuf, 0, rows + HALO_B - 8, cs)
            taps = [w_ref[k:k + 1, cs] for k in range(TAPS_B)]

            def conv_rows(rc, c0, cs=cs, taps=taps, lc=lc):
                base = pl.multiple_of(rc * chunk, chunk)
                acc = jnp.zeros((chunk, LANES), F32)
                for k in range(TAPS_B):
                    mq, n = divmod(TAPS_B - 1 - k, 8)
                    acc = acc + taps[k] * shf[n, pl.ds(base + 8 * mq, chunk), :]
                sg = _sigmoid(bg_ref[pl.ds(base, chunk), cs])
                bv = bv_ref[pl.ds(base, chunk), cs]
                dp_ref[pl.ds(base, chunk), cs] = (acc * sg).astype(BF16)
                dp_ref[pl.ds(base, chunk), wb + lc * LANES:wb + (lc + 1) * LANES] = (
                    acc * bv * sg * (1.0 - sg)).astype(BF16)
                return c0

            lax.fori_loop(0, nrc, conv_rows, 0)

            def dw_rows(rc, accs, cs=cs):
                base = pl.multiple_of(rc * chunk, chunk)
                du_c = dbuf[pl.ds(base, chunk), cs]
                out = []
                for k in range(TAPS_B):
                    mq, n = divmod(TAPS_B - 1 - k, 8)
                    prod = du_c * sh[n, pl.ds(HALO_B - 8 * mq + base, chunk), :]
                    out.append(accs[k] + jnp.sum(prod.reshape(chunk // SUBLANES, SUBLANES, LANES), axis=0))
                return tuple(out)

            accs = lax.fori_loop(0, nrc, dw_rows, tuple(jnp.zeros((SUBLANES, LANES), F32) for _ in range(TAPS_B)))
            for k in range(TAPS_B):
                dwacc[k * SUBLANES:(k + 1) * SUBLANES, cs] += accs[k]
        dbuf[rows:rows + HALO_B, :] = dbuf[0:HALO_B, :]

        @pl.when(i == nt - 1)
        def _():
            for k in range(HALO_B):
                dw_ref[k:k + 1, :] = jnp.sum(dwacc[k * SUBLANES:(k + 1) * SUBLANES, :], axis=0, keepdims=True)

    def rev(cols_blk):
        return pl.BlockSpec((rows, wb), lambda i, cb=cols_blk: (nt - 1 - i, cb))

    return pl.pallas_call(
        body, name="mixer_b_conv_bwd", grid=(nt,),
        in_specs=[ANY, rev(0), rev(0),
                  pl.BlockSpec((HALO_B, wb), lambda i: (jnp.maximum((nt - 1 - i) * per32 - 1, 0), 0)),
                  rev(sec_v), rev(sec_v + 1), pl.BlockSpec((HALO_B, wb), lambda i: (0, 0))],
        out_specs=[pl.BlockSpec((rows, 2 * wb), lambda i: (nt - 1 - i, sec_v // 2)),
                   pl.BlockSpec((HALO_B, wb), lambda i: (0, 0))],
        out_shape=[jax.ShapeDtypeStruct(dproj.shape, BF16), jax.ShapeDtypeStruct((HALO_B, wb), F32)],
        input_output_aliases={0: 0},
        scratch_shapes=[pltpu.VMEM((HALO_B + rows, wb), F32), pltpu.VMEM((rows + HALO_B, wb), F32),
                        pltpu.VMEM((8, HALO_B + rows, LANES), F32), pltpu.VMEM((8, rows + HALO_B, LANES), F32),
                        pltpu.VMEM((HALO_B * SUBLANES, wb), F32)],
        compiler_params=_params(("arbitrary",)),
    )(dproj, du, u0, u0, proj, proj, conv_w)


def _prenorm_bwd(x, r, dh, dx2, norm_g, scale, rows):
    s, d = x.shape

    def body(x_ref, r_ref, dh_ref, dx2_ref, g_ref, sc_ref, gx_ref, dsh_ref, dsc_ref, dg_ref):
        i = pl.program_id(0)
        rv = r_ref[...]
        xn = x_ref[...] * rv
        dhv = dh_ref[...]
        one_sc = 1.0 + sc_ref[...]
        dxn = dhv * one_sc * g_ref[...]
        gx_ref[...] = dx2_ref[...] + rv * (dxn - xn * jnp.mean(dxn * xn, axis=-1, keepdims=True))

        @pl.when(i == 0)
        def _():
            dsh_ref[...] = jnp.zeros_like(dsh_ref)
            dsc_ref[...] = jnp.zeros_like(dsc_ref)
            dg_ref[...] = jnp.zeros_like(dg_ref)

        dsh_ref[...] += jnp.sum(dhv, axis=0, keepdims=True)
        dsc_ref[...] += jnp.sum(dhv * (xn * g_ref[...]), axis=0, keepdims=True)
        dg_ref[...] += jnp.sum(dhv * one_sc * xn, axis=0, keepdims=True)

    blk = pl.BlockSpec((rows, d), lambda i: (i, 0))
    vec = pl.BlockSpec((1, d), lambda i: (0, 0))
    vshape = jax.ShapeDtypeStruct((1, d), F32)
    return pl.pallas_call(
        body, name="prenorm_bwd", grid=(s // rows,),
        in_specs=[blk, pl.BlockSpec((rows, 1), lambda i: (i, 0)), blk, blk, vec, vec],
        out_specs=[blk, vec, vec, vec],
        out_shape=[jax.ShapeDtypeStruct((s, d), F32), vshape, vshape, vshape],
        compiler_params=_params(("arbitrary",)),
    )(x, r, dh, dx2, norm_g, scale)


def _pad_rows(a, rows):
    return jnp.pad(a, ((0, rows - a.shape[0]), (0, 0)))


def _tile(n, want):
    t = min(n, want)
    while n % t:
        t -= 1
    return t


def kernel(x, c, norm_g, w_ada, b_ada, w_in, conv_a_w, conv_b_w, conv_b_b, ln_b_g, ln_b_b, w_out, final_g, loss_target, m_norm_g, m_w_ada, m_b_ada, m_w_in, m_conv_a_w, m_conv_b_w, m_conv_b_b, m_ln_b_g, m_ln_b_b, m_w_out, m_final_g, v_norm_g, v_w_ada, v_b_ada, v_w_in, v_conv_a_w, v_conv_b_w, v_conv_b_b, v_ln_b_g, v_ln_b_b, v_w_out, v_final_g):
    s, d = x.shape[1], x.shape[2]
    wa = conv_b_b.shape[-1]
    dmix = 2 * wa
    ns = w_in.shape[-1]
    din = N_CHIPS * ns
    r4 = w_out.shape[1]
    na = w_ada.shape[-1]
    wsh = conv_a_w.shape[-1]
    px, py, pc = _position()
    chip = 2 * px + py
    me = 4 * px + 2 * py + pc
    pos = jnp.stack([chip, pc]).astype(jnp.int32)
    x2d = x.reshape(s, d)
    target = loss_target.reshape(s, d)

    hc, ho, hrow = ns // 2, r4 // 2, d // 2
    _, cidx = _other_chips(px, py)
    win2 = _cast_halves(w_in[0], _tile(d, 512), "cast_w_in")
    wout_bf = _cast_bf16(w_out[0], _tile(r4, 512), "cast_w_out")

    def gather_plan(b):
        x, y, cc = _position()
        chips, _ = _other_chips(x, y)
        return ([(b[0].at[cc], b[2 + k], (cx, cy, cc)) for k, (cx, cy) in enumerate(chips)]
                + [(b[1].at[pl.ds(cc * ho, ho), :], b[5 + k], (cx, cy, cc)) for k, (cx, cy) in enumerate(chips)])

    def gather_sent(b):
        return [(src, src, dev) for src, _, dev in gather_plan(list(b) + [None] * 6)]

    whole = _to_sibling([lambda ref, cc: ref])

    c8 = jnp.broadcast_to(c, (8, d))
    cw = jnp.concatenate([_pad_rows(conv_a_w[0], HALO_A), _pad_rows(conv_b_w[0], HALO_B)], axis=0)
    c_all, cw_all = _gather_cond(c8, cw)
    c_rows = c_all[:, 0, :]
    cw_full = jnp.transpose(cw_all, (1, 0, 2)).reshape(HALO_A + HALO_B, wa)
    conv_a_full, conv_b_full = cw_full[:HALO_A], cw_full[HALO_A:]

    b_ada_sh = lax.dynamic_slice(b_ada, (0, chip * na), (1, na))
    mod_part = _modulation(_pad_rows(c_rows, 2 * N_DEV), w_ada[0], b_ada_sh, _tile(na, 512), "modulation")[:N_DEV]
    mod_all = _exchange_mod(mod_part)
    mod = lax.dynamic_index_in_dim(mod_all, me, axis=1, keepdims=False).reshape(1, 3 * d)
    shift, scale, gate = mod[:, :d], mod[:, d:2 * d], mod[:, 2 * d:]

    g_sems, g_bufs, g_tok = _start_copies(
        "gather_start", gather_plan, 6,
        [win2, wout_bf] + [lax.empty((d, hc), BF16) for _ in range(3)] + [lax.empty((ho, d), BF16) for _ in range(3)],
        after=[mod_all])
    win2, wout_bf, li, lo = g_bufs[0], g_bufs[1], g_bufs[2:5], g_bufs[5:8]

    h, r = _prenorm(x2d, norm_g, scale, shift + g_tok[0, 0], _tile(s, 256))
    bm = _tile(s, 1024)
    bn = _tile(hc, 896)
    pieces = [None, None]

    def piece(slot, half, w, name, which=None):
        where = jnp.reshape(2 * slot + half, (1,)).astype(jnp.int32)
        pieces[:] = _proj_piece(where, h, w, pieces[0], pieces[1], din, 2 * N_CHIPS, bm, bn, name,
                                half=which)
        return pieces[0]

    def arrive(k, after):
        got, = _wait_copies(f"gather_wait{k}", _landed, [li[k]], g_sems[2 * k:2 * k + 2], after=after, send=False)
        sems, (got, passed), _ = _start_copies(f"pass_on{k}", whole, 1, [got, lax.empty((d, hc), BF16)])
        return got, passed, sems

    proj = piece(chip, 0, win2, "proj_own0", which=0)
    proj = piece(chip, 1, win2, "proj_own1", which=1)
    li0, ld0, f0 = arrive(0, [proj])
    proj = piece(cidx[0], pc, li0, "proj_0a")
    li1, ld1, f1 = arrive(1, [proj])
    proj = piece(cidx[1], pc, li1, "proj_1a")
    li0, ld0 = _wait_copies("pass_wait0", whole, [li0, ld0], f0, after=[proj])
    proj = piece(cidx[0], 1 - pc, ld0, "proj_0b")
    li1, ld1 = _wait_copies("pass_wait1", whole, [li1, ld1], f1, after=[proj])
    proj = piece(cidx[1], 1 - pc, ld1, "proj_1b")
    li2, ld2, f2 = arrive(2, [proj])
    proj = piece(cidx[2], pc, li2, "proj_2a")
    li2, ld2 = _wait_copies("pass_wait2", whole, [li2, ld2], f2, after=[proj])
    proj = piece(cidx[2], 1 - pc, ld2, "proj_2b")
    win_full = pieces[1]

    lo = _wait_copies("gather_wait_out", _landed, lo, g_sems[6:12], after=[proj], send=False)
    o_sems, o_bufs, o_tok = _start_copies(
        "pass_on_out", _to_sibling([lambda ref, cc: ref] * 3), 3,
        [b for k in range(3) for b in (lo[k], lax.empty((ho, d), BF16))])
    win2, wout_bf = _wait_copies("gather_wait_sent", gather_sent, [win2, wout_bf], g_sems, after=[o_tok], recv=False)

    def slot_index(k, chip_, cc, others):
        if k == 0:
            return pl.ds(2 * chip_, 2)
        return 2 * others[(k - 1) % 3] + (cc if k <= 3 else 1 - cc)

    y = _mixer_a_fwd(proj, conv_a_full, wa, _tile(s, 512), _tile(wa, 512))
    u0, u = _mixer_b_conv_fwd(proj, conv_b_full, conv_b_b, wa, _tile(s, 512), _tile(wa, 256), 64)
    y = _mixer_b_gate_fwd(y, u, proj, ln_b_g, ln_b_b, wa, _tile(s, 256))
    o_bufs = _wait_copies("pass_wait_out", _to_sibling([lambda ref, cc: ref] * 3), o_bufs, o_sems, after=[y])
    wout_full = _assemble("assemble_w_out", [wout_bf.reshape(2, ho, d)] + o_bufs[0::2] + o_bufs[1::2],
                          jax.ShapeDtypeStruct((2 * N_CHIPS, ho, d), BF16), slot_index)
    wout2d = wout_full.reshape(dmix, d)
    bd = _tile(d, 1024)
    o = _matmul(
        y, wout2d, grid=(s // bm, d // bd, 1),
        a_spec=pl.BlockSpec((bm, dmix), lambda i, j, k: (i, 0)),
        b_spec=pl.BlockSpec((dmix, bd), lambda i, j, k: (0, j)),
        o_spec=pl.BlockSpec((bm, bd), lambda i, j, k: (i, j)),
        out_shape=jax.ShapeDtypeStruct((s, d), F32), dims=((1,), (0,)), name="out_proj")
    dx2, do, loss_p, gfg_p, dgate_p = _loss_head(x2d, o, target, gate, final_g.reshape(1, d), _tile(s, 128))

    be = _tile(dmix, 1024)
    g_wout = _matmul(
        y, do, grid=(dmix // be, d // bd, 1),
        a_spec=pl.BlockSpec((s, be), lambda i, j, k: (0, i)),
        b_spec=pl.BlockSpec((s, bd), lambda i, j, k: (0, j)),
        o_spec=pl.BlockSpec((be, bd), lambda i, j, k: (i, j)),
        out_shape=jax.ShapeDtypeStruct((dmix, d), F32), dims=((0,), (0,)), name="grad_w_out")
    swap_out = _to_sibling([lambda ref, cc: ref.at[:, pl.ds((1 - cc) * ho, ho), :]])
    so_sems, (g_wout3, ra_out), so_tok = _start_copies(
        "swap_out_start", swap_out, 1, [g_wout.reshape(N_CHIPS, r4, d), lax.empty((N_CHIPS, ho, d), F32)])
    dy = _matmul(
        do, wout2d, grid=(s // bm, dmix // be, 1),
        a_spec=pl.BlockSpec((bm, d), lambda i, j, k: (i, 0)),
        b_spec=pl.BlockSpec((be, d), lambda i, j, k: (j, 0)),
        o_spec=pl.BlockSpec((bm, be), lambda i, j, k: (i, j)),
        out_shape=jax.ShapeDtypeStruct((s, dmix), F32), dims=((1,), (1,)), name="dy", after=[so_tok])
    g_wout3, ra_out = _wait_copies("swap_out_wait", swap_out, [g_wout3, ra_out], so_sems, after=[dy])
    q_out = _chip_partial(pos, g_wout3, ra_out, _tile(ho, 256), "chip_partial_w_out")
    po_sems, po_bufs, po_tok = _start_copies(
        "send_out_start", _slots_to_chips, 3, [q_out] + [lax.empty((ho, d), BF16) for _ in range(3)])
    dproj, dwa_p = _mixer_a_bwd(proj, dy, conv_a_full + po_tok[0, 0], wa, din, _tile(s, 128))
    dproj, du, dlng_p, dlnb_p, dcb_p = _mixer_b_gate_bwd(dproj, dy, u, proj, ln_b_g, ln_b_b, wa, _tile(s, 128))
    dproj, dwb_p = _mixer_b_conv_bwd(dproj, du, u0, proj, conv_b_full, wa, _tile(s, 256), 64)

    bn2 = _tile(ns, 896)
    swap_in = _to_sibling([lambda ref, cc: ref.at[pl.ds((1 - cc) * hrow, hrow), :]])
    slots = [cidx[0], cidx[1], cidx[2], chip]
    core_only = jnp.stack([0 * pc, pc]).astype(jnp.int32)
    g, ra, sw, q, rb, snd = [None] * 4, [None] * 4, [None] * 4, [None] * 3, [None] * 3, [None] * 3
    after = []
    for k in range(4):
        g[k] = _grad_slot(jnp.reshape(slots[k], (1,)).astype(jnp.int32), h, dproj, after, ns, bd, bn2,
                          f"grad_w_in{k}")
        sw[k], (g[k], ra[k]), tok = _start_copies(f"swap_in_start{k}", swap_in, 1,
                                                  [g[k], lax.empty((hrow, ns), F32)])
        after = [tok]
        if k >= 1:
            j = k - 1
            g[j], ra[j] = _wait_copies(f"swap_in_wait{j}", swap_in, [g[j], ra[j]], sw[j], after=[g[k]])
            part = _chip_partial(core_only, g[j][None], ra[j][None], _tile(hrow, 256), f"chip_partial_w_in{j}")
            snd[j], (q[j], rb[j]), tok2 = _start_copies(f"send_in_start{j}", _to_chip(j), 1,
                                                        [part.reshape(hrow, ns), lax.empty((hrow, ns), BF16)])
            after = [tok, tok2]
    dh = _matmul(
        dproj, win_full, grid=(s // bm, d // bd, 2 * N_CHIPS),
        a_spec=pl.BlockSpec((bm, hc), lambda i, j, k: (i, k)),
        b_spec=pl.BlockSpec((None, bd, hc), lambda i, j, k: (k, j, 0)),
        o_spec=pl.BlockSpec((bm, bd), lambda i, j, k: (i, j)),
        out_shape=jax.ShapeDtypeStruct((s, d), F32), dims=((1,), (1,)), name="dh", after=after)
    grad_x, dshift_p, dscale_p, gng_p = _prenorm_bwd(x2d, r, dh, dx2, norm_g, scale, _tile(s, 128))

    def rows_of(v):
        return _pad_rows(v.reshape(-1, wa), 8 * ((v.size // wa + 7) // 8))

    dmod = jnp.concatenate([dshift_p, dscale_p, dgate_p], axis=1)
    parts = [gng_p, dmod, dwa_p, dwb_p, dcb_p, dlng_p, dlnb_p, gfg_p,
             jnp.broadcast_to(loss_p[:, :1], (1, wa))]
    starts, packed = [], []
    for p in parts:
        starts.append(sum(q.shape[0] for q in packed))
        packed.append(rows_of(p) if p.shape[0] == 1 else p)
    small_sum, small_all = _gather_small(jnp.concatenate(packed, axis=0))

    def summed(k, rows):
        return small_sum[starts[k]:starts[k] + rows]

    grad_norm_g = summed(0, d // wa).reshape(1, d)
    grad_b_ada = summed(1, 3 * d // wa).reshape(1, 3 * d)
    grad_conv_a_full = summed(2, TAPS_A)
    grad_conv_b_full = summed(3, TAPS_B)
    grad_conv_b_b = summed(4, 1)
    grad_ln_b_g = summed(5, 1)
    grad_ln_b_b = summed(6, 1)
    grad_final_g = summed(7, d // wa).reshape(d)
    loss = summed(8, 1)[0, 0]
    grad_conv_a_w = lax.dynamic_slice(grad_conv_a_full, (0, chip * wsh), (TAPS_A, wsh))
    grad_conv_b_w = lax.dynamic_slice(grad_conv_b_full, (0, chip * wsh), (TAPS_B, wsh))
    dmod_all = small_all[:, starts[1]:starts[1] + 3 * d // wa, :].reshape(N_DEV, 3 * d)
    dmod_sh = lax.dynamic_slice(dmod_all, (0, chip * na), (N_DEV, na))

    def pairs_to_chips(b):
        x, y, cc = _position()
        chips, _ = _other_chips(x, y)
        return [(b[2 * k], b[2 * k + 1], (cx, cy, cc)) for k, (cx, cy) in enumerate(chips)]

    po_bufs = _wait_copies("send_out_wait", _slots_to_chips, po_bufs, po_sems, after=[small_sum])
    gh_out = _final_half(pos, g_wout3, ra_out, po_bufs[1:], _tile(ho, 256), "final_half_w_out")
    g[3], ra[3] = _wait_copies("swap_in_wait3", swap_in, [g[3], ra[3]], sw[3], after=[small_sum])
    in_bufs = _wait_copies("send_in_wait", pairs_to_chips, [b for k in range(3) for b in (q[k], rb[k])],
                           snd[0] + snd[1] + snd[2], after=[small_sum])
    gh_in = _final_half(core_only, g[3][None], ra[3][None], in_bufs[1::2], _tile(hrow, 256), "final_half_w_in")
    sh_sems, sh_bufs, sh_tok = _start_copies("share_start", _halves_to_sibling, 2, [gh_in, gh_out])

    grad_w_ada, d_wada, nm_wada, nv_wada = _adam_ada(c_rows.T, dmod_sh + sh_tok[0, 0], w_ada[0], m_w_ada[0],
                                                     v_w_ada[0], _tile(d, 128), "adam_w_ada")
    grad_w_in, grad_w_out = _wait_copies("share_wait", _halves_to_sibling, sh_bufs, sh_sems, after=[d_wada])
    d_win, nm_win, nv_win = _adam(w_in[0], grad_w_in, m_w_in[0], v_w_in[0], _tile(d, 128), "adam_w_in")
    d_wout, nm_wout, nv_wout = _adam(w_out[0], grad_w_out, m_w_out[0], v_w_out[0], _tile(r4, 128), "adam_w_out")

    def small_adam(w, g, m, v, name):
        shape = w.shape
        w2 = w.reshape(-1, shape[-1])
        out = _adam(w2, g.reshape(w2.shape), m.reshape(w2.shape), v.reshape(w2.shape), w2.shape[0], name)
        return [o_.reshape(shape) for o_ in out]

    small = {
        "norm_g": small_adam(norm_g, grad_norm_g, m_norm_g, v_norm_g, "adam_norm_g"),
        "b_ada": small_adam(b_ada, grad_b_ada, m_b_ada, v_b_ada, "adam_b_ada"),
        "conv_a_w": small_adam(conv_a_w, grad_conv_a_w, m_conv_a_w, v_conv_a_w, "adam_conv_a_w"),
        "conv_b_w": small_adam(conv_b_w, grad_conv_b_w, m_conv_b_w, v_conv_b_w, "adam_conv_b_w"),
        "conv_b_b": small_adam(conv_b_b, grad_conv_b_b, m_conv_b_b, v_conv_b_b, "adam_conv_b_b"),
        "ln_b_g": small_adam(ln_b_g, grad_ln_b_g, m_ln_b_g, v_ln_b_g, "adam_ln_b_g"),
        "ln_b_b": small_adam(ln_b_b, grad_ln_b_b, m_ln_b_b, v_ln_b_b, "adam_ln_b_b"),
        "final_g": small_adam(final_g.reshape(1, d), grad_final_g, m_final_g.reshape(1, d),
                              v_final_g.reshape(1, d), "adam_final_g"),
    }
    small["final_g"] = [o_.reshape(d) for o_ in small["final_g"]]
    big = {
        "w_ada": [a[None] for a in (d_wada, nm_wada, nv_wada)],
        "w_in": [a[None] for a in (d_win, nm_win, nv_win)],
        "w_out": [a[None] for a in (d_wout, nm_wout, nv_wout)],
    }
    upd = {**small, **big}
    order = ["norm_g", "w_ada", "b_ada", "w_in", "conv_a_w", "conv_b_w", "conv_b_b", "ln_b_g", "ln_b_b",
             "w_out", "final_g"]
    grads = {
        "norm_g": grad_norm_g, "w_ada": grad_w_ada[None], "b_ada": grad_b_ada, "w_in": grad_w_in[None],
        "conv_a_w": grad_conv_a_w[None], "conv_b_w": grad_conv_b_w[None], "conv_b_b": grad_conv_b_b,
        "ln_b_g": grad_ln_b_g, "ln_b_b": grad_ln_b_b, "w_out": grad_w_out[None], "final_g": grad_final_g,
    }
    return (loss, grad_x.reshape(1, s, d), *[grads[n] for n in order], *[upd[n][0] for n in order],
            *[upd[n][1] for n in order], *[upd[n][2] for n in order])
```

```python
import functools

import jax
import jax.numpy as jnp
from jax import lax
from jax.experimental import pallas as pl
from jax.experimental.pallas import tpu as pltpu

F32 = jnp.float32
BF16 = jnp.bfloat16
EPS = 1e-6
N_CHIPS = 4
N_DEV = 8
TAPS_A = 3
TAPS_B = 31
HALO_A = 8
HALO_B = 32
LANES = 128
SUBLANES = 8
ADAM_LR = 0.001
ADAM_B1 = 0.9
ADAM_B2 = 0.999
ADAM_EPS = 1e-08
ADAM_WD = 0.01
ADAM_STEP = 10
VMEM_LIMIT = 56 * 1024 * 1024
MESH = pl.DeviceIdType.MESH
ANY = pl.BlockSpec(memory_space=pl.ANY)
VMEM = pl.BlockSpec(memory_space=pltpu.VMEM)
HBM_SPEC = pl.BlockSpec(memory_space=pltpu.HBM)
SEM_SPEC = pl.BlockSpec(memory_space=pltpu.SEMAPHORE)
EFFECT = pltpu.SideEffectType.DATAFLOW_SIDE_EFFECTING


def _params(sem=None):
    return pltpu.CompilerParams(dimension_semantics=sem, vmem_limit_bytes=VMEM_LIMIT)


def _sigmoid(v):
    return jax.nn.sigmoid(v)


def _position():
    return lax.axis_index("x"), lax.axis_index("y"), lax.axis_index("c")


def _rcopy(src, dst, ssem, rsem, dev):
    return pltpu.make_async_remote_copy(src_ref=src, dst_ref=dst, send_sem=ssem, recv_sem=rsem,
                                        device_id=dev, device_id_type=MESH)


def _other_chips(x, y):
    chips = [(1 - x, y), (x, 1 - y), (1 - x, 1 - y)]
    return chips, [2 * cx + cy for cx, cy in chips]


def _cast_bf16(a, rows, name):
    m, n = a.shape

    def body(a_ref, o_ref):
        o_ref[...] = a_ref[...].astype(BF16)

    return pl.pallas_call(
        body, name=name, grid=(m // rows,),
        in_specs=[pl.BlockSpec((rows, n), lambda i: (i, 0))],
        out_specs=pl.BlockSpec((rows, n), lambda i: (i, 0)),
        out_shape=jax.ShapeDtypeStruct((m, n), BF16),
        compiler_params=_params(("parallel",)),
    )(a)


def _cast_quarters(a, rows, name):
    m, n = a.shape
    hq = n // 4

    def body(a_ref, o_ref):
        o_ref[...] = a_ref[...].astype(BF16)

    return pl.pallas_call(
        body, name=name, grid=(4, m // rows),
        in_specs=[pl.BlockSpec((rows, hq), lambda q, i: (i, q))],
        out_specs=pl.BlockSpec((None, rows, hq), lambda q, i: (q, i, 0)),
        out_shape=jax.ShapeDtypeStruct((4, m, hq), BF16),
        compiler_params=_params(("parallel", "parallel")),
    )(a)


def _proj_piece(where, h, quarters, proj, w_all, din, n_pieces, bm, name, own_half=None):
    s, d = h.shape
    hq = quarters[0].shape[-1]
    if own_half is None:
        q_specs = [pl.BlockSpec((d, hq), lambda j, i, p: (0, 0), pipeline_mode=pl.Buffered(1))] * 2
    else:
        q_specs = [pl.BlockSpec((None, d, hq), lambda j, i, p, k=k: (2 * own_half + k, 0, 0),
                                pipeline_mode=pl.Buffered(1)) for k in range(2)]

    def body(p_ref, h_ref, q0_ref, q1_ref, *rest):
        o_ref, wall_ref = rest[-2], rest[-1]
        for k, q_ref in enumerate((q0_ref, q1_ref)):
            @pl.when(pl.program_id(0) == k)
            def _(q_ref=q_ref):
                o_ref[...] = jnp.dot(h_ref[...], q_ref[...], preferred_element_type=F32)

                @pl.when(pl.program_id(1) == 0)
                def _():
                    wall_ref[...] = q_ref[...]

    args, extra, alias = [where, h, quarters[0], quarters[1]], [], {}
    if proj is not None:
        args, extra, alias = args + [proj, w_all], [ANY, ANY], {4: 0, 5: 1}
    return pl.pallas_call(
        body, name=name,
        grid_spec=pltpu.PrefetchScalarGridSpec(
            num_scalar_prefetch=1, grid=(2, s // bm),
            in_specs=[pl.BlockSpec((bm, d), lambda j, i, p: (i, 0))] + q_specs + extra,
            out_specs=[pl.BlockSpec((bm, hq), lambda j, i, p: (i, 2 * p[0] + j)),
                       pl.BlockSpec((None, d, hq), lambda j, i, p: (p[0], 0, j))]),
        out_shape=[jax.ShapeDtypeStruct((s, din), F32), jax.ShapeDtypeStruct((n_pieces, d, 2 * hq), BF16)],
        input_output_aliases=alias,
        compiler_params=_params(("parallel", "arbitrary")),
    )(*args)


def _grad_slot(slot, h, dproj, after, ns, bd, bn, name):
    s, d = h.shape
    nb = ns // bn

    def body(slot_ref, h_ref, dp_ref, *rest):
        rest[-1][...] = lax.dot_general(h_ref[...], dp_ref[...], (((0,), (0,)), ((), ())),
                                        preferred_element_type=F32)

    return pl.pallas_call(
        body, name=name,
        grid_spec=pltpu.PrefetchScalarGridSpec(
            num_scalar_prefetch=1, grid=(d // bd, nb),
            in_specs=[pl.BlockSpec((s, bd), lambda i, j, sl: (0, i)),
                      pl.BlockSpec((s, bn), lambda i, j, sl: (0, sl[0] * nb + j))] + [ANY] * len(after),
            out_specs=pl.BlockSpec((bd, bn), lambda i, j, sl: (i, j))),
        out_shape=jax.ShapeDtypeStruct((d, ns), F32),
        compiler_params=_params(("parallel", "parallel")),
    )(slot, h, dproj, *after)


def _matmul(a, b, *, grid, a_spec, b_spec, o_spec, out_shape, dims, name, after=()):
    nk = grid[2]
    n_after = len(after)

    def body(a_ref, b_ref, *rest):
        o_ref, acc = rest[n_after], rest[n_after + 1:]
        p = lax.dot_general(a_ref[...], b_ref[...], (dims, ((), ())), preferred_element_type=F32)
        if nk == 1:
            o_ref[...] = p.astype(o_ref.dtype)
        else:
            acc_ref, = acc
            k = pl.program_id(2)

            @pl.when(k == 0)
            def _():
                acc_ref[...] = p

            @pl.when(k > 0)
            def _():
                acc_ref[...] += p

            @pl.when(k == nk - 1)
            def _():
                o_ref[...] = acc_ref[...].astype(o_ref.dtype)

    block = [d for d in o_spec.block_shape if d is not None]
    scratch = [pltpu.VMEM(tuple(block), F32)] if nk > 1 else []
    return pl.pallas_call(
        body, name=name, grid=grid, in_specs=[a_spec, b_spec] + [ANY] * n_after, out_specs=o_spec,
        out_shape=out_shape, scratch_shapes=scratch,
        compiler_params=_params(("parallel", "parallel", "arbitrary")),
    )(a, b, *after)


def _adam_math(w, g, m, v):
    m = ADAM_B1 * m + (1.0 - ADAM_B1) * g
    v = ADAM_B2 * v + (1.0 - ADAM_B2) * (g * g)
    m_hat = m / (1.0 - ADAM_B1 ** ADAM_STEP)
    v_hat = v / (1.0 - ADAM_B2 ** ADAM_STEP)
    delta = -ADAM_LR * (m_hat / (jnp.sqrt(v_hat) + ADAM_EPS) + ADAM_WD * w)
    return delta, m, v


def _adam(w, g, m, v, rows, name):
    r, n = w.shape

    def body(w_ref, g_ref, m_ref, v_ref, d_ref, mo_ref, vo_ref):
        d, mo, vo = _adam_math(w_ref[...], g_ref[...], m_ref[...], v_ref[...])
        d_ref[...] = d
        mo_ref[...] = mo
        vo_ref[...] = vo

    spec = pl.BlockSpec((rows, n), lambda i: (i, 0))
    shape = jax.ShapeDtypeStruct((r, n), F32)
    return pl.pallas_call(
        body, name=name, grid=(r // rows,), in_specs=[spec] * 4, out_specs=[spec] * 3,
        out_shape=[shape] * 3, compiler_params=_params(("parallel",)),
    )(w, g, m, v)


def _adam_ada(c_cols, dmod, w, m, v, rows, name):
    r, n = w.shape

    def body(c_ref, dm_ref, w_ref, m_ref, v_ref, g_ref, d_ref, mo_ref, vo_ref):
        cv = c_ref[...]
        c_act = cv * _sigmoid(cv)
        g = c_act[:, 0:1] * dm_ref[0:1, :]
        for b in range(1, N_DEV):
            g = g + c_act[:, b:b + 1] * dm_ref[b:b + 1, :]
        d, mo, vo = _adam_math(w_ref[...], g, m_ref[...], v_ref[...])
        g_ref[...] = g
        d_ref[...] = d
        mo_ref[...] = mo
        vo_ref[...] = vo

    spec = pl.BlockSpec((rows, n), lambda i: (i, 0))
    shape = jax.ShapeDtypeStruct((r, n), F32)
    return pl.pallas_call(
        body, name=name, grid=(r // rows,),
        in_specs=[pl.BlockSpec((rows, N_DEV), lambda i: (i, 0)), pl.BlockSpec((N_DEV, n), lambda i: (0, 0)),
                  spec, spec, spec],
        out_specs=[spec] * 4, out_shape=[shape] * 4, compiler_params=_params(("parallel",)),
    )(c_cols, dmod, w, m, v)


def _start_copies(name, plan, n, bufs, after=()):
    nb, na = len(bufs), len(after)

    def body(*refs):
        sems = refs[nb + na:nb + na + 2 * n]
        for k, (src, dst, dev) in enumerate(plan(refs[:nb])):
            _rcopy(src, dst, sems[2 * k], sems[2 * k + 1], dev).start()
        refs[-1][...] = jnp.zeros((8, 128), F32)

    outs = pl.pallas_call(
        body, name=name,
        out_shape=[pltpu.SemaphoreType.DMA(())] * (2 * n) + [pltpu.HBM(a.shape, a.dtype) for a in bufs]
        + [jax.ShapeDtypeStruct((8, 128), F32)],
        in_specs=[HBM_SPEC] * nb + [ANY] * na, out_specs=[SEM_SPEC] * (2 * n) + [HBM_SPEC] * nb + [VMEM],
        input_output_aliases={i: 2 * n + i for i in range(nb)},
        compiler_params=pltpu.CompilerParams(has_side_effects=EFFECT),
    )(*[pltpu.with_memory_space_constraint(a, pltpu.HBM) for a in bufs], *after)
    return list(outs[:2 * n]), list(outs[2 * n:2 * n + nb]), outs[-1]


def _wait_copies(name, plan, bufs, sems, after=(), send=True, recv=True):
    nb, nsem = len(bufs), len(sems)

    def body(*refs):
        s = refs[nb:nb + nsem]
        for k, (src, dst, dev) in enumerate(plan(refs[:nb])):
            cp = _rcopy(src, dst, s[2 * k], s[2 * k + 1], dev)
            if send:
                cp.wait_send()
            if recv:
                cp.wait_recv()

    outs = pl.pallas_call(
        body, name=name, out_shape=[pltpu.HBM(a.shape, a.dtype) for a in bufs],
        in_specs=[HBM_SPEC] * nb + [SEM_SPEC] * nsem + [ANY] * len(after), out_specs=[HBM_SPEC] * nb,
        input_output_aliases={i: i for i in range(nb)},
        compiler_params=pltpu.CompilerParams(has_side_effects=EFFECT),
    )(*bufs, *sems, *after)
    return list(outs)


def _to_sibling(views):
    def plan(b):
        x, y, c = _position()
        return [(view(b[2 * k], c), b[2 * k + 1], (x, y, 1 - c)) for k, view in enumerate(views)]
    return plan


def _to_chip(k):
    def plan(b):
        x, y, c = _position()
        cx, cy = _other_chips(x, y)[0][k]
        return [(b[0], b[1], (cx, cy, c))]
    return plan


def _slots_to_chips(b):
    x, y, c = _position()
    chips, cidx = _other_chips(x, y)
    return [(b[0].at[cidx[k]], b[1 + k], (cx, cy, c)) for k, (cx, cy) in enumerate(chips)]


def _halves_to_sibling(b):
    x, y, c = _position()
    views = [r.at[pl.ds(c * (r.shape[0] // 2), r.shape[0] // 2), :] for r in b]
    return [(v, v, (x, y, 1 - c)) for v in views]


def _landed(b):
    x, y, c = _position()
    return [(ref, ref, (x, y, c)) for ref in b]


def _assemble(name, pieces, out_shape, index_of):
    n = len(pieces)

    def body(*refs):
        out_ref, sem = refs[n], refs[n + 1]
        x, y, c = _position()
        _, cidx = _other_chips(x, y)
        cps = [pltpu.make_async_copy(refs[k], out_ref.at[index_of(k, 2 * x + y, c, cidx)], sem.at[k]) for k in range(n)]
        for cp in cps:
            cp.start()
        for cp in cps:
            cp.wait()

    return pl.pallas_call(
        body, name=name, in_specs=[VMEM] * n, out_specs=ANY, out_shape=out_shape,
        scratch_shapes=[pltpu.SemaphoreType.DMA((n,))],
        compiler_params=pltpu.CompilerParams(vmem_limit_bytes=VMEM_LIMIT),
    )(*pieces)


def _gather_cond(c8, cw):
    def body(c8_ref, cw_ref, call_ref, cwall_ref, ssem, rsem, lsem):
        x, y, c = _position()
        chip = 2 * x + y
        me = 4 * x + 2 * y + c
        chips, cidx = _other_chips(x, y)
        own = [pltpu.make_async_copy(c8_ref, call_ref.at[me], lsem.at[0]),
               pltpu.make_async_copy(cw_ref, cwall_ref.at[chip], lsem.at[1])]
        for cp in own:
            cp.start()
        sends = [_rcopy(cw_ref, cwall_ref.at[chip], ssem.at[k], rsem.at[k], (cx, cy, c))
                 for k, (cx, cy) in enumerate(chips)]
        for mask in range(1, N_DEV):
            fx, fy, fc = (mask >> 2) & 1, (mask >> 1) & 1, mask & 1
            dev = (1 - x if fx else x, 1 - y if fy else y, 1 - c if fc else c)
            sends.append(_rcopy(c8_ref, call_ref.at[me], ssem.at[2 + mask], rsem.at[2 + mask], dev))
        for cp in sends:
            cp.start()
        for k in range(3):
            slot = cwall_ref.at[cidx[k]]
            _rcopy(slot, slot, ssem.at[k], rsem.at[k], (x, y, c)).wait_recv()
        for mask in range(1, N_DEV):
            slot = call_ref.at[jnp.bitwise_xor(me, mask)]
            _rcopy(slot, slot, ssem.at[2 + mask], rsem.at[2 + mask], (x, y, c)).wait_recv()
        for cp in sends:
            cp.wait_send()
        for cp in own:
            cp.wait()

    return pl.pallas_call(
        body, name="gather_cond", in_specs=[VMEM, VMEM], out_specs=[VMEM, VMEM],
        out_shape=[jax.ShapeDtypeStruct((N_DEV,) + c8.shape, F32), jax.ShapeDtypeStruct((N_CHIPS,) + cw.shape, F32)],
        scratch_shapes=[pltpu.SemaphoreType.DMA((10,)), pltpu.SemaphoreType.DMA((10,)), pltpu.SemaphoreType.DMA((2,))],
    )(c8, cw)


def _exchange_mod(mod_part):
    def body(mp_ref, out_ref, ssem, rsem, lsem):
        x, y, c = _position()
        chip = 2 * x + y
        chips, cidx = _other_chips(x, y)
        own = pltpu.make_async_copy(mp_ref, out_ref.at[chip], lsem)
        own.start()
        sends = [_rcopy(mp_ref, out_ref.at[chip], ssem.at[k], rsem.at[k], (cx, cy, c))
                 for k, (cx, cy) in enumerate(chips)]
        for cp in sends:
            cp.start()
        for k in range(3):
            slot = out_ref.at[cidx[k]]
            _rcopy(slot, slot, ssem.at[k], rsem.at[k], (x, y, c)).wait_recv()
        for cp in sends:
            cp.wait_send()
        own.wait()

    return pl.pallas_call(
        body, name="exchange_mod", in_specs=[VMEM], out_specs=VMEM,
        out_shape=jax.ShapeDtypeStruct((N_CHIPS,) + mod_part.shape, F32),
        scratch_shapes=[pltpu.SemaphoreType.DMA((3,)), pltpu.SemaphoreType.DMA((3,)), pltpu.SemaphoreType.DMA],
    )(mod_part)


def _gather_small(pack):
    rows, n = pack.shape

    def body(p_ref, sum_ref, all_ref, ssem, rsem, lsem):
        x, y, c = _position()
        me = 4 * x + 2 * y + c
        own = pltpu.make_async_copy(p_ref, all_ref.at[me], lsem)
        own.start()
        sends = []
        for mask in range(1, N_DEV):
            fx, fy, fc = (mask >> 2) & 1, (mask >> 1) & 1, mask & 1
            dev = (1 - x if fx else x, 1 - y if fy else y, 1 - c if fc else c)
            sends.append(_rcopy(p_ref, all_ref.at[me], ssem.at[mask - 1], rsem.at[mask - 1], dev))
        for cp in sends:
            cp.start()
        for mask in range(1, N_DEV):
            slot = all_ref.at[jnp.bitwise_xor(me, mask)]
            _rcopy(slot, slot, ssem.at[mask - 1], rsem.at[mask - 1], (x, y, c)).wait_recv()
        for cp in sends:
            cp.wait_send()
        own.wait()
        acc = all_ref[0]
        for k in range(1, N_DEV):
            acc = acc + all_ref[k]
        sum_ref[...] = acc

    return pl.pallas_call(
        body, name="gather_small", in_specs=[VMEM], out_specs=[VMEM, VMEM],
        out_shape=[jax.ShapeDtypeStruct((rows, n), F32), jax.ShapeDtypeStruct((N_DEV, rows, n), F32)],
        scratch_shapes=[pltpu.SemaphoreType.DMA((7,)), pltpu.SemaphoreType.DMA((7,)), pltpu.SemaphoreType.DMA],
        compiler_params=pltpu.CompilerParams(vmem_limit_bytes=VMEM_LIMIT),
    )(pack)


def _chip_partial(pos, g, recv, rows, name):
    ns, full, n = g.shape
    h = full // 2
    nb = h // rows

    def body(pos_ref, g_ref, r_ref, o_ref):
        o_ref[...] = (g_ref[...] + r_ref[...]).astype(BF16)

    return pl.pallas_call(
        body, name=name,
        grid_spec=pltpu.PrefetchScalarGridSpec(
            num_scalar_prefetch=1, grid=(ns, nb),
            in_specs=[pl.BlockSpec((None, rows, n), lambda s, i, p: (s, p[1] * nb + i, 0)),
                      pl.BlockSpec((None, rows, n), lambda s, i, p: (s, i, 0))],
            out_specs=pl.BlockSpec((None, rows, n), lambda s, i, p: (s, i, 0))),
        out_shape=jax.ShapeDtypeStruct((ns, h, n), BF16),
        compiler_params=_params(("parallel", "parallel")),
    )(pos, g, recv)


def _final_half(pos, g, recv_a, recv_b, rows, name):
    ns, full, n = g.shape
    h = full // 2
    nb = h // rows

    def body(pos_ref, g_ref, ra_ref, rb0_ref, rb1_ref, rb2_ref, o_ref):
        acc = g_ref[...] + ra_ref[...]
        for rb_ref in (rb0_ref, rb1_ref, rb2_ref):
            acc = acc + rb_ref[...].astype(F32)
        o_ref[...] = acc

    part = pl.BlockSpec((rows, n), lambda i, p: (i, 0))
    return pl.pallas_call(
        body, name=name,
        grid_spec=pltpu.PrefetchScalarGridSpec(
            num_scalar_prefetch=1, grid=(nb,),
            in_specs=[pl.BlockSpec((None, rows, n), lambda i, p: (p[0], p[1] * nb + i, 0)),
                      pl.BlockSpec((None, rows, n), lambda i, p: (p[0], i, 0)), part, part, part],
            out_specs=pl.BlockSpec((rows, n), lambda i, p: (p[1] * nb + i, 0))),
        out_shape=jax.ShapeDtypeStruct((full, n), F32),
        compiler_params=_params(("parallel",)),
    )(pos, g, recv_a, *recv_b)


def _modulation(c_rows, w_ada, b_ada, cols, name):
    d, n = w_ada.shape
    rows = c_rows.shape[0]

    def body(c_ref, w_ref, b_ref, o_ref):
        cv = c_ref[...]
        c_act = (cv * _sigmoid(cv)).astype(BF16)
        o_ref[...] = jnp.dot(c_act, w_ref[...].astype(BF16), preferred_element_type=F32) + b_ref[...]

    return pl.pallas_call(
        body, name=name, grid=(n // cols,),
        in_specs=[pl.BlockSpec((rows, d), lambda j: (0, 0)), pl.BlockSpec((d, cols), lambda j: (0, j)),
                  pl.BlockSpec((1, cols), lambda j: (0, j))],
        out_specs=pl.BlockSpec((rows, cols), lambda j: (0, j)),
        out_shape=jax.ShapeDtypeStruct((rows, n), F32),
        compiler_params=_params(("parallel",)),
    )(c_rows, w_ada, b_ada)


def _prenorm(x, norm_g, scale, shift, rows):
    s, d = x.shape

    def body(x_ref, g_ref, sc_ref, sh_ref, h_ref, r_ref):
        xv = x_ref[...]
        r = lax.rsqrt(jnp.mean(xv * xv, axis=-1, keepdims=True) + EPS)
        h = (xv * r * g_ref[...]) * (1.0 + sc_ref[...]) + sh_ref[...]
        h_ref[...] = h.astype(BF16)
        r_ref[...] = r

    vec = pl.BlockSpec((1, d), lambda i: (0, 0))
    return pl.pallas_call(
        body, name="prenorm", grid=(s // rows,),
        in_specs=[pl.BlockSpec((rows, d), lambda i: (i, 0)), vec, vec, vec],
        out_specs=[pl.BlockSpec((rows, d), lambda i: (i, 0)), pl.BlockSpec((rows, 1), lambda i: (i, 0))],
        out_shape=[jax.ShapeDtypeStruct((s, d), BF16), jax.ShapeDtypeStruct((s, 1), F32)],
        compiler_params=_params(("parallel",)),
    )(x, norm_g, scale, shift)


def _mixer_a_fwd(proj, conv_w, wa, rows, cols):
    s = proj.shape[0]
    ncb = wa // cols

    def body(ab_ref, ac_ref, ax_ref, az_ref, w_ref, y_ref, qbuf):
        t = pl.program_id(1)

        @pl.when(t == 0)
        def _():
            qbuf[0:HALO_A, :] = jnp.zeros((HALO_A, cols), F32)

        q = ac_ref[...] * ax_ref[...]
        qbuf[HALO_A:HALO_A + rows, :] = q
        conv = w_ref[2:3, :] * q
        for k in range(TAPS_A - 1):
            off = HALO_A - (TAPS_A - 1) + k
            conv = conv + w_ref[k:k + 1, :] * qbuf[off:off + rows, :]
        zv = az_ref[...]
        y_ref[...] = (ab_ref[...] * conv * (zv * _sigmoid(zv))).astype(BF16)
        qbuf[0:HALO_A, :] = qbuf[rows:rows + HALO_A, :]

    def sec(k):
        return pl.BlockSpec((rows, cols), lambda cb, t, k=k: (t, k * ncb + cb))

    return pl.pallas_call(
        body, name="mixer_a_fwd", grid=(ncb, s // rows),
        in_specs=[sec(0), sec(1), sec(2), sec(3), pl.BlockSpec((HALO_A, cols), lambda cb, t: (0, cb))],
        out_specs=pl.BlockSpec((rows, cols), lambda cb, t: (t, cb)),
        out_shape=jax.ShapeDtypeStruct((s, 2 * wa), BF16),
        scratch_shapes=[pltpu.VMEM((HALO_A + rows, cols), F32)],
        compiler_params=_params(("parallel", "arbitrary")),
    )(proj, proj, proj, proj, conv_w)


def _shifted_back(dst, src, lo, hi, cs):
    for n in range(8):
        dst[n, lo:hi, :] = src[lo - n:hi - n, cs]


def _shifted_fwd(dst, src, lo, hi, cs):
    for n in range(8):
        dst[n, lo:hi, :] = src[lo + n:hi + n, cs]


def _mixer_b_conv_fwd(proj, conv_w, conv_b, wa, rows, cols, chunk):
    s = proj.shape[0]
    wb = conv_w.shape[1]
    ncb = wb // cols
    sec0 = 4 * wa // cols

    def body(bv_ref, bg_ref, w_ref, b_ref, u0_ref, u_ref, ubuf, sh):
        t = pl.program_id(1)

        @pl.when(t == 0)
        def _():
            ubuf[0:HALO_B, :] = jnp.zeros((HALO_B, cols), F32)

        u0 = bv_ref[...] * _sigmoid(bg_ref[...])
        u0_ref[...] = u0
        ubuf[HALO_B:HALO_B + rows, :] = u0
        for lc in range(cols // LANES):
            cs = slice(lc * LANES, (lc + 1) * LANES)
            _shifted_back(sh, ubuf, 8, HALO_B + rows, cs)
            taps = [w_ref[k:k + 1, cs] for k in range(TAPS_B)]
            bias = b_ref[:, cs]

            def row_chunk(rc, carry, cs=cs, taps=taps, bias=bias):
                base = pl.multiple_of(rc * chunk, chunk)
                acc = jnp.zeros((chunk, LANES), F32)
                for k in range(TAPS_B):
                    mq, n = divmod(TAPS_B - 1 - k, 8)
                    acc = acc + taps[k] * sh[n, pl.ds(HALO_B - 8 * mq + base, chunk), :]
                u_ref[pl.ds(base, chunk), cs] = acc + bias
                return carry

            lax.fori_loop(0, rows // chunk, row_chunk, 0)
        ubuf[0:HALO_B, :] = ubuf[rows:rows + HALO_B, :]

    return pl.pallas_call(
        body, name="mixer_b_conv_fwd", grid=(ncb, s // rows),
        in_specs=[pl.BlockSpec((rows, cols), lambda cb, t: (t, sec0 + cb)),
                  pl.BlockSpec((rows, cols), lambda cb, t: (t, sec0 + ncb + cb)),
                  pl.BlockSpec((HALO_B, cols), lambda cb, t: (0, cb)),
                  pl.BlockSpec((1, cols), lambda cb, t: (0, cb))],
        out_specs=[pl.BlockSpec((rows, cols), lambda cb, t: (t, cb))] * 2,
        out_shape=[jax.ShapeDtypeStruct((s, wb), F32)] * 2,
        scratch_shapes=[pltpu.VMEM((HALO_B + rows, cols), F32), pltpu.VMEM((8, HALO_B + rows, LANES), F32)],
        compiler_params=_params(("parallel", "arbitrary")),
    )(proj, proj, conv_w, conv_b)


def _layernorm_stats(u):
    mu = jnp.mean(u, axis=-1, keepdims=True)
    xc = u - mu
    var = jnp.mean(xc * xc, axis=-1, keepdims=True)
    return xc * lax.rsqrt(var + EPS), lax.rsqrt(var + EPS)


def _mixer_b_gate_fwd(y, u, proj, ln_g, ln_b, wa, rows):
    s, wb = u.shape
    sec_z = (4 * wa + 2 * wb) // wb

    def body(y_in, u_ref, bz_ref, g_ref, b_ref, y_ref):
        uh, _ = _layernorm_stats(u_ref[...])
        ln = uh * g_ref[...] + b_ref[...]
        zv = bz_ref[...]
        y_ref[...] = ((ln * _sigmoid(ln)) * (zv * _sigmoid(zv))).astype(BF16)

    vec = pl.BlockSpec((1, wb), lambda i: (0, 0))
    return pl.pallas_call(
        body, name="mixer_b_gate_fwd", grid=(s // rows,),
        in_specs=[ANY, pl.BlockSpec((rows, wb), lambda i: (i, 0)), pl.BlockSpec((rows, wb), lambda i: (i, sec_z)),
                  vec, vec],
        out_specs=pl.BlockSpec((rows, wb), lambda i: (i, wa // wb)),
        out_shape=jax.ShapeDtypeStruct(y.shape, BF16), input_output_aliases={0: 0},
        compiler_params=_params(("parallel",)),
    )(y, u, proj, ln_g, ln_b)


def _loss_head(x, o, target, gate, final_g, rows):
    s, d = x.shape

    def body(x_ref, o_ref, t_ref, gate_ref, fg_ref, dx2_ref, do_ref, loss_ref, gfg_ref, dgate_ref):
        i = pl.program_id(0)
        ov = o_ref[...]
        x2 = x_ref[...] + gate_ref[...] * ov
        r2 = lax.rsqrt(jnp.mean(x2 * x2, axis=-1, keepdims=True) + EPS)
        xn2 = x2 * r2
        diff = xn2 * fg_ref[...] - t_ref[...]
        dout = diff * (1.0 / d)
        dxn2 = dout * fg_ref[...]
        dx2 = r2 * (dxn2 - xn2 * jnp.mean(dxn2 * xn2, axis=-1, keepdims=True))
        dx2_ref[...] = dx2
        do_ref[...] = (gate_ref[...] * dx2).astype(BF16)
        loss_part = 0.5 * jnp.sum(jnp.mean(diff * diff, axis=-1, keepdims=True), axis=0, keepdims=True)
        gfg_part = jnp.sum(dout * xn2, axis=0, keepdims=True)
        dgate_part = jnp.sum(dx2 * ov, axis=0, keepdims=True)

        @pl.when(i == 0)
        def _():
            loss_ref[...] = jnp.zeros_like(loss_ref)
            gfg_ref[...] = jnp.zeros_like(gfg_ref)
            dgate_ref[...] = jnp.zeros_like(dgate_ref)

        loss_ref[...] += jnp.broadcast_to(loss_part, loss_ref.shape)
        gfg_ref[...] += gfg_part
        dgate_ref[...] += dgate_part

    blk = pl.BlockSpec((rows, d), lambda i: (i, 0))
    vec = pl.BlockSpec((1, d), lambda i: (0, 0))
    return pl.pallas_call(
        body, name="loss_head", grid=(s // rows,),
        in_specs=[blk, blk, blk, vec, vec],
        out_specs=[blk, blk, pl.BlockSpec((1, 128), lambda i: (0, 0)), vec, vec],
        out_shape=[jax.ShapeDtypeStruct((s, d), F32), jax.ShapeDtypeStruct((s, d), BF16),
                   jax.ShapeDtypeStruct((1, 128), F32), jax.ShapeDtypeStruct((1, d), F32),
                   jax.ShapeDtypeStruct((1, d), F32)],
        compiler_params=_params(("arbitrary",)),
    )(x, o, target, gate, final_g)


def _mixer_a_bwd(proj, dy, conv_w, wa, din, rows):
    s = proj.shape[0]
    nt = s // rows
    per8 = rows // HALO_A

    def body(ab_ref, ac_ref, ax_ref, az_ref, hc_ref, hx_ref, dy_ref, w_ref, dp_ref, dw_ref, qbuf, dbuf):
        i = pl.program_id(0)

        @pl.when(i == 0)
        def _():
            dbuf[rows:rows + HALO_A, :] = jnp.zeros((HALO_A, wa), F32)
            dw_ref[...] = jnp.zeros_like(dw_ref)

        keep = jnp.where(i == nt - 1, 0.0, 1.0)
        qbuf[0:HALO_A, :] = hc_ref[...] * hx_ref[...] * keep
        acv, axv = ac_ref[...], ax_ref[...]
        q = acv * axv
        qbuf[HALO_A:HALO_A + rows, :] = q
        conv = w_ref[2:3, :] * q
        for k in range(TAPS_A - 1):
            off = HALO_A - (TAPS_A - 1) + k
            conv = conv + w_ref[k:k + 1, :] * qbuf[off:off + rows, :]
        zv, abv, dyv = az_ref[...], ab_ref[...], dy_ref[...]
        sg = _sigmoid(zv)
        sz = zv * sg
        dp_ref[:, 0:wa] = (dyv * conv * sz).astype(BF16)
        dp_ref[:, 3 * wa:4 * wa] = (dyv * abv * conv * (sg * (1.0 + zv * (1.0 - sg)))).astype(BF16)
        dconv = dyv * abv * sz
        dbuf[0:rows, :] = dconv
        dq = w_ref[2:3, :] * dconv
        for k in range(TAPS_A - 1):
            off = TAPS_A - 1 - k
            dq = dq + w_ref[k:k + 1, :] * dbuf[off:off + rows, :]
        dp_ref[:, wa:2 * wa] = (dq * axv).astype(BF16)
        dp_ref[:, 2 * wa:3 * wa] = (dq * acv).astype(BF16)
        for k in range(TAPS_A):
            off = HALO_A - (TAPS_A - 1) + k
            dw_ref[k:k + 1, :] += jnp.sum(dconv * qbuf[off:off + rows, :], axis=0, keepdims=True)
        dbuf[rows:rows + HALO_A, :] = dbuf[0:HALO_A, :]

    def sec(k):
        return pl.BlockSpec((rows, wa), lambda i, k=k: (nt - 1 - i, k))

    def halo(k):
        return pl.BlockSpec((HALO_A, wa), lambda i, k=k: (jnp.maximum((nt - 1 - i) * per8 - 1, 0), k))

    return pl.pallas_call(
        body, name="mixer_a_bwd", grid=(nt,),
        in_specs=[sec(0), sec(1), sec(2), sec(3), halo(1), halo(2),
                  pl.BlockSpec((rows, wa), lambda i: (nt - 1 - i, 0)),
                  pl.BlockSpec((HALO_A, wa), lambda i: (0, 0))],
        out_specs=[pl.BlockSpec((rows, 4 * wa), lambda i: (nt - 1 - i, 0)),
                   pl.BlockSpec((HALO_A, wa), lambda i: (0, 0))],
        out_shape=[jax.ShapeDtypeStruct((s, din), BF16), jax.ShapeDtypeStruct((HALO_A, wa), F32)],
        scratch_shapes=[pltpu.VMEM((HALO_A + rows, wa), F32), pltpu.VMEM((rows + HALO_A, wa), F32)],
        compiler_params=_params(("arbitrary",)),
    )(proj, proj, proj, proj, proj, proj, dy, conv_w)


def _mixer_b_gate_bwd(dproj, dy, u, proj, ln_g, ln_b, wa, rows):
    s, wb = u.shape
    sec_z = (4 * wa + 2 * wb) // wb

    def body(dp_in, dy_ref, u_ref, bz_ref, g_ref, b_ref, dp_ref, du_ref, dg_ref, db_ref, dcb_ref):
        i = pl.program_id(0)
        uh, rs = _layernorm_stats(u_ref[...])
        ln = uh * g_ref[...] + b_ref[...]
        sl = _sigmoid(ln)
        zv = bz_ref[...]
        sg = _sigmoid(zv)
        dyv = dy_ref[...]
        dp_ref[...] = (dyv * (ln * sl) * (sg * (1.0 + zv * (1.0 - sg)))).astype(BF16)
        dln = dyv * (zv * sg) * (sl * (1.0 + ln * (1.0 - sl)))
        duh = dln * g_ref[...]
        du = rs * (duh - jnp.mean(duh, axis=-1, keepdims=True) - uh * jnp.mean(duh * uh, axis=-1, keepdims=True))
        du_ref[...] = du

        @pl.when(i == 0)
        def _():
            dg_ref[...] = jnp.zeros_like(dg_ref)
            db_ref[...] = jnp.zeros_like(db_ref)
            dcb_ref[...] = jnp.zeros_like(dcb_ref)

        dg_ref[...] += jnp.sum(dln * uh, axis=0, keepdims=True)
        db_ref[...] += jnp.sum(dln, axis=0, keepdims=True)
        dcb_ref[...] += jnp.sum(du, axis=0, keepdims=True)

    blk = pl.BlockSpec((rows, wb), lambda i: (i, 0))
    vec = pl.BlockSpec((1, wb), lambda i: (0, 0))
    vshape = jax.ShapeDtypeStruct((1, wb), F32)
    return pl.pallas_call(
        body, name="mixer_b_gate_bwd", grid=(s // rows,),
        in_specs=[ANY, pl.BlockSpec((rows, wb), lambda i: (i, wa // wb)), blk,
                  pl.BlockSpec((rows, wb), lambda i: (i, sec_z)), vec, vec],
        out_specs=[pl.BlockSpec((rows, wb), lambda i: (i, sec_z)), blk, vec, vec, vec],
        out_shape=[jax.ShapeDtypeStruct(dproj.shape, BF16), jax.ShapeDtypeStruct((s, wb), F32), vshape, vshape, vshape],
        input_output_aliases={0: 0},
        compiler_params=_params(("arbitrary",)),
    )(dproj, dy, u, proj, ln_g, ln_b)


def _mixer_b_conv_bwd(dproj, du, u0, proj, conv_w, wa, rows, chunk):
    s, wb = du.shape
    nt = s // rows
    per32 = rows // HALO_B
    sec_v = 4 * wa // wb
    nrc = rows // chunk

    def body(dp_in, du_ref, u0_ref, h0_ref, bv_ref, bg_ref, w_ref, dp_ref, dw_ref, ubuf, dbuf, sh, shf, dwacc):
        i = pl.program_id(0)

        @pl.when(i == 0)
        def _():
            dbuf[rows:rows + HALO_B, :] = jnp.zeros((HALO_B, wb), F32)
            dwacc[...] = jnp.zeros_like(dwacc)

        ubuf[0:HALO_B, :] = h0_ref[...] * jnp.where(i == nt - 1, 0.0, 1.0)
        ubuf[HALO_B:HALO_B + rows, :] = u0_ref[...]
        dbuf[0:rows, :] = du_ref[...]
        for lc in range(wb // LANES):
            cs = slice(lc * LANES, (lc + 1) * LANES)
            _shifted_back(sh, ubuf, 8, HALO_B + rows, cs)
            _shifted_fwd(shf, dbuf, 0, rows + HALO_B - 8, cs)
            taps = [w_ref[k:k + 1, cs] for k in range(TAPS_B)]

            def conv_rows(rc, c0, cs=cs, taps=taps, lc=lc):
                base = pl.multiple_of(rc * chunk, chunk)
                acc = jnp.zeros((chunk, LANES), F32)
                for k in range(TAPS_B):
                    mq, n = divmod(TAPS_B - 1 - k, 8)
                    acc = acc + taps[k] * shf[n, pl.ds(base + 8 * mq, chunk), :]
                sg = _sigmoid(bg_ref[pl.ds(base, chunk), cs])
                bv = bv_ref[pl.ds(base, chunk), cs]
                dp_ref[pl.ds(base, chunk), cs] = (acc * sg).astype(BF16)
                dp_ref[pl.ds(base, chunk), wb + lc * LANES:wb + (lc + 1) * LANES] = (
                    acc * bv * sg * (1.0 - sg)).astype(BF16)
                return c0

            lax.fori_loop(0, nrc, conv_rows, 0)

            def dw_rows(rc, accs, cs=cs):
                base = pl.multiple_of(rc * chunk, chunk)
                du_c = dbuf[pl.ds(base, chunk), cs]
                out = []
                for k in range(TAPS_B):
                    mq, n = divmod(TAPS_B - 1 - k, 8)
                    prod = du_c * sh[n, pl.ds(HALO_B - 8 * mq + base, chunk), :]
                    out.append(accs[k] + jnp.sum(prod.reshape(chunk // SUBLANES, SUBLANES, LANES), axis=0))
                return tuple(out)

            accs = lax.fori_loop(0, nrc, dw_rows, tuple(jnp.zeros((SUBLANES, LANES), F32) for _ in range(TAPS_B)))
            for k in range(TAPS_B):
                dwacc[k * SUBLANES:(k + 1) * SUBLANES, cs] += accs[k]
        dbuf[rows:rows + HALO_B, :] = dbuf[0:HALO_B, :]

        @pl.when(i == nt - 1)
        def _():
            for k in range(HALO_B):
                dw_ref[k:k + 1, :] = jnp.sum(dwacc[k * SUBLANES:(k + 1) * SUBLANES, :], axis=0, keepdims=True)

    def rev(cols_blk):
        return pl.BlockSpec((rows, wb), lambda i, cb=cols_blk: (nt - 1 - i, cb))

    return pl.pallas_call(
        body, name="mixer_b_conv_bwd", grid=(nt,),
        in_specs=[ANY, rev(0), rev(0),
                  pl.BlockSpec((HALO_B, wb), lambda i: (jnp.maximum((nt - 1 - i) * per32 - 1, 0), 0)),
                  rev(sec_v), rev(sec_v + 1), pl.BlockSpec((HALO_B, wb), lambda i: (0, 0))],
        out_specs=[pl.BlockSpec((rows, 2 * wb), lambda i: (nt - 1 - i, sec_v // 2)),
                   pl.BlockSpec((HALO_B, wb), lambda i: (0, 0))],
        out_shape=[jax.ShapeDtypeStruct(dproj.shape, BF16), jax.ShapeDtypeStruct((HALO_B, wb), F32)],
        input_output_aliases={0: 0},
        scratch_shapes=[pltpu.VMEM((HALO_B + rows, wb), F32), pltpu.VMEM((rows + HALO_B, wb), F32),
                        pltpu.VMEM((8, HALO_B + rows, LANES), F32), pltpu.VMEM((8, rows + HALO_B, LANES), F32),
                        pltpu.VMEM((HALO_B * SUBLANES, wb), F32)],
        compiler_params=_params(("arbitrary",)),
    )(dproj, du, u0, u0, proj, proj, conv_w)


def _prenorm_bwd(x, r, dh, dx2, norm_g, scale, rows):
    s, d = x.shape

    def body(x_ref, r_ref, dh_ref, dx2_ref, g_ref, sc_ref, gx_ref, dsh_ref, dsc_ref, dg_ref):
        i = pl.program_id(0)
        rv = r_ref[...]
        xn = x_ref[...] * rv
        dhv = dh_ref[...]
        one_sc = 1.0 + sc_ref[...]
        dxn = dhv * one_sc * g_ref[...]
        gx_ref[...] = dx2_ref[...] + rv * (dxn - xn * jnp.mean(dxn * xn, axis=-1, keepdims=True))

        @pl.when(i == 0)
        def _():
            dsh_ref[...] = jnp.zeros_like(dsh_ref)
            dsc_ref[...] = jnp.zeros_like(dsc_ref)
            dg_ref[...] = jnp.zeros_like(dg_ref)

        dsh_ref[...] += jnp.sum(dhv, axis=0, keepdims=True)
        dsc_ref[...] += jnp.sum(dhv * (xn * g_ref[...]), axis=0, keepdims=True)
        dg_ref[...] += jnp.sum(dhv * one_sc * xn, axis=0, keepdims=True)

    blk = pl.BlockSpec((rows, d), lambda i: (i, 0))
    vec = pl.BlockSpec((1, d), lambda i: (0, 0))
    vshape = jax.ShapeDtypeStruct((1, d), F32)
    return pl.pallas_call(
        body, name="prenorm_bwd", grid=(s // rows,),
        in_specs=[blk, pl.BlockSpec((rows, 1), lambda i: (i, 0)), blk, blk, vec, vec],
        out_specs=[blk, vec, vec, vec],
        out_shape=[jax.ShapeDtypeStruct((s, d), F32), vshape, vshape, vshape],
        compiler_params=_params(("arbitrary",)),
    )(x, r, dh, dx2, norm_g, scale)


def _pad_rows(a, rows):
    return jnp.pad(a, ((0, rows - a.shape[0]), (0, 0)))


def _tile(n, want):
    t = min(n, want)
    while n % t:
        t -= 1
    return t


def kernel(x, c, norm_g, w_ada, b_ada, w_in, conv_a_w, conv_b_w, conv_b_b, ln_b_g, ln_b_b, w_out, final_g, loss_target, m_norm_g, m_w_ada, m_b_ada, m_w_in, m_conv_a_w, m_conv_b_w, m_conv_b_b, m_ln_b_g, m_ln_b_b, m_w_out, m_final_g, v_norm_g, v_w_ada, v_b_ada, v_w_in, v_conv_a_w, v_conv_b_w, v_conv_b_b, v_ln_b_g, v_ln_b_b, v_w_out, v_final_g):
    s, d = x.shape[1], x.shape[2]
    wa = conv_b_b.shape[-1]
    dmix = 2 * wa
    ns = w_in.shape[-1]
    din = N_CHIPS * ns
    r4 = w_out.shape[1]
    na = w_ada.shape[-1]
    wsh = conv_a_w.shape[-1]
    px, py, pc = _position()
    chip = 2 * px + py
    me = 4 * px + 2 * py + pc
    pos = jnp.stack([chip, pc]).astype(jnp.int32)
    x2d = x.reshape(s, d)
    target = loss_target.reshape(s, d)

    hc, ho, hrow = ns // 2, r4 // 2, d // 2
    _, cidx = _other_chips(px, py)
    hq = hc // 2
    win4 = _cast_quarters(w_in[0], _tile(d, 512), "cast_w_in")
    wout_bf = _cast_bf16(w_out[0], _tile(r4, 512), "cast_w_out")

    def gather_plan(b):
        x, y, cc = _position()
        chips, _ = _other_chips(x, y)
        xn, yn = (1 - x, y, cc), (x, 1 - y, cc)
        q0, q1 = b[0].at[2 * cc], b[0].at[2 * cc + 1]
        return ([(q0, b[2], xn), (q1, b[5], yn), (q1, b[3], xn), (q0, b[4], yn)]
                + [(b[1].at[pl.ds(cc * ho, ho), :], b[6 + k], (cx, cy, cc)) for k, (cx, cy) in enumerate(chips)])

    def gather_sent(b):
        return [(src, src, dev) for src, _, dev in gather_plan(list(b) + [None] * 7)]

    def onward_plan(b):
        x, y, cc = _position()
        sib = (x, y, 1 - cc)
        return [(b[0], b[2], (x, 1 - y, cc)), (b[1], b[3], (1 - x, y, cc)), (b[0], b[4], sib), (b[1], b[5], sib)]

    pairs = lambda n: _to_sibling([lambda ref, cc: ref] * n)

    c8 = jnp.broadcast_to(c, (8, d))
    cw = jnp.concatenate([_pad_rows(conv_a_w[0], HALO_A), _pad_rows(conv_b_w[0], HALO_B)], axis=0)
    c_all, cw_all = _gather_cond(c8, cw)
    c_rows = c_all[:, 0, :]
    cw_full = jnp.transpose(cw_all, (1, 0, 2)).reshape(HALO_A + HALO_B, wa)
    conv_a_full, conv_b_full = cw_full[:HALO_A], cw_full[HALO_A:]

    b_ada_sh = lax.dynamic_slice(b_ada, (0, chip * na), (1, na))
    mod_part = _modulation(_pad_rows(c_rows, 2 * N_DEV), w_ada[0], b_ada_sh, _tile(na, 512), "modulation")[:N_DEV]
    mod_all = _exchange_mod(mod_part)
    mod = lax.dynamic_index_in_dim(mod_all, me, axis=1, keepdims=False).reshape(1, 3 * d)
    shift, scale, gate = mod[:, :d], mod[:, d:2 * d], mod[:, 2 * d:]

    def quarter():
        return lax.empty((d, hq), BF16)

    g_sems, g_bufs, g_tok = _start_copies(
        "gather_start", gather_plan, 7,
        [win4, wout_bf] + [quarter() for _ in range(4)] + [lax.empty((ho, d), BF16) for _ in range(3)],
        after=[mod_all])
    win4, wout_bf, (x0, x1, y0, y1), lo = g_bufs[0], g_bufs[1], g_bufs[2:6], g_bufs[6:9]

    h, r = _prenorm(x2d, norm_g, scale, shift + g_tok[0, 0], _tile(s, 256))
    bm = _tile(s, 1024)
    pieces = [None, None]

    def piece(slot, half, quarters, name, own_half=None):
        where = jnp.reshape(2 * slot + half, (1,)).astype(jnp.int32)
        pieces[:] = _proj_piece(where, h, quarters, pieces[0], pieces[1], din, 2 * N_CHIPS, _tile(s, 512), name,
                                own_half=own_half)
        return pieces[0]

    proj = piece(chip, 0, (win4, win4), "proj_own0", own_half=0)
    proj = piece(chip, 1, (win4, win4), "proj_own1", own_half=1)
    x0, y1 = _wait_copies("gather_wait_a", _landed, [x0, y1], g_sems[0:4], after=[proj], send=False)
    on_sems, (x0, y1, dg0, dg1, sx0, sy1), _ = _start_copies(
        "pass_on_a", onward_plan, 4, [x0, y1] + [quarter() for _ in range(4)])
    x1, y0 = _wait_copies("gather_wait_b", _landed, [x1, y0], g_sems[4:8], after=[x0], send=False)
    pb_sems, (x1, sx1, y0, sy0), _ = _start_copies("pass_on_b", pairs(2), 2, [x1, quarter(), y0, quarter()])
    proj = piece(cidx[0], pc, (x0, x1), "proj_xa")
    proj = piece(cidx[1], pc, (y0, y1), "proj_ya")
    sx0, sy1 = _wait_copies("pass_wait_a", _landed, [sx0, sy1], on_sems[4:8], after=[proj], send=False)
    x1, sx1, y0, sy0 = _wait_copies("pass_wait_b", pairs(2), [x1, sx1, y0, sy0], pb_sems, after=[proj])
    proj = piece(cidx[0], 1 - pc, (sx0, sx1), "proj_xb")
    proj = piece(cidx[1], 1 - pc, (sy0, sy1), "proj_yb")
    x0, y1, dg0, dg1 = _wait_copies("diag_wait", lambda b: onward_plan(list(b) + [None, None])[:2],
                                    [x0, y1, dg0, dg1], on_sems[0:4], after=[proj])
    x0, y1 = _wait_copies("pass_sent_a", lambda b: [(b[0], b[0], (0, 0, 0)), (b[1], b[1], (0, 0, 0))],
                          [x0, y1], on_sems[4:8], after=[dg0], recv=False)
    pd_sems, (dg0, sd0, dg1, sd1), _ = _start_copies("pass_on_d", pairs(2), 2, [dg0, quarter(), dg1, quarter()],
                                                     after=[x0])
    proj = piece(cidx[2], pc, (dg0, dg1), "proj_da")
    dg0, sd0, dg1, sd1 = _wait_copies("pass_wait_d", pairs(2), [dg0, sd0, dg1, sd1], pd_sems, after=[proj])
    proj = piece(cidx[2], 1 - pc, (sd0, sd1), "proj_db")
    win_full = pieces[1]

    lo = _wait_copies("gather_wait_out", _landed, lo, g_sems[8:14], after=[proj], send=False)
    o_sems, o_bufs, o_tok = _start_copies(
        "pass_on_out", pairs(3), 3, [b for k in range(3) for b in (lo[k], lax.empty((ho, d), BF16))])
    win4, wout_bf = _wait_copies("gather_wait_sent", gather_sent, [win4, wout_bf], g_sems, after=[o_tok], recv=False)

    def slot_index(k, chip_, cc, others):
        if k == 0:
            return pl.ds(2 * chip_, 2)
        return 2 * others[(k - 1) % 3] + (cc if k <= 3 else 1 - cc)

    y = _mixer_a_fwd(proj, conv_a_full, wa, _tile(s, 512), _tile(wa, 512))
    u0, u = _mixer_b_conv_fwd(proj, conv_b_full, conv_b_b, wa, _tile(s, 512), _tile(wa, 256), 64)
    y = _mixer_b_gate_fwd(y, u, proj, ln_b_g, ln_b_b, wa, _tile(s, 256))
    o_bufs = _wait_copies("pass_wait_out", pairs(3), o_bufs, o_sems, after=[y])
    wout_full = _assemble("assemble_w_out", [wout_bf.reshape(2, ho, d)] + o_bufs[0::2] + o_bufs[1::2],
                          jax.ShapeDtypeStruct((2 * N_CHIPS, ho, d), BF16), slot_index)
    wout2d = wout_full.reshape(dmix, d)
    bd = _tile(d, 1024)
    o = _matmul(
        y, wout2d, grid=(s // bm, d // bd, 1),
        a_spec=pl.BlockSpec((bm, dmix), lambda i, j, k: (i, 0)),
        b_spec=pl.BlockSpec((dmix, bd), lambda i, j, k: (0, j)),
        o_spec=pl.BlockSpec((bm, bd), lambda i, j, k: (i, j)),
        out_shape=jax.ShapeDtypeStruct((s, d), F32), dims=((1,), (0,)), name="out_proj")
    dx2, do, loss_p, gfg_p, dgate_p = _loss_head(x2d, o, target, gate, final_g.reshape(1, d), _tile(s, 128))

    be = _tile(dmix, 1024)
    g_wout = _matmul(
        y, do, grid=(dmix // be, d // bd, 1),
        a_spec=pl.BlockSpec((s, be), lambda i, j, k: (0, i)),
        b_spec=pl.BlockSpec((s, bd), lambda i, j, k: (0, j)),
        o_spec=pl.BlockSpec((be, bd), lambda i, j, k: (i, j)),
        out_shape=jax.ShapeDtypeStruct((dmix, d), F32), dims=((0,), (0,)), name="grad_w_out")
    swap_out = _to_sibling([lambda ref, cc: ref.at[:, pl.ds((1 - cc) * ho, ho), :]])
    so_sems, (g_wout3, ra_out), so_tok = _start_copies(
        "swap_out_start", swap_out, 1, [g_wout.reshape(N_CHIPS, r4, d), lax.empty((N_CHIPS, ho, d), F32)])
    dy = _matmul(
        do, wout2d, grid=(s // bm, dmix // be, 1),
        a_spec=pl.BlockSpec((bm, d), lambda i, j, k: (i, 0)),
        b_spec=pl.BlockSpec((be, d), lambda i, j, k: (j, 0)),
        o_spec=pl.BlockSpec((bm, be), lambda i, j, k: (i, j)),
        out_shape=jax.ShapeDtypeStruct((s, dmix), F32), dims=((1,), (1,)), name="dy", after=[so_tok])
    g_wout3, ra_out = _wait_copies("swap_out_wait", swap_out, [g_wout3, ra_out], so_sems, after=[dy])
    q_out = _chip_partial(pos, g_wout3, ra_out, _tile(ho, 256), "chip_partial_w_out")
    po_sems, po_bufs, po_tok = _start_copies(
        "send_out_start", _slots_to_chips, 3, [q_out] + [lax.empty((ho, d), BF16) for _ in range(3)])
    dproj, dwa_p = _mixer_a_bwd(proj, dy, conv_a_full + po_tok[0, 0], wa, din, _tile(s, 128))
    dproj, du, dlng_p, dlnb_p, dcb_p = _mixer_b_gate_bwd(dproj, dy, u, proj, ln_b_g, ln_b_b, wa, _tile(s, 128))
    dproj, dwb_p = _mixer_b_conv_bwd(dproj, du, u0, proj, conv_b_full, wa, _tile(s, 256), 64)

    bn2 = _tile(ns, 896)
    swap_in = _to_sibling([lambda ref, cc: ref.at[pl.ds((1 - cc) * hrow, hrow), :]])
    slots = [cidx[0], cidx[1], cidx[2], chip]
    core_only = jnp.stack([0 * pc, pc]).astype(jnp.int32)
    g, ra, sw, q, rb, snd = [None] * 4, [None] * 4, [None] * 4, [None] * 3, [None] * 3, [None] * 3
    after = []
    for k in range(4):
        g[k] = _grad_slot(jnp.reshape(slots[k], (1,)).astype(jnp.int32), h, dproj, after, ns, bd, bn2,
                          f"grad_w_in{k}")
        sw[k], (g[k], ra[k]), tok = _start_copies(f"swap_in_start{k}", swap_in, 1,
                                                  [g[k], lax.empty((hrow, ns), F32)])
        after = [tok]
        if k >= 1:
            j = k - 1
            g[j], ra[j] = _wait_copies(f"swap_in_wait{j}", swap_in, [g[j], ra[j]], sw[j], after=[g[k]])
            part = _chip_partial(core_only, g[j][None], ra[j][None], _tile(hrow, 256), f"chip_partial_w_in{j}")
            snd[j], (q[j], rb[j]), tok2 = _start_copies(f"send_in_start{j}", _to_chip(j), 1,
                                                        [part.reshape(hrow, ns), lax.empty((hrow, ns), BF16)])
            after = [tok, tok2]
    dh = _matmul(
        dproj, win_full, grid=(s // bm, d // bd, 2 * N_CHIPS),
        a_spec=pl.BlockSpec((bm, hc), lambda i, j, k: (i, k)),
        b_spec=pl.BlockSpec((None, bd, hc), lambda i, j, k: (k, j, 0)),
        o_spec=pl.BlockSpec((bm, bd), lambda i, j, k: (i, j)),
        out_shape=jax.ShapeDtypeStruct((s, d), F32), dims=((1,), (1,)), name="dh", after=after)
    grad_x, dshift_p, dscale_p, gng_p = _prenorm_bwd(x2d, r, dh, dx2, norm_g, scale, _tile(s, 128))

    def rows_of(v):
        return _pad_rows(v.reshape(-1, wa), 8 * ((v.size // wa + 7) // 8))

    dmod = jnp.concatenate([dshift_p, dscale_p, dgate_p], axis=1)
    parts = [gng_p, dmod, dwa_p, dwb_p, dcb_p, dlng_p, dlnb_p, gfg_p,
             jnp.broadcast_to(loss_p[:, :1], (1, wa))]
    starts, packed = [], []
    for p in parts:
        starts.append(sum(q.shape[0] for q in packed))
        packed.append(rows_of(p) if p.shape[0] == 1 else p)
    small_sum, small_all = _gather_small(jnp.concatenate(packed, axis=0))

    def summed(k, rows):
        return small_sum[starts[k]:starts[k] + rows]

    grad_norm_g = summed(0, d // wa).reshape(1, d)
    grad_b_ada = summed(1, 3 * d // wa).reshape(1, 3 * d)
    grad_conv_a_full = summed(2, TAPS_A)
    grad_conv_b_full = summed(3, TAPS_B)
    grad_conv_b_b = summed(4, 1)
    grad_ln_b_g = summed(5, 1)
    grad_ln_b_b = summed(6, 1)
    grad_final_g = summed(7, d // wa).reshape(d)
    loss = summed(8, 1)[0, 0]
    grad_conv_a_w = lax.dynamic_slice(grad_conv_a_full, (0, chip * wsh), (TAPS_A, wsh))
    grad_conv_b_w = lax.dynamic_slice(grad_conv_b_full, (0, chip * wsh), (TAPS_B, wsh))
    dmod_all = small_all[:, starts[1]:starts[1] + 3 * d // wa, :].reshape(N_DEV, 3 * d)
    dmod_sh = lax.dynamic_slice(dmod_all, (0, chip * na), (N_DEV, na))

    def pairs_to_chips(b):
        x, y, cc = _position()
        chips, _ = _other_chips(x, y)
        return [(b[2 * k], b[2 * k + 1], (cx, cy, cc)) for k, (cx, cy) in enumerate(chips)]

    po_bufs = _wait_copies("send_out_wait", _slots_to_chips, po_bufs, po_sems, after=[small_sum])
    gh_out = _final_half(pos, g_wout3, ra_out, po_bufs[1:], _tile(ho, 256), "final_half_w_out")
    g[3], ra[3] = _wait_copies("swap_in_wait3", swap_in, [g[3], ra[3]], sw[3], after=[small_sum])
    in_bufs = _wait_copies("send_in_wait", pairs_to_chips, [b for k in range(3) for b in (q[k], rb[k])],
                           snd[0] + snd[1] + snd[2], after=[small_sum])
    gh_in = _final_half(core_only, g[3][None], ra[3][None], in_bufs[1::2], _tile(hrow, 256), "final_half_w_in")
    sh_sems, sh_bufs, sh_tok = _start_copies("share_start", _halves_to_sibling, 2, [gh_in, gh_out])

    grad_w_ada, d_wada, nm_wada, nv_wada = _adam_ada(c_rows.T, dmod_sh + sh_tok[0, 0], w_ada[0], m_w_ada[0],
                                                     v_w_ada[0], _tile(d, 128), "adam_w_ada")
    grad_w_in, grad_w_out = _wait_copies("share_wait", _halves_to_sibling, sh_bufs, sh_sems, after=[d_wada])
    d_win, nm_win, nv_win = _adam(w_in[0], grad_w_in, m_w_in[0], v_w_in[0], _tile(d, 128), "adam_w_in")
    d_wout, nm_wout, nv_wout = _adam(w_out[0], grad_w_out, m_w_out[0], v_w_out[0], _tile(r4, 128), "adam_w_out")

    def small_adam(w, g, m, v, name):
        shape = w.shape
        w2 = w.reshape(-1, shape[-1])
        out = _adam(w2, g.reshape(w2.shape), m.reshape(w2.shape), v.reshape(w2.shape), w2.shape[0], name)
        return [o_.reshape(shape) for o_ in out]

    small = {
        "norm_g": small_adam(norm_g, grad_norm_g, m_norm_g, v_norm_g, "adam_norm_g"),
        "b_ada": small_adam(b_ada, grad_b_ada, m_b_ada, v_b_ada, "adam_b_ada"),
        "conv_a_w": small_adam(conv_a_w, grad_conv_a_w, m_conv_a_w, v_conv_a_w, "adam_conv_a_w"),
        "conv_b_w": small_adam(conv_b_w, grad_conv_b_w, m_conv_b_w, v_conv_b_w, "adam_conv_b_w"),
        "conv_b_b": small_adam(conv_b_b, grad_conv_b_b, m_conv_b_b, v_conv_b_b, "adam_conv_b_b"),
        "ln_b_g": small_adam(ln_b_g, grad_ln_b_g, m_ln_b_g, v_ln_b_g, "adam_ln_b_g"),
        "ln_b_b": small_adam(ln_b_b, grad_ln_b_b, m_ln_b_b, v_ln_b_b, "adam_ln_b_b"),
        "final_g": small_adam(final_g.reshape(1, d), grad_final_g, m_final_g.reshape(1, d),
                              v_final_g.reshape(1, d), "adam_final_g"),
    }
    small["final_g"] = [o_.reshape(d) for o_ in small["final_g"]]
    big = {
        "w_ada": [a[None] for a in (d_wada, nm_wada, nv_wada)],
        "w_in": [a[None] for a in (d_win, nm_win, nv_win)],
        "w_out": [a[None] for a in (d_wout, nm_wout, nv_wout)],
    }
    upd = {**small, **big}
    order = ["norm_g", "w_ada", "b_ada", "w_in", "conv_a_w", "conv_b_w", "conv_b_b", "ln_b_g", "ln_b_b",
             "w_out", "final_g"]
    grads = {
        "norm_g": grad_norm_g, "w_ada": grad_w_ada[None], "b_ada": grad_b_ada, "w_in": grad_w_in[None],
        "conv_a_w": grad_conv_a_w[None], "conv_b_w": grad_conv_b_w[None], "conv_b_b": grad_conv_b_b,
        "ln_b_g": grad_ln_b_g, "ln_b_b": grad_ln_b_b, "w_out": grad_w_out[None], "final_g": grad_final_g,
    }
    return (loss, grad_x.reshape(1, s, d), *[grads[n] for n in order], *[upd[n][0] for n in order],
            *[upd[n][1] for n in order], *[upd[n][2] for n in order])
```

```python
import functools

import jax
import jax.numpy as jnp
from jax import lax
from jax.experimental import pallas as pl
from jax.experimental.pallas import tpu as pltpu

F32 = jnp.float32
BF16 = jnp.bfloat16
EPS = 1e-6
N_CHIPS = 4
N_DEV = 8
TAPS_A = 3
TAPS_B = 31
HALO_A = 8
HALO_B = 32
LANES = 128
SUBLANES = 8
ADAM_LR = 0.001
ADAM_B1 = 0.9
ADAM_B2 = 0.999
ADAM_EPS = 1e-08
ADAM_WD = 0.01
ADAM_STEP = 10
VMEM_LIMIT = 56 * 1024 * 1024
MESH = pl.DeviceIdType.MESH
ANY = pl.BlockSpec(memory_space=pl.ANY)
VMEM = pl.BlockSpec(memory_space=pltpu.VMEM)
HBM_SPEC = pl.BlockSpec(memory_space=pltpu.HBM)
SEM_SPEC = pl.BlockSpec(memory_space=pltpu.SEMAPHORE)
EFFECT = pltpu.SideEffectType.DATAFLOW_SIDE_EFFECTING


def _params(sem=None):
    return pltpu.CompilerParams(dimension_semantics=sem, vmem_limit_bytes=VMEM_LIMIT)


def _sigmoid(v):
    return jax.nn.sigmoid(v)


def _position():
    return lax.axis_index("x"), lax.axis_index("y"), lax.axis_index("c")


def _rcopy(src, dst, ssem, rsem, dev):
    return pltpu.make_async_remote_copy(src_ref=src, dst_ref=dst, send_sem=ssem, recv_sem=rsem,
                                        device_id=dev, device_id_type=MESH)


def _other_chips(x, y):
    chips = [(1 - x, y), (x, 1 - y), (1 - x, 1 - y)]
    return chips, [2 * cx + cy for cx, cy in chips]


def _cast_bf16(a, rows, name):
    m, n = a.shape

    def body(a_ref, o_ref):
        o_ref[...] = a_ref[...].astype(BF16)

    return pl.pallas_call(
        body, name=name, grid=(m // rows,),
        in_specs=[pl.BlockSpec((rows, n), lambda i: (i, 0))],
        out_specs=pl.BlockSpec((rows, n), lambda i: (i, 0)),
        out_shape=jax.ShapeDtypeStruct((m, n), BF16),
        compiler_params=_params(("parallel",)),
    )(a)


def _cast_quarters(a, rows, name):
    m, n = a.shape
    hq = n // 4

    def body(a_ref, o_ref):
        o_ref[...] = a_ref[...].astype(BF16)

    return pl.pallas_call(
        body, name=name, grid=(4, m // rows),
        in_specs=[pl.BlockSpec((rows, hq), lambda q, i: (i, q))],
        out_specs=pl.BlockSpec((None, rows, hq), lambda q, i: (q, i, 0)),
        out_shape=jax.ShapeDtypeStruct((4, m, hq), BF16),
        compiler_params=_params(("parallel", "parallel")),
    )(a)


def _proj_piece(where, h, quarters, proj, w_all, din, n_pieces, bm, name, own_half=None):
    s, d = h.shape
    hq = quarters[0].shape[-1]
    nm = s // bm
    if own_half is None:
        q_specs = [pl.BlockSpec((d, hq), lambda i, p: (0, 0), pipeline_mode=pl.Buffered(1))] * 2
    else:
        q_specs = [pl.BlockSpec((None, d, hq), lambda i, p, k=k: (2 * own_half + k, 0, 0),
                                pipeline_mode=pl.Buffered(1)) for k in range(2)]

    def body(p_ref, h_ref, q0_ref, q1_ref, *rest):
        o_ref, wall_ref, wbuf, sem = rest[-4:]
        i = pl.program_id(0)
        filed = pltpu.make_async_copy(wbuf, wall_ref.at[p_ref[0]], sem)

        @pl.when(i == 0)
        def _():
            wbuf[:, 0:hq] = q0_ref[...]
            wbuf[:, hq:2 * hq] = q1_ref[...]
            filed.start()

        o_ref[...] = jnp.dot(h_ref[...], wbuf[...], preferred_element_type=F32)

        @pl.when(i == nm - 1)
        def _():
            filed.wait()

    args, extra, alias = [where, h, quarters[0], quarters[1]], [], {}
    if proj is not None:
        args, extra, alias = args + [proj, w_all], [ANY, ANY], {4: 0, 5: 1}
    return pl.pallas_call(
        body, name=name,
        grid_spec=pltpu.PrefetchScalarGridSpec(
            num_scalar_prefetch=1, grid=(nm,),
            in_specs=[pl.BlockSpec((bm, d), lambda i, p: (i, 0))] + q_specs + extra,
            out_specs=[pl.BlockSpec((bm, 2 * hq), lambda i, p: (i, p[0])), ANY],
            scratch_shapes=[pltpu.VMEM((d, 2 * hq), BF16), pltpu.SemaphoreType.DMA]),
        out_shape=[jax.ShapeDtypeStruct((s, din), F32), jax.ShapeDtypeStruct((n_pieces, d, 2 * hq), BF16)],
        input_output_aliases=alias,
        compiler_params=_params(("arbitrary",)),
    )(*args)


def _grad_slot(slot, h, dproj, after, ns, bd, bn, name):
    s, d = h.shape
    nb = ns // bn

    def body(slot_ref, h_ref, dp_ref, *rest):
        rest[-1][...] = lax.dot_general(h_ref[...], dp_ref[...], (((0,), (0,)), ((), ())),
                                        preferred_element_type=F32)

    return pl.pallas_call(
        body, name=name,
        grid_spec=pltpu.PrefetchScalarGridSpec(
            num_scalar_prefetch=1, grid=(d // bd, nb),
            in_specs=[pl.BlockSpec((s, bd), lambda i, j, sl: (0, i)),
                      pl.BlockSpec((s, bn), lambda i, j, sl: (0, sl[0] * nb + j))] + [ANY] * len(after),
            out_specs=pl.BlockSpec((bd, bn), lambda i, j, sl: (i, j))),
        out_shape=jax.ShapeDtypeStruct((d, ns), F32),
        compiler_params=_params(("parallel", "parallel")),
    )(slot, h, dproj, *after)


def _matmul(a, b, *, grid, a_spec, b_spec, o_spec, out_shape, dims, name, after=()):
    nk = grid[2]
    n_after = len(after)

    def body(a_ref, b_ref, *rest):
        o_ref, acc = rest[n_after], rest[n_after + 1:]
        p = lax.dot_general(a_ref[...], b_ref[...], (dims, ((), ())), preferred_element_type=F32)
        if nk == 1:
            o_ref[...] = p.astype(o_ref.dtype)
        else:
            acc_ref, = acc
            k = pl.program_id(2)

            @pl.when(k == 0)
            def _():
                acc_ref[...] = p

            @pl.when(k > 0)
            def _():
                acc_ref[...] += p

            @pl.when(k == nk - 1)
            def _():
                o_ref[...] = acc_ref[...].astype(o_ref.dtype)

    block = [d for d in o_spec.block_shape if d is not None]
    scratch = [pltpu.VMEM(tuple(block), F32)] if nk > 1 else []
    return pl.pallas_call(
        body, name=name, grid=grid, in_specs=[a_spec, b_spec] + [ANY] * n_after, out_specs=o_spec,
        out_shape=out_shape, scratch_shapes=scratch,
        compiler_params=_params(("parallel", "parallel", "arbitrary")),
    )(a, b, *after)


def _adam_math(w, g, m, v):
    m = ADAM_B1 * m + (1.0 - ADAM_B1) * g
    v = ADAM_B2 * v + (1.0 - ADAM_B2) * (g * g)
    m_hat = m / (1.0 - ADAM_B1 ** ADAM_STEP)
    v_hat = v / (1.0 - ADAM_B2 ** ADAM_STEP)
    delta = -ADAM_LR * (m_hat / (jnp.sqrt(v_hat) + ADAM_EPS) + ADAM_WD * w)
    return delta, m, v


def _adam(w, g, m, v, rows, name, return_grad=False):
    r, n = w.shape

    def body(w_ref, g_ref, m_ref, v_ref, *out):
        gv = g_ref[...]
        d, mo, vo = _adam_math(w_ref[...], gv, m_ref[...], v_ref[...])
        for o_ref, val in zip(out, ([gv] if return_grad else []) + [d, mo, vo]):
            o_ref[...] = val

    spec = pl.BlockSpec((rows, n), lambda i: (i, 0))
    shape = jax.ShapeDtypeStruct((r, n), F32)
    n_out = 4 if return_grad else 3
    return pl.pallas_call(
        body, name=name, grid=(r // rows,), in_specs=[spec] * 4, out_specs=[spec] * n_out,
        out_shape=[shape] * n_out, compiler_params=_params(("parallel",)),
    )(w, g, m, v)


def _adam_ada(c_cols, dmod, w, m, v, rows, name):
    r, n = w.shape

    def body(c_ref, dm_ref, w_ref, m_ref, v_ref, g_ref, d_ref, mo_ref, vo_ref):
        cv = c_ref[...]
        c_act = cv * _sigmoid(cv)
        g = c_act[:, 0:1] * dm_ref[0:1, :]
        for b in range(1, N_DEV):
            g = g + c_act[:, b:b + 1] * dm_ref[b:b + 1, :]
        d, mo, vo = _adam_math(w_ref[...], g, m_ref[...], v_ref[...])
        g_ref[...] = g
        d_ref[...] = d
        mo_ref[...] = mo
        vo_ref[...] = vo

    spec = pl.BlockSpec((rows, n), lambda i: (i, 0))
    shape = jax.ShapeDtypeStruct((r, n), F32)
    return pl.pallas_call(
        body, name=name, grid=(r // rows,),
        in_specs=[pl.BlockSpec((rows, N_DEV), lambda i: (i, 0)), pl.BlockSpec((N_DEV, n), lambda i: (0, 0)),
                  spec, spec, spec],
        out_specs=[spec] * 4, out_shape=[shape] * 4, compiler_params=_params(("parallel",)),
    )(c_cols, dmod, w, m, v)


def _start_copies(name, plan, n, bufs, after=()):
    nb, na = len(bufs), len(after)

    def body(*refs):
        sems = refs[nb + na:nb + na + 2 * n]
        for k, (src, dst, dev) in enumerate(plan(refs[:nb])):
            _rcopy(src, dst, sems[2 * k], sems[2 * k + 1], dev).start()
        refs[-1][...] = jnp.zeros((8, 128), F32)

    outs = pl.pallas_call(
        body, name=name,
        out_shape=[pltpu.SemaphoreType.DMA(())] * (2 * n) + [pltpu.HBM(a.shape, a.dtype) for a in bufs]
        + [jax.ShapeDtypeStruct((8, 128), F32)],
        in_specs=[HBM_SPEC] * nb + [ANY] * na, out_specs=[SEM_SPEC] * (2 * n) + [HBM_SPEC] * nb + [VMEM],
        input_output_aliases={i: 2 * n + i for i in range(nb)},
        compiler_params=pltpu.CompilerParams(has_side_effects=EFFECT),
    )(*[pltpu.with_memory_space_constraint(a, pltpu.HBM) for a in bufs], *after)
    return list(outs[:2 * n]), list(outs[2 * n:2 * n + nb]), outs[-1]


def _wait_copies(name, plan, bufs, sems, after=(), send=True, recv=True):
    nb, nsem = len(bufs), len(sems)

    def body(*refs):
        s = refs[nb:nb + nsem]
        for k, (src, dst, dev) in enumerate(plan(refs[:nb])):
            cp = _rcopy(src, dst, s[2 * k], s[2 * k + 1], dev)
            if send:
                cp.wait_send()
            if recv:
                cp.wait_recv()

    outs = pl.pallas_call(
        body, name=name, out_shape=[pltpu.HBM(a.shape, a.dtype) for a in bufs],
        in_specs=[HBM_SPEC] * nb + [SEM_SPEC] * nsem + [ANY] * len(after), out_specs=[HBM_SPEC] * nb,
        input_output_aliases={i: i for i in range(nb)},
        compiler_params=pltpu.CompilerParams(has_side_effects=EFFECT),
    )(*bufs, *sems, *after)
    return list(outs)


def _to_sibling(views):
    def plan(b):
        x, y, c = _position()
        return [(view(b[2 * k], c), b[2 * k + 1], (x, y, 1 - c)) for k, view in enumerate(views)]
    return plan


def _to_chip(k):
    def plan(b):
        x, y, c = _position()
        cx, cy = _other_chips(x, y)[0][k]
        return [(b[0], b[1], (cx, cy, c))]
    return plan


def _slots_to_chips(b):
    x, y, c = _position()
    chips, cidx = _other_chips(x, y)
    return [(b[0].at[cidx[k]], b[1 + k], (cx, cy, c)) for k, (cx, cy) in enumerate(chips)]


def _halves_to_sibling(b):
    x, y, c = _position()
    views = [r.at[pl.ds(c * (r.shape[0] // 2), r.shape[0] // 2), :] for r in b]
    return [(v, v, (x, y, 1 - c)) for v in views]


def _landed(b):
    x, y, c = _position()
    return [(ref, ref, (x, y, c)) for ref in b]


def _assemble(name, pieces, out_shape, index_of):
    n = len(pieces)

    def body(*refs):
        out_ref, sem = refs[n], refs[n + 1]
        x, y, c = _position()
        _, cidx = _other_chips(x, y)
        cps = [pltpu.make_async_copy(refs[k], out_ref.at[index_of(k, 2 * x + y, c, cidx)], sem.at[k]) for k in range(n)]
        for cp in cps:
            cp.start()
        for cp in cps:
            cp.wait()

    return pl.pallas_call(
        body, name=name, in_specs=[VMEM] * n, out_specs=ANY, out_shape=out_shape,
        scratch_shapes=[pltpu.SemaphoreType.DMA((n,))],
        compiler_params=pltpu.CompilerParams(vmem_limit_bytes=VMEM_LIMIT),
    )(*pieces)


def _gather_cond(c8, cw):
    def body(c8_ref, cw_ref, call_ref, cwall_ref, ssem, rsem, lsem):
        x, y, c = _position()
        chip = 2 * x + y
        me = 4 * x + 2 * y + c
        chips, cidx = _other_chips(x, y)
        own = [pltpu.make_async_copy(c8_ref, call_ref.at[me], lsem.at[0]),
               pltpu.make_async_copy(cw_ref, cwall_ref.at[chip], lsem.at[1])]
        for cp in own:
            cp.start()
        sends = [_rcopy(cw_ref, cwall_ref.at[chip], ssem.at[k], rsem.at[k], (cx, cy, c))
                 for k, (cx, cy) in enumerate(chips)]
        for mask in range(1, N_DEV):
            fx, fy, fc = (mask >> 2) & 1, (mask >> 1) & 1, mask & 1
            dev = (1 - x if fx else x, 1 - y if fy else y, 1 - c if fc else c)
            sends.append(_rcopy(c8_ref, call_ref.at[me], ssem.at[2 + mask], rsem.at[2 + mask], dev))
        for cp in sends:
            cp.start()
        for k in range(3):
            slot = cwall_ref.at[cidx[k]]
            _rcopy(slot, slot, ssem.at[k], rsem.at[k], (x, y, c)).wait_recv()
        for mask in range(1, N_DEV):
            slot = call_ref.at[jnp.bitwise_xor(me, mask)]
            _rcopy(slot, slot, ssem.at[2 + mask], rsem.at[2 + mask], (x, y, c)).wait_recv()
        for cp in sends:
            cp.wait_send()
        for cp in own:
            cp.wait()

    return pl.pallas_call(
        body, name="gather_cond", in_specs=[VMEM, VMEM], out_specs=[VMEM, VMEM],
        out_shape=[jax.ShapeDtypeStruct((N_DEV,) + c8.shape, F32), jax.ShapeDtypeStruct((N_CHIPS,) + cw.shape, F32)],
        scratch_shapes=[pltpu.SemaphoreType.DMA((10,)), pltpu.SemaphoreType.DMA((10,)), pltpu.SemaphoreType.DMA((2,))],
    )(c8, cw)


def _exchange_mod(mod_part):
    def body(mp_ref, out_ref, ssem, rsem, lsem):
        x, y, c = _position()
        chip = 2 * x + y
        chips, cidx = _other_chips(x, y)
        own = pltpu.make_async_copy(mp_ref, out_ref.at[chip], lsem)
        own.start()
        sends = [_rcopy(mp_ref, out_ref.at[chip], ssem.at[k], rsem.at[k], (cx, cy, c))
                 for k, (cx, cy) in enumerate(chips)]
        for cp in sends:
            cp.start()
        for k in range(3):
            slot = out_ref.at[cidx[k]]
            _rcopy(slot, slot, ssem.at[k], rsem.at[k], (x, y, c)).wait_recv()
        for cp in sends:
            cp.wait_send()
        own.wait()

    return pl.pallas_call(
        body, name="exchange_mod", in_specs=[VMEM], out_specs=VMEM,
        out_shape=jax.ShapeDtypeStruct((N_CHIPS,) + mod_part.shape, F32),
        scratch_shapes=[pltpu.SemaphoreType.DMA((3,)), pltpu.SemaphoreType.DMA((3,)), pltpu.SemaphoreType.DMA],
    )(mod_part)


def _gather_small(pack):
    rows, n = pack.shape

    def body(p_ref, sum_ref, all_ref, ssem, rsem, lsem):
        x, y, c = _position()
        me = 4 * x + 2 * y + c
        sib = (x, y, 1 - c)
        chips, cidx = _other_chips(x, y)
        own = pltpu.make_async_copy(p_ref, all_ref.at[me], lsem)
        own.start()
        sends = [_rcopy(p_ref, all_ref.at[me], ssem.at[0], rsem.at[0], sib)]
        sends += [_rcopy(p_ref, all_ref.at[me], ssem.at[1 + k], rsem.at[1 + k], (cx, cy, c))
                  for k, (cx, cy) in enumerate(chips)]
        for cp in sends:
            cp.start()
        for k in range(3):
            slot = all_ref.at[2 * cidx[k] + c]
            _rcopy(slot, slot, ssem.at[1 + k], rsem.at[1 + k], sib).wait_recv()
            fw = _rcopy(slot, slot, ssem.at[4 + k], rsem.at[4 + k], sib)
            fw.start()
            sends.append(fw)
        slot = all_ref.at[jnp.bitwise_xor(me, 1)]
        _rcopy(slot, slot, ssem.at[0], rsem.at[0], sib).wait_recv()
        for k in range(3):
            slot = all_ref.at[2 * cidx[k] + 1 - c]
            _rcopy(slot, slot, ssem.at[4 + k], rsem.at[4 + k], sib).wait_recv()
        for cp in sends:
            cp.wait_send()
        own.wait()
        acc = all_ref[0]
        for k in range(1, N_DEV):
            acc = acc + all_ref[k]
        sum_ref[...] = acc

    return pl.pallas_call(
        body, name="gather_small", in_specs=[VMEM], out_specs=[VMEM, VMEM],
        out_shape=[jax.ShapeDtypeStruct((rows, n), F32), jax.ShapeDtypeStruct((N_DEV, rows, n), F32)],
        scratch_shapes=[pltpu.SemaphoreType.DMA((7,)), pltpu.SemaphoreType.DMA((7,)), pltpu.SemaphoreType.DMA],
        compiler_params=pltpu.CompilerParams(vmem_limit_bytes=VMEM_LIMIT),
    )(pack)


def _chip_partial(pos, g, recv, rows, name):
    ns, full, n = g.shape
    h = full // 2
    nb = h // rows

    def body(pos_ref, g_ref, r_ref, o_ref):
        o_ref[...] = (g_ref[...] + r_ref[...]).astype(BF16)

    return pl.pallas_call(
        body, name=name,
        grid_spec=pltpu.PrefetchScalarGridSpec(
            num_scalar_prefetch=1, grid=(ns, nb),
            in_specs=[pl.BlockSpec((None, rows, n), lambda s, i, p: (s, p[1] * nb + i, 0)),
                      pl.BlockSpec((None, rows, n), lambda s, i, p: (s, i, 0))],
            out_specs=pl.BlockSpec((None, rows, n), lambda s, i, p: (s, i, 0))),
        out_shape=jax.ShapeDtypeStruct((ns, h, n), BF16),
        compiler_params=_params(("parallel", "parallel")),
    )(pos, g, recv)


def _final_half(pos, g, recv_a, recv_b, rows, name):
    ns, full, n = g.shape
    h = full // 2
    nb = h // rows

    def body(pos_ref, g_ref, ra_ref, rb0_ref, rb1_ref, rb2_ref, o_ref):
        acc = g_ref[...] + ra_ref[...]
        for rb_ref in (rb0_ref, rb1_ref, rb2_ref):
            acc = acc + rb_ref[...].astype(F32)
        o_ref[...] = acc

    part = pl.BlockSpec((rows, n), lambda i, p: (i, 0))
    return pl.pallas_call(
        body, name=name,
        grid_spec=pltpu.PrefetchScalarGridSpec(
            num_scalar_prefetch=1, grid=(nb,),
            in_specs=[pl.BlockSpec((None, rows, n), lambda i, p: (p[0], p[1] * nb + i, 0)),
                      pl.BlockSpec((None, rows, n), lambda i, p: (p[0], i, 0)), part, part, part],
            out_specs=pl.BlockSpec((rows, n), lambda i, p: (p[1] * nb + i, 0))),
        out_shape=jax.ShapeDtypeStruct((full, n), F32),
        compiler_params=_params(("parallel",)),
    )(pos, g, recv_a, *recv_b)


def _modulation(c_rows, w_ada, b_ada, cols, name):
    d, n = w_ada.shape
    rows = c_rows.shape[0]

    def body(c_ref, w_ref, b_ref, o_ref):
        cv = c_ref[...]
        c_act = (cv * _sigmoid(cv)).astype(BF16)
        o_ref[...] = jnp.dot(c_act, w_ref[...].astype(BF16), preferred_element_type=F32) + b_ref[...]

    return pl.pallas_call(
        body, name=name, grid=(n // cols,),
        in_specs=[pl.BlockSpec((rows, d), lambda j: (0, 0)), pl.BlockSpec((d, cols), lambda j: (0, j)),
                  pl.BlockSpec((1, cols), lambda j: (0, j))],
        out_specs=pl.BlockSpec((rows, cols), lambda j: (0, j)),
        out_shape=jax.ShapeDtypeStruct((rows, n), F32),
        compiler_params=_params(("parallel",)),
    )(c_rows, w_ada, b_ada)


def _prenorm(x, norm_g, scale, shift, rows):
    s, d = x.shape

    def body(x_ref, g_ref, sc_ref, sh_ref, h_ref, r_ref):
        xv = x_ref[...]
        r = lax.rsqrt(jnp.mean(xv * xv, axis=-1, keepdims=True) + EPS)
        h = (xv * r * g_ref[...]) * (1.0 + sc_ref[...]) + sh_ref[...]
        h_ref[...] = h.astype(BF16)
        r_ref[...] = r

    vec = pl.BlockSpec((1, d), lambda i: (0, 0))
    return pl.pallas_call(
        body, name="prenorm", grid=(s // rows,),
        in_specs=[pl.BlockSpec((rows, d), lambda i: (i, 0)), vec, vec, vec],
        out_specs=[pl.BlockSpec((rows, d), lambda i: (i, 0)), pl.BlockSpec((rows, 1), lambda i: (i, 0))],
        out_shape=[jax.ShapeDtypeStruct((s, d), BF16), jax.ShapeDtypeStruct((s, 1), F32)],
        compiler_params=_params(("parallel",)),
    )(x, norm_g, scale, shift)


def _mixer_a_fwd(proj, conv_w, wa, rows, cols):
    s = proj.shape[0]
    ncb = wa // cols

    def body(ab_ref, ac_ref, ax_ref, az_ref, w_ref, y_ref, qbuf):
        t = pl.program_id(1)

        @pl.when(t == 0)
        def _():
            qbuf[0:HALO_A, :] = jnp.zeros((HALO_A, cols), F32)

        q = ac_ref[...] * ax_ref[...]
        qbuf[HALO_A:HALO_A + rows, :] = q
        conv = w_ref[2:3, :] * q
        for k in range(TAPS_A - 1):
            off = HALO_A - (TAPS_A - 1) + k
            conv = conv + w_ref[k:k + 1, :] * qbuf[off:off + rows, :]
        zv = az_ref[...]
        y_ref[...] = (ab_ref[...] * conv * (zv * _sigmoid(zv))).astype(BF16)
        qbuf[0:HALO_A, :] = qbuf[rows:rows + HALO_A, :]

    def sec(k):
        return pl.BlockSpec((rows, cols), lambda cb, t, k=k: (t, k * ncb + cb))

    return pl.pallas_call(
        body, name="mixer_a_fwd", grid=(ncb, s // rows),
        in_specs=[sec(0), sec(1), sec(2), sec(3), pl.BlockSpec((HALO_A, cols), lambda cb, t: (0, cb))],
        out_specs=pl.BlockSpec((rows, cols), lambda cb, t: (t, cb)),
        out_shape=jax.ShapeDtypeStruct((s, 2 * wa), BF16),
        scratch_shapes=[pltpu.VMEM((HALO_A + rows, cols), F32)],
        compiler_params=_params(("parallel", "arbitrary")),
    )(proj, proj, proj, proj, conv_w)


def _shifted_back(dst, src, lo, hi, cs):
    for n in range(8):
        dst[n, lo:hi, :] = src[lo - n:hi - n, cs]


def _shifted_fwd(dst, src, lo, hi, cs):
    for n in range(8):
        dst[n, lo:hi, :] = src[lo + n:hi + n, cs]


def _mixer_b_conv_fwd(proj, conv_w, conv_b, wa, rows, cols, chunk):
    s = proj.shape[0]
    wb = conv_w.shape[1]
    ncb = wb // cols
    sec0 = 4 * wa // cols

    def body(bv_ref, bg_ref, w_ref, b_ref, u0_ref, u_ref, ubuf, sh):
        t = pl.program_id(1)

        @pl.when(t == 0)
        def _():
            ubuf[0:HALO_B, :] = jnp.zeros((HALO_B, cols), F32)

        u0 = bv_ref[...] * _sigmoid(bg_ref[...])
        u0_ref[...] = u0
        ubuf[HALO_B:HALO_B + rows, :] = u0
        for lc in range(cols // LANES):
            cs = slice(lc * LANES, (lc + 1) * LANES)
            _shifted_back(sh, ubuf, 8, HALO_B + rows, cs)
            taps = [w_ref[k:k + 1, cs] for k in range(TAPS_B)]
            bias = b_ref[:, cs]

            def row_chunk(rc, carry, cs=cs, taps=taps, bias=bias):
                base = pl.multiple_of(rc * chunk, chunk)
                acc = jnp.zeros((chunk, LANES), F32)
                for k in range(TAPS_B):
                    mq, n = divmod(TAPS_B - 1 - k, 8)
                    acc = acc + taps[k] * sh[n, pl.ds(HALO_B - 8 * mq + base, chunk), :]
                u_ref[pl.ds(base, chunk), cs] = acc + bias
                return carry

            lax.fori_loop(0, rows // chunk, row_chunk, 0)
        ubuf[0:HALO_B, :] = ubuf[rows:rows + HALO_B, :]

    return pl.pallas_call(
        body, name="mixer_b_conv_fwd", grid=(ncb, s // rows),
        in_specs=[pl.BlockSpec((rows, cols), lambda cb, t: (t, sec0 + cb)),
                  pl.BlockSpec((rows, cols), lambda cb, t: (t, sec0 + ncb + cb)),
                  pl.BlockSpec((HALO_B, cols), lambda cb, t: (0, cb)),
                  pl.BlockSpec((1, cols), lambda cb, t: (0, cb))],
        out_specs=[pl.BlockSpec((rows, cols), lambda cb, t: (t, cb))] * 2,
        out_shape=[jax.ShapeDtypeStruct((s, wb), F32)] * 2,
        scratch_shapes=[pltpu.VMEM((HALO_B + rows, cols), F32), pltpu.VMEM((8, HALO_B + rows, LANES), F32)],
        compiler_params=_params(("parallel", "arbitrary")),
    )(proj, proj, conv_w, conv_b)


def _layernorm_stats(u):
    mu = jnp.mean(u, axis=-1, keepdims=True)
    xc = u - mu
    var = jnp.mean(xc * xc, axis=-1, keepdims=True)
    return xc * lax.rsqrt(var + EPS), lax.rsqrt(var + EPS)


def _mixer_b_gate_fwd(y, u, proj, ln_g, ln_b, wa, rows):
    s, wb = u.shape
    sec_z = (4 * wa + 2 * wb) // wb

    def body(y_in, u_ref, bz_ref, g_ref, b_ref, y_ref):
        uh, _ = _layernorm_stats(u_ref[...])
        ln = uh * g_ref[...] + b_ref[...]
        zv = bz_ref[...]
        y_ref[...] = ((ln * _sigmoid(ln)) * (zv * _sigmoid(zv))).astype(BF16)

    vec = pl.BlockSpec((1, wb), lambda i: (0, 0))
    return pl.pallas_call(
        body, name="mixer_b_gate_fwd", grid=(s // rows,),
        in_specs=[ANY, pl.BlockSpec((rows, wb), lambda i: (i, 0)), pl.BlockSpec((rows, wb), lambda i: (i, sec_z)),
                  vec, vec],
        out_specs=pl.BlockSpec((rows, wb), lambda i: (i, wa // wb)),
        out_shape=jax.ShapeDtypeStruct(y.shape, BF16), input_output_aliases={0: 0},
        compiler_params=_params(("parallel",)),
    )(y, u, proj, ln_g, ln_b)


def _loss_head(x, o, target, gate, final_g, rows):
    s, d = x.shape

    def body(x_ref, o_ref, t_ref, gate_ref, fg_ref, dx2_ref, do_ref, loss_ref, gfg_ref, dgate_ref):
        i = pl.program_id(0)
        ov = o_ref[...]
        x2 = x_ref[...] + gate_ref[...] * ov
        r2 = lax.rsqrt(jnp.mean(x2 * x2, axis=-1, keepdims=True) + EPS)
        xn2 = x2 * r2
        diff = xn2 * fg_ref[...] - t_ref[...]
        dout = diff * (1.0 / d)
        dxn2 = dout * fg_ref[...]
        dx2 = r2 * (dxn2 - xn2 * jnp.mean(dxn2 * xn2, axis=-1, keepdims=True))
        dx2_ref[...] = dx2
        do_ref[...] = (gate_ref[...] * dx2).astype(BF16)
        loss_part = 0.5 * jnp.sum(jnp.mean(diff * diff, axis=-1, keepdims=True), axis=0, keepdims=True)
        gfg_part = jnp.sum(dout * xn2, axis=0, keepdims=True)
        dgate_part = jnp.sum(dx2 * ov, axis=0, keepdims=True)

        @pl.when(i == 0)
        def _():
            loss_ref[...] = jnp.zeros_like(loss_ref)
            gfg_ref[...] = jnp.zeros_like(gfg_ref)
            dgate_ref[...] = jnp.zeros_like(dgate_ref)

        loss_ref[...] += jnp.broadcast_to(loss_part, loss_ref.shape)
        gfg_ref[...] += gfg_part
        dgate_ref[...] += dgate_part

    blk = pl.BlockSpec((rows, d), lambda i: (i, 0))
    vec = pl.BlockSpec((1, d), lambda i: (0, 0))
    return pl.pallas_call(
        body, name="loss_head", grid=(s // rows,),
        in_specs=[blk, blk, blk, vec, vec],
        out_specs=[blk, blk, pl.BlockSpec((1, 128), lambda i: (0, 0)), vec, vec],
        out_shape=[jax.ShapeDtypeStruct((s, d), F32), jax.ShapeDtypeStruct((s, d), BF16),
                   jax.ShapeDtypeStruct((1, 128), F32), jax.ShapeDtypeStruct((1, d), F32),
                   jax.ShapeDtypeStruct((1, d), F32)],
        compiler_params=_params(("arbitrary",)),
    )(x, o, target, gate, final_g)


def _mixer_a_bwd(proj, dy, conv_w, wa, din, rows):
    s = proj.shape[0]
    nt = s // rows
    per8 = rows // HALO_A

    def body(ab_ref, ac_ref, ax_ref, az_ref, hc_ref, hx_ref, dy_ref, w_ref, dp_ref, dw_ref, qbuf, dbuf):
        i = pl.program_id(0)

        @pl.when(i == 0)
        def _():
            dbuf[rows:rows + HALO_A, :] = jnp.zeros((HALO_A, wa), F32)
            dw_ref[...] = jnp.zeros_like(dw_ref)

        keep = jnp.where(i == nt - 1, 0.0, 1.0)
        qbuf[0:HALO_A, :] = hc_ref[...] * hx_ref[...] * keep
        acv, axv = ac_ref[...], ax_ref[...]
        q = acv * axv
        qbuf[HALO_A:HALO_A + rows, :] = q
        conv = w_ref[2:3, :] * q
        for k in range(TAPS_A - 1):
            off = HALO_A - (TAPS_A - 1) + k
            conv = conv + w_ref[k:k + 1, :] * qbuf[off:off + rows, :]
        zv, abv, dyv = az_ref[...], ab_ref[...], dy_ref[...]
        sg = _sigmoid(zv)
        sz = zv * sg
        dp_ref[:, 0:wa] = (dyv * conv * sz).astype(BF16)
        dp_ref[:, 3 * wa:4 * wa] = (dyv * abv * conv * (sg * (1.0 + zv * (1.0 - sg)))).astype(BF16)
        dconv = dyv * abv * sz
        dbuf[0:rows, :] = dconv
        dq = w_ref[2:3, :] * dconv
        for k in range(TAPS_A - 1):
            off = TAPS_A - 1 - k
            dq = dq + w_ref[k:k + 1, :] * dbuf[off:off + rows, :]
        dp_ref[:, wa:2 * wa] = (dq * axv).astype(BF16)
        dp_ref[:, 2 * wa:3 * wa] = (dq * acv).astype(BF16)
        for k in range(TAPS_A):
            off = HALO_A - (TAPS_A - 1) + k
            dw_ref[k:k + 1, :] += jnp.sum(dconv * qbuf[off:off + rows, :], axis=0, keepdims=True)
        dbuf[rows:rows + HALO_A, :] = dbuf[0:HALO_A, :]

    def sec(k):
        return pl.BlockSpec((rows, wa), lambda i, k=k: (nt - 1 - i, k))

    def halo(k):
        return pl.BlockSpec((HALO_A, wa), lambda i, k=k: (jnp.maximum((nt - 1 - i) * per8 - 1, 0), k))

    return pl.pallas_call(
        body, name="mixer_a_bwd", grid=(nt,),
        in_specs=[sec(0), sec(1), sec(2), sec(3), halo(1), halo(2),
                  pl.BlockSpec((rows, wa), lambda i: (nt - 1 - i, 0)),
                  pl.BlockSpec((HALO_A, wa), lambda i: (0, 0))],
        out_specs=[pl.BlockSpec((rows, 4 * wa), lambda i: (nt - 1 - i, 0)),
                   pl.BlockSpec((HALO_A, wa), lambda i: (0, 0))],
        out_shape=[jax.ShapeDtypeStruct((s, din), BF16), jax.ShapeDtypeStruct((HALO_A, wa), F32)],
        scratch_shapes=[pltpu.VMEM((HALO_A + rows, wa), F32), pltpu.VMEM((rows + HALO_A, wa), F32)],
        compiler_params=_params(("arbitrary",)),
    )(proj, proj, proj, proj, proj, proj, dy, conv_w)


def _mixer_b_gate_bwd(dproj, dy, u, proj, ln_g, ln_b, wa, rows):
    s, wb = u.shape
    sec_z = (4 * wa + 2 * wb) // wb

    def body(dp_in, dy_ref, u_ref, bz_ref, g_ref, b_ref, dp_ref, du_ref, dg_ref, db_ref, dcb_ref):
        i = pl.program_id(0)
        uh, rs = _layernorm_stats(u_ref[...])
        ln = uh * g_ref[...] + b_ref[...]
        sl = _sigmoid(ln)
        zv = bz_ref[...]
        sg = _sigmoid(zv)
        dyv = dy_ref[...]
        dp_ref[...] = (dyv * (ln * sl) * (sg * (1.0 + zv * (1.0 - sg)))).astype(BF16)
        dln = dyv * (zv * sg) * (sl * (1.0 + ln * (1.0 - sl)))
        duh = dln * g_ref[...]
        du = rs * (duh - jnp.mean(duh, axis=-1, keepdims=True) - uh * jnp.mean(duh * uh, axis=-1, keepdims=True))
        du_ref[...] = du

        @pl.when(i == 0)
        def _():
            dg_ref[...] = jnp.zeros_like(dg_ref)
            db_ref[...] = jnp.zeros_like(db_ref)
            dcb_ref[...] = jnp.zeros_like(dcb_ref)

        dg_ref[...] += jnp.sum(dln * uh, axis=0, keepdims=True)
        db_ref[...] += jnp.sum(dln, axis=0, keepdims=True)
        dcb_ref[...] += jnp.sum(du, axis=0, keepdims=True)

    blk = pl.BlockSpec((rows, wb), lambda i: (i, 0))
    vec = pl.BlockSpec((1, wb), lambda i: (0, 0))
    vshape = jax.ShapeDtypeStruct((1, wb), F32)
    return pl.pallas_call(
        body, name="mixer_b_gate_bwd", grid=(s // rows,),
        in_specs=[ANY, pl.BlockSpec((rows, wb), lambda i: (i, wa // wb)), blk,
                  pl.BlockSpec((rows, wb), lambda i: (i, sec_z)), vec, vec],
        out_specs=[pl.BlockSpec((rows, wb), lambda i: (i, sec_z)), blk, vec, vec, vec],
        out_shape=[jax.ShapeDtypeStruct(dproj.shape, BF16), jax.ShapeDtypeStruct((s, wb), F32), vshape, vshape, vshape],
        input_output_aliases={0: 0},
        compiler_params=_params(("arbitrary",)),
    )(dproj, dy, u, proj, ln_g, ln_b)


def _mixer_b_conv_bwd(dproj, du, u0, proj, conv_w, wa, rows, chunk):
    s, wb = du.shape
    nt = s // rows
    per32 = rows // HALO_B
    sec_v = 4 * wa // wb
    nrc = rows // chunk

    def body(dp_in, du_ref, u0_ref, h0_ref, bv_ref, bg_ref, w_ref, dp_ref, dw_ref, ubuf, dbuf, sh, shf, dwacc):
        i = pl.program_id(0)

        @pl.when(i == 0)
        def _():
            dbuf[rows:rows + HALO_B, :] = jnp.zeros((HALO_B, wb), F32)
            dwacc[...] = jnp.zeros_like(dwacc)

        ubuf[0:HALO_B, :] = h0_ref[...] * jnp.where(i == nt - 1, 0.0, 1.0)
        ubuf[HALO_B:HALO_B + rows, :] = u0_ref[...]
        dbuf[0:rows, :] = du_ref[...]
        for lc in range(wb // LANES):
            cs = slice(lc * LANES, (lc + 1) * LANES)
            _shifted_back(sh, ubuf, 8, HALO_B + rows, cs)
            _shifted_fwd(shf, dbuf, 0, rows + HALO_B - 8, cs)
            taps = [w_ref[k:k + 1, cs] for k in range(TAPS_B)]

            def conv_rows(rc, c0, cs=cs, taps=taps, lc=lc):
                base = pl.multiple_of(rc * chunk, chunk)
                acc = jnp.zeros((chunk, LANES), F32)
                for k in range(TAPS_B):
                    mq, n = divmod(TAPS_B - 1 - k, 8)
                    acc = acc + taps[k] * shf[n, pl.ds(base + 8 * mq, chunk), :]
                sg = _sigmoid(bg_ref[pl.ds(base, chunk), cs])
                bv = bv_ref[pl.ds(base, chunk), cs]
                dp_ref[pl.ds(base, chunk), cs] = (acc * sg).astype(BF16)
                dp_ref[pl.ds(base, chunk), wb + lc * LANES:wb + (lc + 1) * LANES] = (
                    acc * bv * sg * (1.0 - sg)).astype(BF16)
                return c0

            lax.fori_loop(0, nrc, conv_rows, 0)

            def dw_rows(rc, accs, cs=cs):
                base = pl.multiple_of(rc * chunk, chunk)
                du_c = dbuf[pl.ds(base, chunk), cs]
                out = []
                for k in range(TAPS_B):
                    mq, n = divmod(TAPS_B - 1 - k, 8)
                    prod = du_c * sh[n, pl.ds(HALO_B - 8 * mq + base, chunk), :]
                    out.append(accs[k] + jnp.sum(prod.reshape(chunk // SUBLANES, SUBLANES, LANES), axis=0))
                return tuple(out)

            accs = lax.fori_loop(0, nrc, dw_rows, tuple(jnp.zeros((SUBLANES, LANES), F32) for _ in range(TAPS_B)))
            for k in range(TAPS_B):
                dwacc[k * SUBLANES:(k + 1) * SUBLANES, cs] += accs[k]
        dbuf[rows:rows + HALO_B, :] = dbuf[0:HALO_B, :]

        @pl.when(i == nt - 1)
        def _():
            for k in range(HALO_B):
                dw_ref[k:k + 1, :] = jnp.sum(dwacc[k * SUBLANES:(k + 1) * SUBLANES, :], axis=0, keepdims=True)

    def rev(cols_blk):
        return pl.BlockSpec((rows, wb), lambda i, cb=cols_blk: (nt - 1 - i, cb))

    return pl.pallas_call(
        body, name="mixer_b_conv_bwd", grid=(nt,),
        in_specs=[ANY, rev(0), rev(0),
                  pl.BlockSpec((HALO_B, wb), lambda i: (jnp.maximum((nt - 1 - i) * per32 - 1, 0), 0)),
                  rev(sec_v), rev(sec_v + 1), pl.BlockSpec((HALO_B, wb), lambda i: (0, 0))],
        out_specs=[pl.BlockSpec((rows, 2 * wb), lambda i: (nt - 1 - i, sec_v // 2)),
                   pl.BlockSpec((HALO_B, wb), lambda i: (0, 0))],
        out_shape=[jax.ShapeDtypeStruct(dproj.shape, BF16), jax.ShapeDtypeStruct((HALO_B, wb), F32)],
        input_output_aliases={0: 0},
        scratch_shapes=[pltpu.VMEM((HALO_B + rows, wb), F32), pltpu.VMEM((rows + HALO_B, wb), F32),
                        pltpu.VMEM((8, HALO_B + rows, LANES), F32), pltpu.VMEM((8, rows + HALO_B, LANES), F32),
                        pltpu.VMEM((HALO_B * SUBLANES, wb), F32)],
        compiler_params=_params(("arbitrary",)),
    )(dproj, du, u0, u0, proj, proj, conv_w)


def _prenorm_bwd(x, r, dh, dx2, norm_g, scale, rows):
    s, d = x.shape

    def body(x_ref, r_ref, dh_ref, dx2_ref, g_ref, sc_ref, gx_ref, dsh_ref, dsc_ref, dg_ref):
        i = pl.program_id(0)
        rv = r_ref[...]
        xn = x_ref[...] * rv
        dhv = dh_ref[...]
        one_sc = 1.0 + sc_ref[...]
        dxn = dhv * one_sc * g_ref[...]
        gx_ref[...] = dx2_ref[...] + rv * (dxn - xn * jnp.mean(dxn * xn, axis=-1, keepdims=True))

        @pl.when(i == 0)
        def _():
            dsh_ref[...] = jnp.zeros_like(dsh_ref)
            dsc_ref[...] = jnp.zeros_like(dsc_ref)
            dg_ref[...] = jnp.zeros_like(dg_ref)

        dsh_ref[...] += jnp.sum(dhv, axis=0, keepdims=True)
        dsc_ref[...] += jnp.sum(dhv * (xn * g_ref[...]), axis=0, keepdims=True)
        dg_ref[...] += jnp.sum(dhv * one_sc * xn, axis=0, keepdims=True)

    blk = pl.BlockSpec((rows, d), lambda i: (i, 0))
    vec = pl.BlockSpec((1, d), lambda i: (0, 0))
    vshape = jax.ShapeDtypeStruct((1, d), F32)
    return pl.pallas_call(
        body, name="prenorm_bwd", grid=(s // rows,),
        in_specs=[blk, pl.BlockSpec((rows, 1), lambda i: (i, 0)), blk, blk, vec, vec],
        out_specs=[blk, vec, vec, vec],
        out_shape=[jax.ShapeDtypeStruct((s, d), F32), vshape, vshape, vshape],
        compiler_params=_params(("arbitrary",)),
    )(x, r, dh, dx2, norm_g, scale)


def _pad_rows(a, rows):
    return jnp.pad(a, ((0, rows - a.shape[0]), (0, 0)))


def _tile(n, want):
    t = min(n, want)
    while n % t:
        t -= 1
    return t


def kernel(x, c, norm_g, w_ada, b_ada, w_in, conv_a_w, conv_b_w, conv_b_b, ln_b_g, ln_b_b, w_out, final_g, loss_target, m_norm_g, m_w_ada, m_b_ada, m_w_in, m_conv_a_w, m_conv_b_w, m_conv_b_b, m_ln_b_g, m_ln_b_b, m_w_out, m_final_g, v_norm_g, v_w_ada, v_b_ada, v_w_in, v_conv_a_w, v_conv_b_w, v_conv_b_b, v_ln_b_g, v_ln_b_b, v_w_out, v_final_g):
    s, d = x.shape[1], x.shape[2]
    wa = conv_b_b.shape[-1]
    dmix = 2 * wa
    ns = w_in.shape[-1]
    din = N_CHIPS * ns
    r4 = w_out.shape[1]
    na = w_ada.shape[-1]
    wsh = conv_a_w.shape[-1]
    px, py, pc = _position()
    chip = 2 * px + py
    me = 4 * px + 2 * py + pc
    pos = jnp.stack([chip, pc]).astype(jnp.int32)
    x2d = x.reshape(s, d)
    target = loss_target.reshape(s, d)

    hc, ho, hrow = ns // 2, r4 // 2, d // 2
    _, cidx = _other_chips(px, py)
    hq = hc // 2
    win4 = _cast_quarters(w_in[0], _tile(d, 512), "cast_w_in")
    wout_bf = _cast_bf16(w_out[0], _tile(r4, 512), "cast_w_out")

    def gather_plan(b):
        x, y, cc = _position()
        chips, _ = _other_chips(x, y)
        xn, yn = (1 - x, y, cc), (x, 1 - y, cc)
        q0, q1 = b[0].at[2 * cc], b[0].at[2 * cc + 1]
        return ([(q0, b[2], xn), (q1, b[5], yn), (q1, b[3], xn), (q0, b[4], yn)]
                + [(b[1].at[pl.ds(cc * ho, ho), :], b[6 + k], (cx, cy, cc)) for k, (cx, cy) in enumerate(chips)])

    def gather_sent(b):
        return [(src, src, dev) for src, _, dev in gather_plan(list(b) + [None] * 7)]

    def onward_plan(b):
        x, y, cc = _position()
        sib = (x, y, 1 - cc)
        return [(b[0], b[2], (x, 1 - y, cc)), (b[1], b[3], (1 - x, y, cc)), (b[0], b[4], sib), (b[1], b[5], sib)]

    pairs = lambda n: _to_sibling([lambda ref, cc: ref] * n)

    c8 = jnp.broadcast_to(c, (8, d))
    cw = jnp.concatenate([_pad_rows(conv_a_w[0], HALO_A), _pad_rows(conv_b_w[0], HALO_B)], axis=0)
    c_all, cw_all = _gather_cond(c8, cw)
    c_rows = c_all[:, 0, :]
    cw_full = jnp.transpose(cw_all, (1, 0, 2)).reshape(HALO_A + HALO_B, wa)
    conv_a_full, conv_b_full = cw_full[:HALO_A], cw_full[HALO_A:]

    b_ada_sh = lax.dynamic_slice(b_ada, (0, chip * na), (1, na))
    mod_part = _modulation(_pad_rows(c_rows, 2 * N_DEV), w_ada[0], b_ada_sh, _tile(na, 512), "modulation")[:N_DEV]
    mod_all = _exchange_mod(mod_part)
    mod = lax.dynamic_index_in_dim(mod_all, me, axis=1, keepdims=False).reshape(1, 3 * d)
    shift, scale, gate = mod[:, :d], mod[:, d:2 * d], mod[:, 2 * d:]

    def quarter():
        return lax.empty((d, hq), BF16)

    g_sems, g_bufs, g_tok = _start_copies(
        "gather_start", gather_plan, 7,
        [win4, wout_bf] + [quarter() for _ in range(4)] + [lax.empty((ho, d), BF16) for _ in range(3)],
        after=[mod_all])
    win4, wout_bf, (x0, x1, y0, y1), lo = g_bufs[0], g_bufs[1], g_bufs[2:6], g_bufs[6:9]

    h, r = _prenorm(x2d, norm_g, scale, shift + g_tok[0, 0], _tile(s, 256))
    bm = _tile(s, 1024)
    pieces = [None, None]

    def piece(slot, half, quarters, name, own_half=None):
        where = jnp.reshape(2 * slot + half, (1,)).astype(jnp.int32)
        pieces[:] = _proj_piece(where, h, quarters, pieces[0], pieces[1], din, 2 * N_CHIPS, _tile(s, 512), name,
                                own_half=own_half)
        return pieces[0]

    proj = piece(chip, 0, (win4, win4), "proj_own0", own_half=0)
    proj = piece(chip, 1, (win4, win4), "proj_own1", own_half=1)
    x0, y1 = _wait_copies("gather_wait_a", _landed, [x0, y1], g_sems[0:4], after=[proj], send=False)
    on_sems, (x0, y1, dg0, dg1, sx0, sy1), _ = _start_copies(
        "pass_on_a", onward_plan, 4, [x0, y1] + [quarter() for _ in range(4)])
    x1, y0 = _wait_copies("gather_wait_b", _landed, [x1, y0], g_sems[4:8], after=[x0], send=False)
    pb_sems, (x1, sx1, y0, sy0), _ = _start_copies("pass_on_b", pairs(2), 2, [x1, quarter(), y0, quarter()])
    proj = piece(cidx[0], pc, (x0, x1), "proj_xa")
    proj = piece(cidx[1], pc, (y0, y1), "proj_ya")
    sx0, sy1 = _wait_copies("pass_wait_a", _landed, [sx0, sy1], on_sems[4:8], after=[proj], send=False)
    x1, sx1, y0, sy0 = _wait_copies("pass_wait_b", pairs(2), [x1, sx1, y0, sy0], pb_sems, after=[proj])
    proj = piece(cidx[0], 1 - pc, (sx0, sx1), "proj_xb")
    proj = piece(cidx[1], 1 - pc, (sy0, sy1), "proj_yb")
    x0, y1, dg0, dg1 = _wait_copies("diag_wait", lambda b: onward_plan(list(b) + [None, None])[:2],
                                    [x0, y1, dg0, dg1], on_sems[0:4], after=[proj])
    x0, y1 = _wait_copies("pass_sent_a", lambda b: [(b[0], b[0], (0, 0, 0)), (b[1], b[1], (0, 0, 0))],
                          [x0, y1], on_sems[4:8], after=[dg0], recv=False)
    pd_sems, (dg0, sd0, dg1, sd1), _ = _start_copies("pass_on_d", pairs(2), 2, [dg0, quarter(), dg1, quarter()],
                                                     after=[x0])
    proj = piece(cidx[2], pc, (dg0, dg1), "proj_da")
    dg0, sd0, dg1, sd1 = _wait_copies("pass_wait_d", pairs(2), [dg0, sd0, dg1, sd1], pd_sems, after=[proj])
    proj = piece(cidx[2], 1 - pc, (sd0, sd1), "proj_db")
    win_full = pieces[1]

    lo = _wait_copies("gather_wait_out", _landed, lo, g_sems[8:14], after=[proj], send=False)
    o_sems, o_bufs, o_tok = _start_copies(
        "pass_on_out", pairs(3), 3, [b for k in range(3) for b in (lo[k], lax.empty((ho, d), BF16))])
    win4, wout_bf = _wait_copies("gather_wait_sent", gather_sent, [win4, wout_bf], g_sems, after=[o_tok], recv=False)

    def slot_index(k, chip_, cc, others):
        if k == 0:
            return pl.ds(2 * chip_, 2)
        return 2 * others[(k - 1) % 3] + (cc if k <= 3 else 1 - cc)

    y = _mixer_a_fwd(proj, conv_a_full, wa, _tile(s, 512), _tile(wa, 512))
    u0, u = _mixer_b_conv_fwd(proj, conv_b_full, conv_b_b, wa, _tile(s, 512), _tile(wa, 256), 64)
    y = _mixer_b_gate_fwd(y, u, proj, ln_b_g, ln_b_b, wa, _tile(s, 256))
    o_bufs = _wait_copies("pass_wait_out", pairs(3), o_bufs, o_sems, after=[y])
    wout_full = _assemble("assemble_w_out", [wout_bf.reshape(2, ho, d)] + o_bufs[0::2] + o_bufs[1::2],
                          jax.ShapeDtypeStruct((2 * N_CHIPS, ho, d), BF16), slot_index)
    wout2d = wout_full.reshape(dmix, d)
    bd = _tile(d, 1024)
    o = _matmul(
        y, wout2d, grid=(s // bm, d // bd, 1),
        a_spec=pl.BlockSpec((bm, dmix), lambda i, j, k: (i, 0)),
        b_spec=pl.BlockSpec((dmix, bd), lambda i, j, k: (0, j)),
        o_spec=pl.BlockSpec((bm, bd), lambda i, j, k: (i, j)),
        out_shape=jax.ShapeDtypeStruct((s, d), F32), dims=((1,), (0,)), name="out_proj")
    dx2, do, loss_p, gfg_p, dgate_p = _loss_head(x2d, o, target, gate, final_g.reshape(1, d), _tile(s, 128))

    be = _tile(dmix, 1024)
    g_wout = _matmul(
        y, do, grid=(dmix // be, d // bd, 1),
        a_spec=pl.BlockSpec((s, be), lambda i, j, k: (0, i)),
        b_spec=pl.BlockSpec((s, bd), lambda i, j, k: (0, j)),
        o_spec=pl.BlockSpec((be, bd), lambda i, j, k: (i, j)),
        out_shape=jax.ShapeDtypeStruct((dmix, d), F32), dims=((0,), (0,)), name="grad_w_out")
    swap_out = _to_sibling([lambda ref, cc: ref.at[:, pl.ds((1 - cc) * ho, ho), :]])
    so_sems, (g_wout3, ra_out), so_tok = _start_copies(
        "swap_out_start", swap_out, 1, [g_wout.reshape(N_CHIPS, r4, d), lax.empty((N_CHIPS, ho, d), F32)])
    dy = _matmul(
        do, wout2d, grid=(s // bm, dmix // be, 1),
        a_spec=pl.BlockSpec((bm, d), lambda i, j, k: (i, 0)),
        b_spec=pl.BlockSpec((be, d), lambda i, j, k: (j, 0)),
        o_spec=pl.BlockSpec((bm, be), lambda i, j, k: (i, j)),
        out_shape=jax.ShapeDtypeStruct((s, dmix), F32), dims=((1,), (1,)), name="dy", after=[so_tok])
    g_wout3, ra_out = _wait_copies("swap_out_wait", swap_out, [g_wout3, ra_out], so_sems, after=[dy])
    q_out = _chip_partial(pos, g_wout3, ra_out, _tile(ho, 256), "chip_partial_w_out")
    po_sems, po_bufs, po_tok = _start_copies(
        "send_out_start", _slots_to_chips, 3, [q_out] + [lax.empty((ho, d), BF16) for _ in range(3)])
    dproj, dwa_p = _mixer_a_bwd(proj, dy, conv_a_full + po_tok[0, 0], wa, din, _tile(s, 128))
    dproj, du, dlng_p, dlnb_p, dcb_p = _mixer_b_gate_bwd(dproj, dy, u, proj, ln_b_g, ln_b_b, wa, _tile(s, 128))
    dproj, dwb_p = _mixer_b_conv_bwd(dproj, du, u0, proj, conv_b_full, wa, _tile(s, 256), 64)

    swap_in =_to_sibling([lambda ref, cc: ref.at[pl.ds((1 - cc) * hrow, hrow), :]])
    slots = [cidx[0], cidx[1], cidx[2], chip]
    core_only = jnp.stack([0 * pc, pc]).astype(jnp.int32)
    g, ra, sw, q, rb, snd = [None] * 4, [None] * 4, [None] * 4, [None] * 3, [None] * 3, [None] * 3
    after = []
    for k in range(4):
        g[k] = _grad_slot(jnp.reshape(slots[k], (1,)).astype(jnp.int32), h, dproj, after, ns, _tile(d, 512), hc,
                          f"grad_w_in{k}")
        sw[k], (g[k], ra[k]), tok = _start_copies(f"swap_in_start{k}", swap_in, 1,
                                                  [g[k], lax.empty((hrow, ns), F32)])
        after = [tok]
        if k >= 1:
            j = k - 1
            g[j], ra[j] = _wait_copies(f"swap_in_wait{j}", swap_in, [g[j], ra[j]], sw[j], after=[g[k]])
            part = _chip_partial(core_only, g[j][None], ra[j][None], _tile(hrow, 256), f"chip_partial_w_in{j}")
            snd[j], (q[j], rb[j]), tok2 = _start_copies(f"send_in_start{j}", _to_chip(j), 1,
                                                        [part.reshape(hrow, ns), lax.empty((hrow, ns), BF16)])
            after = [tok, tok2]
    dh = _matmul(
        dproj, win_full, grid=(s // bm, d // bd, 2 * N_CHIPS),
        a_spec=pl.BlockSpec((bm, hc), lambda i, j, k: (i, k)),
        b_spec=pl.BlockSpec((None, bd, hc), lambda i, j, k: (k, j, 0)),
        o_spec=pl.BlockSpec((bm, bd), lambda i, j, k: (i, j)),
        out_shape=jax.ShapeDtypeStruct((s, d), F32), dims=((1,), (1,)), name="dh", after=after)
    grad_x, dshift_p, dscale_p, gng_p = _prenorm_bwd(x2d, r, dh, dx2, norm_g, scale, _tile(s, 128))

    def rows_of(v):
        return _pad_rows(v.reshape(-1, wa), 8 * ((v.size // wa + 7) // 8))

    dmod = jnp.concatenate([dshift_p, dscale_p, dgate_p], axis=1)
    parts = [gng_p, dmod, dwa_p, dwb_p, dcb_p, dlng_p, dlnb_p, gfg_p,
             jnp.broadcast_to(loss_p[:, :1], (1, wa))]
    starts, packed = [], []
    for p in parts:
        starts.append(sum(q.shape[0] for q in packed))
        packed.append(rows_of(p) if p.shape[0] == 1 else p)
    small_sum, small_all = _gather_small(jnp.concatenate(packed, axis=0))

    def summed(k, rows):
        return small_sum[starts[k]:starts[k] + rows]

    grad_norm_g = summed(0, d // wa).reshape(1, d)
    grad_b_ada = summed(1, 3 * d // wa).reshape(1, 3 * d)
    grad_conv_a_full = summed(2, TAPS_A)
    grad_conv_b_full = summed(3, TAPS_B)
    grad_conv_b_b = summed(4, 1)
    grad_ln_b_g = summed(5, 1)
    grad_ln_b_b = summed(6, 1)
    grad_final_g = summed(7, d // wa).reshape(d)
    loss = summed(8, 1)[0, 0]
    grad_conv_a_w = lax.dynamic_slice(grad_conv_a_full, (0, chip * wsh), (TAPS_A, wsh))
    grad_conv_b_w = lax.dynamic_slice(grad_conv_b_full, (0, chip * wsh), (TAPS_B, wsh))
    dmod_all = small_all[:, starts[1]:starts[1] + 3 * d // wa, :].reshape(N_DEV, 3 * d)
    dmod_sh = lax.dynamic_slice(dmod_all, (0, chip * na), (N_DEV, na))

    def pairs_to_chips(b):
        x, y, cc = _position()
        chips, _ = _other_chips(x, y)
        return [(b[2 * k], b[2 * k + 1], (cx, cy, cc)) for k, (cx, cy) in enumerate(chips)]

    po_bufs = _wait_copies("send_out_wait", _slots_to_chips, po_bufs, po_sems, after=[small_sum])
    gh_out = _final_half(pos, g_wout3, ra_out, po_bufs[1:], _tile(ho, 256), "final_half_w_out")
    g[3], ra[3] = _wait_copies("swap_in_wait3", swap_in, [g[3], ra[3]], sw[3], after=[small_sum])
    in_bufs = _wait_copies("send_in_wait", pairs_to_chips, [b for k in range(3) for b in (q[k], rb[k])],
                           snd[0] + snd[1] + snd[2], after=[small_sum])
    gh_in = _final_half(core_only, g[3][None], ra[3][None], in_bufs[1::2], _tile(hrow, 256), "final_half_w_in")
    sh_sems, sh_bufs, sh_tok = _start_copies("share_start", _halves_to_sibling, 2, [gh_in, gh_out])

    grad_w_ada, d_wada, nm_wada, nv_wada = _adam_ada(c_rows.T, dmod_sh + sh_tok[0, 0], w_ada[0], m_w_ada[0],
                                                     v_w_ada[0], _tile(d, 128), "adam_w_ada")
    gw_in, gw_out = _wait_copies("share_wait", _halves_to_sibling, sh_bufs, sh_sems, after=[d_wada])
    grad_w_in, d_win, nm_win, nv_win = _adam(w_in[0], gw_in, m_w_in[0], v_w_in[0], _tile(d, 128), "adam_w_in",
                                             return_grad=True)
    grad_w_out, d_wout, nm_wout, nv_wout = _adam(w_out[0], gw_out, m_w_out[0], v_w_out[0], _tile(r4, 128),
                                                 "adam_w_out", return_grad=True)

    def small_adam(w, g, m, v, name):
        shape = w.shape
        w2 = w.reshape(-1, shape[-1])
        out = _adam(w2, g.reshape(w2.shape), m.reshape(w2.shape), v.reshape(w2.shape), w2.shape[0], name)
        return [o_.reshape(shape) for o_ in out]

    small = {
        "norm_g": small_adam(norm_g, grad_norm_g, m_norm_g, v_norm_g, "adam_norm_g"),
        "b_ada": small_adam(b_ada, grad_b_ada, m_b_ada, v_b_ada, "adam_b_ada"),
        "conv_a_w": small_adam(conv_a_w, grad_conv_a_w, m_conv_a_w, v_conv_a_w, "adam_conv_a_w"),
        "conv_b_w": small_adam(conv_b_w, grad_conv_b_w, m_conv_b_w, v_conv_b_w, "adam_conv_b_w"),
        "conv_b_b": small_adam(conv_b_b, grad_conv_b_b, m_conv_b_b, v_conv_b_b, "adam_conv_b_b"),
        "ln_b_g": small_adam(ln_b_g, grad_ln_b_g, m_ln_b_g, v_ln_b_g, "adam_ln_b_g"),
        "ln_b_b": small_adam(ln_b_b, grad_ln_b_b, m_ln_b_b, v_ln_b_b, "adam_ln_b_b"),
        "final_g": small_adam(final_g.reshape(1, d), grad_final_g, m_final_g.reshape(1, d),
                              v_final_g.reshape(1, d), "adam_final_g"),
    }
    small["final_g"] = [o_.reshape(d) for o_ in small["final_g"]]
    big = {
        "w_ada": [a[None] for a in (d_wada, nm_wada, nv_wada)],
        "w_in": [a[None] for a in (d_win, nm_win, nv_win)],
        "w_out": [a[None] for a in (d_wout, nm_wout, nv_wout)],
    }
    upd = {**small, **big}
    order = ["norm_g", "w_ada", "b_ada", "w_in", "conv_a_w", "conv_b_w", "conv_b_b", "ln_b_g", "ln_b_b",
             "w_out", "final_g"]
    grads = {
        "norm_g": grad_norm_g, "w_ada": grad_w_ada[None], "b_ada": grad_b_ada, "w_in": grad_w_in[None],
        "conv_a_w": grad_conv_a_w[None], "conv_b_w": grad_conv_b_w[None], "conv_b_b": grad_conv_b_b,
        "ln_b_g": grad_ln_b_g, "ln_b_b": grad_ln_b_b, "w_out": grad_w_out[None], "final_g": grad_final_g,
    }
    return (loss, grad_x.reshape(1, s, d), *[grads[n] for n in order], *[upd[n][0] for n in order],
            *[upd[n][1] for n in order], *[upd[n][2] for n in order])
```

```python
import functools

import jax
import jax.numpy as jnp
from jax import lax
from jax.experimental import pallas as pl
from jax.experimental.pallas import tpu as pltpu

F32 = jnp.float32
BF16 = jnp.bfloat16
EPS = 1e-6
N_CHIPS = 4
N_DEV = 8
TAPS_A = 3
TAPS_B = 31
HALO_A = 8
HALO_IN = 16
HALO_B = 32
LANES = 128
SUBLANES = 8
ADAM_LR = 0.001
ADAM_B1 = 0.9
ADAM_B2 = 0.999
ADAM_EPS = 1e-08
ADAM_WD = 0.01
ADAM_STEP = 10
VMEM_LIMIT = 56 * 1024 * 1024
MESH = pl.DeviceIdType.MESH
ANY = pl.BlockSpec(memory_space=pl.ANY)
VMEM = pl.BlockSpec(memory_space=pltpu.VMEM)
HBM_SPEC = pl.BlockSpec(memory_space=pltpu.HBM)
SEM_SPEC = pl.BlockSpec(memory_space=pltpu.SEMAPHORE)
EFFECT = pltpu.SideEffectType.DATAFLOW_SIDE_EFFECTING


def _params(sem=None):
    return pltpu.CompilerParams(dimension_semantics=sem, vmem_limit_bytes=VMEM_LIMIT)


def _sigmoid(v):
    return jax.nn.sigmoid(v)


def _position():
    return lax.axis_index("x"), lax.axis_index("y"), lax.axis_index("c")


def _rcopy(src, dst, ssem, rsem, dev):
    return pltpu.make_async_remote_copy(src_ref=src, dst_ref=dst, send_sem=ssem, recv_sem=rsem,
                                        device_id=dev, device_id_type=MESH)


def _other_chips(x, y):
    chips = [(1 - x, y), (x, 1 - y), (1 - x, 1 - y)]
    return chips, [2 * cx + cy for cx, cy in chips]


def _cast_bf16(a, rows, name):
    m, n = a.shape

    def body(a_ref, o_ref):
        o_ref[...] = a_ref[...].astype(BF16)

    return pl.pallas_call(
        body, name=name, grid=(m // rows,),
        in_specs=[pl.BlockSpec((rows, n), lambda i: (i, 0))],
        out_specs=pl.BlockSpec((rows, n), lambda i: (i, 0)),
        out_shape=jax.ShapeDtypeStruct((m, n), BF16),
        compiler_params=_params(("parallel",)),
    )(a)


def _cast_quarters(a, rows, name):
    m, n = a.shape
    hq = n // 4

    def body(a_ref, o_ref):
        o_ref[...] = a_ref[...].astype(BF16)

    return pl.pallas_call(
        body, name=name, grid=(4, m // rows),
        in_specs=[pl.BlockSpec((rows, hq), lambda q, i: (i, q))],
        out_specs=pl.BlockSpec((None, rows, hq), lambda q, i: (q, i, 0)),
        out_shape=jax.ShapeDtypeStruct((4, m, hq), BF16),
        compiler_params=_params(("parallel", "parallel")),
    )(a)


def _proj_piece(where, h, quarters, proj, w_all, din, n_pieces, bm, name, own_half=None):
    s, d = h.shape
    hq = quarters[0].shape[-1]
    nm = s // bm
    if own_half is None:
        q_specs = [pl.BlockSpec((d, hq), lambda i, p: (0, 0), pipeline_mode=pl.Buffered(1))] * 2
    else:
        q_specs = [pl.BlockSpec((None, d, hq), lambda i, p, k=k: (2 * own_half + k, 0, 0),
                                pipeline_mode=pl.Buffered(1)) for k in range(2)]

    def body(p_ref, h_ref, q0_ref, q1_ref, *rest):
        o_ref, wall_ref, wbuf, sem = rest[-4:]
        i = pl.program_id(0)
        filed = pltpu.make_async_copy(wbuf, wall_ref.at[p_ref[0]], sem)

        @pl.when(i == 0)
        def _():
            wbuf[:, 0:hq] = q0_ref[...]
            wbuf[:, hq:2 * hq] = q1_ref[...]
            filed.start()

        o_ref[...] = jnp.dot(h_ref[...], wbuf[...], preferred_element_type=F32).astype(BF16)

        @pl.when(i == nm - 1)
        def _():
            filed.wait()

    args, extra, alias = [where, h, quarters[0], quarters[1]], [], {}
    if proj is not None:
        args, extra, alias = args + [proj, w_all], [ANY, ANY], {4: 0, 5: 1}
    return pl.pallas_call(
        body, name=name,
        grid_spec=pltpu.PrefetchScalarGridSpec(
            num_scalar_prefetch=1, grid=(nm,),
            in_specs=[pl.BlockSpec((bm, d), lambda i, p: (i, 0))] + q_specs + extra,
            out_specs=[pl.BlockSpec((bm, 2 * hq), lambda i, p: (i, p[0])), ANY],
            scratch_shapes=[pltpu.VMEM((d, 2 * hq), BF16), pltpu.SemaphoreType.DMA]),
        out_shape=[jax.ShapeDtypeStruct((s, din), BF16), jax.ShapeDtypeStruct((n_pieces, d, 2 * hq), BF16)],
        input_output_aliases=alias,
        compiler_params=_params(("arbitrary",)),
    )(*args)


def _grad_slot(slot, h, dproj, after, ns, bd, bn, name):
    s, d = h.shape
    nb = ns // bn

    def body(slot_ref, h_ref, dp_ref, *rest):
        rest[-1][...] = lax.dot_general(h_ref[...], dp_ref[...], (((0,), (0,)), ((), ())),
                                        preferred_element_type=F32)

    return pl.pallas_call(
        body, name=name,
        grid_spec=pltpu.PrefetchScalarGridSpec(
            num_scalar_prefetch=1, grid=(d // bd, nb),
            in_specs=[pl.BlockSpec((s, bd), lambda i, j, sl: (0, i)),
                      pl.BlockSpec((s, bn), lambda i, j, sl: (0, sl[0] * nb + j))] + [ANY] * len(after),
            out_specs=pl.BlockSpec((bd, bn), lambda i, j, sl: (i, j))),
        out_shape=jax.ShapeDtypeStruct((d, ns), F32),
        compiler_params=_params(("parallel", "parallel")),
    )(slot, h, dproj, *after)


def _matmul(a, b, *, grid, a_spec, b_spec, o_spec, out_shape, dims, name, after=()):
    nk = grid[2]
    n_after = len(after)

    def body(a_ref, b_ref, *rest):
        o_ref, acc = rest[n_after], rest[n_after + 1:]
        p = lax.dot_general(a_ref[...], b_ref[...], (dims, ((), ())), preferred_element_type=F32)
        if nk == 1:
            o_ref[...] = p.astype(o_ref.dtype)
        else:
            acc_ref, = acc
            k = pl.program_id(2)

            @pl.when(k == 0)
            def _():
                acc_ref[...] = p

            @pl.when(k > 0)
            def _():
                acc_ref[...] += p

            @pl.when(k == nk - 1)
            def _():
                o_ref[...] = acc_ref[...].astype(o_ref.dtype)

    block = [d for d in o_spec.block_shape if d is not None]
    scratch = [pltpu.VMEM(tuple(block), F32)] if nk > 1 else []
    return pl.pallas_call(
        body, name=name, grid=grid, in_specs=[a_spec, b_spec] + [ANY] * n_after, out_specs=o_spec,
        out_shape=out_shape, scratch_shapes=scratch,
        compiler_params=_params(("parallel", "parallel", "arbitrary")),
    )(a, b, *after)


def _adam_math(w, g, m, v):
    m = ADAM_B1 * m + (1.0 - ADAM_B1) * g
    v = ADAM_B2 * v + (1.0 - ADAM_B2) * (g * g)
    m_hat = m / (1.0 - ADAM_B1 ** ADAM_STEP)
    v_hat = v / (1.0 - ADAM_B2 ** ADAM_STEP)
    delta = -ADAM_LR * (m_hat / (jnp.sqrt(v_hat) + ADAM_EPS) + ADAM_WD * w)
    return delta, m, v


def _adam(w, g, m, v, rows, name, return_grad=False):
    r, n = w.shape

    def body(w_ref, g_ref, m_ref, v_ref, *out):
        gv = g_ref[...]
        d, mo, vo = _adam_math(w_ref[...], gv, m_ref[...], v_ref[...])
        for o_ref, val in zip(out, ([gv] if return_grad else []) + [d, mo, vo]):
            o_ref[...] = val

    spec = pl.BlockSpec((rows, n), lambda i: (i, 0))
    shape = jax.ShapeDtypeStruct((r, n), F32)
    n_out = 4 if return_grad else 3
    return pl.pallas_call(
        body, name=name, grid=(r // rows,), in_specs=[spec] * 4, out_specs=[spec] * n_out,
        out_shape=[shape] * n_out, compiler_params=_params(("parallel",)),
    )(w, g, m, v)


def _adam_ada(c_cols, dmod, w, m, v, rows, name):
    r, n = w.shape

    def body(c_ref, dm_ref, w_ref, m_ref, v_ref, g_ref, d_ref, mo_ref, vo_ref):
        cv = c_ref[...]
        c_act = cv * _sigmoid(cv)
        g = c_act[:, 0:1] * dm_ref[0:1, :]
        for b in range(1, N_DEV):
            g = g + c_act[:, b:b + 1] * dm_ref[b:b + 1, :]
        d, mo, vo = _adam_math(w_ref[...], g, m_ref[...], v_ref[...])
        g_ref[...] = g
        d_ref[...] = d
        mo_ref[...] = mo
        vo_ref[...] = vo

    spec = pl.BlockSpec((rows, n), lambda i: (i, 0))
    shape = jax.ShapeDtypeStruct((r, n), F32)
    return pl.pallas_call(
        body, name=name, grid=(r // rows,),
        in_specs=[pl.BlockSpec((rows, N_DEV), lambda i: (i, 0)), pl.BlockSpec((N_DEV, n), lambda i: (0, 0)),
                  spec, spec, spec],
        out_specs=[spec] * 4, out_shape=[shape] * 4, compiler_params=_params(("parallel",)),
    )(c_cols, dmod, w, m, v)


def _start_copies(name, plan, n, bufs, after=()):
    nb, na = len(bufs), len(after)

    def body(*refs):
        sems = refs[nb + na:nb + na + 2 * n]
        for k, (src, dst, dev) in enumerate(plan(refs[:nb])):
            _rcopy(src, dst, sems[2 * k], sems[2 * k + 1], dev).start()
        refs[-1][...] = jnp.zeros((8, 128), F32)

    outs = pl.pallas_call(
        body, name=name,
        out_shape=[pltpu.SemaphoreType.DMA(())] * (2 * n) + [pltpu.HBM(a.shape, a.dtype) for a in bufs]
        + [jax.ShapeDtypeStruct((8, 128), F32)],
        in_specs=[HBM_SPEC] * nb + [ANY] * na, out_specs=[SEM_SPEC] * (2 * n) + [HBM_SPEC] * nb + [VMEM],
        input_output_aliases={i: 2 * n + i for i in range(nb)},
        compiler_params=pltpu.CompilerParams(has_side_effects=EFFECT),
    )(*[pltpu.with_memory_space_constraint(a, pltpu.HBM) for a in bufs], *after)
    return list(outs[:2 * n]), list(outs[2 * n:2 * n + nb]), outs[-1]


def _wait_copies(name, plan, bufs, sems, after=(), send=True, recv=True):
    nb, nsem = len(bufs), len(sems)

    def body(*refs):
        s = refs[nb:nb + nsem]
        for k, (src, dst, dev) in enumerate(plan(refs[:nb])):
            cp = _rcopy(src, dst, s[2 * k], s[2 * k + 1], dev)
            if send:
                cp.wait_send()
            if recv:
                cp.wait_recv()

    outs = pl.pallas_call(
        body, name=name, out_shape=[pltpu.HBM(a.shape, a.dtype) for a in bufs],
        in_specs=[HBM_SPEC] * nb + [SEM_SPEC] * nsem + [ANY] * len(after), out_specs=[HBM_SPEC] * nb,
        input_output_aliases={i: i for i in range(nb)},
        compiler_params=pltpu.CompilerParams(has_side_effects=EFFECT),
    )(*bufs, *sems, *after)
    return list(outs)


def _to_sibling(views):
    def plan(b):
        x, y, c = _position()
        return [(view(b[2 * k], c), b[2 * k + 1], (x, y, 1 - c)) for k, view in enumerate(views)]
    return plan


def _to_chip(k):
    def plan(b):
        x, y, c = _position()
        cx, cy = _other_chips(x, y)[0][k]
        return [(b[0], b[1], (cx, cy, c))]
    return plan


def _slots_to_chips(b):
    x, y, c = _position()
    chips, cidx = _other_chips(x, y)
    return [(b[0].at[cidx[k]], b[1 + k], (cx, cy, c)) for k, (cx, cy) in enumerate(chips)]


def _halves_to_sibling(b):
    x, y, c = _position()
    views = [r.at[pl.ds(c * (r.shape[0] // 2), r.shape[0] // 2), :] for r in b]
    return [(v, v, (x, y, 1 - c)) for v in views]


def _landed(b):
    x, y, c = _position()
    return [(ref, ref, (x, y, c)) for ref in b]


def _assemble(name, pieces, out_shape, index_of):
    n = len(pieces)

    def body(*refs):
        out_ref, sem = refs[n], refs[n + 1]
        x, y, c = _position()
        _, cidx = _other_chips(x, y)
        cps = [pltpu.make_async_copy(refs[k], out_ref.at[index_of(k, 2 * x + y, c, cidx)], sem.at[k]) for k in range(n)]
        for cp in cps:
            cp.start()
        for cp in cps:
            cp.wait()

    return pl.pallas_call(
        body, name=name, in_specs=[VMEM] * n, out_specs=ANY, out_shape=out_shape,
        scratch_shapes=[pltpu.SemaphoreType.DMA((n,))],
        compiler_params=pltpu.CompilerParams(vmem_limit_bytes=VMEM_LIMIT),
    )(*pieces)


def _gather_cond(c8, cw):
    def body(c8_ref, cw_ref, call_ref, cwall_ref, ssem, rsem, lsem):
        x, y, c = _position()
        chip = 2 * x + y
        me = 4 * x + 2 * y + c
        chips, cidx = _other_chips(x, y)
        own = [pltpu.make_async_copy(c8_ref, call_ref.at[me], lsem.at[0]),
               pltpu.make_async_copy(cw_ref, cwall_ref.at[chip], lsem.at[1])]
        for cp in own:
            cp.start()
        sends = [_rcopy(cw_ref, cwall_ref.at[chip], ssem.at[k], rsem.at[k], (cx, cy, c))
                 for k, (cx, cy) in enumerate(chips)]
        for mask in range(1, N_DEV):
            fx, fy, fc = (mask >> 2) & 1, (mask >> 1) & 1, mask & 1
            dev = (1 - x if fx else x, 1 - y if fy else y, 1 - c if fc else c)
            sends.append(_rcopy(c8_ref, call_ref.at[me], ssem.at[2 + mask], rsem.at[2 + mask], dev))
        for cp in sends:
            cp.start()
        for k in range(3):
            slot = cwall_ref.at[cidx[k]]
            _rcopy(slot, slot, ssem.at[k], rsem.at[k], (x, y, c)).wait_recv()
        for mask in range(1, N_DEV):
            slot = call_ref.at[jnp.bitwise_xor(me, mask)]
            _rcopy(slot, slot, ssem.at[2 + mask], rsem.at[2 + mask], (x, y, c)).wait_recv()
        for cp in sends:
            cp.wait_send()
        for cp in own:
            cp.wait()

    return pl.pallas_call(
        body, name="gather_cond", in_specs=[VMEM, VMEM], out_specs=[VMEM, VMEM],
        out_shape=[jax.ShapeDtypeStruct((N_DEV,) + c8.shape, F32), jax.ShapeDtypeStruct((N_CHIPS,) + cw.shape, F32)],
        scratch_shapes=[pltpu.SemaphoreType.DMA((10,)), pltpu.SemaphoreType.DMA((10,)), pltpu.SemaphoreType.DMA((2,))],
    )(c8, cw)


def _exchange_mod(mod_part):
    def body(mp_ref, out_ref, ssem, rsem, lsem):
        x, y, c = _position()
        chip = 2 * x + y
        chips, cidx = _other_chips(x, y)
        own = pltpu.make_async_copy(mp_ref, out_ref.at[chip], lsem)
        own.start()
        sends = [_rcopy(mp_ref, out_ref.at[chip], ssem.at[k], rsem.at[k], (cx, cy, c))
                 for k, (cx, cy) in enumerate(chips)]
        for cp in sends:
            cp.start()
        for k in range(3):
            slot = out_ref.at[cidx[k]]
            _rcopy(slot, slot, ssem.at[k], rsem.at[k], (x, y, c)).wait_recv()
        for cp in sends:
            cp.wait_send()
        own.wait()

    return pl.pallas_call(
        body, name="exchange_mod", in_specs=[VMEM], out_specs=VMEM,
        out_shape=jax.ShapeDtypeStruct((N_CHIPS,) + mod_part.shape, F32),
        scratch_shapes=[pltpu.SemaphoreType.DMA((3,)), pltpu.SemaphoreType.DMA((3,)), pltpu.SemaphoreType.DMA],
    )(mod_part)


def _gather_small(pack):
    rows, n = pack.shape

    def body(p_ref, sum_ref, all_ref, ssem, rsem, lsem):
        x, y, c = _position()
        me = 4 * x + 2 * y + c
        sib = (x, y, 1 - c)
        chips, cidx = _other_chips(x, y)
        own = pltpu.make_async_copy(p_ref, all_ref.at[me], lsem)
        own.start()
        sends = [_rcopy(p_ref, all_ref.at[me], ssem.at[0], rsem.at[0], sib)]
        sends += [_rcopy(p_ref, all_ref.at[me], ssem.at[1 + k], rsem.at[1 + k], (cx, cy, c))
                  for k, (cx, cy) in enumerate(chips)]
        for cp in sends:
            cp.start()
        for k in range(3):
            slot = all_ref.at[2 * cidx[k] + c]
            _rcopy(slot, slot, ssem.at[1 + k], rsem.at[1 + k], sib).wait_recv()
            fw = _rcopy(slot, slot, ssem.at[4 + k], rsem.at[4 + k], sib)
            fw.start()
            sends.append(fw)
        slot = all_ref.at[jnp.bitwise_xor(me, 1)]
        _rcopy(slot, slot, ssem.at[0], rsem.at[0], sib).wait_recv()
        for k in range(3):
            slot = all_ref.at[2 * cidx[k] + 1 - c]
            _rcopy(slot, slot, ssem.at[4 + k], rsem.at[4 + k], sib).wait_recv()
        for cp in sends:
            cp.wait_send()
        own.wait()
        acc = all_ref[0]
        for k in range(1, N_DEV):
            acc = acc + all_ref[k]
        sum_ref[...] = acc

    return pl.pallas_call(
        body, name="gather_small", in_specs=[VMEM], out_specs=[VMEM, VMEM],
        out_shape=[jax.ShapeDtypeStruct((rows, n), F32), jax.ShapeDtypeStruct((N_DEV, rows, n), F32)],
        scratch_shapes=[pltpu.SemaphoreType.DMA((7,)), pltpu.SemaphoreType.DMA((7,)), pltpu.SemaphoreType.DMA],
        compiler_params=pltpu.CompilerParams(vmem_limit_bytes=VMEM_LIMIT),
    )(pack)


def _chip_partial(pos, g, recv, rows, name):
    ns, full, n = g.shape
    h = full // 2
    nb = h // rows

    def body(pos_ref, g_ref, r_ref, o_ref):
        o_ref[...] = (g_ref[...] + r_ref[...]).astype(BF16)

    return pl.pallas_call(
        body, name=name,
        grid_spec=pltpu.PrefetchScalarGridSpec(
            num_scalar_prefetch=1, grid=(ns, nb),
            in_specs=[pl.BlockSpec((None, rows, n), lambda s, i, p: (s, p[1] * nb + i, 0)),
                      pl.BlockSpec((None, rows, n), lambda s, i, p: (s, i, 0))],
            out_specs=pl.BlockSpec((None, rows, n), lambda s, i, p: (s, i, 0))),
        out_shape=jax.ShapeDtypeStruct((ns, h, n), BF16),
        compiler_params=_params(("parallel", "parallel")),
    )(pos, g, recv)


def _final_half(pos, g, recv_a, recv_b, rows, name):
    ns, full, n = g.shape
    h = full // 2
    nb = h // rows

    def body(pos_ref, g_ref, ra_ref, rb0_ref, rb1_ref, rb2_ref, o_ref):
        acc = g_ref[...] + ra_ref[...]
        for rb_ref in (rb0_ref, rb1_ref, rb2_ref):
            acc = acc + rb_ref[...].astype(F32)
        o_ref[...] = acc

    part = pl.BlockSpec((rows, n), lambda i, p: (i, 0))
    return pl.pallas_call(
        body, name=name,
        grid_spec=pltpu.PrefetchScalarGridSpec(
            num_scalar_prefetch=1, grid=(nb,),
            in_specs=[pl.BlockSpec((None, rows, n), lambda i, p: (p[0], p[1] * nb + i, 0)),
                      pl.BlockSpec((None, rows, n), lambda i, p: (p[0], i, 0)), part, part, part],
            out_specs=pl.BlockSpec((rows, n), lambda i, p: (p[1] * nb + i, 0))),
        out_shape=jax.ShapeDtypeStruct((full, n), F32),
        compiler_params=_params(("parallel",)),
    )(pos, g, recv_a, *recv_b)


def _modulation(c_rows, w_ada, b_ada, cols, name):
    d, n = w_ada.shape
    rows = c_rows.shape[0]

    def body(c_ref, w_ref, b_ref, o_ref):
        cv = c_ref[...]
        c_act = (cv * _sigmoid(cv)).astype(BF16)
        o_ref[...] = jnp.dot(c_act, w_ref[...].astype(BF16), preferred_element_type=F32) + b_ref[...]

    return pl.pallas_call(
        body, name=name, grid=(n // cols,),
        in_specs=[pl.BlockSpec((rows, d), lambda j: (0, 0)), pl.BlockSpec((d, cols), lambda j: (0, j)),
                  pl.BlockSpec((1, cols), lambda j: (0, j))],
        out_specs=pl.BlockSpec((rows, cols), lambda j: (0, j)),
        out_shape=jax.ShapeDtypeStruct((rows, n), F32),
        compiler_params=_params(("parallel",)),
    )(c_rows, w_ada, b_ada)


def _prenorm(x, norm_g, scale, shift, rows):
    s, d = x.shape

    def body(x_ref, g_ref, sc_ref, sh_ref, h_ref, r_ref):
        xv = x_ref[...]
        r = lax.rsqrt(jnp.mean(xv * xv, axis=-1, keepdims=True) + EPS)
        h = (xv * r * g_ref[...]) * (1.0 + sc_ref[...]) + sh_ref[...]
        h_ref[...] = h.astype(BF16)
        r_ref[...] = r

    vec = pl.BlockSpec((1, d), lambda i: (0, 0))
    return pl.pallas_call(
        body, name="prenorm", grid=(s // rows,),
        in_specs=[pl.BlockSpec((rows, d), lambda i: (i, 0)), vec, vec, vec],
        out_specs=[pl.BlockSpec((rows, d), lambda i: (i, 0)), pl.BlockSpec((rows, 1), lambda i: (i, 0))],
        out_shape=[jax.ShapeDtypeStruct((s, d), BF16), jax.ShapeDtypeStruct((s, 1), F32)],
        compiler_params=_params(("parallel",)),
    )(x, norm_g, scale, shift)


def _mixer_a_fwd(proj, conv_w, wa, rows, cols):
    s = proj.shape[0]
    ncb = wa // cols

    def body(ab_ref, ac_ref, ax_ref, az_ref, w_ref, y_ref, qbuf):
        t = pl.program_id(1)

        @pl.when(t == 0)
        def _():
            qbuf[0:HALO_A, :] = jnp.zeros((HALO_A, cols), F32)

        q = ac_ref[...].astype(F32) * ax_ref[...].astype(F32)
        qbuf[HALO_A:HALO_A + rows, :] = q
        conv = w_ref[2:3, :] * q
        for k in range(TAPS_A - 1):
            off = HALO_A - (TAPS_A - 1) + k
            conv = conv + w_ref[k:k + 1, :] * qbuf[off:off + rows, :]
        zv = az_ref[...].astype(F32)
        y_ref[...] = (ab_ref[...].astype(F32) * conv * (zv * _sigmoid(zv))).astype(BF16)
        qbuf[0:HALO_A, :] = qbuf[rows:rows + HALO_A, :]

    def sec(k):
        return pl.BlockSpec((rows, cols), lambda cb, t, k=k: (t, k * ncb + cb))

    return pl.pallas_call(
        body, name="mixer_a_fwd", grid=(ncb, s // rows),
        in_specs=[sec(0), sec(1), sec(2), sec(3), pl.BlockSpec((HALO_A, cols), lambda cb, t: (0, cb))],
        out_specs=pl.BlockSpec((rows, cols), lambda cb, t: (t, cb)),
        out_shape=jax.ShapeDtypeStruct((s, 2 * wa), BF16),
        scratch_shapes=[pltpu.VMEM((HALO_A + rows, cols), F32)],
        compiler_params=_params(("parallel", "arbitrary")),
    )(proj, proj, proj, proj, conv_w)


def _shifted_back(dst, src, lo, hi, cs):
    for n in range(8):
        dst[n, lo:hi, :] = src[lo - n:hi - n, cs]


def _shifted_fwd(dst, src, lo, hi, cs):
    for n in range(8):
        dst[n, lo:hi, :] = src[lo + n:hi + n, cs]


def _mixer_b_conv_fwd(proj, conv_w, conv_b, wa, rows, cols, chunk):
    s = proj.shape[0]
    wb = conv_w.shape[1]
    ncb = wb // cols
    sec0 = 4 * wa // cols

    def body(bv_ref, bg_ref, w_ref, b_ref, u0_ref, u_ref, ubuf, sh):
        t = pl.program_id(1)

        @pl.when(t == 0)
        def _():
            ubuf[0:HALO_B, :] = jnp.zeros((HALO_B, cols), F32)

        u0 = bv_ref[...].astype(F32) * _sigmoid(bg_ref[...].astype(F32))
        u0_ref[...] = u0
        ubuf[HALO_B:HALO_B + rows, :] = u0
        for lc in range(cols // LANES):
            cs = slice(lc * LANES, (lc + 1) * LANES)
            _shifted_back(sh, ubuf, 8, HALO_B + rows, cs)
            taps = [w_ref[k:k + 1, cs] for k in range(TAPS_B)]
            bias = b_ref[:, cs]

            def row_chunk(rc, carry, cs=cs, taps=taps, bias=bias):
                base = pl.multiple_of(rc * chunk, chunk)
                acc = jnp.zeros((chunk, LANES), F32)
                for k in range(TAPS_B):
                    mq, n = divmod(TAPS_B - 1 - k, 8)
                    acc = acc + taps[k] * sh[n, pl.ds(HALO_B - 8 * mq + base, chunk), :]
                u_ref[pl.ds(base, chunk), cs] = acc + bias
                return carry

            lax.fori_loop(0, rows // chunk, row_chunk, 0)
        ubuf[0:HALO_B, :] = ubuf[rows:rows + HALO_B, :]

    return pl.pallas_call(
        body, name="mixer_b_conv_fwd", grid=(ncb, s // rows),
        in_specs=[pl.BlockSpec((rows, cols), lambda cb, t: (t, sec0 + cb)),
                  pl.BlockSpec((rows, cols), lambda cb, t: (t, sec0 + ncb + cb)),
                  pl.BlockSpec((HALO_B, cols), lambda cb, t: (0, cb)),
                  pl.BlockSpec((1, cols), lambda cb, t: (0, cb))],
        out_specs=[pl.BlockSpec((rows, cols), lambda cb, t: (t, cb))] * 2,
        out_shape=[jax.ShapeDtypeStruct((s, wb), F32)] * 2,
        scratch_shapes=[pltpu.VMEM((HALO_B + rows, cols), F32), pltpu.VMEM((8, HALO_B + rows, LANES), F32)],
        compiler_params=_params(("parallel", "arbitrary")),
    )(proj, proj, conv_w, conv_b)


def _layernorm_stats(u):
    mu = jnp.mean(u, axis=-1, keepdims=True)
    xc = u - mu
    var = jnp.mean(xc * xc, axis=-1, keepdims=True)
    return xc * lax.rsqrt(var + EPS), lax.rsqrt(var + EPS)


def _mixer_b_gate_fwd(y, u, proj, ln_g, ln_b, wa, rows):
    s, wb = u.shape
    sec_z = (4 * wa + 2 * wb) // wb

    def body(y_in, u_ref, bz_ref, g_ref, b_ref, y_ref):
        uh, _ = _layernorm_stats(u_ref[...])
        ln = uh * g_ref[...] + b_ref[...]
        zv = bz_ref[...].astype(F32)
        y_ref[...] = ((ln * _sigmoid(ln)) * (zv * _sigmoid(zv))).astype(BF16)

    vec = pl.BlockSpec((1, wb), lambda i: (0, 0))
    return pl.pallas_call(
        body, name="mixer_b_gate_fwd", grid=(s // rows,),
        in_specs=[ANY, pl.BlockSpec((rows, wb), lambda i: (i, 0)), pl.BlockSpec((rows, wb), lambda i: (i, sec_z)),
                  vec, vec],
        out_specs=pl.BlockSpec((rows, wb), lambda i: (i, wa // wb)),
        out_shape=jax.ShapeDtypeStruct(y.shape, BF16), input_output_aliases={0: 0},
        compiler_params=_params(("parallel",)),
    )(y, u, proj, ln_g, ln_b)


def _loss_head(x, o, target, gate, final_g, rows):
    s, d = x.shape

    def body(x_ref, o_ref, t_ref, gate_ref, fg_ref, dx2_ref, do_ref, loss_ref, gfg_ref, dgate_ref):
        i = pl.program_id(0)
        ov = o_ref[...]
        x2 = x_ref[...] + gate_ref[...] * ov
        r2 = lax.rsqrt(jnp.mean(x2 * x2, axis=-1, keepdims=True) + EPS)
        xn2 = x2 * r2
        diff = xn2 * fg_ref[...] - t_ref[...]
        dout = diff * (1.0 / d)
        dxn2 = dout * fg_ref[...]
        dx2 = r2 * (dxn2 - xn2 * jnp.mean(dxn2 * xn2, axis=-1, keepdims=True))
        dx2_ref[...] = dx2
        do_ref[...] = (gate_ref[...] * dx2).astype(BF16)
        loss_part = 0.5 * jnp.sum(jnp.mean(diff * diff, axis=-1, keepdims=True), axis=0, keepdims=True)
        gfg_part = jnp.sum(dout * xn2, axis=0, keepdims=True)
        dgate_part = jnp.sum(dx2 * ov, axis=0, keepdims=True)

        @pl.when(i == 0)
        def _():
            loss_ref[...] = jnp.zeros_like(loss_ref)
            gfg_ref[...] = jnp.zeros_like(gfg_ref)
            dgate_ref[...] = jnp.zeros_like(dgate_ref)

        loss_ref[...] += jnp.broadcast_to(loss_part, loss_ref.shape)
        gfg_ref[...] += gfg_part
        dgate_ref[...] += dgate_part

    blk = pl.BlockSpec((rows, d), lambda i: (i, 0))
    vec = pl.BlockSpec((1, d), lambda i: (0, 0))
    return pl.pallas_call(
        body, name="loss_head", grid=(s // rows,),
        in_specs=[blk, blk, blk, vec, vec],
        out_specs=[blk, blk, pl.BlockSpec((1, 128), lambda i: (0, 0)), vec, vec],
        out_shape=[jax.ShapeDtypeStruct((s, d), F32), jax.ShapeDtypeStruct((s, d), BF16),
                   jax.ShapeDtypeStruct((1, 128), F32), jax.ShapeDtypeStruct((1, d), F32),
                   jax.ShapeDtypeStruct((1, d), F32)],
        compiler_params=_params(("arbitrary",)),
    )(x, o, target, gate, final_g)


def _mixer_a_bwd(proj, dy, conv_w, wa, din, rows):
    s = proj.shape[0]
    nt = s // rows
    per_halo = rows // HALO_IN

    def body(ab_ref, ac_ref, ax_ref, az_ref, hc_ref, hx_ref, dy_ref, w_ref, dp_ref, dw_ref, qbuf, dbuf):
        i = pl.program_id(0)

        @pl.when(i == 0)
        def _():
            dbuf[rows:rows + HALO_A, :] = jnp.zeros((HALO_A, wa), F32)
            dw_ref[...] = jnp.zeros_like(dw_ref)

        keep = jnp.where(i == nt - 1, 0.0, 1.0)
        before = hc_ref[...].astype(F32) * hx_ref[...].astype(F32) * keep
        qbuf[0:HALO_A, :] = before[HALO_IN - HALO_A:HALO_IN, :]
        acv, axv = ac_ref[...].astype(F32), ax_ref[...].astype(F32)
        q = acv * axv
        qbuf[HALO_A:HALO_A + rows, :] = q
        conv = w_ref[2:3, :] * q
        for k in range(TAPS_A - 1):
            off = HALO_A - (TAPS_A - 1) + k
            conv = conv + w_ref[k:k + 1, :] * qbuf[off:off + rows, :]
        zv, abv, dyv = az_ref[...].astype(F32), ab_ref[...].astype(F32), dy_ref[...]
        sg = _sigmoid(zv)
        sz = zv * sg
        dp_ref[:, 0:wa] = (dyv * conv * sz).astype(BF16)
        dp_ref[:, 3 * wa:4 * wa] = (dyv * abv * conv * (sg * (1.0 + zv * (1.0 - sg)))).astype(BF16)
        dconv = dyv * abv * sz
        dbuf[0:rows, :] = dconv
        dq = w_ref[2:3, :] * dconv
        for k in range(TAPS_A - 1):
            off = TAPS_A - 1 - k
            dq = dq + w_ref[k:k + 1, :] * dbuf[off:off + rows, :]
        dp_ref[:, wa:2 * wa] = (dq * axv).astype(BF16)
        dp_ref[:, 2 * wa:3 * wa] = (dq * acv).astype(BF16)
        for k in range(TAPS_A):
            off = HALO_A - (TAPS_A - 1) + k
            dw_ref[k:k + 1, :] += jnp.sum(dconv * qbuf[off:off + rows, :], axis=0, keepdims=True)
        dbuf[rows:rows + HALO_A, :] = dbuf[0:HALO_A, :]

    def sec(k):
        return pl.BlockSpec((rows, wa), lambda i, k=k: (nt - 1 - i, k))

    def halo(k):
        return pl.BlockSpec((HALO_IN, wa), lambda i, k=k: (jnp.maximum((nt - 1 - i) * per_halo - 1, 0), k))

    return pl.pallas_call(
        body, name="mixer_a_bwd", grid=(nt,),
        in_specs=[sec(0), sec(1), sec(2), sec(3), halo(1), halo(2),
                  pl.BlockSpec((rows, wa), lambda i: (nt - 1 - i, 0)),
                  pl.BlockSpec((HALO_A, wa), lambda i: (0, 0))],
        out_specs=[pl.BlockSpec((rows, 4 * wa), lambda i: (nt - 1 - i, 0)),
                   pl.BlockSpec((HALO_A, wa), lambda i: (0, 0))],
        out_shape=[jax.ShapeDtypeStruct((s, din), BF16), jax.ShapeDtypeStruct((HALO_A, wa), F32)],
        scratch_shapes=[pltpu.VMEM((HALO_A + rows, wa), F32), pltpu.VMEM((rows + HALO_A, wa), F32)],
        compiler_params=_params(("arbitrary",)),
    )(proj, proj, proj, proj, proj, proj, dy, conv_w)


def _mixer_b_gate_bwd(dproj, dy, u, proj, ln_g, ln_b, wa, rows):
    s, wb = u.shape
    sec_z = (4 * wa + 2 * wb) // wb

    def body(dp_in, dy_ref, u_ref, bz_ref, g_ref, b_ref, dp_ref, du_ref, dg_ref, db_ref, dcb_ref):
        i = pl.program_id(0)
        uh, rs = _layernorm_stats(u_ref[...])
        ln = uh * g_ref[...] + b_ref[...]
        sl = _sigmoid(ln)
        zv = bz_ref[...].astype(F32)
        sg = _sigmoid(zv)
        dyv = dy_ref[...]
        dp_ref[...] = (dyv * (ln * sl) * (sg * (1.0 + zv * (1.0 - sg)))).astype(BF16)
        dln = dyv * (zv * sg) * (sl * (1.0 + ln * (1.0 - sl)))
        duh = dln * g_ref[...]
        du = rs * (duh - jnp.mean(duh, axis=-1, keepdims=True) - uh * jnp.mean(duh * uh, axis=-1, keepdims=True))
        du_ref[...] = du

        @pl.when(i == 0)
        def _():
            dg_ref[...] = jnp.zeros_like(dg_ref)
            db_ref[...] = jnp.zeros_like(db_ref)
            dcb_ref[...] = jnp.zeros_like(dcb_ref)

        dg_ref[...] += jnp.sum(dln * uh, axis=0, keepdims=True)
        db_ref[...] += jnp.sum(dln, axis=0, keepdims=True)
        dcb_ref[...] += jnp.sum(du, axis=0, keepdims=True)

    blk = pl.BlockSpec((rows, wb), lambda i: (i, 0))
    vec = pl.BlockSpec((1, wb), lambda i: (0, 0))
    vshape = jax.ShapeDtypeStruct((1, wb), F32)
    return pl.pallas_call(
        body, name="mixer_b_gate_bwd", grid=(s // rows,),
        in_specs=[ANY, pl.BlockSpec((rows, wb), lambda i: (i, wa // wb)), blk,
                  pl.BlockSpec((rows, wb), lambda i: (i, sec_z)), vec, vec],
        out_specs=[pl.BlockSpec((rows, wb), lambda i: (i, sec_z)), blk, vec, vec, vec],
        out_shape=[jax.ShapeDtypeStruct(dproj.shape, BF16), jax.ShapeDtypeStruct((s, wb), F32), vshape, vshape, vshape],
        input_output_aliases={0: 0},
        compiler_params=_params(("arbitrary",)),
    )(dproj, dy, u, proj, ln_g, ln_b)


def _mixer_b_conv_bwd(dproj, du, u0, proj, conv_w, wa, rows, chunk):
    s, wb = du.shape
    nt = s // rows
    per32 = rows // HALO_B
    sec_v = 4 * wa // wb
    nrc = rows // chunk

    def body(dp_in, du_ref, u0_ref, h0_ref, bv_ref, bg_ref, w_ref, dp_ref, dw_ref, ubuf, dbuf, sh, shf, dwacc):
        i = pl.program_id(0)

        @pl.when(i == 0)
        def _():
            dbuf[rows:rows + HALO_B, :] = jnp.zeros((HALO_B, wb), F32)
            dwacc[...] = jnp.zeros_like(dwacc)

        ubuf[0:HALO_B, :] = h0_ref[...] * jnp.where(i == nt - 1, 0.0, 1.0)
        ubuf[HALO_B:HALO_B + rows, :] = u0_ref[...]
        dbuf[0:rows, :] = du_ref[...]
        for lc in range(wb // LANES):
            cs = slice(lc * LANES, (lc + 1) * LANES)
            _shifted_back(sh, ubuf, 8, HALO_B + rows, cs)
            _shifted_fwd(shf, dbuf, 0, rows + HALO_B - 8, cs)
            taps = [w_ref[k:k + 1, cs] for k in range(TAPS_B)]

            def conv_rows(rc, c0, cs=cs, taps=taps, lc=lc):
                base = pl.multiple_of(rc * chunk, chunk)
                acc = jnp.zeros((chunk, LANES), F32)
                for k in range(TAPS_B):
                    mq, n = divmod(TAPS_B - 1 - k, 8)
                    acc = acc + taps[k] * shf[n, pl.ds(base + 8 * mq, chunk), :]
                sg = _sigmoid(bg_ref[pl.ds(base, chunk), cs].astype(F32))
                bv = bv_ref[pl.ds(base, chunk), cs].astype(F32)
                dp_ref[pl.ds(base, chunk), cs] = (acc * sg).astype(BF16)
                dp_ref[pl.ds(base, chunk), wb + lc * LANES:wb + (lc + 1) * LANES] = (
                    acc * bv * sg * (1.0 - sg)).astype(BF16)
                return c0

            lax.fori_loop(0, nrc, conv_rows, 0)

            def dw_rows(rc, accs, cs=cs):
                base = pl.multiple_of(rc * chunk, chunk)
                du_c = dbuf[pl.ds(base, chunk), cs]
                out = []
                for k in range(TAPS_B):
                    mq, n = divmod(TAPS_B - 1 - k, 8)
                    prod = du_c * sh[n, pl.ds(HALO_B - 8 * mq + base, chunk), :]
                    out.append(accs[k] + jnp.sum(prod.reshape(chunk // SUBLANES, SUBLANES, LANES), axis=0))
                return tuple(out)

            accs = lax.fori_loop(0, nrc, dw_rows, tuple(jnp.zeros((SUBLANES, LANES), F32) for _ in range(TAPS_B)))
            for k in range(TAPS_B):
                dwacc[k * SUBLANES:(k + 1) * SUBLANES, cs] += accs[k]
        dbuf[rows:rows + HALO_B, :] = dbuf[0:HALO_B, :]

        @pl.when(i == nt - 1)
        def _():
            for k in range(HALO_B):
                dw_ref[k:k + 1, :] = jnp.sum(dwacc[k * SUBLANES:(k + 1) * SUBLANES, :], axis=0, keepdims=True)

    def rev(cols_blk):
        return pl.BlockSpec((rows, wb), lambda i, cb=cols_blk: (nt - 1 - i, cb))

    return pl.pallas_call(
        body, name="mixer_b_conv_bwd", grid=(nt,),
        in_specs=[ANY, rev(0), rev(0),
                  pl.BlockSpec((HALO_B, wb), lambda i: (jnp.maximum((nt - 1 - i) * per32 - 1, 0), 0)),
                  rev(sec_v), rev(sec_v + 1), pl.BlockSpec((HALO_B, wb), lambda i: (0, 0))],
        out_specs=[pl.BlockSpec((rows, 2 * wb), lambda i: (nt - 1 - i, sec_v // 2)),
                   pl.BlockSpec((HALO_B, wb), lambda i: (0, 0))],
        out_shape=[jax.ShapeDtypeStruct(dproj.shape, BF16), jax.ShapeDtypeStruct((HALO_B, wb), F32)],
        input_output_aliases={0: 0},
        scratch_shapes=[pltpu.VMEM((HALO_B + rows, wb), F32), pltpu.VMEM((rows + HALO_B, wb), F32),
                        pltpu.VMEM((8, HALO_B + rows, LANES), F32), pltpu.VMEM((8, rows + HALO_B, LANES), F32),
                        pltpu.VMEM((HALO_B * SUBLANES, wb), F32)],
        compiler_params=_params(("arbitrary",)),
    )(dproj, du, u0, u0, proj, proj, conv_w)


def _prenorm_bwd(x, r, dh, dx2, norm_g, scale, rows):
    s, d = x.shape

    def body(x_ref, r_ref, dh_ref, dx2_ref, g_ref, sc_ref, gx_ref, dsh_ref, dsc_ref, dg_ref):
        i = pl.program_id(0)
        rv = r_ref[...]
        xn = x_ref[...] * rv
        dhv = dh_ref[...]
        one_sc = 1.0 + sc_ref[...]
        dxn = dhv * one_sc * g_ref[...]
        gx_ref[...] = dx2_ref[...] + rv * (dxn - xn * jnp.mean(dxn * xn, axis=-1, keepdims=True))

        @pl.when(i == 0)
        def _():
            dsh_ref[...] = jnp.zeros_like(dsh_ref)
            dsc_ref[...] = jnp.zeros_like(dsc_ref)
            dg_ref[...] = jnp.zeros_like(dg_ref)

        dsh_ref[...] += jnp.sum(dhv, axis=0, keepdims=True)
        dsc_ref[...] += jnp.sum(dhv * (xn * g_ref[...]), axis=0, keepdims=True)
        dg_ref[...] += jnp.sum(dhv * one_sc * xn, axis=0, keepdims=True)

    blk = pl.BlockSpec((rows, d), lambda i: (i, 0))
    vec = pl.BlockSpec((1, d), lambda i: (0, 0))
    vshape = jax.ShapeDtypeStruct((1, d), F32)
    return pl.pallas_call(
        body, name="prenorm_bwd", grid=(s // rows,),
        in_specs=[blk, pl.BlockSpec((rows, 1), lambda i: (i, 0)), blk, blk, vec, vec],
        out_specs=[blk, vec, vec, vec],
        out_shape=[jax.ShapeDtypeStruct((s, d), F32), vshape, vshape, vshape],
        compiler_params=_params(("arbitrary",)),
    )(x, r, dh, dx2, norm_g, scale)


def _pad_rows(a, rows):
    return jnp.pad(a, ((0, rows - a.shape[0]), (0, 0)))


def _tile(n, want):
    t = min(n, want)
    while n % t:
        t -= 1
    return t


def kernel(x, c, norm_g, w_ada, b_ada, w_in, conv_a_w, conv_b_w, conv_b_b, ln_b_g, ln_b_b, w_out, final_g, loss_target, m_norm_g, m_w_ada, m_b_ada, m_w_in, m_conv_a_w, m_conv_b_w, m_conv_b_b, m_ln_b_g, m_ln_b_b, m_w_out, m_final_g, v_norm_g, v_w_ada, v_b_ada, v_w_in, v_conv_a_w, v_conv_b_w, v_conv_b_b, v_ln_b_g, v_ln_b_b, v_w_out, v_final_g):
    s, d = x.shape[1], x.shape[2]
    wa = conv_b_b.shape[-1]
    dmix = 2 * wa
    ns = w_in.shape[-1]
    din = N_CHIPS * ns
    r4 = w_out.shape[1]
    na = w_ada.shape[-1]
    wsh = conv_a_w.shape[-1]
    px, py, pc = _position()
    chip = 2 * px + py
    me = 4 * px + 2 * py + pc
    pos = jnp.stack([chip, pc]).astype(jnp.int32)
    x2d = x.reshape(s, d)
    target = loss_target.reshape(s, d)

    hc, ho, hrow = ns // 2, r4 // 2, d // 2
    _, cidx = _other_chips(px, py)
    hq = hc // 2
    win4 = _cast_quarters(w_in[0], _tile(d, 512), "cast_w_in")
    wout_bf = _cast_bf16(w_out[0], _tile(r4, 512), "cast_w_out")

    def gather_plan(b):
        x, y, cc = _position()
        chips, _ = _other_chips(x, y)
        xn, yn = (1 - x, y, cc), (x, 1 - y, cc)
        q0, q1 = b[0].at[2 * cc], b[0].at[2 * cc + 1]
        return ([(q0, b[2], xn), (q1, b[5], yn), (q1, b[3], xn), (q0, b[4], yn)]
                + [(b[1].at[pl.ds(cc * ho, ho), :], b[6 + k], (cx, cy, cc)) for k, (cx, cy) in enumerate(chips)])

    def gather_sent(b):
        return [(src, src, dev) for src, _, dev in gather_plan(list(b) + [None] * 7)]

    def onward_plan(b):
        x, y, cc = _position()
        sib = (x, y, 1 - cc)
        return [(b[0], b[2], (x, 1 - y, cc)), (b[1], b[3], (1 - x, y, cc)), (b[0], b[4], sib), (b[1], b[5], sib)]

    pairs = lambda n: _to_sibling([lambda ref, cc: ref] * n)

    c8 = jnp.broadcast_to(c, (8, d))
    cw = jnp.concatenate([_pad_rows(conv_a_w[0], HALO_A), _pad_rows(conv_b_w[0], HALO_B)], axis=0)
    c_all, cw_all = _gather_cond(c8, cw)
    c_rows = c_all[:, 0, :]
    cw_full = jnp.transpose(cw_all, (1, 0, 2)).reshape(HALO_A + HALO_B, wa)
    conv_a_full, conv_b_full = cw_full[:HALO_A], cw_full[HALO_A:]

    b_ada_sh = lax.dynamic_slice(b_ada, (0, chip * na), (1, na))
    mod_part = _modulation(_pad_rows(c_rows, 2 * N_DEV), w_ada[0], b_ada_sh, _tile(na, 512), "modulation")[:N_DEV]
    mod_all = _exchange_mod(mod_part)
    mod = lax.dynamic_index_in_dim(mod_all, me, axis=1, keepdims=False).reshape(1, 3 * d)
    shift, scale, gate = mod[:, :d], mod[:, d:2 * d], mod[:, 2 * d:]

    def quarter():
        return lax.empty((d, hq), BF16)

    g_sems, g_bufs, g_tok = _start_copies(
        "gather_start", gather_plan, 7,
        [win4, wout_bf] + [quarter() for _ in range(4)] + [lax.empty((ho, d), BF16) for _ in range(3)],
        after=[mod_all])
    win4, wout_bf, (x0, x1, y0, y1), lo = g_bufs[0], g_bufs[1], g_bufs[2:6], g_bufs[6:9]

    h, r = _prenorm(x2d, norm_g, scale, shift + g_tok[0, 0], _tile(s, 256))
    bm = _tile(s, 1024)
    pieces = [None, None]

    def piece(slot, half, quarters, name, own_half=None):
        where = jnp.reshape(2 * slot + half, (1,)).astype(jnp.int32)
        pieces[:] = _proj_piece(where, h, quarters, pieces[0], pieces[1], din, 2 * N_CHIPS, _tile(s, 512), name,
                                own_half=own_half)
        return pieces[0]

    proj = piece(chip, 0, (win4, win4), "proj_own0", own_half=0)
    proj = piece(chip, 1, (win4, win4), "proj_own1", own_half=1)
    x0, y1 = _wait_copies("gather_wait_a", _landed, [x0, y1], g_sems[0:4], after=[proj], send=False)
    on_sems, (x0, y1, dg0, dg1, sx0, sy1), _ = _start_copies(
        "pass_on_a", onward_plan, 4, [x0, y1] + [quarter() for _ in range(4)])
    x1, y0 = _wait_copies("gather_wait_b", _landed, [x1, y0], g_sems[4:8], after=[x0], send=False)
    pb_sems, (x1, sx1, y0, sy0), _ = _start_copies("pass_on_b", pairs(2), 2, [x1, quarter(), y0, quarter()])
    proj = piece(cidx[0], pc, (x0, x1), "proj_xa")
    proj = piece(cidx[1], pc, (y0, y1), "proj_ya")
    sx0, sy1 = _wait_copies("pass_wait_a", _landed, [sx0, sy1], on_sems[4:8], after=[proj], send=False)
    x1, sx1, y0, sy0 = _wait_copies("pass_wait_b", pairs(2), [x1, sx1, y0, sy0], pb_sems, after=[proj])
    proj = piece(cidx[0], 1 - pc, (sx0, sx1), "proj_xb")
    proj = piece(cidx[1], 1 - pc, (sy0, sy1), "proj_yb")
    x0, y1, dg0, dg1 = _wait_copies("diag_wait", lambda b: onward_plan(list(b) + [None, None])[:2],
                                    [x0, y1, dg0, dg1], on_sems[0:4], after=[proj])
    x0, y1 = _wait_copies("pass_sent_a", lambda b: [(b[0], b[0], (0, 0, 0)), (b[1], b[1], (0, 0, 0))],
                          [x0, y1], on_sems[4:8], after=[dg0], recv=False)
    pd_sems, (dg0, sd0, dg1, sd1), _ = _start_copies("pass_on_d", pairs(2), 2, [dg0, quarter(), dg1, quarter()],
                                                     after=[x0])
    proj = piece(cidx[2], pc, (dg0, dg1), "proj_da")
    dg0, sd0, dg1, sd1 = _wait_copies("pass_wait_d", pairs(2), [dg0, sd0, dg1, sd1], pd_sems, after=[proj])
    proj = piece(cidx[2], 1 - pc, (sd0, sd1), "proj_db")
    win_full = pieces[1]

    lo = _wait_copies("gather_wait_out", _landed, lo, g_sems[8:14], after=[proj], send=False)
    o_sems, o_bufs, o_tok = _start_copies(
        "pass_on_out", pairs(3), 3, [b for k in range(3) for b in (lo[k], lax.empty((ho, d), BF16))])
    win4, wout_bf = _wait_copies("gather_wait_sent", gather_sent, [win4, wout_bf], g_sems, after=[o_tok], recv=False)

    def slot_index(k, chip_, cc, others):
        if k == 0:
            return pl.ds(2 * chip_, 2)
        return 2 * others[(k - 1) % 3] + (cc if k <= 3 else 1 - cc)

    y = _mixer_a_fwd(proj, conv_a_full, wa, _tile(s, 512), _tile(wa, 512))
    u0, u = _mixer_b_conv_fwd(proj, conv_b_full, conv_b_b, wa, _tile(s, 512), _tile(wa, 256), 64)
    y = _mixer_b_gate_fwd(y, u, proj, ln_b_g, ln_b_b, wa, _tile(s, 256))
    o_bufs = _wait_copies("pass_wait_out", pairs(3), o_bufs, o_sems, after=[y])
    wout_full = _assemble("assemble_w_out", [wout_bf.reshape(2, ho, d)] + o_bufs[0::2] + o_bufs[1::2],
                          jax.ShapeDtypeStruct((2 * N_CHIPS, ho, d), BF16), slot_index)
    wout2d = wout_full.reshape(dmix, d)
    bd = _tile(d, 1024)
    o = _matmul(
        y, wout2d, grid=(s // bm, d // bd, 1),
        a_spec=pl.BlockSpec((bm, dmix), lambda i, j, k: (i, 0)),
        b_spec=pl.BlockSpec((dmix, bd), lambda i, j, k: (0, j)),
        o_spec=pl.BlockSpec((bm, bd), lambda i, j, k: (i, j)),
        out_shape=jax.ShapeDtypeStruct((s, d), F32), dims=((1,), (0,)), name="out_proj")
    dx2, do, loss_p, gfg_p, dgate_p = _loss_head(x2d, o, target, gate, final_g.reshape(1, d), _tile(s, 128))

    be = _tile(dmix, 1024)
    g_wout = _matmul(
        y, do, grid=(dmix // be, d // bd, 1),
        a_spec=pl.BlockSpec((s, be), lambda i, j, k: (0, i)),
        b_spec=pl.BlockSpec((s, bd), lambda i, j, k: (0, j)),
        o_spec=pl.BlockSpec((be, bd), lambda i, j, k: (i, j)),
        out_shape=jax.ShapeDtypeStruct((dmix, d), F32), dims=((0,), (0,)), name="grad_w_out")
    swap_out = _to_sibling([lambda ref, cc: ref.at[:, pl.ds((1 - cc) * ho, ho), :]])
    so_sems, (g_wout3, ra_out), so_tok = _start_copies(
        "swap_out_start", swap_out, 1, [g_wout.reshape(N_CHIPS, r4, d), lax.empty((N_CHIPS, ho, d), F32)])
    dy = _matmul(
        do, wout2d, grid=(s // bm, dmix // be, 1),
        a_spec=pl.BlockSpec((bm, d), lambda i, j, k: (i, 0)),
        b_spec=pl.BlockSpec((be, d), lambda i, j, k: (j, 0)),
        o_spec=pl.BlockSpec((bm, be), lambda i, j, k: (i, j)),
        out_shape=jax.ShapeDtypeStruct((s, dmix), F32), dims=((1,), (1,)), name="dy", after=[so_tok])
    g_wout3, ra_out = _wait_copies("swap_out_wait", swap_out, [g_wout3, ra_out], so_sems, after=[dy])
    q_out = _chip_partial(pos, g_wout3, ra_out, _tile(ho, 256), "chip_partial_w_out")
    po_sems, po_bufs, po_tok = _start_copies(
        "send_out_start", _slots_to_chips, 3, [q_out] + [lax.empty((ho, d), BF16) for _ in range(3)])
    dproj, dwa_p = _mixer_a_bwd(proj, dy, conv_a_full + po_tok[0, 0], wa, din, _tile(s, 128))
    dproj, du, dlng_p, dlnb_p, dcb_p = _mixer_b_gate_bwd(dproj, dy, u, proj, ln_b_g, ln_b_b, wa, _tile(s, 128))
    dproj, dwb_p = _mixer_b_conv_bwd(dproj, du, u0, proj, conv_b_full, wa, _tile(s, 256), 64)

    swap_in =_to_sibling([lambda ref, cc: ref.at[pl.ds((1 - cc) * hrow, hrow), :]])
    slots = [cidx[0], cidx[1], cidx[2], chip]
    core_only = jnp.stack([0 * pc, pc]).astype(jnp.int32)
    g, ra, sw, q, rb, snd = [None] * 4, [None] * 4, [None] * 4, [None] * 3, [None] * 3, [None] * 3
    after = []
    for k in range(4):
        g[k] = _grad_slot(jnp.reshape(slots[k], (1,)).astype(jnp.int32), h, dproj, after, ns, _tile(d, 512), hc,
                          f"grad_w_in{k}")
        sw[k], (g[k], ra[k]), tok = _start_copies(f"swap_in_start{k}", swap_in, 1,
                                                  [g[k], lax.empty((hrow, ns), F32)])
        after = [tok]
        if k >= 1:
            j = k - 1
            g[j], ra[j] = _wait_copies(f"swap_in_wait{j}", swap_in, [g[j], ra[j]], sw[j], after=[g[k]])
            part = _chip_partial(core_only, g[j][None], ra[j][None], _tile(hrow, 256), f"chip_partial_w_in{j}")
            snd[j], (q[j], rb[j]), tok2 = _start_copies(f"send_in_start{j}", _to_chip(j), 1,
                                                        [part.reshape(hrow, ns), lax.empty((hrow, ns), BF16)])
            after = [tok, tok2]
    dh = _matmul(
        dproj, win_full, grid=(s // bm, d // bd, 2 * N_CHIPS),
        a_spec=pl.BlockSpec((bm, hc), lambda i, j, k: (i, k)),
        b_spec=pl.BlockSpec((None, bd, hc), lambda i, j, k: (k, j, 0)),
        o_spec=pl.BlockSpec((bm, bd), lambda i, j, k: (i, j)),
        out_shape=jax.ShapeDtypeStruct((s, d), F32), dims=((1,), (1,)), name="dh", after=after)
    grad_x, dshift_p, dscale_p, gng_p = _prenorm_bwd(x2d, r, dh, dx2, norm_g, scale, _tile(s, 128))

    def rows_of(v):
        return _pad_rows(v.reshape(-1, wa), 8 * ((v.size // wa + 7) // 8))

    dmod = jnp.concatenate([dshift_p, dscale_p, dgate_p], axis=1)
    parts = [gng_p, dmod, dwa_p, dwb_p, dcb_p, dlng_p, dlnb_p, gfg_p,
             jnp.broadcast_to(loss_p[:, :1], (1, wa))]
    starts, packed = [], []
    for p in parts:
        starts.append(sum(q.shape[0] for q in packed))
        packed.append(rows_of(p) if p.shape[0] == 1 else p)
    small_sum, small_all = _gather_small(jnp.concatenate(packed, axis=0))

    def summed(k, rows):
        return small_sum[starts[k]:starts[k] + rows]

    grad_norm_g = summed(0, d // wa).reshape(1, d)
    grad_b_ada = summed(1, 3 * d // wa).reshape(1, 3 * d)
    grad_conv_a_full = summed(2, TAPS_A)
    grad_conv_b_full = summed(3, TAPS_B)
    grad_conv_b_b = summed(4, 1)
    grad_ln_b_g = summed(5, 1)
    grad_ln_b_b = summed(6, 1)
    grad_final_g = summed(7, d // wa).reshape(d)
    loss = summed(8, 1)[0, 0]
    grad_conv_a_w = lax.dynamic_slice(grad_conv_a_full, (0, chip * wsh), (TAPS_A, wsh))
    grad_conv_b_w = lax.dynamic_slice(grad_conv_b_full, (0, chip * wsh), (TAPS_B, wsh))
    dmod_all = small_all[:, starts[1]:starts[1] + 3 * d // wa, :].reshape(N_DEV, 3 * d)
    dmod_sh = lax.dynamic_slice(dmod_all, (0, chip * na), (N_DEV, na))

    def pairs_to_chips(b):
        x, y, cc = _position()
        chips, _ = _other_chips(x, y)
        return [(b[2 * k], b[2 * k + 1], (cx, cy, cc)) for k, (cx, cy) in enumerate(chips)]

    po_bufs = _wait_copies("send_out_wait", _slots_to_chips, po_bufs, po_sems, after=[small_sum])
    gh_out = _final_half(pos, g_wout3, ra_out, po_bufs[1:], _tile(ho, 256), "final_half_w_out")
    g[3], ra[3] = _wait_copies("swap_in_wait3", swap_in, [g[3], ra[3]], sw[3], after=[small_sum])
    in_bufs = _wait_copies("send_in_wait", pairs_to_chips, [b for k in range(3) for b in (q[k], rb[k])],
                           snd[0] + snd[1] + snd[2], after=[small_sum])
    gh_in = _final_half(core_only, g[3][None], ra[3][None], in_bufs[1::2], _tile(hrow, 256), "final_half_w_in")
    sh_sems, sh_bufs, sh_tok = _start_copies("share_start", _halves_to_sibling, 2, [gh_in, gh_out])

    grad_w_ada, d_wada, nm_wada, nv_wada = _adam_ada(c_rows.T, dmod_sh + sh_tok[0, 0], w_ada[0], m_w_ada[0],
                                                     v_w_ada[0], _tile(d, 128), "adam_w_ada")
    gw_in, gw_out = _wait_copies("share_wait", _halves_to_sibling, sh_bufs, sh_sems, after=[d_wada])
    grad_w_in, d_win, nm_win, nv_win = _adam(w_in[0], gw_in, m_w_in[0], v_w_in[0], _tile(d, 128), "adam_w_in",
                                             return_grad=True)
    grad_w_out, d_wout, nm_wout, nv_wout = _adam(w_out[0], gw_out, m_w_out[0], v_w_out[0], _tile(r4, 128),
                                                 "adam_w_out", return_grad=True)

    def small_adam(w, g, m, v, name):
        shape = w.shape
        w2 = w.reshape(-1, shape[-1])
        out = _adam(w2, g.reshape(w2.shape), m.reshape(w2.shape), v.reshape(w2.shape), w2.shape[0], name)
        return [o_.reshape(shape) for o_ in out]

    small = {
        "norm_g": small_adam(norm_g, grad_norm_g, m_norm_g, v_norm_g, "adam_norm_g"),
        "b_ada": small_adam(b_ada, grad_b_ada, m_b_ada, v_b_ada, "adam_b_ada"),
        "conv_a_w": small_adam(conv_a_w, grad_conv_a_w, m_conv_a_w, v_conv_a_w, "adam_conv_a_w"),
        "conv_b_w": small_adam(conv_b_w, grad_conv_b_w, m_conv_b_w, v_conv_b_w, "adam_conv_b_w"),
        "conv_b_b": small_adam(conv_b_b, grad_conv_b_b, m_conv_b_b, v_conv_b_b, "adam_conv_b_b"),
        "ln_b_g": small_adam(ln_b_g, grad_ln_b_g, m_ln_b_g, v_ln_b_g, "adam_ln_b_g"),
        "ln_b_b": small_adam(ln_b_b, grad_ln_b_b, m_ln_b_b, v_ln_b_b, "adam_ln_b_b"),
        "final_g": small_adam(final_g.reshape(1, d), grad_final_g, m_final_g.reshape(1, d),
                              v_final_g.reshape(1, d), "adam_final_g"),
    }
    small["final_g"] = [o_.reshape(d) for o_ in small["final_g"]]
    big = {
        "w_ada": [a[None] for a in (d_wada, nm_wada, nv_wada)],
        "w_in": [a[None] for a in (d_win, nm_win, nv_win)],
        "w_out": [a[None] for a in (d_wout, nm_wout, nv_wout)],
    }
    upd = {**small, **big}
    order = ["norm_g", "w_ada", "b_ada", "w_in", "conv_a_w", "conv_b_w", "conv_b_b", "ln_b_g", "ln_b_b",
             "w_out", "final_g"]
    grads = {
        "norm_g": grad_norm_g, "w_ada": grad_w_ada[None], "b_ada": grad_b_ada, "w_in": grad_w_in[None],
        "conv_a_w": grad_conv_a_w[None], "conv_b_w": grad_conv_b_w[None], "conv_b_b": grad_conv_b_b,
        "ln_b_g": grad_ln_b_g, "ln_b_b": grad_ln_b_b, "w_out": grad_w_out[None], "final_g": grad_final_g,
    }
    return (loss, grad_x.reshape(1, s, d), *[grads[n] for n in order], *[upd[n][0] for n in order],
            *[upd[n][1] for n in order], *[upd[n][2] for n in order])
```

```python
import functools

import jax
import jax.numpy as jnp
from jax import lax
from jax.experimental import pallas as pl
from jax.experimental.pallas import tpu as pltpu

F32 = jnp.float32
BF16 = jnp.bfloat16
EPS = 1e-6
N_CHIPS = 4
N_DEV = 8
TAPS_A = 3
TAPS_B = 31
HALO_A = 8
HALO_IN = 16
HALO_B = 32
LANES = 128
SUBLANES = 8
ADAM_LR = 0.001
ADAM_B1 = 0.9
ADAM_B2 = 0.999
ADAM_EPS = 1e-08
ADAM_WD = 0.01
ADAM_STEP = 10
VMEM_LIMIT = 56 * 1024 * 1024
MESH = pl.DeviceIdType.MESH
ANY = pl.BlockSpec(memory_space=pl.ANY)
VMEM = pl.BlockSpec(memory_space=pltpu.VMEM)
HBM_SPEC = pl.BlockSpec(memory_space=pltpu.HBM)
SEM_SPEC = pl.BlockSpec(memory_space=pltpu.SEMAPHORE)
EFFECT = pltpu.SideEffectType.DATAFLOW_SIDE_EFFECTING


def _params(sem=None):
    return pltpu.CompilerParams(dimension_semantics=sem, vmem_limit_bytes=VMEM_LIMIT)


def _sigmoid(v):
    return jax.nn.sigmoid(v)


def _position():
    return lax.axis_index("x"), lax.axis_index("y"), lax.axis_index("c")


def _rcopy(src, dst, ssem, rsem, dev):
    return pltpu.make_async_remote_copy(src_ref=src, dst_ref=dst, send_sem=ssem, recv_sem=rsem,
                                        device_id=dev, device_id_type=MESH)


def _other_chips(x, y):
    chips = [(1 - x, y), (x, 1 - y), (1 - x, 1 - y)]
    return chips, [2 * cx + cy for cx, cy in chips]


def _cast_bf16(a, rows, name):
    m, n = a.shape

    def body(a_ref, o_ref):
        o_ref[...] = a_ref[...].astype(BF16)

    return pl.pallas_call(
        body, name=name, grid=(m // rows,),
        in_specs=[pl.BlockSpec((rows, n), lambda i: (i, 0))],
        out_specs=pl.BlockSpec((rows, n), lambda i: (i, 0)),
        out_shape=jax.ShapeDtypeStruct((m, n), BF16),
        compiler_params=_params(("parallel",)),
    )(a)


def _cast_quarters(a, rows, name):
    m, n = a.shape
    hq = n // 4

    def body(a_ref, o_ref):
        o_ref[...] = a_ref[...].astype(BF16)

    return pl.pallas_call(
        body, name=name, grid=(4, m // rows),
        in_specs=[pl.BlockSpec((rows, hq), lambda q, i: (i, q))],
        out_specs=pl.BlockSpec((None, rows, hq), lambda q, i: (q, i, 0)),
        out_shape=jax.ShapeDtypeStruct((4, m, hq), BF16),
        compiler_params=_params(("parallel", "parallel")),
    )(a)


def _proj_piece(where, h, quarters, proj, w_all, din, n_pieces, bm, name, own_half=None):
    s, d = h.shape
    hq = quarters[0].shape[-1]
    nm = s // bm
    if own_half is None:
        q_specs = [pl.BlockSpec((d, hq), lambda i, p: (0, 0), pipeline_mode=pl.Buffered(1))] * 2
    else:
        q_specs = [pl.BlockSpec((None, d, hq), lambda i, p, k=k: (2 * own_half + k, 0, 0),
                                pipeline_mode=pl.Buffered(1)) for k in range(2)]

    def body(p_ref, h_ref, q0_ref, q1_ref, *rest):
        o_ref, wall_ref, wbuf, sem = rest[-4:]
        i = pl.program_id(0)
        filed = pltpu.make_async_copy(wbuf, wall_ref.at[p_ref[0]], sem)

        @pl.when(i == 0)
        def _():
            wbuf[:, 0:hq] = q0_ref[...]
            wbuf[:, hq:2 * hq] = q1_ref[...]
            filed.start()

        o_ref[...] = jnp.dot(h_ref[...], wbuf[...], preferred_element_type=F32).astype(BF16)

        @pl.when(i == nm - 1)
        def _():
            filed.wait()

    args, extra, alias = [where, h, quarters[0], quarters[1]], [], {}
    if proj is not None:
        args, extra, alias = args + [proj, w_all], [ANY, ANY], {4: 0, 5: 1}
    return pl.pallas_call(
        body, name=name,
        grid_spec=pltpu.PrefetchScalarGridSpec(
            num_scalar_prefetch=1, grid=(nm,),
            in_specs=[pl.BlockSpec((bm, d), lambda i, p: (i, 0))] + q_specs + extra,
            out_specs=[pl.BlockSpec((bm, 2 * hq), lambda i, p: (i, p[0])), ANY],
            scratch_shapes=[pltpu.VMEM((d, 2 * hq), BF16), pltpu.SemaphoreType.DMA]),
        out_shape=[jax.ShapeDtypeStruct((s, din), BF16), jax.ShapeDtypeStruct((n_pieces, d, 2 * hq), BF16)],
        input_output_aliases=alias,
        compiler_params=_params(("arbitrary",)),
    )(*args)


def _grad_slot(where, h, dproj, after, ns, bd, bn, name, add=None, out_dtype=F32):
    s, d = h.shape
    nb = ns // bn
    ni = d // 2 // bd
    extra = [] if add is None else [add]

    def body(where_ref, h_ref, dp_ref, *rest):
        acc = lax.dot_general(h_ref[...], dp_ref[...], (((0,), (0,)), ((), ())), preferred_element_type=F32)
        if add is not None:
            acc = acc + rest[0][...]
        rest[-1][...] = acc.astype(out_dtype)

    return pl.pallas_call(
        body, name=name,
        grid_spec=pltpu.PrefetchScalarGridSpec(
            num_scalar_prefetch=1, grid=(nb, ni),
            in_specs=[pl.BlockSpec((s, bd), lambda j, i, w: (0, w[1] * ni + i)),
                      pl.BlockSpec((s, bn), lambda j, i, w: (0, w[0] * nb + j))]
            + [pl.BlockSpec((bd, bn), lambda j, i, w: (i, j))] * len(extra) + [ANY] * len(after),
            out_specs=pl.BlockSpec((bd, bn), lambda j, i, w: (i, j))),
        out_shape=jax.ShapeDtypeStruct((d // 2, ns), out_dtype),
        compiler_params=_params(("parallel", "parallel")),
    )(where, h, dproj, *extra, *after)


def _matmul(a, b, *, grid, a_spec, b_spec, o_spec, out_shape, dims, name, after=()):
    nk = grid[2]
    n_after = len(after)

    def body(a_ref, b_ref, *rest):
        o_ref, acc = rest[n_after], rest[n_after + 1:]
        p = lax.dot_general(a_ref[...], b_ref[...], (dims, ((), ())), preferred_element_type=F32)
        if nk == 1:
            o_ref[...] = p.astype(o_ref.dtype)
        else:
            acc_ref, = acc
            k = pl.program_id(2)

            @pl.when(k == 0)
            def _():
                acc_ref[...] = p

            @pl.when(k > 0)
            def _():
                acc_ref[...] += p

            @pl.when(k == nk - 1)
            def _():
                o_ref[...] = acc_ref[...].astype(o_ref.dtype)

    block = [d for d in o_spec.block_shape if d is not None]
    scratch = [pltpu.VMEM(tuple(block), F32)] if nk > 1 else []
    return pl.pallas_call(
        body, name=name, grid=grid, in_specs=[a_spec, b_spec] + [ANY] * n_after, out_specs=o_spec,
        out_shape=out_shape, scratch_shapes=scratch,
        compiler_params=_params(("parallel", "parallel", "arbitrary")),
    )(a, b, *after)


def _adam_math(w, g, m, v):
    m = ADAM_B1 * m + (1.0 - ADAM_B1) * g
    v = ADAM_B2 * v + (1.0 - ADAM_B2) * (g * g)
    m_hat = m / (1.0 - ADAM_B1 ** ADAM_STEP)
    v_hat = v / (1.0 - ADAM_B2 ** ADAM_STEP)
    delta = -ADAM_LR * (m_hat / (jnp.sqrt(v_hat) + ADAM_EPS) + ADAM_WD * w)
    return delta, m, v


def _adam(w, g, m, v, rows, name, return_grad=False):
    r, n = w.shape

    def body(w_ref, g_ref, m_ref, v_ref, *out):
        gv = g_ref[...]
        d, mo, vo = _adam_math(w_ref[...], gv, m_ref[...], v_ref[...])
        for o_ref, val in zip(out, ([gv] if return_grad else []) + [d, mo, vo]):
            o_ref[...] = val

    spec = pl.BlockSpec((rows, n), lambda i: (i, 0))
    shape = jax.ShapeDtypeStruct((r, n), F32)
    n_out = 4 if return_grad else 3
    return pl.pallas_call(
        body, name=name, grid=(r // rows,), in_specs=[spec] * 4, out_specs=[spec] * n_out,
        out_shape=[shape] * n_out, compiler_params=_params(("parallel",)),
    )(w, g, m, v)


def _adam_ada(c_cols, dmod, w, m, v, rows, name):
    r, n = w.shape

    def body(c_ref, dm_ref, w_ref, m_ref, v_ref, g_ref, d_ref, mo_ref, vo_ref):
        cv = c_ref[...]
        c_act = cv * _sigmoid(cv)
        g = c_act[:, 0:1] * dm_ref[0:1, :]
        for b in range(1, N_DEV):
            g = g + c_act[:, b:b + 1] * dm_ref[b:b + 1, :]
        d, mo, vo = _adam_math(w_ref[...], g, m_ref[...], v_ref[...])
        g_ref[...] = g
        d_ref[...] = d
        mo_ref[...] = mo
        vo_ref[...] = vo

    spec = pl.BlockSpec((rows, n), lambda i: (i, 0))
    shape = jax.ShapeDtypeStruct((r, n), F32)
    return pl.pallas_call(
        body, name=name, grid=(r // rows,),
        in_specs=[pl.BlockSpec((rows, N_DEV), lambda i: (i, 0)), pl.BlockSpec((N_DEV, n), lambda i: (0, 0)),
                  spec, spec, spec],
        out_specs=[spec] * 4, out_shape=[shape] * 4, compiler_params=_params(("parallel",)),
    )(c_cols, dmod, w, m, v)


def _start_copies(name, plan, n, bufs, after=()):
    nb, na = len(bufs), len(after)

    def body(*refs):
        sems = refs[nb + na:nb + na + 2 * n]
        for k, (src, dst, dev) in enumerate(plan(refs[:nb])):
            _rcopy(src, dst, sems[2 * k], sems[2 * k + 1], dev).start()
        refs[-1][...] = jnp.zeros((8, 128), F32)

    outs = pl.pallas_call(
        body, name=name,
        out_shape=[pltpu.SemaphoreType.DMA(())] * (2 * n) + [pltpu.HBM(a.shape, a.dtype) for a in bufs]
        + [jax.ShapeDtypeStruct((8, 128), F32)],
        in_specs=[HBM_SPEC] * nb + [ANY] * na, out_specs=[SEM_SPEC] * (2 * n) + [HBM_SPEC] * nb + [VMEM],
        input_output_aliases={i: 2 * n + i for i in range(nb)},
        compiler_params=pltpu.CompilerParams(has_side_effects=EFFECT),
    )(*[pltpu.with_memory_space_constraint(a, pltpu.HBM) for a in bufs], *after)
    return list(outs[:2 * n]), list(outs[2 * n:2 * n + nb]), outs[-1]


def _wait_copies(name, plan, bufs, sems, after=(), send=True, recv=True):
    nb, nsem = len(bufs), len(sems)

    def body(*refs):
        s = refs[nb:nb + nsem]
        for k, (src, dst, dev) in enumerate(plan(refs[:nb])):
            cp = _rcopy(src, dst, s[2 * k], s[2 * k + 1], dev)
            if send:
                cp.wait_send()
            if recv:
                cp.wait_recv()

    outs = pl.pallas_call(
        body, name=name, out_shape=[pltpu.HBM(a.shape, a.dtype) for a in bufs],
        in_specs=[HBM_SPEC] * nb + [SEM_SPEC] * nsem + [ANY] * len(after), out_specs=[HBM_SPEC] * nb,
        input_output_aliases={i: i for i in range(nb)},
        compiler_params=pltpu.CompilerParams(has_side_effects=EFFECT),
    )(*bufs, *sems, *after)
    return list(outs)


def _to_sibling(views):
    def plan(b):
        x, y, c = _position()
        return [(view(b[2 * k], c), b[2 * k + 1], (x, y, 1 - c)) for k, view in enumerate(views)]
    return plan


def _to_chip(k):
    def plan(b):
        x, y, c = _position()
        cx, cy = _other_chips(x, y)[0][k]
        return [(b[0], b[1], (cx, cy, c))]
    return plan


def _slots_to_chips(b):
    x, y, c = _position()
    chips, cidx = _other_chips(x, y)
    return [(b[0].at[cidx[k]], b[1 + k], (cx, cy, c)) for k, (cx, cy) in enumerate(chips)]


def _halves_to_sibling(b):
    x, y, c = _position()
    views = [r.at[pl.ds(c * (r.shape[0] // 2), r.shape[0] // 2), :] for r in b]
    return [(v, v, (x, y, 1 - c)) for v in views]


def _landed(b):
    x, y, c = _position()
    return [(ref, ref, (x, y, c)) for ref in b]


def _assemble(name, pieces, out_shape, index_of):
    n = len(pieces)

    def body(*refs):
        out_ref, sem = refs[n], refs[n + 1]
        x, y, c = _position()
        _, cidx = _other_chips(x, y)
        cps = [pltpu.make_async_copy(refs[k], out_ref.at[index_of(k, 2 * x + y, c, cidx)], sem.at[k]) for k in range(n)]
        for cp in cps:
            cp.start()
        for cp in cps:
            cp.wait()

    return pl.pallas_call(
        body, name=name, in_specs=[VMEM] * n, out_specs=ANY, out_shape=out_shape,
        scratch_shapes=[pltpu.SemaphoreType.DMA((n,))],
        compiler_params=pltpu.CompilerParams(vmem_limit_bytes=VMEM_LIMIT),
    )(*pieces)


def _gather_cond(c8, cw):
    def body(c8_ref, cw_ref, call_ref, cwall_ref, ssem, rsem, lsem):
        x, y, c = _position()
        chip = 2 * x + y
        me = 4 * x + 2 * y + c
        chips, cidx = _other_chips(x, y)
        own = [pltpu.make_async_copy(c8_ref, call_ref.at[me], lsem.at[0]),
               pltpu.make_async_copy(cw_ref, cwall_ref.at[chip], lsem.at[1])]
        for cp in own:
            cp.start()
        sends = [_rcopy(cw_ref, cwall_ref.at[chip], ssem.at[k], rsem.at[k], (cx, cy, c))
                 for k, (cx, cy) in enumerate(chips)]
        for mask in range(1, N_DEV):
            fx, fy, fc = (mask >> 2) & 1, (mask >> 1) & 1, mask & 1
            dev = (1 - x if fx else x, 1 - y if fy else y, 1 - c if fc else c)
            sends.append(_rcopy(c8_ref, call_ref.at[me], ssem.at[2 + mask], rsem.at[2 + mask], dev))
        for cp in sends:
            cp.start()
        for k in range(3):
            slot = cwall_ref.at[cidx[k]]
            _rcopy(slot, slot, ssem.at[k], rsem.at[k], (x, y, c)).wait_recv()
        for mask in range(1, N_DEV):
            slot = call_ref.at[jnp.bitwise_xor(me, mask)]
            _rcopy(slot, slot, ssem.at[2 + mask], rsem.at[2 + mask], (x, y, c)).wait_recv()
        for cp in sends:
            cp.wait_send()
        for cp in own:
            cp.wait()

    return pl.pallas_call(
        body, name="gather_cond", in_specs=[VMEM, VMEM], out_specs=[VMEM, VMEM],
        out_shape=[jax.ShapeDtypeStruct((N_DEV,) + c8.shape, F32), jax.ShapeDtypeStruct((N_CHIPS,) + cw.shape, F32)],
        scratch_shapes=[pltpu.SemaphoreType.DMA((10,)), pltpu.SemaphoreType.DMA((10,)), pltpu.SemaphoreType.DMA((2,))],
    )(c8, cw)


def _exchange_mod(mod_part):
    def body(mp_ref, out_ref, ssem, rsem, lsem):
        x, y, c = _position()
        chip = 2 * x + y
        chips, cidx = _other_chips(x, y)
        own = pltpu.make_async_copy(mp_ref, out_ref.at[chip], lsem)
        own.start()
        sends = [_rcopy(mp_ref, out_ref.at[chip], ssem.at[k], rsem.at[k], (cx, cy, c))
                 for k, (cx, cy) in enumerate(chips)]
        for cp in sends:
            cp.start()
        for k in range(3):
            slot = out_ref.at[cidx[k]]
            _rcopy(slot, slot, ssem.at[k], rsem.at[k], (x, y, c)).wait_recv()
        for cp in sends:
            cp.wait_send()
        own.wait()

    return pl.pallas_call(
        body, name="exchange_mod", in_specs=[VMEM], out_specs=VMEM,
        out_shape=jax.ShapeDtypeStruct((N_CHIPS,) + mod_part.shape, F32),
        scratch_shapes=[pltpu.SemaphoreType.DMA((3,)), pltpu.SemaphoreType.DMA((3,)), pltpu.SemaphoreType.DMA],
    )(mod_part)


def _gather_small(pack):
    rows, n = pack.shape

    def body(p_ref, sum_ref, all_ref, ssem, rsem, lsem):
        x, y, c = _position()
        me = 4 * x + 2 * y + c
        sib = (x, y, 1 - c)
        chips, cidx = _other_chips(x, y)
        own = pltpu.make_async_copy(p_ref, all_ref.at[me], lsem)
        own.start()
        sends = [_rcopy(p_ref, all_ref.at[me], ssem.at[0], rsem.at[0], sib)]
        sends += [_rcopy(p_ref, all_ref.at[me], ssem.at[1 + k], rsem.at[1 + k], (cx, cy, c))
                  for k, (cx, cy) in enumerate(chips)]
        for cp in sends:
            cp.start()
        for k in range(3):
            slot = all_ref.at[2 * cidx[k] + c]
            _rcopy(slot, slot, ssem.at[1 + k], rsem.at[1 + k], sib).wait_recv()
            fw = _rcopy(slot, slot, ssem.at[4 + k], rsem.at[4 + k], sib)
            fw.start()
            sends.append(fw)
        slot = all_ref.at[jnp.bitwise_xor(me, 1)]
        _rcopy(slot, slot, ssem.at[0], rsem.at[0], sib).wait_recv()
        for k in range(3):
            slot = all_ref.at[2 * cidx[k] + 1 - c]
            _rcopy(slot, slot, ssem.at[4 + k], rsem.at[4 + k], sib).wait_recv()
        for cp in sends:
            cp.wait_send()
        own.wait()
        acc = all_ref[0]
        for k in range(1, N_DEV):
            acc = acc + all_ref[k]
        sum_ref[...] = acc

    return pl.pallas_call(
        body, name="gather_small", in_specs=[VMEM], out_specs=[VMEM, VMEM],
        out_shape=[jax.ShapeDtypeStruct((rows, n), F32), jax.ShapeDtypeStruct((N_DEV, rows, n), F32)],
        scratch_shapes=[pltpu.SemaphoreType.DMA((7,)), pltpu.SemaphoreType.DMA((7,)), pltpu.SemaphoreType.DMA],
        compiler_params=pltpu.CompilerParams(vmem_limit_bytes=VMEM_LIMIT),
    )(pack)


def _chip_partial(pos, g, recv, rows, name):
    ns, full, n = g.shape
    h = full // 2
    nb = h // rows

    def body(pos_ref, g_ref, r_ref, o_ref):
        o_ref[...] = (g_ref[...] + r_ref[...]).astype(BF16)

    return pl.pallas_call(
        body, name=name,
        grid_spec=pltpu.PrefetchScalarGridSpec(
            num_scalar_prefetch=1, grid=(ns, nb),
            in_specs=[pl.BlockSpec((None, rows, n), lambda s, i, p: (s, p[1] * nb + i, 0)),
                      pl.BlockSpec((None, rows, n), lambda s, i, p: (s, i, 0))],
            out_specs=pl.BlockSpec((None, rows, n), lambda s, i, p: (s, i, 0))),
        out_shape=jax.ShapeDtypeStruct((ns, h, n), BF16),
        compiler_params=_params(("parallel", "parallel")),
    )(pos, g, recv)


def _final_sum(pos, first, parts, rows, name):
    h, n = first.shape
    nb = h // rows

    def body(pos_ref, f_ref, rb0_ref, rb1_ref, rb2_ref, o_ref):
        acc = f_ref[...]
        for rb_ref in (rb0_ref, rb1_ref, rb2_ref):
            acc = acc + rb_ref[...].astype(F32)
        o_ref[...] = acc

    part = pl.BlockSpec((rows, n), lambda i, p: (i, 0))
    return pl.pallas_call(
        body, name=name,
        grid_spec=pltpu.PrefetchScalarGridSpec(
            num_scalar_prefetch=1, grid=(nb,), in_specs=[part] * 4,
            out_specs=pl.BlockSpec((rows, n), lambda i, p: (p[1] * nb + i, 0))),
        out_shape=jax.ShapeDtypeStruct((2 * h, n), F32),
        compiler_params=_params(("parallel",)),
    )(pos, first, *parts)


def _final_half(pos, g, recv_a, recv_b, rows, name):
    ns, full, n = g.shape
    h = full // 2
    nb = h // rows

    def body(pos_ref, g_ref, ra_ref, rb0_ref, rb1_ref, rb2_ref, o_ref):
        acc = g_ref[...] + ra_ref[...]
        for rb_ref in (rb0_ref, rb1_ref, rb2_ref):
            acc = acc + rb_ref[...].astype(F32)
        o_ref[...] = acc

    part = pl.BlockSpec((rows, n), lambda i, p: (i, 0))
    return pl.pallas_call(
        body, name=name,
        grid_spec=pltpu.PrefetchScalarGridSpec(
            num_scalar_prefetch=1, grid=(nb,),
            in_specs=[pl.BlockSpec((None, rows, n), lambda i, p: (p[0], p[1] * nb + i, 0)),
                      pl.BlockSpec((None, rows, n), lambda i, p: (p[0], i, 0)), part, part, part],
            out_specs=pl.BlockSpec((rows, n), lambda i, p: (p[1] * nb + i, 0))),
        out_shape=jax.ShapeDtypeStruct((full, n), F32),
        compiler_params=_params(("parallel",)),
    )(pos, g, recv_a, *recv_b)


def _modulation(c_rows, w_ada, b_ada, cols, name):
    d, n = w_ada.shape
    rows = c_rows.shape[0]

    def body(c_ref, w_ref, b_ref, o_ref):
        cv = c_ref[...]
        c_act = (cv * _sigmoid(cv)).astype(BF16)
        o_ref[...] = jnp.dot(c_act, w_ref[...].astype(BF16), preferred_element_type=F32) + b_ref[...]

    return pl.pallas_call(
        body, name=name, grid=(n // cols,),
        in_specs=[pl.BlockSpec((rows, d), lambda j: (0, 0)), pl.BlockSpec((d, cols), lambda j: (0, j)),
                  pl.BlockSpec((1, cols), lambda j: (0, j))],
        out_specs=pl.BlockSpec((rows, cols), lambda j: (0, j)),
        out_shape=jax.ShapeDtypeStruct((rows, n), F32),
        compiler_params=_params(("parallel",)),
    )(c_rows, w_ada, b_ada)


def _prenorm(x, norm_g, scale, shift, rows):
    s, d = x.shape

    def body(x_ref, g_ref, sc_ref, sh_ref, h_ref, r_ref):
        xv = x_ref[...]
        r = lax.rsqrt(jnp.mean(xv * xv, axis=-1, keepdims=True) + EPS)
        h = (xv * r * g_ref[...]) * (1.0 + sc_ref[...]) + sh_ref[...]
        h_ref[...] = h.astype(BF16)
        r_ref[...] = r

    vec = pl.BlockSpec((1, d), lambda i: (0, 0))
    return pl.pallas_call(
        body, name="prenorm", grid=(s // rows,),
        in_specs=[pl.BlockSpec((rows, d), lambda i: (i, 0)), vec, vec, vec],
        out_specs=[pl.BlockSpec((rows, d), lambda i: (i, 0)), pl.BlockSpec((rows, 1), lambda i: (i, 0))],
        out_shape=[jax.ShapeDtypeStruct((s, d), BF16), jax.ShapeDtypeStruct((s, 1), F32)],
        compiler_params=_params(("parallel",)),
    )(x, norm_g, scale, shift)


def _mixer_a_fwd(proj, conv_w, wa, rows, cols):
    s = proj.shape[0]
    ncb = wa // cols

    def body(ab_ref, ac_ref, ax_ref, az_ref, w_ref, y_ref, qbuf):
        t = pl.program_id(1)

        @pl.when(t == 0)
        def _():
            qbuf[0:HALO_A, :] = jnp.zeros((HALO_A, cols), F32)

        q = ac_ref[...].astype(F32) * ax_ref[...].astype(F32)
        qbuf[HALO_A:HALO_A + rows, :] = q
        conv = w_ref[2:3, :] * q
        for k in range(TAPS_A - 1):
            off = HALO_A - (TAPS_A - 1) + k
            conv = conv + w_ref[k:k + 1, :] * qbuf[off:off + rows, :]
        zv = az_ref[...].astype(F32)
        y_ref[...] = (ab_ref[...].astype(F32) * conv * (zv * _sigmoid(zv))).astype(BF16)
        qbuf[0:HALO_A, :] = qbuf[rows:rows + HALO_A, :]

    def sec(k):
        return pl.BlockSpec((rows, cols), lambda cb, t, k=k: (t, k * ncb + cb))

    return pl.pallas_call(
        body, name="mixer_a_fwd", grid=(ncb, s // rows),
        in_specs=[sec(0), sec(1), sec(2), sec(3), pl.BlockSpec((HALO_A, cols), lambda cb, t: (0, cb))],
        out_specs=pl.BlockSpec((rows, cols), lambda cb, t: (t, cb)),
        out_shape=jax.ShapeDtypeStruct((s, 2 * wa), BF16),
        scratch_shapes=[pltpu.VMEM((HALO_A + rows, cols), F32)],
        compiler_params=_params(("parallel", "arbitrary")),
    )(proj, proj, proj, proj, conv_w)


def _shifted_back(dst, src, lo, hi, cs):
    for n in range(8):
        dst[n, lo:hi, :] = src[lo - n:hi - n, cs]


def _shifted_fwd(dst, src, lo, hi, cs):
    for n in range(8):
        dst[n, lo:hi, :] = src[lo + n:hi + n, cs]


def _mixer_b_conv_fwd(proj, conv_w, conv_b, wa, rows, cols, chunk):
    s = proj.shape[0]
    wb = conv_w.shape[1]
    ncb = wb // cols
    sec0 = 4 * wa // cols

    def body(bv_ref, bg_ref, w_ref, b_ref, u0_ref, u_ref, ubuf, sh):
        t = pl.program_id(1)

        @pl.when(t == 0)
        def _():
            ubuf[0:HALO_B, :] = jnp.zeros((HALO_B, cols), F32)

        u0 = bv_ref[...].astype(F32) * _sigmoid(bg_ref[...].astype(F32))
        u0_ref[...] = u0
        ubuf[HALO_B:HALO_B + rows, :] = u0
        for lc in range(cols // LANES):
            cs = slice(lc * LANES, (lc + 1) * LANES)
            _shifted_back(sh, ubuf, 8, HALO_B + rows, cs)
            taps = [w_ref[k:k + 1, cs] for k in range(TAPS_B)]
            bias = b_ref[:, cs]

            def row_chunk(rc, carry, cs=cs, taps=taps, bias=bias):
                base = pl.multiple_of(rc * chunk, chunk)
                acc = jnp.zeros((chunk, LANES), F32)
                for k in range(TAPS_B):
                    mq, n = divmod(TAPS_B - 1 - k, 8)
                    acc = acc + taps[k] * sh[n, pl.ds(HALO_B - 8 * mq + base, chunk), :]
                u_ref[pl.ds(base, chunk), cs] = acc + bias
                return carry

            lax.fori_loop(0, rows // chunk, row_chunk, 0)
        ubuf[0:HALO_B, :] = ubuf[rows:rows + HALO_B, :]

    return pl.pallas_call(
        body, name="mixer_b_conv_fwd", grid=(ncb, s // rows),
        in_specs=[pl.BlockSpec((rows, cols), lambda cb, t: (t, sec0 + cb)),
                  pl.BlockSpec((rows, cols), lambda cb, t: (t, sec0 + ncb + cb)),
                  pl.BlockSpec((HALO_B, cols), lambda cb, t: (0, cb)),
                  pl.BlockSpec((1, cols), lambda cb, t: (0, cb))],
        out_specs=[pl.BlockSpec((rows, cols), lambda cb, t: (t, cb))] * 2,
        out_shape=[jax.ShapeDtypeStruct((s, wb), F32)] * 2,
        scratch_shapes=[pltpu.VMEM((HALO_B + rows, cols), F32), pltpu.VMEM((8, HALO_B + rows, LANES), F32)],
        compiler_params=_params(("parallel", "arbitrary")),
    )(proj, proj, conv_w, conv_b)


def _layernorm_stats(u):
    mu = jnp.mean(u, axis=-1, keepdims=True)
    xc = u - mu
    var = jnp.mean(xc * xc, axis=-1, keepdims=True)
    return xc * lax.rsqrt(var + EPS), lax.rsqrt(var + EPS)


def _mixer_b_gate_fwd(y, u, proj, ln_g, ln_b, wa, rows):
    s, wb = u.shape
    sec_z = (4 * wa + 2 * wb) // wb

    def body(y_in, u_ref, bz_ref, g_ref, b_ref, y_ref):
        uh, _ = _layernorm_stats(u_ref[...])
        ln = uh * g_ref[...] + b_ref[...]
        zv = bz_ref[...].astype(F32)
        y_ref[...] = ((ln * _sigmoid(ln)) * (zv * _sigmoid(zv))).astype(BF16)

    vec = pl.BlockSpec((1, wb), lambda i: (0, 0))
    return pl.pallas_call(
        body, name="mixer_b_gate_fwd", grid=(s // rows,),
        in_specs=[ANY, pl.BlockSpec((rows, wb), lambda i: (i, 0)), pl.BlockSpec((rows, wb), lambda i: (i, sec_z)),
                  vec, vec],
        out_specs=pl.BlockSpec((rows, wb), lambda i: (i, wa // wb)),
        out_shape=jax.ShapeDtypeStruct(y.shape, BF16), input_output_aliases={0: 0},
        compiler_params=_params(("parallel",)),
    )(y, u, proj, ln_g, ln_b)


def _loss_head(x, o, target, gate, final_g, rows):
    s, d = x.shape

    def body(x_ref, o_ref, t_ref, gate_ref, fg_ref, dx2_ref, do_ref, loss_ref, gfg_ref, dgate_ref):
        i = pl.program_id(0)
        ov = o_ref[...]
        x2 = x_ref[...] + gate_ref[...] * ov
        r2 = lax.rsqrt(jnp.mean(x2 * x2, axis=-1, keepdims=True) + EPS)
        xn2 = x2 * r2
        diff = xn2 * fg_ref[...] - t_ref[...]
        dout = diff * (1.0 / d)
        dxn2 = dout * fg_ref[...]
        dx2 = r2 * (dxn2 - xn2 * jnp.mean(dxn2 * xn2, axis=-1, keepdims=True))
        dx2_ref[...] = dx2
        do_ref[...] = (gate_ref[...] * dx2).astype(BF16)
        loss_part = 0.5 * jnp.sum(jnp.mean(diff * diff, axis=-1, keepdims=True), axis=0, keepdims=True)
        gfg_part = jnp.sum(dout * xn2, axis=0, keepdims=True)
        dgate_part = jnp.sum(dx2 * ov, axis=0, keepdims=True)

        @pl.when(i == 0)
        def _():
            loss_ref[...] = jnp.zeros_like(loss_ref)
            gfg_ref[...] = jnp.zeros_like(gfg_ref)
            dgate_ref[...] = jnp.zeros_like(dgate_ref)

        loss_ref[...] += jnp.broadcast_to(loss_part, loss_ref.shape)
        gfg_ref[...] += gfg_part
        dgate_ref[...] += dgate_part

    blk = pl.BlockSpec((rows, d), lambda i: (i, 0))
    vec = pl.BlockSpec((1, d), lambda i: (0, 0))
    return pl.pallas_call(
        body, name="loss_head", grid=(s // rows,),
        in_specs=[blk, blk, blk, vec, vec],
        out_specs=[blk, blk, pl.BlockSpec((1, 128), lambda i: (0, 0)), vec, vec],
        out_shape=[jax.ShapeDtypeStruct((s, d), F32), jax.ShapeDtypeStruct((s, d), BF16),
                   jax.ShapeDtypeStruct((1, 128), F32), jax.ShapeDtypeStruct((1, d), F32),
                   jax.ShapeDtypeStruct((1, d), F32)],
        compiler_params=_params(("arbitrary",)),
    )(x, o, target, gate, final_g)


def _mixer_a_bwd(proj, dy, conv_w, wa, din, rows):
    s = proj.shape[0]
    nt = s // rows
    per_halo = rows // HALO_IN

    def body(ab_ref, ac_ref, ax_ref, az_ref, hc_ref, hx_ref, dy_ref, w_ref, dp_ref, dw_ref, qbuf, dbuf):
        i = pl.program_id(0)

        @pl.when(i == 0)
        def _():
            dbuf[rows:rows + HALO_A, :] = jnp.zeros((HALO_A, wa), F32)
            dw_ref[...] = jnp.zeros_like(dw_ref)

        keep = jnp.where(i == nt - 1, 0.0, 1.0)
        before = hc_ref[...].astype(F32) * hx_ref[...].astype(F32) * keep
        qbuf[0:HALO_A, :] = before[HALO_IN - HALO_A:HALO_IN, :]
        acv, axv = ac_ref[...].astype(F32), ax_ref[...].astype(F32)
        q = acv * axv
        qbuf[HALO_A:HALO_A + rows, :] = q
        conv = w_ref[2:3, :] * q
        for k in range(TAPS_A - 1):
            off = HALO_A - (TAPS_A - 1) + k
            conv = conv + w_ref[k:k + 1, :] * qbuf[off:off + rows, :]
        zv, abv, dyv = az_ref[...].astype(F32), ab_ref[...].astype(F32), dy_ref[...]
        sg = _sigmoid(zv)
        sz = zv * sg
        dp_ref[:, 0:wa] = (dyv * conv * sz).astype(BF16)
        dp_ref[:, 3 * wa:4 * wa] = (dyv * abv * conv * (sg * (1.0 + zv * (1.0 - sg)))).astype(BF16)
        dconv = dyv * abv * sz
        dbuf[0:rows, :] = dconv
        dq = w_ref[2:3, :] * dconv
        for k in range(TAPS_A - 1):
            off = TAPS_A - 1 - k
            dq = dq + w_ref[k:k + 1, :] * dbuf[off:off + rows, :]
        dp_ref[:, wa:2 * wa] = (dq * axv).astype(BF16)
        dp_ref[:, 2 * wa:3 * wa] = (dq * acv).astype(BF16)
        for k in range(TAPS_A):
            off = HALO_A - (TAPS_A - 1) + k
            dw_ref[k:k + 1, :] += jnp.sum(dconv * qbuf[off:off + rows, :], axis=0, keepdims=True)
        dbuf[rows:rows + HALO_A, :] = dbuf[0:HALO_A, :]

    def sec(k):
        return pl.BlockSpec((rows, wa), lambda i, k=k: (nt - 1 - i, k))

    def halo(k):
        return pl.BlockSpec((HALO_IN, wa), lambda i, k=k: (jnp.maximum((nt - 1 - i) * per_halo - 1, 0), k))

    return pl.pallas_call(
        body, name="mixer_a_bwd", grid=(nt,),
        in_specs=[sec(0), sec(1), sec(2), sec(3), halo(1), halo(2),
                  pl.BlockSpec((rows, wa), lambda i: (nt - 1 - i, 0)),
                  pl.BlockSpec((HALO_A, wa), lambda i: (0, 0))],
        out_specs=[pl.BlockSpec((rows, 4 * wa), lambda i: (nt - 1 - i, 0)),
                   pl.BlockSpec((HALO_A, wa), lambda i: (0, 0))],
        out_shape=[jax.ShapeDtypeStruct((s, din), BF16), jax.ShapeDtypeStruct((HALO_A, wa), F32)],
        scratch_shapes=[pltpu.VMEM((HALO_A + rows, wa), F32), pltpu.VMEM((rows + HALO_A, wa), F32)],
        compiler_params=_params(("arbitrary",)),
    )(proj, proj, proj, proj, proj, proj, dy, conv_w)


def _mixer_b_gate_bwd(dproj, dy, u, proj, ln_g, ln_b, wa, rows):
    s, wb = u.shape
    sec_z = (4 * wa + 2 * wb) // wb

    def body(dp_in, dy_ref, u_ref, bz_ref, g_ref, b_ref, dp_ref, du_ref, dg_ref, db_ref, dcb_ref):
        i = pl.program_id(0)
        uh, rs = _layernorm_stats(u_ref[...])
        ln = uh * g_ref[...] + b_ref[...]
        sl = _sigmoid(ln)
        zv = bz_ref[...].astype(F32)
        sg = _sigmoid(zv)
        dyv = dy_ref[...]
        dp_ref[...] = (dyv * (ln * sl) * (sg * (1.0 + zv * (1.0 - sg)))).astype(BF16)
        dln = dyv * (zv * sg) * (sl * (1.0 + ln * (1.0 - sl)))
        duh = dln * g_ref[...]
        du = rs * (duh - jnp.mean(duh, axis=-1, keepdims=True) - uh * jnp.mean(duh * uh, axis=-1, keepdims=True))
        du_ref[...] = du

        @pl.when(i == 0)
        def _():
            dg_ref[...] = jnp.zeros_like(dg_ref)
            db_ref[...] = jnp.zeros_like(db_ref)
            dcb_ref[...] = jnp.zeros_like(dcb_ref)

        dg_ref[...] += jnp.sum(dln * uh, axis=0, keepdims=True)
        db_ref[...] += jnp.sum(dln, axis=0, keepdims=True)
        dcb_ref[...] += jnp.sum(du, axis=0, keepdims=True)

    blk = pl.BlockSpec((rows, wb), lambda i: (i, 0))
    vec = pl.BlockSpec((1, wb), lambda i: (0, 0))
    vshape = jax.ShapeDtypeStruct((1, wb), F32)
    return pl.pallas_call(
        body, name="mixer_b_gate_bwd", grid=(s // rows,),
        in_specs=[ANY, pl.BlockSpec((rows, wb), lambda i: (i, wa // wb)), blk,
                  pl.BlockSpec((rows, wb), lambda i: (i, sec_z)), vec, vec],
        out_specs=[pl.BlockSpec((rows, wb), lambda i: (i, sec_z)), blk, vec, vec, vec],
        out_shape=[jax.ShapeDtypeStruct(dproj.shape, BF16), jax.ShapeDtypeStruct((s, wb), F32), vshape, vshape, vshape],
        input_output_aliases={0: 0},
        compiler_params=_params(("arbitrary",)),
    )(dproj, dy, u, proj, ln_g, ln_b)


def _mixer_b_conv_bwd(dproj, du, u0, proj, conv_w, wa, rows, chunk):
    s, wb = du.shape
    nt = s // rows
    per32 = rows // HALO_B
    sec_v = 4 * wa // wb
    nrc = rows // chunk

    def body(dp_in, du_ref, u0_ref, h0_ref, bv_ref, bg_ref, w_ref, dp_ref, dw_ref, ubuf, dbuf, sh, shf, dwacc):
        i = pl.program_id(0)

        @pl.when(i == 0)
        def _():
            dbuf[rows:rows + HALO_B, :] = jnp.zeros((HALO_B, wb), F32)
            dwacc[...] = jnp.zeros_like(dwacc)

        ubuf[0:HALO_B, :] = h0_ref[...] * jnp.where(i == nt - 1, 0.0, 1.0)
        ubuf[HALO_B:HALO_B + rows, :] = u0_ref[...]
        dbuf[0:rows, :] = du_ref[...]
        for lc in range(wb // LANES):
            cs = slice(lc * LANES, (lc + 1) * LANES)
            _shifted_back(sh, ubuf, 8, HALO_B + rows, cs)
            _shifted_fwd(shf, dbuf, 0, rows + HALO_B - 8, cs)
            taps = [w_ref[k:k + 1, cs] for k in range(TAPS_B)]

            def conv_rows(rc, c0, cs=cs, taps=taps, lc=lc):
                base = pl.multiple_of(rc * chunk, chunk)
                acc = jnp.zeros((chunk, LANES), F32)
                for k in range(TAPS_B):
                    mq, n = divmod(TAPS_B - 1 - k, 8)
                    acc = acc + taps[k] * shf[n, pl.ds(base + 8 * mq, chunk), :]
                sg = _sigmoid(bg_ref[pl.ds(base, chunk), cs].astype(F32))
                bv = bv_ref[pl.ds(base, chunk), cs].astype(F32)
                dp_ref[pl.ds(base, chunk), cs] = (acc * sg).astype(BF16)
                dp_ref[pl.ds(base, chunk), wb + lc * LANES:wb + (lc + 1) * LANES] = (
                    acc * bv * sg * (1.0 - sg)).astype(BF16)
                return c0

            lax.fori_loop(0, nrc, conv_rows, 0)

            def dw_rows(rc, accs, cs=cs):
                base = pl.multiple_of(rc * chunk, chunk)
                du_c = dbuf[pl.ds(base, chunk), cs]
                out = []
                for k in range(TAPS_B):
                    mq, n = divmod(TAPS_B - 1 - k, 8)
                    prod = du_c * sh[n, pl.ds(HALO_B - 8 * mq + base, chunk), :]
                    out.append(accs[k] + jnp.sum(prod.reshape(chunk // SUBLANES, SUBLANES, LANES), axis=0))
                return tuple(out)

            accs = lax.fori_loop(0, nrc, dw_rows, tuple(jnp.zeros((SUBLANES, LANES), F32) for _ in range(TAPS_B)))
            for k in range(TAPS_B):
                dwacc[k * SUBLANES:(k + 1) * SUBLANES, cs] += accs[k]
        dbuf[rows:rows + HALO_B, :] = dbuf[0:HALO_B, :]

        @pl.when(i == nt - 1)
        def _():
            for k in range(HALO_B):
                dw_ref[k:k + 1, :] = jnp.sum(dwacc[k * SUBLANES:(k + 1) * SUBLANES, :], axis=0, keepdims=True)

    def rev(cols_blk):
        return pl.BlockSpec((rows, wb), lambda i, cb=cols_blk: (nt - 1 - i, cb))

    return pl.pallas_call(
        body, name="mixer_b_conv_bwd", grid=(nt,),
        in_specs=[ANY, rev(0), rev(0),
                  pl.BlockSpec((HALO_B, wb), lambda i: (jnp.maximum((nt - 1 - i) * per32 - 1, 0), 0)),
                  rev(sec_v), rev(sec_v + 1), pl.BlockSpec((HALO_B, wb), lambda i: (0, 0))],
        out_specs=[pl.BlockSpec((rows, 2 * wb), lambda i: (nt - 1 - i, sec_v // 2)),
                   pl.BlockSpec((HALO_B, wb), lambda i: (0, 0))],
        out_shape=[jax.ShapeDtypeStruct(dproj.shape, BF16), jax.ShapeDtypeStruct((HALO_B, wb), F32)],
        input_output_aliases={0: 0},
        scratch_shapes=[pltpu.VMEM((HALO_B + rows, wb), F32), pltpu.VMEM((rows + HALO_B, wb), F32),
                        pltpu.VMEM((8, HALO_B + rows, LANES), F32), pltpu.VMEM((8, rows + HALO_B, LANES), F32),
                        pltpu.VMEM((HALO_B * SUBLANES, wb), F32)],
        compiler_params=_params(("arbitrary",)),
    )(dproj, du, u0, u0, proj, proj, conv_w)


def _prenorm_bwd(x, r, dh, dx2, norm_g, scale, rows):
    s, d = x.shape

    def body(x_ref, r_ref, dh_ref, dx2_ref, g_ref, sc_ref, gx_ref, dsh_ref, dsc_ref, dg_ref):
        i = pl.program_id(0)
        rv = r_ref[...]
        xn = x_ref[...] * rv
        dhv = dh_ref[...]
        one_sc = 1.0 + sc_ref[...]
        dxn = dhv * one_sc * g_ref[...]
        gx_ref[...] = dx2_ref[...] + rv * (dxn - xn * jnp.mean(dxn * xn, axis=-1, keepdims=True))

        @pl.when(i == 0)
        def _():
            dsh_ref[...] = jnp.zeros_like(dsh_ref)
            dsc_ref[...] = jnp.zeros_like(dsc_ref)
            dg_ref[...] = jnp.zeros_like(dg_ref)

        dsh_ref[...] += jnp.sum(dhv, axis=0, keepdims=True)
        dsc_ref[...] += jnp.sum(dhv * (xn * g_ref[...]), axis=0, keepdims=True)
        dg_ref[...] += jnp.sum(dhv * one_sc * xn, axis=0, keepdims=True)

    blk = pl.BlockSpec((rows, d), lambda i: (i, 0))
    vec = pl.BlockSpec((1, d), lambda i: (0, 0))
    vshape = jax.ShapeDtypeStruct((1, d), F32)
    return pl.pallas_call(
        body, name="prenorm_bwd", grid=(s // rows,),
        in_specs=[blk, pl.BlockSpec((rows, 1), lambda i: (i, 0)), blk, blk, vec, vec],
        out_specs=[blk, vec, vec, vec],
        out_shape=[jax.ShapeDtypeStruct((s, d), F32), vshape, vshape, vshape],
        compiler_params=_params(("arbitrary",)),
    )(x, r, dh, dx2, norm_g, scale)


def _pad_rows(a, rows):
    return jnp.pad(a, ((0, rows - a.shape[0]), (0, 0)))


def _tile(n, want):
    t = min(n, want)
    while n % t:
        t -= 1
    return t


def kernel(x, c, norm_g, w_ada, b_ada, w_in, conv_a_w, conv_b_w, conv_b_b, ln_b_g, ln_b_b, w_out, final_g, loss_target, m_norm_g, m_w_ada, m_b_ada, m_w_in, m_conv_a_w, m_conv_b_w, m_conv_b_b, m_ln_b_g, m_ln_b_b, m_w_out, m_final_g, v_norm_g, v_w_ada, v_b_ada, v_w_in, v_conv_a_w, v_conv_b_w, v_conv_b_b, v_ln_b_g, v_ln_b_b, v_w_out, v_final_g):
    s, d = x.shape[1], x.shape[2]
    wa = conv_b_b.shape[-1]
    dmix = 2 * wa
    ns = w_in.shape[-1]
    din = N_CHIPS * ns
    r4 = w_out.shape[1]
    na = w_ada.shape[-1]
    wsh = conv_a_w.shape[-1]
    px, py, pc = _position()
    chip = 2 * px + py
    me = 4 * px + 2 * py + pc
    pos = jnp.stack([chip, pc]).astype(jnp.int32)
    x2d = x.reshape(s, d)
    target = loss_target.reshape(s, d)

    hc, ho, hrow = ns // 2, r4 // 2, d // 2
    _, cidx = _other_chips(px, py)
    hq = hc // 2
    win4 = _cast_quarters(w_in[0], _tile(d, 512), "cast_w_in")
    wout_bf = _cast_bf16(w_out[0], _tile(r4, 512), "cast_w_out")

    def gather_plan(b):
        x, y, cc = _position()
        chips, _ = _other_chips(x, y)
        xn, yn = (1 - x, y, cc), (x, 1 - y, cc)
        q0, q1 = b[0].at[2 * cc], b[0].at[2 * cc + 1]
        return ([(q0, b[2], xn), (q1, b[5], yn), (q1, b[3], xn), (q0, b[4], yn)]
                + [(b[1].at[pl.ds(cc * ho, ho), :], b[6 + k], (cx, cy, cc)) for k, (cx, cy) in enumerate(chips)])

    def gather_sent(b):
        return [(src, src, dev) for src, _, dev in gather_plan(list(b) + [None] * 7)]

    def onward_plan(b):
        x, y, cc = _position()
        sib = (x, y, 1 - cc)
        return [(b[0], b[2], (x, 1 - y, cc)), (b[1], b[3], (1 - x, y, cc)), (b[0], b[4], sib), (b[1], b[5], sib)]

    pairs = lambda n: _to_sibling([lambda ref, cc: ref] * n)

    c8 = jnp.broadcast_to(c, (8, d))
    cw = jnp.concatenate([_pad_rows(conv_a_w[0], HALO_A), _pad_rows(conv_b_w[0], HALO_B)], axis=0)
    c_all, cw_all = _gather_cond(c8, cw)
    c_rows = c_all[:, 0, :]
    cw_full = jnp.transpose(cw_all, (1, 0, 2)).reshape(HALO_A + HALO_B, wa)
    conv_a_full, conv_b_full = cw_full[:HALO_A], cw_full[HALO_A:]

    b_ada_sh = lax.dynamic_slice(b_ada, (0, chip * na), (1, na))
    mod_part = _modulation(_pad_rows(c_rows, 2 * N_DEV), w_ada[0], b_ada_sh, _tile(na, 512), "modulation")[:N_DEV]
    mod_all = _exchange_mod(mod_part)
    mod = lax.dynamic_index_in_dim(mod_all, me, axis=1, keepdims=False).reshape(1, 3 * d)
    shift, scale, gate = mod[:, :d], mod[:, d:2 * d], mod[:, 2 * d:]

    def quarter():
        return lax.empty((d, hq), BF16)

    g_sems, g_bufs, g_tok = _start_copies(
        "gather_start", gather_plan, 7,
        [win4, wout_bf] + [quarter() for _ in range(4)] + [lax.empty((ho, d), BF16) for _ in range(3)],
        after=[mod_all])
    win4, wout_bf, (x0, x1, y0, y1), lo = g_bufs[0], g_bufs[1], g_bufs[2:6], g_bufs[6:9]

    h, r = _prenorm(x2d, norm_g, scale, shift + g_tok[0, 0], _tile(s, 256))
    bm = _tile(s, 1024)
    pieces = [None, None]

    def piece(slot, half, quarters, name, own_half=None):
        where = jnp.reshape(2 * slot + half, (1,)).astype(jnp.int32)
        pieces[:] = _proj_piece(where, h, quarters, pieces[0], pieces[1], din, 2 * N_CHIPS, _tile(s, 512), name,
                                own_half=own_half)
        return pieces[0]

    proj = piece(chip, 0, (win4, win4), "proj_own0", own_half=0)
    proj = piece(chip, 1, (win4, win4), "proj_own1", own_half=1)
    x0, y1 = _wait_copies("gather_wait_a", _landed, [x0, y1], g_sems[0:4], after=[proj], send=False)
    on_sems, (x0, y1, dg0, dg1, sx0, sy1), _ = _start_copies(
        "pass_on_a", onward_plan, 4, [x0, y1] + [quarter() for _ in range(4)])
    x1, y0 = _wait_copies("gather_wait_b", _landed, [x1, y0], g_sems[4:8], after=[x0], send=False)
    pb_sems, (x1, sx1, y0, sy0), _ = _start_copies("pass_on_b", pairs(2), 2, [x1, quarter(), y0, quarter()])
    proj = piece(cidx[0], pc, (x0, x1), "proj_xa")
    proj = piece(cidx[1], pc, (y0, y1), "proj_ya")
    sx0, sy1 = _wait_copies("pass_wait_a", _landed, [sx0, sy1], on_sems[4:8], after=[proj], send=False)
    x1, sx1, y0, sy0 = _wait_copies("pass_wait_b", pairs(2), [x1, sx1, y0, sy0], pb_sems, after=[proj])
    proj = piece(cidx[0], 1 - pc, (sx0, sx1), "proj_xb")
    proj = piece(cidx[1], 1 - pc, (sy0, sy1), "proj_yb")
    x0, y1, dg0, dg1 = _wait_copies("diag_wait", lambda b: onward_plan(list(b) + [None, None])[:2],
                                    [x0, y1, dg0, dg1], on_sems[0:4], after=[proj])
    x0, y1 = _wait_copies("pass_sent_a", lambda b: [(b[0], b[0], (0, 0, 0)), (b[1], b[1], (0, 0, 0))],
                          [x0, y1], on_sems[4:8], after=[dg0], recv=False)
    pd_sems, (dg0, sd0, dg1, sd1), _ = _start_copies("pass_on_d", pairs(2), 2, [dg0, quarter(), dg1, quarter()],
                                                     after=[x0])
    proj = piece(cidx[2], pc, (dg0, dg1), "proj_da")
    dg0, sd0, dg1, sd1 = _wait_copies("pass_wait_d", pairs(2), [dg0, sd0, dg1, sd1], pd_sems, after=[proj])
    proj = piece(cidx[2], 1 - pc, (sd0, sd1), "proj_db")
    win_full = pieces[1]

    lo = _wait_copies("gather_wait_out", _landed, lo, g_sems[8:14], after=[proj], send=False)
    o_sems, o_bufs, o_tok = _start_copies(
        "pass_on_out", pairs(3), 3, [b for k in range(3) for b in (lo[k], lax.empty((ho, d), BF16))])
    win4, wout_bf = _wait_copies("gather_wait_sent", gather_sent, [win4, wout_bf], g_sems, after=[o_tok], recv=False)

    def slot_index(k, chip_, cc, others):
        if k == 0:
            return pl.ds(2 * chip_, 2)
        return 2 * others[(k - 1) % 3] + (cc if k <= 3 else 1 - cc)

    y = _mixer_a_fwd(proj, conv_a_full, wa, _tile(s, 512), _tile(wa, 512))
    u0, u = _mixer_b_conv_fwd(proj, conv_b_full, conv_b_b, wa, _tile(s, 512), _tile(wa, 256), 64)
    y = _mixer_b_gate_fwd(y, u, proj, ln_b_g, ln_b_b, wa, _tile(s, 256))
    o_bufs = _wait_copies("pass_wait_out", pairs(3), o_bufs, o_sems, after=[y])
    wout_full = _assemble("assemble_w_out", [wout_bf.reshape(2, ho, d)] + o_bufs[0::2] + o_bufs[1::2],
                          jax.ShapeDtypeStruct((2 * N_CHIPS, ho, d), BF16), slot_index)
    wout2d = wout_full.reshape(dmix, d)
    bd = _tile(d, 1024)
    o = _matmul(
        y, wout2d, grid=(s // bm, d // bd, 1),
        a_spec=pl.BlockSpec((bm, dmix), lambda i, j, k: (i, 0)),
        b_spec=pl.BlockSpec((dmix, bd), lambda i, j, k: (0, j)),
        o_spec=pl.BlockSpec((bm, bd), lambda i, j, k: (i, j)),
        out_shape=jax.ShapeDtypeStruct((s, d), F32), dims=((1,), (0,)), name="out_proj")
    dx2, do, loss_p, gfg_p, dgate_p = _loss_head(x2d, o, target, gate, final_g.reshape(1, d), _tile(s, 128))

    be = _tile(dmix, 1024)
    g_wout = _matmul(
        y, do, grid=(dmix // be, d // bd, 1),
        a_spec=pl.BlockSpec((s, be), lambda i, j, k: (0, i)),
        b_spec=pl.BlockSpec((s, bd), lambda i, j, k: (0, j)),
        o_spec=pl.BlockSpec((be, bd), lambda i, j, k: (i, j)),
        out_shape=jax.ShapeDtypeStruct((dmix, d), F32), dims=((0,), (0,)), name="grad_w_out")
    swap_out = _to_sibling([lambda ref, cc: ref.at[:, pl.ds((1 - cc) * ho, ho), :]])
    so_sems, (g_wout3, ra_out), so_tok = _start_copies(
        "swap_out_start", swap_out, 1, [g_wout.reshape(N_CHIPS, r4, d), lax.empty((N_CHIPS, ho, d), F32)])
    dy = _matmul(
        do, wout2d, grid=(s // bm, dmix // be, 1),
        a_spec=pl.BlockSpec((bm, d), lambda i, j, k: (i, 0)),
        b_spec=pl.BlockSpec((be, d), lambda i, j, k: (j, 0)),
        o_spec=pl.BlockSpec((bm, be), lambda i, j, k: (i, j)),
        out_shape=jax.ShapeDtypeStruct((s, dmix), F32), dims=((1,), (1,)), name="dy", after=[so_tok])
    g_wout3, ra_out = _wait_copies("swap_out_wait", swap_out, [g_wout3, ra_out], so_sems, after=[dy])
    q_out = _chip_partial(pos, g_wout3, ra_out, _tile(ho, 256), "chip_partial_w_out")
    po_sems, po_bufs, po_tok = _start_copies(
        "send_out_start", _slots_to_chips, 3, [q_out] + [lax.empty((ho, d), BF16) for _ in range(3)])
    dproj, dwa_p = _mixer_a_bwd(proj, dy, conv_a_full + po_tok[0, 0], wa, din, _tile(s, 128))
    dproj, du, dlng_p, dlnb_p, dcb_p = _mixer_b_gate_bwd(dproj, dy, u, proj, ln_b_g, ln_b_b, wa, _tile(s, 128))
    dproj, dwb_p = _mixer_b_conv_bwd(dproj, du, u0, proj, conv_b_full, wa, _tile(s, 256), 64)

    slots = [cidx[0], cidx[1], cidx[2], chip]
    q, rb, snd = [None] * 3, [None] * 3, [None] * 3
    after = []
    for pair in ((0, 1), (2, 3)):
        given = {}
        for k in pair:
            theirs = _grad_slot(jnp.stack([slots[k], 1 - pc]).astype(jnp.int32), h, dproj, after, ns,
                                _tile(hrow, 512), hc, f"grad_w_in{k}a")
            sems, bufs, tok = _start_copies(f"swap_in_start{k}", pairs(1), 1, [theirs, lax.empty((hrow, ns), F32)])
            given[k] = (sems, bufs)
            after = [tok]
        for k in pair:
            sems, bufs = given[k]
            _, from_sibling = _wait_copies(f"swap_in_wait{k}", pairs(1), bufs, sems, after=after)
            mine = _grad_slot(jnp.stack([slots[k], pc]).astype(jnp.int32), h, dproj, [], ns, _tile(hrow, 512), hc,
                              f"grad_w_in{k}b", add=from_sibling, out_dtype=BF16 if k < 3 else F32)
            if k < 3:
                snd[k], (q[k], rb[k]), tok = _start_copies(f"send_in_start{k}", _to_chip(k), 1,
                                                           [mine, lax.empty((hrow, ns), BF16)])
                after = [tok]
            else:
                own_half, after = mine, [mine]
    dh = _matmul(
        dproj, win_full, grid=(s // bm, d // bd, 2 * N_CHIPS),
        a_spec=pl.BlockSpec((bm, hc), lambda i, j, k: (i, k)),
        b_spec=pl.BlockSpec((None, bd, hc), lambda i, j, k: (k, j, 0)),
        o_spec=pl.BlockSpec((bm, bd), lambda i, j, k: (i, j)),
        out_shape=jax.ShapeDtypeStruct((s, d), F32), dims=((1,), (1,)), name="dh", after=after)
    grad_x, dshift_p, dscale_p, gng_p = _prenorm_bwd(x2d, r, dh, dx2, norm_g, scale, _tile(s, 128))

    def rows_of(v):
        return _pad_rows(v.reshape(-1, wa), 8 * ((v.size // wa + 7) // 8))

    dmod = jnp.concatenate([dshift_p, dscale_p, dgate_p], axis=1)
    parts = [gng_p, dmod, dwa_p, dwb_p, dcb_p, dlng_p, dlnb_p, gfg_p,
             jnp.broadcast_to(loss_p[:, :1], (1, wa))]
    starts, packed = [], []
    for p in parts:
        starts.append(sum(q.shape[0] for q in packed))
        packed.append(rows_of(p) if p.shape[0] == 1 else p)
    small_sum, small_all = _gather_small(jnp.concatenate(packed, axis=0))

    def summed(k, rows):
        return small_sum[starts[k]:starts[k] + rows]

    grad_norm_g = summed(0, d // wa).reshape(1, d)
    grad_b_ada = summed(1, 3 * d // wa).reshape(1, 3 * d)
    grad_conv_a_full = summed(2, TAPS_A)
    grad_conv_b_full = summed(3, TAPS_B)
    grad_conv_b_b = summed(4, 1)
    grad_ln_b_g = summed(5, 1)
    grad_ln_b_b = summed(6, 1)
    grad_final_g = summed(7, d // wa).reshape(d)
    loss = summed(8, 1)[0, 0]
    grad_conv_a_w = lax.dynamic_slice(grad_conv_a_full, (0, chip * wsh), (TAPS_A, wsh))
    grad_conv_b_w = lax.dynamic_slice(grad_conv_b_full, (0, chip * wsh), (TAPS_B, wsh))
    dmod_all = small_all[:, starts[1]:starts[1] + 3 * d // wa, :].reshape(N_DEV, 3 * d)
    dmod_sh = lax.dynamic_slice(dmod_all, (0, chip * na), (N_DEV, na))

    def pairs_to_chips(b):
        x, y, cc = _position()
        chips, _ = _other_chips(x, y)
        return [(b[2 * k], b[2 * k + 1], (cx, cy, cc)) for k, (cx, cy) in enumerate(chips)]

    po_bufs = _wait_copies("send_out_wait", _slots_to_chips, po_bufs, po_sems, after=[small_sum])
    gh_out = _final_half(pos, g_wout3, ra_out, po_bufs[1:], _tile(ho, 256), "final_half_w_out")
    in_bufs = _wait_copies("send_in_wait", pairs_to_chips, [b for k in range(3) for b in (q[k], rb[k])],
                           snd[0] + snd[1] + snd[2], after=[small_sum])
    gh_in = _final_sum(pos, own_half, in_bufs[1::2], _tile(hrow, 256), "final_half_w_in")
    sh_sems, sh_bufs, sh_tok = _start_copies("share_start", _halves_to_sibling, 2, [gh_in, gh_out])

    grad_w_ada, d_wada, nm_wada, nv_wada = _adam_ada(c_rows.T, dmod_sh + sh_tok[0, 0], w_ada[0], m_w_ada[0],
                                                     v_w_ada[0], _tile(d, 128), "adam_w_ada")
    gw_in, gw_out = _wait_copies("share_wait", _halves_to_sibling, sh_bufs, sh_sems, after=[d_wada])
    grad_w_in, d_win, nm_win, nv_win = _adam(w_in[0], gw_in, m_w_in[0], v_w_in[0], _tile(d, 128), "adam_w_in",
                                             return_grad=True)
    grad_w_out, d_wout, nm_wout, nv_wout = _adam(w_out[0], gw_out, m_w_out[0], v_w_out[0], _tile(r4, 128),
                                                 "adam_w_out", return_grad=True)

    def small_adam(w, g, m, v, name):
        shape = w.shape
        w2 = w.reshape(-1, shape[-1])
        out = _adam(w2, g.reshape(w2.shape), m.reshape(w2.shape), v.reshape(w2.shape), w2.shape[0], name)
        return [o_.reshape(shape) for o_ in out]

    small = {
        "norm_g": small_adam(norm_g, grad_norm_g, m_norm_g, v_norm_g, "adam_norm_g"),
        "b_ada": small_adam(b_ada, grad_b_ada, m_b_ada, v_b_ada, "adam_b_ada"),
        "conv_a_w": small_adam(conv_a_w, grad_conv_a_w, m_conv_a_w, v_conv_a_w, "adam_conv_a_w"),
        "conv_b_w": small_adam(conv_b_w, grad_conv_b_w, m_conv_b_w, v_conv_b_w, "adam_conv_b_w"),
        "conv_b_b": small_adam(conv_b_b, grad_conv_b_b, m_conv_b_b, v_conv_b_b, "adam_conv_b_b"),
        "ln_b_g": small_adam(ln_b_g, grad_ln_b_g, m_ln_b_g, v_ln_b_g, "adam_ln_b_g"),
        "ln_b_b": small_adam(ln_b_b, grad_ln_b_b, m_ln_b_b, v_ln_b_b, "adam_ln_b_b"),
        "final_g": small_adam(final_g.reshape(1, d), grad_final_g, m_final_g.reshape(1, d),
                              v_final_g.reshape(1, d), "adam_final_g"),
    }
    small["final_g"] = [o_.reshape(d) for o_ in small["final_g"]]
    big = {
        "w_ada": [a[None] for a in (d_wada, nm_wada, nv_wada)],
        "w_in": [a[None] for a in (d_win, nm_win, nv_win)],
        "w_out": [a[None] for a in (d_wout, nm_wout, nv_wout)],
    }
    upd = {**small, **big}
    order = ["norm_g", "w_ada", "b_ada", "w_in", "conv_a_w", "conv_b_w", "conv_b_b", "ln_b_g", "ln_b_b",
             "w_out", "final_g"]
    grads = {
        "norm_g": grad_norm_g, "w_ada": grad_w_ada[None], "b_ada": grad_b_ada, "w_in": grad_w_in[None],
        "conv_a_w": grad_conv_a_w[None], "conv_b_w": grad_conv_b_w[None], "conv_b_b": grad_conv_b_b,
        "ln_b_g": grad_ln_b_g, "ln_b_b": grad_ln_b_b, "w_out": grad_w_out[None], "final_g": grad_final_g,
    }
    return (loss, grad_x.reshape(1, s, d), *[grads[n] for n in order], *[upd[n][0] for n in order],
            *[upd[n][1] for n in order], *[upd[n][2] for n in order])
```

```python
import functools

import jax
import jax.numpy as jnp
from jax import lax
from jax.experimental import pallas as pl
from jax.experimental.pallas import tpu as pltpu

F32 = jnp.float32
BF16 = jnp.bfloat16
EPS = 1e-6
N_CHIPS = 4
N_DEV = 8
TAPS_A = 3
TAPS_B = 31
HALO_A = 8
HALO_IN = 16
HALO_B = 32
LANES = 128
SUBLANES = 8
ADAM_LR = 0.001
ADAM_B1 = 0.9
ADAM_B2 = 0.999
ADAM_EPS = 1e-08
ADAM_WD = 0.01
ADAM_STEP = 10
VMEM_LIMIT = 56 * 1024 * 1024
MESH = pl.DeviceIdType.MESH
ANY = pl.BlockSpec(memory_space=pl.ANY)
VMEM = pl.BlockSpec(memory_space=pltpu.VMEM)
HBM_SPEC = pl.BlockSpec(memory_space=pltpu.HBM)
SEM_SPEC = pl.BlockSpec(memory_space=pltpu.SEMAPHORE)
EFFECT = pltpu.SideEffectType.DATAFLOW_SIDE_EFFECTING


def _params(sem=None):
    return pltpu.CompilerParams(dimension_semantics=sem, vmem_limit_bytes=VMEM_LIMIT)


def _sigmoid(v):
    return jax.nn.sigmoid(v)


def _position():
    return lax.axis_index("x"), lax.axis_index("y"), lax.axis_index("c")


def _rcopy(src, dst, ssem, rsem, dev):
    return pltpu.make_async_remote_copy(src_ref=src, dst_ref=dst, send_sem=ssem, recv_sem=rsem,
                                        device_id=dev, device_id_type=MESH)


def _other_chips(x, y):
    chips = [(1 - x, y), (x, 1 - y), (1 - x, 1 - y)]
    return chips, [2 * cx + cy for cx, cy in chips]


def _cast_bf16(a, rows, name):
    m, n = a.shape

    def body(a_ref, o_ref):
        o_ref[...] = a_ref[...].astype(BF16)

    return pl.pallas_call(
        body, name=name, grid=(m // rows,),
        in_specs=[pl.BlockSpec((rows, n), lambda i: (i, 0))],
        out_specs=pl.BlockSpec((rows, n), lambda i: (i, 0)),
        out_shape=jax.ShapeDtypeStruct((m, n), BF16),
        compiler_params=_params(("parallel",)),
    )(a)


def _cast_quarters(a, rows, name):
    m, n = a.shape
    hq = n // 4

    def body(a_ref, o_ref):
        o_ref[...] = a_ref[...].astype(BF16)

    return pl.pallas_call(
        body, name=name, grid=(4, m // rows),
        in_specs=[pl.BlockSpec((rows, hq), lambda q, i: (i, q))],
        out_specs=pl.BlockSpec((None, rows, hq), lambda q, i: (q, i, 0)),
        out_shape=jax.ShapeDtypeStruct((4, m, hq), BF16),
        compiler_params=_params(("parallel", "parallel")),
    )(a)


def _proj_piece(where, h, quarters, proj, w_all, din, n_pieces, bm, name, own_half=None):
    s, d = h.shape
    hq = quarters[0].shape[-1]
    nm = s // bm
    if own_half is None:
        q_specs = [pl.BlockSpec((d, hq), lambda i, p: (0, 0), pipeline_mode=pl.Buffered(1))] * 2
    else:
        q_specs = [pl.BlockSpec((None, d, hq), lambda i, p, k=k: (2 * own_half + k, 0, 0),
                                pipeline_mode=pl.Buffered(1)) for k in range(2)]

    def body(p_ref, h_ref, q0_ref, q1_ref, *rest):
        o_ref, wall_ref, wbuf, sem = rest[-4:]
        i = pl.program_id(0)
        filed = pltpu.make_async_copy(wbuf, wall_ref.at[p_ref[0]], sem)

        @pl.when(i == 0)
        def _():
            wbuf[:, 0:hq] = q0_ref[...]
            wbuf[:, hq:2 * hq] = q1_ref[...]
            filed.start()

        o_ref[...] = jnp.dot(h_ref[...], wbuf[...], preferred_element_type=F32).astype(BF16)

        @pl.when(i == nm - 1)
        def _():
            filed.wait()

    args, extra, alias = [where, h, quarters[0], quarters[1]], [], {}
    if proj is not None:
        args, extra, alias = args + [proj, w_all], [ANY, ANY], {4: 0, 5: 1}
    return pl.pallas_call(
        body, name=name,
        grid_spec=pltpu.PrefetchScalarGridSpec(
            num_scalar_prefetch=1, grid=(nm,),
            in_specs=[pl.BlockSpec((bm, d), lambda i, p: (i, 0))] + q_specs + extra,
            out_specs=[pl.BlockSpec((bm, 2 * hq), lambda i, p: (i, p[0])), ANY],
            scratch_shapes=[pltpu.VMEM((d, 2 * hq), BF16), pltpu.SemaphoreType.DMA]),
        out_shape=[jax.ShapeDtypeStruct((s, din), BF16), jax.ShapeDtypeStruct((n_pieces, d, 2 * hq), BF16)],
        input_output_aliases=alias,
        compiler_params=_params(("arbitrary",)),
    )(*args)


def _grad_slot(where, h, dproj, after, ns, bd, bn, name, add=None, out_dtype=F32):
    s, d = h.shape
    nb = ns // bn
    ni = d // 2 // bd
    extra = [] if add is None else [add]

    def body(where_ref, h_ref, dp_ref, *rest):
        acc = lax.dot_general(h_ref[...], dp_ref[...], (((0,), (0,)), ((), ())), preferred_element_type=F32)
        if add is not None:
            acc = acc + rest[0][...]
        rest[-1][...] = acc.astype(out_dtype)

    return pl.pallas_call(
        body, name=name,
        grid_spec=pltpu.PrefetchScalarGridSpec(
            num_scalar_prefetch=1, grid=(nb, ni),
            in_specs=[pl.BlockSpec((s, bd), lambda j, i, w: (0, w[1] * ni + i)),
                      pl.BlockSpec((s, bn), lambda j, i, w: (0, w[0] * nb + j))]
            + [pl.BlockSpec((bd, bn), lambda j, i, w: (i, j))] * len(extra) + [ANY] * len(after),
            out_specs=pl.BlockSpec((bd, bn), lambda j, i, w: (i, j))),
        out_shape=jax.ShapeDtypeStruct((d // 2, ns), out_dtype),
        compiler_params=_params(("parallel", "parallel")),
    )(where, h, dproj, *extra, *after)


def _matmul(a, b, *, grid, a_spec, b_spec, o_spec, out_shape, dims, name, after=()):
    nk = grid[2]
    n_after = len(after)

    def body(a_ref, b_ref, *rest):
        o_ref, acc = rest[n_after], rest[n_after + 1:]
        p = lax.dot_general(a_ref[...], b_ref[...], (dims, ((), ())), preferred_element_type=F32)
        if nk == 1:
            o_ref[...] = p.astype(o_ref.dtype)
        else:
            acc_ref, = acc
            k = pl.program_id(2)

            @pl.when(k == 0)
            def _():
                acc_ref[...] = p

            @pl.when(k > 0)
            def _():
                acc_ref[...] += p

            @pl.when(k == nk - 1)
            def _():
                o_ref[...] = acc_ref[...].astype(o_ref.dtype)

    block = [d for d in o_spec.block_shape if d is not None]
    scratch = [pltpu.VMEM(tuple(block), F32)] if nk > 1 else []
    return pl.pallas_call(
        body, name=name, grid=grid, in_specs=[a_spec, b_spec] + [ANY] * n_after, out_specs=o_spec,
        out_shape=out_shape, scratch_shapes=scratch,
        compiler_params=_params(("parallel", "parallel", "arbitrary")),
    )(a, b, *after)


def _matmul_by_pieces(a, pieces, bm, bn, name, after=()):
    s = a.shape[0]
    n, rows, width = pieces.shape
    nk = n // 2
    n_after = len(after)

    def body(a_ref, b_ref, *rest):
        o_ref, acc_ref = rest[n_after], rest[n_after + 1]
        k = pl.program_id(2)
        nt = (((1,), (1,)), ((), ()))
        p = (lax.dot_general(a_ref[:, 0:width], b_ref[0], nt, preferred_element_type=F32)
             + lax.dot_general(a_ref[:, width:2 * width], b_ref[1], nt, preferred_element_type=F32))

        @pl.when(k == 0)
        def _():
            acc_ref[...] = p

        @pl.when(k > 0)
        def _():
            acc_ref[...] += p

        @pl.when(k == nk - 1)
        def _():
            o_ref[...] = acc_ref[...]

    return pl.pallas_call(
        body, name=name, grid=(s // bm, rows // bn, nk),
        in_specs=[pl.BlockSpec((bm, 2 * width), lambda i, j, k: (i, k)),
                  pl.BlockSpec((2, bn, width), lambda i, j, k: (k, j, 0))] + [ANY] * n_after,
        out_specs=pl.BlockSpec((bm, bn), lambda i, j, k: (i, j)),
        out_shape=jax.ShapeDtypeStruct((s, rows), F32), scratch_shapes=[pltpu.VMEM((bm, bn), F32)],
        compiler_params=_params(("parallel", "parallel", "arbitrary")),
    )(a, pieces, *after)


def _adam_math(w, g, m, v):
    m = ADAM_B1 * m + (1.0 - ADAM_B1) * g
    v = ADAM_B2 * v + (1.0 - ADAM_B2) * (g * g)
    m_hat = m / (1.0 - ADAM_B1 ** ADAM_STEP)
    v_hat = v / (1.0 - ADAM_B2 ** ADAM_STEP)
    delta = -ADAM_LR * (m_hat / (jnp.sqrt(v_hat) + ADAM_EPS) + ADAM_WD * w)
    return delta, m, v


def _adam(w, g, m, v, rows, name, return_grad=False):
    r, n = w.shape

    def body(w_ref, g_ref, m_ref, v_ref, *out):
        gv = g_ref[...]
        d, mo, vo = _adam_math(w_ref[...], gv, m_ref[...], v_ref[...])
        for o_ref, val in zip(out, ([gv] if return_grad else []) + [d, mo, vo]):
            o_ref[...] = val

    spec = pl.BlockSpec((rows, n), lambda i: (i, 0))
    shape = jax.ShapeDtypeStruct((r, n), F32)
    n_out = 4 if return_grad else 3
    return pl.pallas_call(
        body, name=name, grid=(r // rows,), in_specs=[spec] * 4, out_specs=[spec] * n_out,
        out_shape=[shape] * n_out, compiler_params=_params(("parallel",)),
    )(w, g, m, v)


def _adam_ada(c_cols, dmod, w, m, v, rows, name):
    r, n = w.shape

    def body(c_ref, dm_ref, w_ref, m_ref, v_ref, g_ref, d_ref, mo_ref, vo_ref):
        cv = c_ref[...]
        c_act = cv * _sigmoid(cv)
        g = c_act[:, 0:1] * dm_ref[0:1, :]
        for b in range(1, N_DEV):
            g = g + c_act[:, b:b + 1] * dm_ref[b:b + 1, :]
        d, mo, vo = _adam_math(w_ref[...], g, m_ref[...], v_ref[...])
        g_ref[...] = g
        d_ref[...] = d
        mo_ref[...] = mo
        vo_ref[...] = vo

    spec = pl.BlockSpec((rows, n), lambda i: (i, 0))
    shape = jax.ShapeDtypeStruct((r, n), F32)
    return pl.pallas_call(
        body, name=name, grid=(r // rows,),
        in_specs=[pl.BlockSpec((rows, N_DEV), lambda i: (i, 0)), pl.BlockSpec((N_DEV, n), lambda i: (0, 0)),
                  spec, spec, spec],
        out_specs=[spec] * 4, out_shape=[shape] * 4, compiler_params=_params(("parallel",)),
    )(c_cols, dmod, w, m, v)


def _start_copies(name, plan, n, bufs, after=()):
    nb, na = len(bufs), len(after)

    def body(*refs):
        sems = refs[nb + na:nb + na + 2 * n]
        for k, (src, dst, dev) in enumerate(plan(refs[:nb])):
            _rcopy(src, dst, sems[2 * k], sems[2 * k + 1], dev).start()
        refs[-1][...] = jnp.zeros((8, 128), F32)

    outs = pl.pallas_call(
        body, name=name,
        out_shape=[pltpu.SemaphoreType.DMA(())] * (2 * n) + [pltpu.HBM(a.shape, a.dtype) for a in bufs]
        + [jax.ShapeDtypeStruct((8, 128), F32)],
        in_specs=[HBM_SPEC] * nb + [ANY] * na, out_specs=[SEM_SPEC] * (2 * n) + [HBM_SPEC] * nb + [VMEM],
        input_output_aliases={i: 2 * n + i for i in range(nb)},
        compiler_params=pltpu.CompilerParams(has_side_effects=EFFECT),
    )(*[pltpu.with_memory_space_constraint(a, pltpu.HBM) for a in bufs], *after)
    return list(outs[:2 * n]), list(outs[2 * n:2 * n + nb]), outs[-1]


def _wait_copies(name, plan, bufs, sems, after=(), send=True, recv=True):
    nb, nsem = len(bufs), len(sems)

    def body(*refs):
        s = refs[nb:nb + nsem]
        for k, (src, dst, dev) in enumerate(plan(refs[:nb])):
            cp = _rcopy(src, dst, s[2 * k], s[2 * k + 1], dev)
            if send:
                cp.wait_send()
            if recv:
                cp.wait_recv()

    outs = pl.pallas_call(
        body, name=name, out_shape=[pltpu.HBM(a.shape, a.dtype) for a in bufs],
        in_specs=[HBM_SPEC] * nb + [SEM_SPEC] * nsem + [ANY] * len(after), out_specs=[HBM_SPEC] * nb,
        input_output_aliases={i: i for i in range(nb)},
        compiler_params=pltpu.CompilerParams(has_side_effects=EFFECT),
    )(*bufs, *sems, *after)
    return list(outs)


def _to_sibling(views):
    def plan(b):
        x, y, c = _position()
        return [(view(b[2 * k], c), b[2 * k + 1], (x, y, 1 - c)) for k, view in enumerate(views)]
    return plan


def _to_chip(k):
    def plan(b):
        x, y, c = _position()
        cx, cy = _other_chips(x, y)[0][k]
        return [(b[0], b[1], (cx, cy, c))]
    return plan


def _slots_to_chips(b):
    x, y, c = _position()
    chips, cidx = _other_chips(x, y)
    return [(b[0].at[cidx[k]], b[1 + k], (cx, cy, c)) for k, (cx, cy) in enumerate(chips)]


def _halves_to_sibling(b):
    x, y, c = _position()
    views = [r.at[pl.ds(c * (r.shape[0] // 2), r.shape[0] // 2), :] for r in b]
    return [(v, v, (x, y, 1 - c)) for v in views]


def _landed(b):
    x, y, c = _position()
    return [(ref, ref, (x, y, c)) for ref in b]


def _assemble(name, pieces, out_shape, index_of):
    n = len(pieces)

    def body(*refs):
        out_ref, sem = refs[n], refs[n + 1]
        x, y, c = _position()
        _, cidx = _other_chips(x, y)
        cps = [pltpu.make_async_copy(refs[k], out_ref.at[index_of(k, 2 * x + y, c, cidx)], sem.at[k]) for k in range(n)]
        for cp in cps:
            cp.start()
        for cp in cps:
            cp.wait()

    return pl.pallas_call(
        body, name=name, in_specs=[VMEM] * n, out_specs=ANY, out_shape=out_shape,
        scratch_shapes=[pltpu.SemaphoreType.DMA((n,))],
        compiler_params=pltpu.CompilerParams(vmem_limit_bytes=VMEM_LIMIT),
    )(*pieces)


def _gather_cond(c8, cw):
    def body(c8_ref, cw_ref, call_ref, cwall_ref, ssem, rsem, lsem):
        x, y, c = _position()
        chip = 2 * x + y
        me = 4 * x + 2 * y + c
        chips, cidx = _other_chips(x, y)
        own = [pltpu.make_async_copy(c8_ref, call_ref.at[me], lsem.at[0]),
               pltpu.make_async_copy(cw_ref, cwall_ref.at[chip], lsem.at[1])]
        for cp in own:
            cp.start()
        sends = [_rcopy(cw_ref, cwall_ref.at[chip], ssem.at[k], rsem.at[k], (cx, cy, c))
                 for k, (cx, cy) in enumerate(chips)]
        for mask in range(1, N_DEV):
            fx, fy, fc = (mask >> 2) & 1, (mask >> 1) & 1, mask & 1
            dev = (1 - x if fx else x, 1 - y if fy else y, 1 - c if fc else c)
            sends.append(_rcopy(c8_ref, call_ref.at[me], ssem.at[2 + mask], rsem.at[2 + mask], dev))
        for cp in sends:
            cp.start()
        for k in range(3):
            slot = cwall_ref.at[cidx[k]]
            _rcopy(slot, slot, ssem.at[k], rsem.at[k], (x, y, c)).wait_recv()
        for mask in range(1, N_DEV):
            slot = call_ref.at[jnp.bitwise_xor(me, mask)]
            _rcopy(slot, slot, ssem.at[2 + mask], rsem.at[2 + mask], (x, y, c)).wait_recv()
        for cp in sends:
            cp.wait_send()
        for cp in own:
            cp.wait()

    return pl.pallas_call(
        body, name="gather_cond", in_specs=[VMEM, VMEM], out_specs=[VMEM, VMEM],
        out_shape=[jax.ShapeDtypeStruct((N_DEV,) + c8.shape, F32), jax.ShapeDtypeStruct((N_CHIPS,) + cw.shape, F32)],
        scratch_shapes=[pltpu.SemaphoreType.DMA((10,)), pltpu.SemaphoreType.DMA((10,)), pltpu.SemaphoreType.DMA((2,))],
    )(c8, cw)


def _exchange_mod(mod_part):
    def body(mp_ref, out_ref, ssem, rsem, lsem):
        x, y, c = _position()
        chip = 2 * x + y
        chips, cidx = _other_chips(x, y)
        own = pltpu.make_async_copy(mp_ref, out_ref.at[chip], lsem)
        own.start()
        sends = [_rcopy(mp_ref, out_ref.at[chip], ssem.at[k], rsem.at[k], (cx, cy, c))
                 for k, (cx, cy) in enumerate(chips)]
        for cp in sends:
            cp.start()
        for k in range(3):
            slot = out_ref.at[cidx[k]]
            _rcopy(slot, slot, ssem.at[k], rsem.at[k], (x, y, c)).wait_recv()
        for cp in sends:
            cp.wait_send()
        own.wait()

    return pl.pallas_call(
        body, name="exchange_mod", in_specs=[VMEM], out_specs=VMEM,
        out_shape=jax.ShapeDtypeStruct((N_CHIPS,) + mod_part.shape, F32),
        scratch_shapes=[pltpu.SemaphoreType.DMA((3,)), pltpu.SemaphoreType.DMA((3,)), pltpu.SemaphoreType.DMA],
    )(mod_part)


def _gather_small(pack):
    rows, n = pack.shape

    def body(p_ref, sum_ref, all_ref, ssem, rsem, lsem):
        x, y, c = _position()
        me = 4 * x + 2 * y + c
        sib = (x, y, 1 - c)
        chips, cidx = _other_chips(x, y)
        own = pltpu.make_async_copy(p_ref, all_ref.at[me], lsem)
        own.start()
        sends = [_rcopy(p_ref, all_ref.at[me], ssem.at[0], rsem.at[0], sib)]
        sends += [_rcopy(p_ref, all_ref.at[me], ssem.at[1 + k], rsem.at[1 + k], (cx, cy, c))
                  for k, (cx, cy) in enumerate(chips)]
        for cp in sends:
            cp.start()
        for k in range(3):
            slot = all_ref.at[2 * cidx[k] + c]
            _rcopy(slot, slot, ssem.at[1 + k], rsem.at[1 + k], sib).wait_recv()
            fw = _rcopy(slot, slot, ssem.at[4 + k], rsem.at[4 + k], sib)
            fw.start()
            sends.append(fw)
        slot = all_ref.at[jnp.bitwise_xor(me, 1)]
        _rcopy(slot, slot, ssem.at[0], rsem.at[0], sib).wait_recv()
        for k in range(3):
            slot = all_ref.at[2 * cidx[k] + 1 - c]
            _rcopy(slot, slot, ssem.at[4 + k], rsem.at[4 + k], sib).wait_recv()
        for cp in sends:
            cp.wait_send()
        own.wait()
        acc = all_ref[0]
        for k in range(1, N_DEV):
            acc = acc + all_ref[k]
        sum_ref[...] = acc

    return pl.pallas_call(
        body, name="gather_small", in_specs=[VMEM], out_specs=[VMEM, VMEM],
        out_shape=[jax.ShapeDtypeStruct((rows, n), F32), jax.ShapeDtypeStruct((N_DEV, rows, n), F32)],
        scratch_shapes=[pltpu.SemaphoreType.DMA((7,)), pltpu.SemaphoreType.DMA((7,)), pltpu.SemaphoreType.DMA],
        compiler_params=pltpu.CompilerParams(vmem_limit_bytes=VMEM_LIMIT),
    )(pack)


def _chip_partial(pos, g, recv, rows, name):
    ns, full, n = g.shape
    h = full // 2
    nb = h // rows

    def body(pos_ref, g_ref, r_ref, o_ref):
        o_ref[...] = (g_ref[...] + r_ref[...]).astype(BF16)

    return pl.pallas_call(
        body, name=name,
        grid_spec=pltpu.PrefetchScalarGridSpec(
            num_scalar_prefetch=1, grid=(ns, nb),
            in_specs=[pl.BlockSpec((None, rows, n), lambda s, i, p: (s, p[1] * nb + i, 0)),
                      pl.BlockSpec((None, rows, n), lambda s, i, p: (s, i, 0))],
            out_specs=pl.BlockSpec((None, rows, n), lambda s, i, p: (s, i, 0))),
        out_shape=jax.ShapeDtypeStruct((ns, h, n), BF16),
        compiler_params=_params(("parallel", "parallel")),
    )(pos, g, recv)


def _final_sum(pos, first, parts, rows, name):
    h, n = first.shape
    nb = h // rows

    def body(pos_ref, f_ref, rb0_ref, rb1_ref, rb2_ref, o_ref):
        acc = f_ref[...]
        for rb_ref in (rb0_ref, rb1_ref, rb2_ref):
            acc = acc + rb_ref[...].astype(F32)
        o_ref[...] = acc

    part = pl.BlockSpec((rows, n), lambda i, p: (i, 0))
    return pl.pallas_call(
        body, name=name,
        grid_spec=pltpu.PrefetchScalarGridSpec(
            num_scalar_prefetch=1, grid=(nb,), in_specs=[part] * 4,
            out_specs=pl.BlockSpec((rows, n), lambda i, p: (p[1] * nb + i, 0))),
        out_shape=jax.ShapeDtypeStruct((2 * h, n), F32),
        compiler_params=_params(("parallel",)),
    )(pos, first, *parts)


def _final_half(pos, g, recv_a, recv_b, rows, name):
    ns, full, n = g.shape
    h = full // 2
    nb = h // rows

    def body(pos_ref, g_ref, ra_ref, rb0_ref, rb1_ref, rb2_ref, o_ref):
        acc = g_ref[...] + ra_ref[...]
        for rb_ref in (rb0_ref, rb1_ref, rb2_ref):
            acc = acc + rb_ref[...].astype(F32)
        o_ref[...] = acc

    part = pl.BlockSpec((rows, n), lambda i, p: (i, 0))
    return pl.pallas_call(
        body, name=name,
        grid_spec=pltpu.PrefetchScalarGridSpec(
            num_scalar_prefetch=1, grid=(nb,),
            in_specs=[pl.BlockSpec((None, rows, n), lambda i, p: (p[0], p[1] * nb + i, 0)),
                      pl.BlockSpec((None, rows, n), lambda i, p: (p[0], i, 0)), part, part, part],
            out_specs=pl.BlockSpec((rows, n), lambda i, p: (p[1] * nb + i, 0))),
        out_shape=jax.ShapeDtypeStruct((full, n), F32),
        compiler_params=_params(("parallel",)),
    )(pos, g, recv_a, *recv_b)


def _modulation(c_rows, w_ada, b_ada, cols, name):
    d, n = w_ada.shape
    rows = c_rows.shape[0]

    def body(c_ref, w_ref, b_ref, o_ref):
        cv = c_ref[...]
        c_act = (cv * _sigmoid(cv)).astype(BF16)
        o_ref[...] = jnp.dot(c_act, w_ref[...].astype(BF16), preferred_element_type=F32) + b_ref[...]

    return pl.pallas_call(
        body, name=name, grid=(n // cols,),
        in_specs=[pl.BlockSpec((rows, d), lambda j: (0, 0)), pl.BlockSpec((d, cols), lambda j: (0, j)),
                  pl.BlockSpec((1, cols), lambda j: (0, j))],
        out_specs=pl.BlockSpec((rows, cols), lambda j: (0, j)),
        out_shape=jax.ShapeDtypeStruct((rows, n), F32),
        compiler_params=_params(("parallel",)),
    )(c_rows, w_ada, b_ada)


def _prenorm(x, norm_g, scale, shift, rows):
    s, d = x.shape

    def body(x_ref, g_ref, sc_ref, sh_ref, h_ref, r_ref):
        xv = x_ref[...]
        r = lax.rsqrt(jnp.mean(xv * xv, axis=-1, keepdims=True) + EPS)
        h = (xv * r * g_ref[...]) * (1.0 + sc_ref[...]) + sh_ref[...]
        h_ref[...] = h.astype(BF16)
        r_ref[...] = r

    vec = pl.BlockSpec((1, d), lambda i: (0, 0))
    return pl.pallas_call(
        body, name="prenorm", grid=(s // rows,),
        in_specs=[pl.BlockSpec((rows, d), lambda i: (i, 0)), vec, vec, vec],
        out_specs=[pl.BlockSpec((rows, d), lambda i: (i, 0)), pl.BlockSpec((rows, 1), lambda i: (i, 0))],
        out_shape=[jax.ShapeDtypeStruct((s, d), BF16), jax.ShapeDtypeStruct((s, 1), F32)],
        compiler_params=_params(("parallel",)),
    )(x, norm_g, scale, shift)


def _mixer_a_fwd(proj, conv_w, wa, rows, cols):
    s = proj.shape[0]
    ncb = wa // cols

    def body(ab_ref, ac_ref, ax_ref, az_ref, w_ref, y_ref, qbuf):
        t = pl.program_id(1)

        @pl.when(t == 0)
        def _():
            qbuf[0:HALO_A, :] = jnp.zeros((HALO_A, cols), F32)

        q = ac_ref[...].astype(F32) * ax_ref[...].astype(F32)
        qbuf[HALO_A:HALO_A + rows, :] = q
        conv = w_ref[2:3, :] * q
        for k in range(TAPS_A - 1):
            off = HALO_A - (TAPS_A - 1) + k
            conv = conv + w_ref[k:k + 1, :] * qbuf[off:off + rows, :]
        zv = az_ref[...].astype(F32)
        y_ref[...] = (ab_ref[...].astype(F32) * conv * (zv * _sigmoid(zv))).astype(BF16)
        qbuf[0:HALO_A, :] = qbuf[rows:rows + HALO_A, :]

    def sec(k):
        return pl.BlockSpec((rows, cols), lambda cb, t, k=k: (t, k * ncb + cb))

    return pl.pallas_call(
        body, name="mixer_a_fwd", grid=(ncb, s // rows),
        in_specs=[sec(0), sec(1), sec(2), sec(3), pl.BlockSpec((HALO_A, cols), lambda cb, t: (0, cb))],
        out_specs=pl.BlockSpec((rows, cols), lambda cb, t: (t, cb)),
        out_shape=jax.ShapeDtypeStruct((s, 2 * wa), BF16),
        scratch_shapes=[pltpu.VMEM((HALO_A + rows, cols), F32)],
        compiler_params=_params(("parallel", "arbitrary")),
    )(proj, proj, proj, proj, conv_w)


def _shifted_back(dst, src, lo, hi, cs):
    for n in range(1, 8):
        dst[n, lo:hi, :] = src[lo - n:hi - n, cs]


def _shifted_fwd(dst, src, lo, hi, cs):
    for n in range(1, 8):
        dst[n, lo:hi, :] = src[lo + n:hi + n, cs]


def _shift_rows(shifted, plain, n, start, size, cs):
    return plain[pl.ds(start, size), cs] if n == 0 else shifted[n, pl.ds(start, size), :]


def _mixer_b_conv_fwd(proj, conv_w, conv_b, wa, rows, cols, chunk):
    s = proj.shape[0]
    wb = conv_w.shape[1]
    ncb = wb // cols
    sec0 = 4 * wa // cols

    def body(bv_ref, bg_ref, w_ref, b_ref, u0_ref, u_ref, ubuf, sh):
        t = pl.program_id(1)

        @pl.when(t == 0)
        def _():
            ubuf[0:HALO_B, :] = jnp.zeros((HALO_B, cols), F32)

        u0 = bv_ref[...].astype(F32) * _sigmoid(bg_ref[...].astype(F32))
        u0_ref[...] = u0
        ubuf[HALO_B:HALO_B + rows, :] = u0
        for lc in range(cols // LANES):
            cs = slice(lc * LANES, (lc + 1) * LANES)
            _shifted_back(sh, ubuf, 8, HALO_B + rows, cs)
            taps = [w_ref[k:k + 1, cs] for k in range(TAPS_B)]
            bias = b_ref[:, cs]

            def row_chunk(rc, carry, cs=cs, taps=taps, bias=bias):
                base = pl.multiple_of(rc * chunk, chunk)
                acc = jnp.zeros((chunk, LANES), F32)
                for k in range(TAPS_B):
                    mq, n = divmod(TAPS_B - 1 - k, 8)
                    acc = acc + taps[k] * _shift_rows(sh, ubuf, n, HALO_B - 8 * mq + base, chunk, cs)
                u_ref[pl.ds(base, chunk), cs] = acc + bias
                return carry

            lax.fori_loop(0, rows // chunk, row_chunk, 0)
        ubuf[0:HALO_B, :] = ubuf[rows:rows + HALO_B, :]

    return pl.pallas_call(
        body, name="mixer_b_conv_fwd", grid=(ncb, s // rows),
        in_specs=[pl.BlockSpec((rows, cols), lambda cb, t: (t, sec0 + cb)),
                  pl.BlockSpec((rows, cols), lambda cb, t: (t, sec0 + ncb + cb)),
                  pl.BlockSpec((HALO_B, cols), lambda cb, t: (0, cb)),
                  pl.BlockSpec((1, cols), lambda cb, t: (0, cb))],
        out_specs=[pl.BlockSpec((rows, cols), lambda cb, t: (t, cb))] * 2,
        out_shape=[jax.ShapeDtypeStruct((s, wb), F32)] * 2,
        scratch_shapes=[pltpu.VMEM((HALO_B + rows, cols), F32), pltpu.VMEM((8, HALO_B + rows, LANES), F32)],
        compiler_params=_params(("parallel", "arbitrary")),
    )(proj, proj, conv_w, conv_b)


def _layernorm_stats(u):
    mu = jnp.mean(u, axis=-1, keepdims=True)
    xc = u - mu
    var = jnp.mean(xc * xc, axis=-1, keepdims=True)
    return xc * lax.rsqrt(var + EPS), lax.rsqrt(var + EPS)


def _mixer_b_gate_fwd(y, u, proj, ln_g, ln_b, wa, rows):
    s, wb = u.shape
    sec_z = (4 * wa + 2 * wb) // wb

    def body(y_in, u_ref, bz_ref, g_ref, b_ref, y_ref):
        uh, _ = _layernorm_stats(u_ref[...])
        ln = uh * g_ref[...] + b_ref[...]
        zv = bz_ref[...].astype(F32)
        y_ref[...] = ((ln * _sigmoid(ln)) * (zv * _sigmoid(zv))).astype(BF16)

    vec = pl.BlockSpec((1, wb), lambda i: (0, 0))
    return pl.pallas_call(
        body, name="mixer_b_gate_fwd", grid=(s // rows,),
        in_specs=[ANY, pl.BlockSpec((rows, wb), lambda i: (i, 0)), pl.BlockSpec((rows, wb), lambda i: (i, sec_z)),
                  vec, vec],
        out_specs=pl.BlockSpec((rows, wb), lambda i: (i, wa // wb)),
        out_shape=jax.ShapeDtypeStruct(y.shape, BF16), input_output_aliases={0: 0},
        compiler_params=_params(("parallel",)),
    )(y, u, proj, ln_g, ln_b)


def _loss_head(x, o, target, gate, final_g, rows):
    s, d = x.shape

    def body(x_ref, o_ref, t_ref, gate_ref, fg_ref, dx2_ref, do_ref, loss_ref, gfg_ref, dgate_ref):
        i = pl.program_id(0)
        ov = o_ref[...]
        x2 = x_ref[...] + gate_ref[...] * ov
        r2 = lax.rsqrt(jnp.mean(x2 * x2, axis=-1, keepdims=True) + EPS)
        xn2 = x2 * r2
        diff = xn2 * fg_ref[...] - t_ref[...]
        dout = diff * (1.0 / d)
        dxn2 = dout * fg_ref[...]
        dx2 = r2 * (dxn2 - xn2 * jnp.mean(dxn2 * xn2, axis=-1, keepdims=True))
        dx2_ref[...] = dx2
        do_ref[...] = (gate_ref[...] * dx2).astype(BF16)
        loss_part = 0.5 * jnp.sum(jnp.mean(diff * diff, axis=-1, keepdims=True), axis=0, keepdims=True)
        gfg_part = jnp.sum(dout * xn2, axis=0, keepdims=True)
        dgate_part = jnp.sum(dx2 * ov, axis=0, keepdims=True)

        @pl.when(i == 0)
        def _():
            loss_ref[...] = jnp.zeros_like(loss_ref)
            gfg_ref[...] = jnp.zeros_like(gfg_ref)
            dgate_ref[...] = jnp.zeros_like(dgate_ref)

        loss_ref[...] += jnp.broadcast_to(loss_part, loss_ref.shape)
        gfg_ref[...] += gfg_part
        dgate_ref[...] += dgate_part

    blk = pl.BlockSpec((rows, d), lambda i: (i, 0))
    vec = pl.BlockSpec((1, d), lambda i: (0, 0))
    return pl.pallas_call(
        body, name="loss_head", grid=(s // rows,),
        in_specs=[blk, blk, blk, vec, vec],
        out_specs=[blk, blk, pl.BlockSpec((1, 128), lambda i: (0, 0)), vec, vec],
        out_shape=[jax.ShapeDtypeStruct((s, d), F32), jax.ShapeDtypeStruct((s, d), BF16),
                   jax.ShapeDtypeStruct((1, 128), F32), jax.ShapeDtypeStruct((1, d), F32),
                   jax.ShapeDtypeStruct((1, d), F32)],
        compiler_params=_params(("arbitrary",)),
    )(x, o, target, gate, final_g)


def _mixer_a_bwd(proj, dy, conv_w, wa, din, rows):
    s = proj.shape[0]
    nt = s // rows
    per_halo = rows // HALO_IN

    def body(ab_ref, ac_ref, ax_ref, az_ref, hc_ref, hx_ref, dy_ref, w_ref, dp_ref, dw_ref, qbuf, dbuf):
        i = pl.program_id(0)

        @pl.when(i == 0)
        def _():
            dbuf[rows:rows + HALO_A, :] = jnp.zeros((HALO_A, wa), F32)
            dw_ref[...] = jnp.zeros_like(dw_ref)

        keep = jnp.where(i == nt - 1, 0.0, 1.0)
        before = hc_ref[...].astype(F32) * hx_ref[...].astype(F32) * keep
        qbuf[0:HALO_A, :] = before[HALO_IN - HALO_A:HALO_IN, :]
        acv, axv = ac_ref[...].astype(F32), ax_ref[...].astype(F32)
        q = acv * axv
        qbuf[HALO_A:HALO_A + rows, :] = q
        conv = w_ref[2:3, :] * q
        for k in range(TAPS_A - 1):
            off = HALO_A - (TAPS_A - 1) + k
            conv = conv + w_ref[k:k + 1, :] * qbuf[off:off + rows, :]
        zv, abv, dyv = az_ref[...].astype(F32), ab_ref[...].astype(F32), dy_ref[...]
        sg = _sigmoid(zv)
        sz = zv * sg
        dp_ref[:, 0:wa] = (dyv * conv * sz).astype(BF16)
        dp_ref[:, 3 * wa:4 * wa] = (dyv * abv * conv * (sg * (1.0 + zv * (1.0 - sg)))).astype(BF16)
        dconv = dyv * abv * sz
        dbuf[0:rows, :] = dconv
        dq = w_ref[2:3, :] * dconv
        for k in range(TAPS_A - 1):
            off = TAPS_A - 1 - k
            dq = dq + w_ref[k:k + 1, :] * dbuf[off:off + rows, :]
        dp_ref[:, wa:2 * wa] = (dq * axv).astype(BF16)
        dp_ref[:, 2 * wa:3 * wa] = (dq * acv).astype(BF16)
        for k in range(TAPS_A):
            off = HALO_A - (TAPS_A - 1) + k
            dw_ref[k:k + 1, :] += jnp.sum(dconv * qbuf[off:off + rows, :], axis=0, keepdims=True)
        dbuf[rows:rows + HALO_A, :] = dbuf[0:HALO_A, :]

    def sec(k):
        return pl.BlockSpec((rows, wa), lambda i, k=k: (nt - 1 - i, k))

    def halo(k):
        return pl.BlockSpec((HALO_IN, wa), lambda i, k=k: (jnp.maximum((nt - 1 - i) * per_halo - 1, 0), k))

    return pl.pallas_call(
        body, name="mixer_a_bwd", grid=(nt,),
        in_specs=[sec(0), sec(1), sec(2), sec(3), halo(1), halo(2),
                  pl.BlockSpec((rows, wa), lambda i: (nt - 1 - i, 0)),
                  pl.BlockSpec((HALO_A, wa), lambda i: (0, 0))],
        out_specs=[pl.BlockSpec((rows, 4 * wa), lambda i: (nt - 1 - i, 0)),
                   pl.BlockSpec((HALO_A, wa), lambda i: (0, 0))],
        out_shape=[jax.ShapeDtypeStruct((s, din), BF16), jax.ShapeDtypeStruct((HALO_A, wa), F32)],
        scratch_shapes=[pltpu.VMEM((HALO_A + rows, wa), F32), pltpu.VMEM((rows + HALO_A, wa), F32)],
        compiler_params=_params(("arbitrary",)),
    )(proj, proj, proj, proj, proj, proj, dy, conv_w)


def _mixer_b_gate_bwd(dproj, dy, u, proj, ln_g, ln_b, wa, rows):
    s, wb = u.shape
    sec_z = (4 * wa + 2 * wb) // wb

    def body(dp_in, dy_ref, u_ref, bz_ref, g_ref, b_ref, dp_ref, du_ref, dg_ref, db_ref, dcb_ref):
        i = pl.program_id(0)
        uh, rs = _layernorm_stats(u_ref[...])
        ln = uh * g_ref[...] + b_ref[...]
        sl = _sigmoid(ln)
        zv = bz_ref[...].astype(F32)
        sg = _sigmoid(zv)
        dyv = dy_ref[...]
        dp_ref[...] = (dyv * (ln * sl) * (sg * (1.0 + zv * (1.0 - sg)))).astype(BF16)
        dln = dyv * (zv * sg) * (sl * (1.0 + ln * (1.0 - sl)))
        duh = dln * g_ref[...]
        du = rs * (duh - jnp.mean(duh, axis=-1, keepdims=True) - uh * jnp.mean(duh * uh, axis=-1, keepdims=True))
        du_ref[...] = du

        @pl.when(i == 0)
        def _():
            dg_ref[...] = jnp.zeros_like(dg_ref)
            db_ref[...] = jnp.zeros_like(db_ref)
            dcb_ref[...] = jnp.zeros_like(dcb_ref)

        dg_ref[...] += jnp.sum(dln * uh, axis=0, keepdims=True)
        db_ref[...] += jnp.sum(dln, axis=0, keepdims=True)
        dcb_ref[...] += jnp.sum(du, axis=0, keepdims=True)

    blk = pl.BlockSpec((rows, wb), lambda i: (i, 0))
    vec = pl.BlockSpec((1, wb), lambda i: (0, 0))
    vshape = jax.ShapeDtypeStruct((1, wb), F32)
    return pl.pallas_call(
        body, name="mixer_b_gate_bwd", grid=(s // rows,),
        in_specs=[ANY, pl.BlockSpec((rows, wb), lambda i: (i, wa // wb)), blk,
                  pl.BlockSpec((rows, wb), lambda i: (i, sec_z)), vec, vec],
        out_specs=[pl.BlockSpec((rows, wb), lambda i: (i, sec_z)), blk, vec, vec, vec],
        out_shape=[jax.ShapeDtypeStruct(dproj.shape, BF16), jax.ShapeDtypeStruct((s, wb), F32), vshape, vshape, vshape],
        input_output_aliases={0: 0},
        compiler_params=_params(("arbitrary",)),
    )(dproj, dy, u, proj, ln_g, ln_b)


def _mixer_b_conv_bwd(dproj, du, u0, proj, conv_w, wa, rows, chunk):
    s, wb = du.shape
    nt = s // rows
    per32 = rows // HALO_B
    sec_v = 4 * wa // wb
    nrc = rows // chunk

    def body(dp_in, du_ref, u0_ref, h0_ref, bv_ref, bg_ref, w_ref, dp_ref, dw_ref, ubuf, dbuf, sh, shf, dwacc):
        i = pl.program_id(0)

        @pl.when(i == 0)
        def _():
            dbuf[rows:rows + HALO_B, :] = jnp.zeros((HALO_B, wb), F32)
            dwacc[...] = jnp.zeros_like(dwacc)

        ubuf[0:HALO_B, :] = h0_ref[...] * jnp.where(i == nt - 1, 0.0, 1.0)
        ubuf[HALO_B:HALO_B + rows, :] = u0_ref[...]
        dbuf[0:rows, :] = du_ref[...]
        for lc in range(wb // LANES):
            cs = slice(lc * LANES, (lc + 1) * LANES)
            _shifted_back(sh, ubuf, 8, HALO_B + rows, cs)
            _shifted_fwd(shf, dbuf, 0, rows + HALO_B - 8, cs)
            taps = [w_ref[k:k + 1, cs] for k in range(TAPS_B)]

            def conv_rows(rc, c0, cs=cs, taps=taps, lc=lc):
                base = pl.multiple_of(rc * chunk, chunk)
                acc = jnp.zeros((chunk, LANES), F32)
                for k in range(TAPS_B):
                    mq, n = divmod(TAPS_B - 1 - k, 8)
                    acc = acc + taps[k] * _shift_rows(shf, dbuf, n, base + 8 * mq, chunk, cs)
                sg = _sigmoid(bg_ref[pl.ds(base, chunk), cs].astype(F32))
                bv = bv_ref[pl.ds(base, chunk), cs].astype(F32)
                dp_ref[pl.ds(base, chunk), cs] = (acc * sg).astype(BF16)
                dp_ref[pl.ds(base, chunk), wb + lc * LANES:wb + (lc + 1) * LANES] = (
                    acc * bv * sg * (1.0 - sg)).astype(BF16)
                return c0

            lax.fori_loop(0, nrc, conv_rows, 0)

            def dw_rows(rc, accs, cs=cs):
                base = pl.multiple_of(rc * chunk, chunk)
                du_c = dbuf[pl.ds(base, chunk), cs]
                out = []
                for k in range(TAPS_B):
                    mq, n = divmod(TAPS_B - 1 - k, 8)
                    prod = du_c * _shift_rows(sh, ubuf, n, HALO_B - 8 * mq + base, chunk, cs)
                    out.append(accs[k] + jnp.sum(prod.reshape(chunk // SUBLANES, SUBLANES, LANES), axis=0))
                return tuple(out)

            accs = lax.fori_loop(0, nrc, dw_rows, tuple(jnp.zeros((SUBLANES, LANES), F32) for _ in range(TAPS_B)))
            for k in range(TAPS_B):
                dwacc[k * SUBLANES:(k + 1) * SUBLANES, cs] += accs[k]
        dbuf[rows:rows + HALO_B, :] = dbuf[0:HALO_B, :]

        @pl.when(i == nt - 1)
        def _():
            for k in range(HALO_B):
                dw_ref[k:k + 1, :] = jnp.sum(dwacc[k * SUBLANES:(k + 1) * SUBLANES, :], axis=0, keepdims=True)

    def rev(cols_blk):
        return pl.BlockSpec((rows, wb), lambda i, cb=cols_blk: (nt - 1 - i, cb))

    return pl.pallas_call(
        body, name="mixer_b_conv_bwd", grid=(nt,),
        in_specs=[ANY, rev(0), rev(0),
                  pl.BlockSpec((HALO_B, wb), lambda i: (jnp.maximum((nt - 1 - i) * per32 - 1, 0), 0)),
                  rev(sec_v), rev(sec_v + 1), pl.BlockSpec((HALO_B, wb), lambda i: (0, 0))],
        out_specs=[pl.BlockSpec((rows, 2 * wb), lambda i: (nt - 1 - i, sec_v // 2)),
                   pl.BlockSpec((HALO_B, wb), lambda i: (0, 0))],
        out_shape=[jax.ShapeDtypeStruct(dproj.shape, BF16), jax.ShapeDtypeStruct((HALO_B, wb), F32)],
        input_output_aliases={0: 0},
        scratch_shapes=[pltpu.VMEM((HALO_B + rows, wb), F32), pltpu.VMEM((rows + HALO_B, wb), F32),
                        pltpu.VMEM((8, HALO_B + rows, LANES), F32), pltpu.VMEM((8, rows + HALO_B, LANES), F32),
                        pltpu.VMEM((HALO_B * SUBLANES, wb), F32)],
        compiler_params=_params(("arbitrary",)),
    )(dproj, du, u0, u0, proj, proj, conv_w)


def _prenorm_bwd(x, r, dh, dx2, norm_g, scale, rows):
    s, d = x.shape

    def body(x_ref, r_ref, dh_ref, dx2_ref, g_ref, sc_ref, gx_ref, dsh_ref, dsc_ref, dg_ref):
        i = pl.program_id(0)
        rv = r_ref[...]
        xn = x_ref[...] * rv
        dhv = dh_ref[...]
        one_sc = 1.0 + sc_ref[...]
        dxn = dhv * one_sc * g_ref[...]
        gx_ref[...] = dx2_ref[...] + rv * (dxn - xn * jnp.mean(dxn * xn, axis=-1, keepdims=True))

        @pl.when(i == 0)
        def _():
            dsh_ref[...] = jnp.zeros_like(dsh_ref)
            dsc_ref[...] = jnp.zeros_like(dsc_ref)
            dg_ref[...] = jnp.zeros_like(dg_ref)

        dsh_ref[...] += jnp.sum(dhv, axis=0, keepdims=True)
        dsc_ref[...] += jnp.sum(dhv * (xn * g_ref[...]), axis=0, keepdims=True)
        dg_ref[...] += jnp.sum(dhv * one_sc * xn, axis=0, keepdims=True)

    blk = pl.BlockSpec((rows, d), lambda i: (i, 0))
    vec = pl.BlockSpec((1, d), lambda i: (0, 0))
    vshape = jax.ShapeDtypeStruct((1, d), F32)
    return pl.pallas_call(
        body, name="prenorm_bwd", grid=(s // rows,),
        in_specs=[blk, pl.BlockSpec((rows, 1), lambda i: (i, 0)), blk, blk, vec, vec],
        out_specs=[blk, vec, vec, vec],
        out_shape=[jax.ShapeDtypeStruct((s, d), F32), vshape, vshape, vshape],
        compiler_params=_params(("arbitrary",)),
    )(x, r, dh, dx2, norm_g, scale)


def _pad_rows(a, rows):
    return jnp.pad(a, ((0, rows - a.shape[0]), (0, 0)))


def _tile(n, want):
    t = min(n, want)
    while n % t:
        t -= 1
    return t


def kernel(x, c, norm_g, w_ada, b_ada, w_in, conv_a_w, conv_b_w, conv_b_b, ln_b_g, ln_b_b, w_out, final_g, loss_target, m_norm_g, m_w_ada, m_b_ada, m_w_in, m_conv_a_w, m_conv_b_w, m_conv_b_b, m_ln_b_g, m_ln_b_b, m_w_out, m_final_g, v_norm_g, v_w_ada, v_b_ada, v_w_in, v_conv_a_w, v_conv_b_w, v_conv_b_b, v_ln_b_g, v_ln_b_b, v_w_out, v_final_g):
    s, d = x.shape[1], x.shape[2]
    wa = conv_b_b.shape[-1]
    dmix = 2 * wa
    ns = w_in.shape[-1]
    din = N_CHIPS * ns
    r4 = w_out.shape[1]
    na = w_ada.shape[-1]
    wsh = conv_a_w.shape[-1]
    px, py, pc = _position()
    chip = 2 * px + py
    me = 4 * px + 2 * py + pc
    pos = jnp.stack([chip, pc]).astype(jnp.int32)
    x2d = x.reshape(s, d)
    target = loss_target.reshape(s, d)

    hc, ho, hrow = ns // 2, r4 // 2, d // 2
    _, cidx = _other_chips(px, py)
    hq = hc // 2
    win4 = _cast_quarters(w_in[0], _tile(d, 512), "cast_w_in")
    wout_bf = _cast_bf16(w_out[0], _tile(r4, 512), "cast_w_out")

    def gather_plan(b):
        x, y, cc = _position()
        chips, _ = _other_chips(x, y)
        xn, yn = (1 - x, y, cc), (x, 1 - y, cc)
        q0, q1 = b[0].at[2 * cc], b[0].at[2 * cc + 1]
        return ([(q0, b[2], xn), (q1, b[5], yn), (q1, b[3], xn), (q0, b[4], yn)]
                + [(b[1].at[pl.ds(cc * ho, ho), :], b[6 + k], (cx, cy, cc)) for k, (cx, cy) in enumerate(chips)])

    def gather_sent(b):
        return [(src, src, dev) for src, _, dev in gather_plan(list(b) + [None] * 7)]

    def onward_plan(b):
        x, y, cc = _position()
        sib = (x, y, 1 - cc)
        return [(b[0], b[2], (x, 1 - y, cc)), (b[1], b[3], (1 - x, y, cc)), (b[0], b[4], sib), (b[1], b[5], sib)]

    pairs = lambda n: _to_sibling([lambda ref, cc: ref] * n)

    c8 = jnp.broadcast_to(c, (8, d))
    cw = jnp.concatenate([_pad_rows(conv_a_w[0], HALO_A), _pad_rows(conv_b_w[0], HALO_B)], axis=0)
    c_all, cw_all = _gather_cond(c8, cw)
    c_rows = c_all[:, 0, :]
    cw_full = jnp.transpose(cw_all, (1, 0, 2)).reshape(HALO_A + HALO_B, wa)
    conv_a_full, conv_b_full = cw_full[:HALO_A], cw_full[HALO_A:]

    b_ada_sh = lax.dynamic_slice(b_ada, (0, chip * na), (1, na))
    mod_part = _modulation(_pad_rows(c_rows, 2 * N_DEV), w_ada[0], b_ada_sh, _tile(na, 512), "modulation")[:N_DEV]
    mod_all = _exchange_mod(mod_part)
    mod = lax.dynamic_index_in_dim(mod_all, me, axis=1, keepdims=False).reshape(1, 3 * d)
    shift, scale, gate = mod[:, :d], mod[:, d:2 * d], mod[:, 2 * d:]

    def quarter():
        return lax.empty((d, hq), BF16)

    g_sems, g_bufs, g_tok = _start_copies(
        "gather_start", gather_plan, 7,
        [win4, wout_bf] + [quarter() for _ in range(4)] + [lax.empty((ho, d), BF16) for _ in range(3)],
        after=[mod_all])
    win4, wout_bf, (x0, x1, y0, y1), lo = g_bufs[0], g_bufs[1], g_bufs[2:6], g_bufs[6:9]

    h, r = _prenorm(x2d, norm_g, scale, shift + g_tok[0, 0], _tile(s, 256))
    bm = _tile(s, 1024)
    pieces = [None, None]

    def piece(slot, half, quarters, name, own_half=None):
        where = jnp.reshape(2 * slot + half, (1,)).astype(jnp.int32)
        pieces[:] = _proj_piece(where, h, quarters, pieces[0], pieces[1], din, 2 * N_CHIPS, _tile(s, 512), name,
                                own_half=own_half)
        return pieces[0]

    proj = piece(chip, 0, (win4, win4), "proj_own0", own_half=0)
    proj = piece(chip, 1, (win4, win4), "proj_own1", own_half=1)
    x0, y1 = _wait_copies("gather_wait_a", _landed, [x0, y1], g_sems[0:4], after=[proj], send=False)
    on_sems, (x0, y1, dg0, dg1, sx0, sy1), _ = _start_copies(
        "pass_on_a", onward_plan, 4, [x0, y1] + [quarter() for _ in range(4)])
    x1, y0 = _wait_copies("gather_wait_b", _landed, [x1, y0], g_sems[4:8], after=[x0], send=False)
    pb_sems, (x1, sx1, y0, sy0), _ = _start_copies("pass_on_b", pairs(2), 2, [x1, quarter(), y0, quarter()])
    proj = piece(cidx[0], pc, (x0, x1), "proj_xa")
    proj = piece(cidx[1], pc, (y0, y1), "proj_ya")
    sx0, sy1 = _wait_copies("pass_wait_a", _landed, [sx0, sy1], on_sems[4:8], after=[proj], send=False)
    x1, sx1, y0, sy0 = _wait_copies("pass_wait_b", pairs(2), [x1, sx1, y0, sy0], pb_sems, after=[proj])
    proj = piece(cidx[0], 1 - pc, (sx0, sx1), "proj_xb")
    proj = piece(cidx[1], 1 - pc, (sy0, sy1), "proj_yb")
    x0, y1, dg0, dg1 = _wait_copies("diag_wait", lambda b: onward_plan(list(b) + [None, None])[:2],
                                    [x0, y1, dg0, dg1], on_sems[0:4], after=[proj])
    x0, y1 = _wait_copies("pass_sent_a", lambda b: [(b[0], b[0], (0, 0, 0)), (b[1], b[1], (0, 0, 0))],
                          [x0, y1], on_sems[4:8], after=[dg0], recv=False)
    pd_sems, (dg0, sd0, dg1, sd1), _ = _start_copies("pass_on_d", pairs(2), 2, [dg0, quarter(), dg1, quarter()],
                                                     after=[x0])
    proj = piece(cidx[2], pc, (dg0, dg1), "proj_da")
    dg0, sd0, dg1, sd1 = _wait_copies("pass_wait_d", pairs(2), [dg0, sd0, dg1, sd1], pd_sems, after=[proj])
    proj = piece(cidx[2], 1 - pc, (sd0, sd1), "proj_db")
    win_full = pieces[1]

    lo = _wait_copies("gather_wait_out", _landed, lo, g_sems[8:14], after=[proj], send=False)
    o_sems, o_bufs, o_tok = _start_copies(
        "pass_on_out", pairs(3), 3, [b for k in range(3) for b in (lo[k], lax.empty((ho, d), BF16))])
    win4, wout_bf = _wait_copies("gather_wait_sent", gather_sent, [win4, wout_bf], g_sems, after=[o_tok], recv=False)

    def slot_index(k, chip_, cc, others):
        if k == 0:
            return pl.ds(2 * chip_, 2)
        return 2 * others[(k - 1) % 3] + (cc if k <= 3 else 1 - cc)

    y = _mixer_a_fwd(proj, conv_a_full, wa, _tile(s, 512), _tile(wa, 512))
    u0, u = _mixer_b_conv_fwd(proj, conv_b_full, conv_b_b, wa, _tile(s, 512), _tile(wa, 256), 64)
    y = _mixer_b_gate_fwd(y, u, proj, ln_b_g, ln_b_b, wa, _tile(s, 256))
    o_bufs = _wait_copies("pass_wait_out", pairs(3), o_bufs, o_sems, after=[y])
    wout_full = _assemble("assemble_w_out", [wout_bf.reshape(2, ho, d)] + o_bufs[0::2] + o_bufs[1::2],
                          jax.ShapeDtypeStruct((2 * N_CHIPS, ho, d), BF16), slot_index)
    wout2d = wout_full.reshape(dmix, d)
    bd = _tile(d, 1024)
    o = _matmul(
        y, wout2d, grid=(s // bm, d // bd, 1),
        a_spec=pl.BlockSpec((bm, dmix), lambda i, j, k: (i, 0)),
        b_spec=pl.BlockSpec((dmix, bd), lambda i, j, k: (0, j)),
        o_spec=pl.BlockSpec((bm, bd), lambda i, j, k: (i, j)),
        out_shape=jax.ShapeDtypeStruct((s, d), F32), dims=((1,), (0,)), name="out_proj")
    dx2, do, loss_p, gfg_p, dgate_p = _loss_head(x2d, o, target, gate, final_g.reshape(1, d), _tile(s, 128))

    be = _tile(dmix, 1024)
    g_wout = _matmul(
        y, do, grid=(dmix // be, d // bd, 1),
        a_spec=pl.BlockSpec((s, be), lambda i, j, k: (0, i)),
        b_spec=pl.BlockSpec((s, bd), lambda i, j, k: (0, j)),
        o_spec=pl.BlockSpec((be, bd), lambda i, j, k: (i, j)),
        out_shape=jax.ShapeDtypeStruct((dmix, d), F32), dims=((0,), (0,)), name="grad_w_out")
    swap_out = _to_sibling([lambda ref, cc: ref.at[:, pl.ds((1 - cc) * ho, ho), :]])
    so_sems, (g_wout3, ra_out), so_tok = _start_copies(
        "swap_out_start", swap_out, 1, [g_wout.reshape(N_CHIPS, r4, d), lax.empty((N_CHIPS, ho, d), F32)])
    dy = _matmul(
        do, wout2d, grid=(s // bm, dmix // be, 1),
        a_spec=pl.BlockSpec((bm, d), lambda i, j, k: (i, 0)),
        b_spec=pl.BlockSpec((be, d), lambda i, j, k: (j, 0)),
        o_spec=pl.BlockSpec((bm, be), lambda i, j, k: (i, j)),
        out_shape=jax.ShapeDtypeStruct((s, dmix), F32), dims=((1,), (1,)), name="dy", after=[so_tok])
    g_wout3, ra_out = _wait_copies("swap_out_wait", swap_out, [g_wout3, ra_out], so_sems, after=[dy])
    q_out = _chip_partial(pos, g_wout3, ra_out, _tile(ho, 256), "chip_partial_w_out")
    po_sems, po_bufs, po_tok = _start_copies(
        "send_out_start", _slots_to_chips, 3, [q_out] + [lax.empty((ho, d), BF16) for _ in range(3)])
    dproj, dwa_p = _mixer_a_bwd(proj, dy, conv_a_full + po_tok[0, 0], wa, din, _tile(s, 128))
    dproj, du, dlng_p, dlnb_p, dcb_p = _mixer_b_gate_bwd(dproj, dy, u, proj, ln_b_g, ln_b_b, wa, _tile(s, 128))
    dproj, dwb_p = _mixer_b_conv_bwd(dproj, du, u0, proj, conv_b_full, wa, _tile(s, 256), 64)

    slots = [cidx[0], cidx[1], cidx[2], chip]
    q, rb, snd = [None] * 3, [None] * 3, [None] * 3
    after = []
    for pair in ((0, 1), (2, 3)):
        given = {}
        for k in pair:
            theirs = _grad_slot(jnp.stack([slots[k], 1 - pc]).astype(jnp.int32), h, dproj, after, ns,
                                _tile(hrow, 512), hc, f"grad_w_in{k}a")
            sems, bufs, tok = _start_copies(f"swap_in_start{k}", pairs(1), 1, [theirs, lax.empty((hrow, ns), F32)])
            given[k] = (sems, bufs)
            after = [tok]
        for k in pair:
            sems, bufs = given[k]
            _, from_sibling = _wait_copies(f"swap_in_wait{k}", pairs(1), bufs, sems, after=after)
            mine = _grad_slot(jnp.stack([slots[k], pc]).astype(jnp.int32), h, dproj, [], ns, _tile(hrow, 512), hc,
                              f"grad_w_in{k}b", add=from_sibling, out_dtype=BF16 if k < 3 else F32)
            if k < 3:
                snd[k], (q[k], rb[k]), tok = _start_copies(f"send_in_start{k}", _to_chip(k), 1,
                                                           [mine, lax.empty((hrow, ns), BF16)])
                after = [tok]
            else:
                own_half, after = mine, [mine]
    dh = _matmul_by_pieces(dproj, win_full, bm, bd, "dh", after=after)
    grad_x, dshift_p, dscale_p, gng_p = _prenorm_bwd(x2d, r, dh, dx2, norm_g, scale, _tile(s, 128))

    def rows_of(v):
        return _pad_rows(v.reshape(-1, wa), 8 * ((v.size // wa + 7) // 8))

    dmod = jnp.concatenate([dshift_p, dscale_p, dgate_p], axis=1)
    parts = [gng_p, dmod, dwa_p, dwb_p, dcb_p, dlng_p, dlnb_p, gfg_p,
             jnp.broadcast_to(loss_p[:, :1], (1, wa))]
    starts, packed = [], []
    for p in parts:
        starts.append(sum(q.shape[0] for q in packed))
        packed.append(rows_of(p) if p.shape[0] == 1 else p)
    small_sum, small_all = _gather_small(jnp.concatenate(packed, axis=0))

    def summed(k, rows):
        return small_sum[starts[k]:starts[k] + rows]

    grad_norm_g = summed(0, d // wa).reshape(1, d)
    grad_b_ada = summed(1, 3 * d // wa).reshape(1, 3 * d)
    grad_conv_a_full = summed(2, TAPS_A)
    grad_conv_b_full = summed(3, TAPS_B)
    grad_conv_b_b = summed(4, 1)
    grad_ln_b_g = summed(5, 1)
    grad_ln_b_b = summed(6, 1)
    grad_final_g = summed(7, d // wa).reshape(d)
    loss = summed(8, 1)[0, 0]
    grad_conv_a_w = lax.dynamic_slice(grad_conv_a_full, (0, chip * wsh), (TAPS_A, wsh))
    grad_conv_b_w = lax.dynamic_slice(grad_conv_b_full, (0, chip * wsh), (TAPS_B, wsh))
    dmod_all = small_all[:, starts[1]:starts[1] + 3 * d // wa, :].reshape(N_DEV, 3 * d)
    dmod_sh = lax.dynamic_slice(dmod_all, (0, chip * na), (N_DEV, na))

    def pairs_to_chips(b):
        x, y, cc = _position()
        chips, _ = _other_chips(x, y)
        return [(b[2 * k], b[2 * k + 1], (cx, cy, cc)) for k, (cx, cy) in enumerate(chips)]

    po_bufs = _wait_copies("send_out_wait", _slots_to_chips, po_bufs, po_sems, after=[small_sum])
    gh_out = _final_half(pos, g_wout3, ra_out, po_bufs[1:], _tile(ho, 256), "final_half_w_out")
    in_bufs = _wait_copies("send_in_wait", pairs_to_chips, [b for k in range(3) for b in (q[k], rb[k])],
                           snd[0] + snd[1] + snd[2], after=[small_sum])
    gh_in = _final_sum(pos, own_half, in_bufs[1::2], _tile(hrow, 256), "final_half_w_in")
    sh_sems, sh_bufs, sh_tok = _start_copies("share_start", _halves_to_sibling, 2, [gh_in, gh_out])

    grad_w_ada, d_wada, nm_wada, nv_wada = _adam_ada(c_rows.T, dmod_sh + sh_tok[0, 0], w_ada[0], m_w_ada[0],
                                                     v_w_ada[0], _tile(d, 128), "adam_w_ada")
    gw_in, gw_out = _wait_copies("share_wait", _halves_to_sibling, sh_bufs, sh_sems, after=[d_wada])
    grad_w_in, d_win, nm_win, nv_win = _adam(w_in[0], gw_in, m_w_in[0], v_w_in[0], _tile(d, 128), "adam_w_in",
                                             return_grad=True)
    grad_w_out, d_wout, nm_wout, nv_wout = _adam(w_out[0], gw_out, m_w_out[0], v_w_out[0], _tile(r4, 128),
                                                 "adam_w_out", return_grad=True)

    def small_adam(w, g, m, v, name):
        shape = w.shape
        w2 = w.reshape(-1, shape[-1])
        out = _adam(w2, g.reshape(w2.shape), m.reshape(w2.shape), v.reshape(w2.shape), w2.shape[0], name)
        return [o_.reshape(shape) for o_ in out]

    small = {
        "norm_g": small_adam(norm_g, grad_norm_g, m_norm_g, v_norm_g, "adam_norm_g"),
        "b_ada": small_adam(b_ada, grad_b_ada, m_b_ada, v_b_ada, "adam_b_ada"),
        "conv_a_w": small_adam(conv_a_w, grad_conv_a_w, m_conv_a_w, v_conv_a_w, "adam_conv_a_w"),
        "conv_b_w": small_adam(conv_b_w, grad_conv_b_w, m_conv_b_w, v_conv_b_w, "adam_conv_b_w"),
        "conv_b_b": small_adam(conv_b_b, grad_conv_b_b, m_conv_b_b, v_conv_b_b, "adam_conv_b_b"),
        "ln_b_g": small_adam(ln_b_g, grad_ln_b_g, m_ln_b_g, v_ln_b_g, "adam_ln_b_g"),
        "ln_b_b": small_adam(ln_b_b, grad_ln_b_b, m_ln_b_b, v_ln_b_b, "adam_ln_b_b"),
        "final_g": small_adam(final_g.reshape(1, d), grad_final_g, m_final_g.reshape(1, d),
                              v_final_g.reshape(1, d), "adam_final_g"),
    }
    small["final_g"] = [o_.reshape(d) for o_ in small["final_g"]]
    big = {
        "w_ada": [a[None] for a in (d_wada, nm_wada, nv_wada)],
        "w_in": [a[None] for a in (d_win, nm_win, nv_win)],
        "w_out": [a[None] for a in (d_wout, nm_wout, nv_wout)],
    }
    upd = {**small, **big}
    order = ["norm_g", "w_ada", "b_ada", "w_in", "conv_a_w", "conv_b_w", "conv_b_b", "ln_b_g", "ln_b_b",
             "w_out", "final_g"]
    grads = {
        "norm_g": grad_norm_g, "w_ada": grad_w_ada[None], "b_ada": grad_b_ada, "w_in": grad_w_in[None],
        "conv_a_w": grad_conv_a_w[None], "conv_b_w": grad_conv_b_w[None], "conv_b_b": grad_conv_b_b,
        "ln_b_g": grad_ln_b_g, "ln_b_b": grad_ln_b_b, "w_out": grad_w_out[None], "final_g": grad_final_g,
    }
    return (loss, grad_x.reshape(1, s, d), *[grads[n] for n in order], *[upd[n][0] for n in order],
            *[upd[n][1] for n in order], *[upd[n][2] for n in order])
```

```python
import functools

import jax
import jax.numpy as jnp
from jax import lax
from jax.experimental import pallas as pl
from jax.experimental.pallas import tpu as pltpu

F32 = jnp.float32
BF16 = jnp.bfloat16
EPS = 1e-6
N_CHIPS = 4
N_DEV = 8
TAPS_A = 3
TAPS_B = 31
HALO_A = 8
HALO_IN = 16
HALO_B = 32
LANES = 128
SUBLANES = 8
ADAM_LR = 0.001
ADAM_B1 = 0.9
ADAM_B2 = 0.999
ADAM_EPS = 1e-08
ADAM_WD = 0.01
ADAM_STEP = 10
VMEM_LIMIT = 56 * 1024 * 1024
MESH = pl.DeviceIdType.MESH
ANY = pl.BlockSpec(memory_space=pl.ANY)
VMEM = pl.BlockSpec(memory_space=pltpu.VMEM)
HBM_SPEC = pl.BlockSpec(memory_space=pltpu.HBM)
SEM_SPEC = pl.BlockSpec(memory_space=pltpu.SEMAPHORE)
EFFECT = pltpu.SideEffectType.DATAFLOW_SIDE_EFFECTING


def _params(sem=None):
    return pltpu.CompilerParams(dimension_semantics=sem, vmem_limit_bytes=VMEM_LIMIT)


def _sigmoid(v):
    return jax.nn.sigmoid(v)


def _position():
    return lax.axis_index("x"), lax.axis_index("y"), lax.axis_index("c")


def _rcopy(src, dst, ssem, rsem, dev):
    return pltpu.make_async_remote_copy(src_ref=src, dst_ref=dst, send_sem=ssem, recv_sem=rsem,
                                        device_id=dev, device_id_type=MESH)


def _other_chips(x, y):
    chips = [(1 - x, y), (x, 1 - y), (1 - x, 1 - y)]
    return chips, [2 * cx + cy for cx, cy in chips]


def _cast_bf16(a, rows, name):
    m, n = a.shape

    def body(a_ref, o_ref):
        o_ref[...] = a_ref[...].astype(BF16)

    return pl.pallas_call(
        body, name=name, grid=(m // rows,),
        in_specs=[pl.BlockSpec((rows, n), lambda i: (i, 0))],
        out_specs=pl.BlockSpec((rows, n), lambda i: (i, 0)),
        out_shape=jax.ShapeDtypeStruct((m, n), BF16),
        compiler_params=_params(("parallel",)),
    )(a)


def _cast_quarters(a, rows, name):
    m, n = a.shape
    hq = n // 4

    def body(a_ref, o_ref):
        o_ref[...] = a_ref[...].astype(BF16)

    return pl.pallas_call(
        body, name=name, grid=(4, m // rows),
        in_specs=[pl.BlockSpec((rows, hq), lambda q, i: (i, q))],
        out_specs=pl.BlockSpec((None, rows, hq), lambda q, i: (q, i, 0)),
        out_shape=jax.ShapeDtypeStruct((4, m, hq), BF16),
        compiler_params=_params(("parallel", "parallel")),
    )(a)


def _proj_piece(where, h, quarters, proj, w_all, din, n_pieces, bm, name, own_half=None):
    s, d = h.shape
    hq = quarters[0].shape[-1]
    nm = s // bm
    if own_half is None:
        q_specs = [pl.BlockSpec((d, hq), lambda i, p: (0, 0), pipeline_mode=pl.Buffered(1))] * 2
    else:
        q_specs = [pl.BlockSpec((None, d, hq), lambda i, p, k=k: (2 * own_half + k, 0, 0),
                                pipeline_mode=pl.Buffered(1)) for k in range(2)]

    def body(p_ref, h_ref, q0_ref, q1_ref, *rest):
        o_ref, wall_ref, wbuf, sem = rest[-4:]
        i = pl.program_id(0)
        filed = pltpu.make_async_copy(wbuf, wall_ref.at[p_ref[0]], sem)

        @pl.when(i == 0)
        def _():
            wbuf[:, 0:hq] = q0_ref[...]
            wbuf[:, hq:2 * hq] = q1_ref[...]
            filed.start()

        o_ref[...] = jnp.dot(h_ref[...], wbuf[...], preferred_element_type=F32).astype(BF16)

        @pl.when(i == nm - 1)
        def _():
            filed.wait()

    args, extra, alias = [where, h, quarters[0], quarters[1]], [], {}
    if proj is not None:
        args, extra, alias = args + [proj, w_all], [ANY, ANY], {4: 0, 5: 1}
    return pl.pallas_call(
        body, name=name,
        grid_spec=pltpu.PrefetchScalarGridSpec(
            num_scalar_prefetch=1, grid=(nm,),
            in_specs=[pl.BlockSpec((bm, d), lambda i, p: (i, 0))] + q_specs + extra,
            out_specs=[pl.BlockSpec((bm, 2 * hq), lambda i, p: (i, p[0])), ANY],
            scratch_shapes=[pltpu.VMEM((d, 2 * hq), BF16), pltpu.SemaphoreType.DMA]),
        out_shape=[jax.ShapeDtypeStruct((s, din), BF16), jax.ShapeDtypeStruct((n_pieces, d, 2 * hq), BF16)],
        input_output_aliases=alias,
        compiler_params=_params(("arbitrary",)),
    )(*args)


def _proj_quarter_pair(where, h, quarters, places, proj, w_all, bm, name):
    s, d = h.shape
    hq = quarters[0].shape[-1]
    nm = s // bm
    q_spec = pl.BlockSpec((d, hq), lambda j, i, w: (0, 0), pipeline_mode=pl.Buffered(1))

    def body(w_ref, h_ref, qa_ref, qb_ref, proj_in, wall_in, o_ref, wall_ref, sem):
        j, i = pl.program_id(0), pl.program_id(1)
        for k, q_ref in enumerate((qa_ref, qb_ref)):
            @pl.when(j == k)
            def _(k=k, q_ref=q_ref):
                o_ref[...] = jnp.dot(h_ref[...], q_ref[...], preferred_element_type=F32).astype(BF16)
                filed = pltpu.make_async_copy(
                    q_ref, wall_ref.at[w_ref[k], :, pl.ds(places[k] * hq, hq)], sem.at[k])

                @pl.when(i == 0)
                def _():
                    filed.start()

                @pl.when(i == nm - 1)
                def _():
                    filed.wait()

    return pl.pallas_call(
        body, name=name,
        grid_spec=pltpu.PrefetchScalarGridSpec(
            num_scalar_prefetch=1, grid=(2, nm),
            in_specs=[pl.BlockSpec((bm, d), lambda j, i, w: (i, 0)), q_spec, q_spec, ANY, ANY],
            out_specs=[pl.BlockSpec((bm, hq),
                                    lambda j, i, w: (i, 2 * w[j] + places[0] + j * (places[1] - places[0]))), ANY],
            scratch_shapes=[pltpu.SemaphoreType.DMA((2,))]),
        out_shape=[jax.ShapeDtypeStruct(proj.shape, proj.dtype), jax.ShapeDtypeStruct(w_all.shape, w_all.dtype)],
        input_output_aliases={4: 0, 5: 1},
        compiler_params=_params(("arbitrary", "arbitrary")),
    )(where, h, quarters[0], quarters[1], proj, w_all)


def _grad_slot(where, h, dproj, after, ns, bd, bn, name, add=None, out_dtype=F32):
    s, d = h.shape
    nb = ns // bn
    ni = d // 2 // bd
    extra = [] if add is None else [add]

    def body(where_ref, h_ref, dp_ref, *rest):
        acc = lax.dot_general(h_ref[...], dp_ref[...], (((0,), (0,)), ((), ())), preferred_element_type=F32)
        if add is not None:
            acc = acc + rest[0][...]
        rest[-1][...] = acc.astype(out_dtype)

    return pl.pallas_call(
        body, name=name,
        grid_spec=pltpu.PrefetchScalarGridSpec(
            num_scalar_prefetch=1, grid=(nb, ni),
            in_specs=[pl.BlockSpec((s, bd), lambda j, i, w: (0, w[1] * ni + i)),
                      pl.BlockSpec((s, bn), lambda j, i, w: (0, w[0] * nb + j))]
            + [pl.BlockSpec((bd, bn), lambda j, i, w: (i, j))] * len(extra) + [ANY] * len(after),
            out_specs=pl.BlockSpec((bd, bn), lambda j, i, w: (i, j))),
        out_shape=jax.ShapeDtypeStruct((d // 2, ns), out_dtype),
        compiler_params=_params(("parallel", "parallel")),
    )(where, h, dproj, *extra, *after)


def _matmul(a, b, *, grid, a_spec, b_spec, o_spec, out_shape, dims, name, after=()):
    nk = grid[2]
    n_after = len(after)

    def body(a_ref, b_ref, *rest):
        o_ref, acc = rest[n_after], rest[n_after + 1:]
        p = lax.dot_general(a_ref[...], b_ref[...], (dims, ((), ())), preferred_element_type=F32)
        if nk == 1:
            o_ref[...] = p.astype(o_ref.dtype)
        else:
            acc_ref, = acc
            k = pl.program_id(2)

            @pl.when(k == 0)
            def _():
                acc_ref[...] = p

            @pl.when(k > 0)
            def _():
                acc_ref[...] += p

            @pl.when(k == nk - 1)
            def _():
                o_ref[...] = acc_ref[...].astype(o_ref.dtype)

    block = [d for d in o_spec.block_shape if d is not None]
    scratch = [pltpu.VMEM(tuple(block), F32)] if nk > 1 else []
    return pl.pallas_call(
        body, name=name, grid=grid, in_specs=[a_spec, b_spec] + [ANY] * n_after, out_specs=o_spec,
        out_shape=out_shape, scratch_shapes=scratch,
        compiler_params=_params(("parallel", "parallel", "arbitrary")),
    )(a, b, *after)


def _matmul_by_pieces(a, pieces, bm, bn, name, after=()):
    s = a.shape[0]
    n, rows, width = pieces.shape
    nk = n // 2
    n_after = len(after)

    def body(a_ref, b_ref, *rest):
        o_ref, acc_ref = rest[n_after], rest[n_after + 1]
        k = pl.program_id(2)
        nt = (((1,), (1,)), ((), ()))
        p = (lax.dot_general(a_ref[:, 0:width], b_ref[0], nt, preferred_element_type=F32)
             + lax.dot_general(a_ref[:, width:2 * width], b_ref[1], nt, preferred_element_type=F32))

        @pl.when(k == 0)
        def _():
            acc_ref[...] = p

        @pl.when(k > 0)
        def _():
            acc_ref[...] += p

        @pl.when(k == nk - 1)
        def _():
            o_ref[...] = acc_ref[...]

    return pl.pallas_call(
        body, name=name, grid=(s // bm, rows // bn, nk),
        in_specs=[pl.BlockSpec((bm, 2 * width), lambda i, j, k: (i, k)),
                  pl.BlockSpec((2, bn, width), lambda i, j, k: (k, j, 0))] + [ANY] * n_after,
        out_specs=pl.BlockSpec((bm, bn), lambda i, j, k: (i, j)),
        out_shape=jax.ShapeDtypeStruct((s, rows), F32), scratch_shapes=[pltpu.VMEM((bm, bn), F32)],
        compiler_params=_params(("parallel", "parallel", "arbitrary")),
    )(a, pieces, *after)


def _adam_math(w, g, m, v):
    m = ADAM_B1 * m + (1.0 - ADAM_B1) * g
    v = ADAM_B2 * v + (1.0 - ADAM_B2) * (g * g)
    m_hat = m / (1.0 - ADAM_B1 ** ADAM_STEP)
    v_hat = v / (1.0 - ADAM_B2 ** ADAM_STEP)
    delta = -ADAM_LR * (m_hat / (jnp.sqrt(v_hat) + ADAM_EPS) + ADAM_WD * w)
    return delta, m, v


def _adam(w, g, m, v, rows, name, return_grad=False):
    r, n = w.shape

    def body(w_ref, g_ref, m_ref, v_ref, *out):
        gv = g_ref[...]
        d, mo, vo = _adam_math(w_ref[...], gv, m_ref[...], v_ref[...])
        for o_ref, val in zip(out, ([gv] if return_grad else []) + [d, mo, vo]):
            o_ref[...] = val

    spec = pl.BlockSpec((rows, n), lambda i: (i, 0))
    shape = jax.ShapeDtypeStruct((r, n), F32)
    n_out = 4 if return_grad else 3
    return pl.pallas_call(
        body, name=name, grid=(r // rows,), in_specs=[spec] * 4, out_specs=[spec] * n_out,
        out_shape=[shape] * n_out, compiler_params=_params(("parallel",)),
    )(w, g, m, v)


def _adam_ada(c_cols, dmod, w, m, v, rows, name):
    r, n = w.shape

    def body(c_ref, dm_ref, w_ref, m_ref, v_ref, g_ref, d_ref, mo_ref, vo_ref):
        cv = c_ref[...]
        c_act = cv * _sigmoid(cv)
        g = c_act[:, 0:1] * dm_ref[0:1, :]
        for b in range(1, N_DEV):
            g = g + c_act[:, b:b + 1] * dm_ref[b:b + 1, :]
        d, mo, vo = _adam_math(w_ref[...], g, m_ref[...], v_ref[...])
        g_ref[...] = g
        d_ref[...] = d
        mo_ref[...] = mo
        vo_ref[...] = vo

    spec = pl.BlockSpec((rows, n), lambda i: (i, 0))
    shape = jax.ShapeDtypeStruct((r, n), F32)
    return pl.pallas_call(
        body, name=name, grid=(r // rows,),
        in_specs=[pl.BlockSpec((rows, N_DEV), lambda i: (i, 0)), pl.BlockSpec((N_DEV, n), lambda i: (0, 0)),
                  spec, spec, spec],
        out_specs=[spec] * 4, out_shape=[shape] * 4, compiler_params=_params(("parallel",)),
    )(c_cols, dmod, w, m, v)


def _start_copies(name, plan, n, bufs, after=()):
    nb, na = len(bufs), len(after)

    def body(*refs):
        sems = refs[nb + na:nb + na + 2 * n]
        for k, (src, dst, dev) in enumerate(plan(refs[:nb])):
            _rcopy(src, dst, sems[2 * k], sems[2 * k + 1], dev).start()
        refs[-1][...] = jnp.zeros((8, 128), F32)

    outs = pl.pallas_call(
        body, name=name,
        out_shape=[pltpu.SemaphoreType.DMA(())] * (2 * n) + [pltpu.HBM(a.shape, a.dtype) for a in bufs]
        + [jax.ShapeDtypeStruct((8, 128), F32)],
        in_specs=[HBM_SPEC] * nb + [ANY] * na, out_specs=[SEM_SPEC] * (2 * n) + [HBM_SPEC] * nb + [VMEM],
        input_output_aliases={i: 2 * n + i for i in range(nb)},
        compiler_params=pltpu.CompilerParams(has_side_effects=EFFECT),
    )(*[pltpu.with_memory_space_constraint(a, pltpu.HBM) for a in bufs], *after)
    return list(outs[:2 * n]), list(outs[2 * n:2 * n + nb]), outs[-1]


def _wait_copies(name, plan, bufs, sems, after=(), send=True, recv=True):
    nb, nsem = len(bufs), len(sems)

    def body(*refs):
        s = refs[nb:nb + nsem]
        for k, (src, dst, dev) in enumerate(plan(refs[:nb])):
            cp = _rcopy(src, dst, s[2 * k], s[2 * k + 1], dev)
            if send:
                cp.wait_send()
            if recv:
                cp.wait_recv()

    outs = pl.pallas_call(
        body, name=name, out_shape=[pltpu.HBM(a.shape, a.dtype) for a in bufs],
        in_specs=[HBM_SPEC] * nb + [SEM_SPEC] * nsem + [ANY] * len(after), out_specs=[HBM_SPEC] * nb,
        input_output_aliases={i: i for i in range(nb)},
        compiler_params=pltpu.CompilerParams(has_side_effects=EFFECT),
    )(*bufs, *sems, *after)
    return list(outs)


def _to_sibling(views):
    def plan(b):
        x, y, c = _position()
        return [(view(b[2 * k], c), b[2 * k + 1], (x, y, 1 - c)) for k, view in enumerate(views)]
    return plan


def _to_chip(k):
    def plan(b):
        x, y, c = _position()
        cx, cy = _other_chips(x, y)[0][k]
        return [(b[0], b[1], (cx, cy, c))]
    return plan


def _slots_to_chips(b):
    x, y, c = _position()
    chips, cidx = _other_chips(x, y)
    return [(b[0].at[cidx[k]], b[1 + k], (cx, cy, c)) for k, (cx, cy) in enumerate(chips)]


def _halves_to_sibling(b):
    x, y, c = _position()
    views = [r.at[pl.ds(c * (r.shape[0] // 2), r.shape[0] // 2), :] for r in b]
    return [(v, v, (x, y, 1 - c)) for v in views]


def _landed(b):
    x, y, c = _position()
    return [(ref, ref, (x, y, c)) for ref in b]


def _assemble(name, pieces, out_shape, index_of):
    n = len(pieces)

    def body(*refs):
        out_ref, sem = refs[n], refs[n + 1]
        x, y, c = _position()
        _, cidx = _other_chips(x, y)
        cps = [pltpu.make_async_copy(refs[k], out_ref.at[index_of(k, 2 * x + y, c, cidx)], sem.at[k]) for k in range(n)]
        for cp in cps:
            cp.start()
        for cp in cps:
            cp.wait()

    return pl.pallas_call(
        body, name=name, in_specs=[VMEM] * n, out_specs=ANY, out_shape=out_shape,
        scratch_shapes=[pltpu.SemaphoreType.DMA((n,))],
        compiler_params=pltpu.CompilerParams(vmem_limit_bytes=VMEM_LIMIT),
    )(*pieces)


def _gather_cond(c8, cw):
    def body(c8_ref, cw_ref, call_ref, cwall_ref, ssem, rsem, lsem):
        x, y, c = _position()
        chip = 2 * x + y
        me = 4 * x + 2 * y + c
        chips, cidx = _other_chips(x, y)
        own = [pltpu.make_async_copy(c8_ref, call_ref.at[me], lsem.at[0]),
               pltpu.make_async_copy(cw_ref, cwall_ref.at[chip], lsem.at[1])]
        for cp in own:
            cp.start()
        sends = [_rcopy(cw_ref, cwall_ref.at[chip], ssem.at[k], rsem.at[k], (cx, cy, c))
                 for k, (cx, cy) in enumerate(chips)]
        for mask in range(1, N_DEV):
            fx, fy, fc = (mask >> 2) & 1, (mask >> 1) & 1, mask & 1
            dev = (1 - x if fx else x, 1 - y if fy else y, 1 - c if fc else c)
            sends.append(_rcopy(c8_ref, call_ref.at[me], ssem.at[2 + mask], rsem.at[2 + mask], dev))
        for cp in sends:
            cp.start()
        for k in range(3):
            slot = cwall_ref.at[cidx[k]]
            _rcopy(slot, slot, ssem.at[k], rsem.at[k], (x, y, c)).wait_recv()
        for mask in range(1, N_DEV):
            slot = call_ref.at[jnp.bitwise_xor(me, mask)]
            _rcopy(slot, slot, ssem.at[2 + mask], rsem.at[2 + mask], (x, y, c)).wait_recv()
        for cp in sends:
            cp.wait_send()
        for cp in own:
            cp.wait()

    return pl.pallas_call(
        body, name="gather_cond", in_specs=[VMEM, VMEM], out_specs=[VMEM, VMEM],
        out_shape=[jax.ShapeDtypeStruct((N_DEV,) + c8.shape, F32), jax.ShapeDtypeStruct((N_CHIPS,) + cw.shape, F32)],
        scratch_shapes=[pltpu.SemaphoreType.DMA((10,)), pltpu.SemaphoreType.DMA((10,)), pltpu.SemaphoreType.DMA((2,))],
    )(c8, cw)


def _exchange_mod(mod_part):
    def body(mp_ref, out_ref, ssem, rsem, lsem):
        x, y, c = _position()
        chip = 2 * x + y
        chips, cidx = _other_chips(x, y)
        own = pltpu.make_async_copy(mp_ref, out_ref.at[chip], lsem)
        own.start()
        sends = [_rcopy(mp_ref, out_ref.at[chip], ssem.at[k], rsem.at[k], (cx, cy, c))
                 for k, (cx, cy) in enumerate(chips)]
        for cp in sends:
            cp.start()
        for k in range(3):
            slot = out_ref.at[cidx[k]]
            _rcopy(slot, slot, ssem.at[k], rsem.at[k], (x, y, c)).wait_recv()
        for cp in sends:
            cp.wait_send()
        own.wait()

    return pl.pallas_call(
        body, name="exchange_mod", in_specs=[VMEM], out_specs=VMEM,
        out_shape=jax.ShapeDtypeStruct((N_CHIPS,) + mod_part.shape, F32),
        scratch_shapes=[pltpu.SemaphoreType.DMA((3,)), pltpu.SemaphoreType.DMA((3,)), pltpu.SemaphoreType.DMA],
    )(mod_part)


def _gather_small(pack):
    rows, n = pack.shape

    def body(p_ref, sum_ref, all_ref, ssem, rsem, lsem):
        x, y, c = _position()
        me = 4 * x + 2 * y + c
        sib = (x, y, 1 - c)
        chips, cidx = _other_chips(x, y)
        own = pltpu.make_async_copy(p_ref, all_ref.at[me], lsem)
        own.start()
        sends = [_rcopy(p_ref, all_ref.at[me], ssem.at[0], rsem.at[0], sib)]
        sends += [_rcopy(p_ref, all_ref.at[me], ssem.at[1 + k], rsem.at[1 + k], (cx, cy, c))
                  for k, (cx, cy) in enumerate(chips)]
        for cp in sends:
            cp.start()
        for k in range(3):
            slot = all_ref.at[2 * cidx[k] + c]
            _rcopy(slot, slot, ssem.at[1 + k], rsem.at[1 + k], sib).wait_recv()
            fw = _rcopy(slot, slot, ssem.at[4 + k], rsem.at[4 + k], sib)
            fw.start()
            sends.append(fw)
        slot = all_ref.at[jnp.bitwise_xor(me, 1)]
        _rcopy(slot, slot, ssem.at[0], rsem.at[0], sib).wait_recv()
        for k in range(3):
            slot = all_ref.at[2 * cidx[k] + 1 - c]
            _rcopy(slot, slot, ssem.at[4 + k], rsem.at[4 + k], sib).wait_recv()
        for cp in sends:
            cp.wait_send()
        own.wait()
        acc = all_ref[0]
        for k in range(1, N_DEV):
            acc = acc + all_ref[k]
        sum_ref[...] = acc

    return pl.pallas_call(
        body, name="gather_small", in_specs=[VMEM], out_specs=[VMEM, VMEM],
        out_shape=[jax.ShapeDtypeStruct((rows, n), F32), jax.ShapeDtypeStruct((N_DEV, rows, n), F32)],
        scratch_shapes=[pltpu.SemaphoreType.DMA((7,)), pltpu.SemaphoreType.DMA((7,)), pltpu.SemaphoreType.DMA],
        compiler_params=pltpu.CompilerParams(vmem_limit_bytes=VMEM_LIMIT),
    )(pack)


def _chip_partial(pos, g, recv, rows, name):
    ns, full, n = g.shape
    h = full // 2
    nb = h // rows

    def body(pos_ref, g_ref, r_ref, o_ref):
        o_ref[...] = (g_ref[...] + r_ref[...]).astype(BF16)

    return pl.pallas_call(
        body, name=name,
        grid_spec=pltpu.PrefetchScalarGridSpec(
            num_scalar_prefetch=1, grid=(ns, nb),
            in_specs=[pl.BlockSpec((None, rows, n), lambda s, i, p: (s, p[1] * nb + i, 0)),
                      pl.BlockSpec((None, rows, n), lambda s, i, p: (s, i, 0))],
            out_specs=pl.BlockSpec((None, rows, n), lambda s, i, p: (s, i, 0))),
        out_shape=jax.ShapeDtypeStruct((ns, h, n), BF16),
        compiler_params=_params(("parallel", "parallel")),
    )(pos, g, recv)


def _final_sum(pos, first, parts, rows, name):
    h, n = first.shape
    nb = h // rows

    def body(pos_ref, f_ref, rb0_ref, rb1_ref, rb2_ref, o_ref):
        acc = f_ref[...]
        for rb_ref in (rb0_ref, rb1_ref, rb2_ref):
            acc = acc + rb_ref[...].astype(F32)
        o_ref[...] = acc

    part = pl.BlockSpec((rows, n), lambda i, p: (i, 0))
    return pl.pallas_call(
        body, name=name,
        grid_spec=pltpu.PrefetchScalarGridSpec(
            num_scalar_prefetch=1, grid=(nb,), in_specs=[part] * 4,
            out_specs=pl.BlockSpec((rows, n), lambda i, p: (p[1] * nb + i, 0))),
        out_shape=jax.ShapeDtypeStruct((2 * h, n), F32),
        compiler_params=_params(("parallel",)),
    )(pos, first, *parts)


def _final_half(pos, g, recv_a, recv_b, rows, name):
    ns, full, n = g.shape
    h = full // 2
    nb = h // rows

    def body(pos_ref, g_ref, ra_ref, rb0_ref, rb1_ref, rb2_ref, o_ref):
        acc = g_ref[...] + ra_ref[...]
        for rb_ref in (rb0_ref, rb1_ref, rb2_ref):
            acc = acc + rb_ref[...].astype(F32)
        o_ref[...] = acc

    part = pl.BlockSpec((rows, n), lambda i, p: (i, 0))
    return pl.pallas_call(
        body, name=name,
        grid_spec=pltpu.PrefetchScalarGridSpec(
            num_scalar_prefetch=1, grid=(nb,),
            in_specs=[pl.BlockSpec((None, rows, n), lambda i, p: (p[0], p[1] * nb + i, 0)),
                      pl.BlockSpec((None, rows, n), lambda i, p: (p[0], i, 0)), part, part, part],
            out_specs=pl.BlockSpec((rows, n), lambda i, p: (p[1] * nb + i, 0))),
        out_shape=jax.ShapeDtypeStruct((full, n), F32),
        compiler_params=_params(("parallel",)),
    )(pos, g, recv_a, *recv_b)


def _modulation(c_rows, w_ada, b_ada, cols, name):
    d, n = w_ada.shape
    rows = c_rows.shape[0]

    def body(c_ref, w_ref, b_ref, o_ref):
        cv = c_ref[...]
        c_act = (cv * _sigmoid(cv)).astype(BF16)
        o_ref[...] = jnp.dot(c_act, w_ref[...].astype(BF16), preferred_element_type=F32) + b_ref[...]

    return pl.pallas_call(
        body, name=name, grid=(n // cols,),
        in_specs=[pl.BlockSpec((rows, d), lambda j: (0, 0)), pl.BlockSpec((d, cols), lambda j: (0, j)),
                  pl.BlockSpec((1, cols), lambda j: (0, j))],
        out_specs=pl.BlockSpec((rows, cols), lambda j: (0, j)),
        out_shape=jax.ShapeDtypeStruct((rows, n), F32),
        compiler_params=_params(("parallel",)),
    )(c_rows, w_ada, b_ada)


def _prenorm(x, norm_g, scale, shift, rows):
    s, d = x.shape

    def body(x_ref, g_ref, sc_ref, sh_ref, h_ref, r_ref):
        xv = x_ref[...]
        r = lax.rsqrt(jnp.mean(xv * xv, axis=-1, keepdims=True) + EPS)
        h = (xv * r * g_ref[...]) * (1.0 + sc_ref[...]) + sh_ref[...]
        h_ref[...] = h.astype(BF16)
        r_ref[...] = r

    vec = pl.BlockSpec((1, d), lambda i: (0, 0))
    return pl.pallas_call(
        body, name="prenorm", grid=(s // rows,),
        in_specs=[pl.BlockSpec((rows, d), lambda i: (i, 0)), vec, vec, vec],
        out_specs=[pl.BlockSpec((rows, d), lambda i: (i, 0)), pl.BlockSpec((rows, 1), lambda i: (i, 0))],
        out_shape=[jax.ShapeDtypeStruct((s, d), BF16), jax.ShapeDtypeStruct((s, 1), F32)],
        compiler_params=_params(("parallel",)),
    )(x, norm_g, scale, shift)


def _mixer_a_fwd(proj, conv_w, wa, rows, cols):
    s = proj.shape[0]
    ncb = wa // cols

    def body(ab_ref, ac_ref, ax_ref, az_ref, w_ref, y_ref, qbuf):
        t = pl.program_id(1)

        @pl.when(t == 0)
        def _():
            qbuf[0:HALO_A, :] = jnp.zeros((HALO_A, cols), F32)

        q = ac_ref[...].astype(F32) * ax_ref[...].astype(F32)
        qbuf[HALO_A:HALO_A + rows, :] = q
        conv = w_ref[2:3, :] * q
        for k in range(TAPS_A - 1):
            off = HALO_A - (TAPS_A - 1) + k
            conv = conv + w_ref[k:k + 1, :] * qbuf[off:off + rows, :]
        zv = az_ref[...].astype(F32)
        y_ref[...] = (ab_ref[...].astype(F32) * conv * (zv * _sigmoid(zv))).astype(BF16)
        qbuf[0:HALO_A, :] = qbuf[rows:rows + HALO_A, :]

    def sec(k):
        return pl.BlockSpec((rows, cols), lambda cb, t, k=k: (t, k * ncb + cb))

    return pl.pallas_call(
        body, name="mixer_a_fwd", grid=(ncb, s // rows),
        in_specs=[sec(0), sec(1), sec(2), sec(3), pl.BlockSpec((HALO_A, cols), lambda cb, t: (0, cb))],
        out_specs=pl.BlockSpec((rows, cols), lambda cb, t: (t, cb)),
        out_shape=jax.ShapeDtypeStruct((s, 2 * wa), BF16),
        scratch_shapes=[pltpu.VMEM((HALO_A + rows, cols), F32)],
        compiler_params=_params(("parallel", "arbitrary")),
    )(proj, proj, proj, proj, conv_w)


def _shifted_back(dst, src, lo, hi, cs):
    for n in range(1, 8):
        dst[n, lo:hi, :] = src[lo - n:hi - n, cs]


def _shifted_fwd(dst, src, lo, hi, cs):
    for n in range(1, 8):
        dst[n, lo:hi, :] = src[lo + n:hi + n, cs]


def _shift_rows(shifted, plain, n, start, size, cs):
    return plain[pl.ds(start, size), cs] if n == 0 else shifted[n, pl.ds(start, size), :]


def _mixer_b_conv_fwd(proj, conv_w, conv_b, wa, rows, cols, chunk):
    s = proj.shape[0]
    wb = conv_w.shape[1]
    ncb = wb // cols
    sec0 = 4 * wa // cols

    def body(bv_ref, bg_ref, w_ref, b_ref, u0_ref, u_ref, ubuf, sh):
        t = pl.program_id(1)

        @pl.when(t == 0)
        def _():
            ubuf[0:HALO_B, :] = jnp.zeros((HALO_B, cols), F32)

        u0 = bv_ref[...].astype(F32) * _sigmoid(bg_ref[...].astype(F32))
        u0_ref[...] = u0
        ubuf[HALO_B:HALO_B + rows, :] = u0
        for lc in range(cols // LANES):
            cs = slice(lc * LANES, (lc + 1) * LANES)
            _shifted_back(sh, ubuf, 8, HALO_B + rows, cs)
            taps = [w_ref[k:k + 1, cs] for k in range(TAPS_B)]
            bias = b_ref[:, cs]

            def row_chunk(rc, carry, cs=cs, taps=taps, bias=bias):
                base = pl.multiple_of(rc * chunk, chunk)
                acc = jnp.zeros((chunk, LANES), F32)
                for k in range(TAPS_B):
                    mq, n = divmod(TAPS_B - 1 - k, 8)
                    acc = acc + taps[k] * _shift_rows(sh, ubuf, n, HALO_B - 8 * mq + base, chunk, cs)
                u_ref[pl.ds(base, chunk), cs] = acc + bias
                return carry

            lax.fori_loop(0, rows // chunk, row_chunk, 0)
        ubuf[0:HALO_B, :] = ubuf[rows:rows + HALO_B, :]

    return pl.pallas_call(
        body, name="mixer_b_conv_fwd", grid=(ncb, s // rows),
        in_specs=[pl.BlockSpec((rows, cols), lambda cb, t: (t, sec0 + cb)),
                  pl.BlockSpec((rows, cols), lambda cb, t: (t, sec0 + ncb + cb)),
                  pl.BlockSpec((HALO_B, cols), lambda cb, t: (0, cb)),
                  pl.BlockSpec((1, cols), lambda cb, t: (0, cb))],
        out_specs=[pl.BlockSpec((rows, cols), lambda cb, t: (t, cb))] * 2,
        out_shape=[jax.ShapeDtypeStruct((s, wb), F32)] * 2,
        scratch_shapes=[pltpu.VMEM((HALO_B + rows, cols), F32), pltpu.VMEM((8, HALO_B + rows, LANES), F32)],
        compiler_params=_params(("parallel", "arbitrary")),
    )(proj, proj, conv_w, conv_b)


def _layernorm_stats(u):
    mu = jnp.mean(u, axis=-1, keepdims=True)
    xc = u - mu
    var = jnp.mean(xc * xc, axis=-1, keepdims=True)
    return xc * lax.rsqrt(var + EPS), lax.rsqrt(var + EPS)


def _mixer_b_gate_fwd(y, u, proj, ln_g, ln_b, wa, rows):
    s, wb = u.shape
    sec_z = (4 * wa + 2 * wb) // wb

    def body(y_in, u_ref, bz_ref, g_ref, b_ref, y_ref):
        uh, _ = _layernorm_stats(u_ref[...])
        ln = uh * g_ref[...] + b_ref[...]
        zv = bz_ref[...].astype(F32)
        y_ref[...] = ((ln * _sigmoid(ln)) * (zv * _sigmoid(zv))).astype(BF16)

    vec = pl.BlockSpec((1, wb), lambda i: (0, 0))
    return pl.pallas_call(
        body, name="mixer_b_gate_fwd", grid=(s // rows,),
        in_specs=[ANY, pl.BlockSpec((rows, wb), lambda i: (i, 0)), pl.BlockSpec((rows, wb), lambda i: (i, sec_z)),
                  vec, vec],
        out_specs=pl.BlockSpec((rows, wb), lambda i: (i, wa // wb)),
        out_shape=jax.ShapeDtypeStruct(y.shape, BF16), input_output_aliases={0: 0},
        compiler_params=_params(("parallel",)),
    )(y, u, proj, ln_g, ln_b)


def _loss_head(x, o, target, gate, final_g, rows):
    s, d = x.shape

    def body(x_ref, o_ref, t_ref, gate_ref, fg_ref, dx2_ref, do_ref, loss_ref, gfg_ref, dgate_ref):
        i = pl.program_id(0)
        ov = o_ref[...]
        x2 = x_ref[...] + gate_ref[...] * ov
        r2 = lax.rsqrt(jnp.mean(x2 * x2, axis=-1, keepdims=True) + EPS)
        xn2 = x2 * r2
        diff = xn2 * fg_ref[...] - t_ref[...]
        dout = diff * (1.0 / d)
        dxn2 = dout * fg_ref[...]
        dx2 = r2 * (dxn2 - xn2 * jnp.mean(dxn2 * xn2, axis=-1, keepdims=True))
        dx2_ref[...] = dx2
        do_ref[...] = (gate_ref[...] * dx2).astype(BF16)
        loss_part = 0.5 * jnp.sum(jnp.mean(diff * diff, axis=-1, keepdims=True), axis=0, keepdims=True)
        gfg_part = jnp.sum(dout * xn2, axis=0, keepdims=True)
        dgate_part = jnp.sum(dx2 * ov, axis=0, keepdims=True)

        @pl.when(i == 0)
        def _():
            loss_ref[...] = jnp.zeros_like(loss_ref)
            gfg_ref[...] = jnp.zeros_like(gfg_ref)
            dgate_ref[...] = jnp.zeros_like(dgate_ref)

        loss_ref[...] += jnp.broadcast_to(loss_part, loss_ref.shape)
        gfg_ref[...] += gfg_part
        dgate_ref[...] += dgate_part

    blk = pl.BlockSpec((rows, d), lambda i: (i, 0))
    vec = pl.BlockSpec((1, d), lambda i: (0, 0))
    return pl.pallas_call(
        body, name="loss_head", grid=(s // rows,),
        in_specs=[blk, blk, blk, vec, vec],
        out_specs=[blk, blk, pl.BlockSpec((1, 128), lambda i: (0, 0)), vec, vec],
        out_shape=[jax.ShapeDtypeStruct((s, d), F32), jax.ShapeDtypeStruct((s, d), BF16),
                   jax.ShapeDtypeStruct((1, 128), F32), jax.ShapeDtypeStruct((1, d), F32),
                   jax.ShapeDtypeStruct((1, d), F32)],
        compiler_params=_params(("arbitrary",)),
    )(x, o, target, gate, final_g)


def _mixer_a_bwd(proj, dy, conv_w, wa, din, rows):
    s = proj.shape[0]
    nt = s // rows
    per_halo = rows // HALO_IN

    def body(ab_ref, ac_ref, ax_ref, az_ref, hc_ref, hx_ref, dy_ref, w_ref, dp_ref, dw_ref, qbuf, dbuf):
        i = pl.program_id(0)

        @pl.when(i == 0)
        def _():
            dbuf[rows:rows + HALO_A, :] = jnp.zeros((HALO_A, wa), F32)
            dw_ref[...] = jnp.zeros_like(dw_ref)

        keep = jnp.where(i == nt - 1, 0.0, 1.0)
        before = hc_ref[...].astype(F32) * hx_ref[...].astype(F32) * keep
        qbuf[0:HALO_A, :] = before[HALO_IN - HALO_A:HALO_IN, :]
        acv, axv = ac_ref[...].astype(F32), ax_ref[...].astype(F32)
        q = acv * axv
        qbuf[HALO_A:HALO_A + rows, :] = q
        conv = w_ref[2:3, :] * q
        for k in range(TAPS_A - 1):
            off = HALO_A - (TAPS_A - 1) + k
            conv = conv + w_ref[k:k + 1, :] * qbuf[off:off + rows, :]
        zv, abv, dyv = az_ref[...].astype(F32), ab_ref[...].astype(F32), dy_ref[...]
        sg = _sigmoid(zv)
        sz = zv * sg
        dp_ref[:, 0:wa] = (dyv * conv * sz).astype(BF16)
        dp_ref[:, 3 * wa:4 * wa] = (dyv * abv * conv * (sg * (1.0 + zv * (1.0 - sg)))).astype(BF16)
        dconv = dyv * abv * sz
        dbuf[0:rows, :] = dconv
        dq = w_ref[2:3, :] * dconv
        for k in range(TAPS_A - 1):
            off = TAPS_A - 1 - k
            dq = dq + w_ref[k:k + 1, :] * dbuf[off:off + rows, :]
        dp_ref[:, wa:2 * wa] = (dq * axv).astype(BF16)
        dp_ref[:, 2 * wa:3 * wa] = (dq * acv).astype(BF16)
        for k in range(TAPS_A):
            off = HALO_A - (TAPS_A - 1) + k
            dw_ref[k:k + 1, :] += jnp.sum(dconv * qbuf[off:off + rows, :], axis=0, keepdims=True)
        dbuf[rows:rows + HALO_A, :] = dbuf[0:HALO_A, :]

    def sec(k):
        return pl.BlockSpec((rows, wa), lambda i, k=k: (nt - 1 - i, k))

    def halo(k):
        return pl.BlockSpec((HALO_IN, wa), lambda i, k=k: (jnp.maximum((nt - 1 - i) * per_halo - 1, 0), k))

    return pl.pallas_call(
        body, name="mixer_a_bwd", grid=(nt,),
        in_specs=[sec(0), sec(1), sec(2), sec(3), halo(1), halo(2),
                  pl.BlockSpec((rows, wa), lambda i: (nt - 1 - i, 0)),
                  pl.BlockSpec((HALO_A, wa), lambda i: (0, 0))],
        out_specs=[pl.BlockSpec((rows, 4 * wa), lambda i: (nt - 1 - i, 0)),
                   pl.BlockSpec((HALO_A, wa), lambda i: (0, 0))],
        out_shape=[jax.ShapeDtypeStruct((s, din), BF16), jax.ShapeDtypeStruct((HALO_A, wa), F32)],
        scratch_shapes=[pltpu.VMEM((HALO_A + rows, wa), F32), pltpu.VMEM((rows + HALO_A, wa), F32)],
        compiler_params=_params(("arbitrary",)),
    )(proj, proj, proj, proj, proj, proj, dy, conv_w)


def _mixer_b_gate_bwd(dproj, dy, u, proj, ln_g, ln_b, wa, rows):
    s, wb = u.shape
    sec_z = (4 * wa + 2 * wb) // wb

    def body(dp_in, dy_ref, u_ref, bz_ref, g_ref, b_ref, dp_ref, du_ref, dg_ref, db_ref, dcb_ref):
        i = pl.program_id(0)
        uh, rs = _layernorm_stats(u_ref[...])
        ln = uh * g_ref[...] + b_ref[...]
        sl = _sigmoid(ln)
        zv = bz_ref[...].astype(F32)
        sg = _sigmoid(zv)
        dyv = dy_ref[...]
        dp_ref[...] = (dyv * (ln * sl) * (sg * (1.0 + zv * (1.0 - sg)))).astype(BF16)
        dln = dyv * (zv * sg) * (sl * (1.0 + ln * (1.0 - sl)))
        duh = dln * g_ref[...]
        du = rs * (duh - jnp.mean(duh, axis=-1, keepdims=True) - uh * jnp.mean(duh * uh, axis=-1, keepdims=True))
        du_ref[...] = du

        @pl.when(i == 0)
        def _():
            dg_ref[...] = jnp.zeros_like(dg_ref)
            db_ref[...] = jnp.zeros_like(db_ref)
            dcb_ref[...] = jnp.zeros_like(dcb_ref)

        dg_ref[...] += jnp.sum(dln * uh, axis=0, keepdims=True)
        db_ref[...] += jnp.sum(dln, axis=0, keepdims=True)
        dcb_ref[...] += jnp.sum(du, axis=0, keepdims=True)

    blk = pl.BlockSpec((rows, wb), lambda i: (i, 0))
    vec = pl.BlockSpec((1, wb), lambda i: (0, 0))
    vshape = jax.ShapeDtypeStruct((1, wb), F32)
    return pl.pallas_call(
        body, name="mixer_b_gate_bwd", grid=(s // rows,),
        in_specs=[ANY, pl.BlockSpec((rows, wb), lambda i: (i, wa // wb)), blk,
                  pl.BlockSpec((rows, wb), lambda i: (i, sec_z)), vec, vec],
        out_specs=[pl.BlockSpec((rows, wb), lambda i: (i, sec_z)), blk, vec, vec, vec],
        out_shape=[jax.ShapeDtypeStruct(dproj.shape, BF16), jax.ShapeDtypeStruct((s, wb), F32), vshape, vshape, vshape],
        input_output_aliases={0: 0},
        compiler_params=_params(("arbitrary",)),
    )(dproj, dy, u, proj, ln_g, ln_b)


def _mixer_b_conv_bwd(dproj, du, u0, proj, conv_w, wa, rows, chunk):
    s, wb = du.shape
    nt = s // rows
    per32 = rows // HALO_B
    sec_v = 4 * wa // wb
    nrc = rows // chunk

    def body(dp_in, du_ref, u0_ref, h0_ref, bv_ref, bg_ref, w_ref, dp_ref, dw_ref, ubuf, dbuf, sh, shf, dwacc):
        i = pl.program_id(0)

        @pl.when(i == 0)
        def _():
            dbuf[rows:rows + HALO_B, :] = jnp.zeros((HALO_B, wb), F32)
            dwacc[...] = jnp.zeros_like(dwacc)

        ubuf[0:HALO_B, :] = h0_ref[...] * jnp.where(i == nt - 1, 0.0, 1.0)
        ubuf[HALO_B:HALO_B + rows, :] = u0_ref[...]
        dbuf[0:rows, :] = du_ref[...]
        for lc in range(wb // LANES):
            cs = slice(lc * LANES, (lc + 1) * LANES)
            _shifted_back(sh, ubuf, 8, HALO_B + rows, cs)
            _shifted_fwd(shf, dbuf, 0, rows + HALO_B - 8, cs)
            taps = [w_ref[k:k + 1, cs] for k in range(TAPS_B)]

            def conv_rows(rc, c0, cs=cs, taps=taps, lc=lc):
                base = pl.multiple_of(rc * chunk, chunk)
                acc = jnp.zeros((chunk, LANES), F32)
                for k in range(TAPS_B):
                    mq, n = divmod(TAPS_B - 1 - k, 8)
                    acc = acc + taps[k] * _shift_rows(shf, dbuf, n, base + 8 * mq, chunk, cs)
                sg = _sigmoid(bg_ref[pl.ds(base, chunk), cs].astype(F32))
                bv = bv_ref[pl.ds(base, chunk), cs].astype(F32)
                dp_ref[pl.ds(base, chunk), cs] = (acc * sg).astype(BF16)
                dp_ref[pl.ds(base, chunk), wb + lc * LANES:wb + (lc + 1) * LANES] = (
                    acc * bv * sg * (1.0 - sg)).astype(BF16)
                return c0

            lax.fori_loop(0, nrc, conv_rows, 0)

            def dw_rows(rc, accs, cs=cs):
                base = pl.multiple_of(rc * chunk, chunk)
                du_c = dbuf[pl.ds(base, chunk), cs]
                out = []
                for k in range(TAPS_B):
                    mq, n = divmod(TAPS_B - 1 - k, 8)
                    prod = du_c * _shift_rows(sh, ubuf, n, HALO_B - 8 * mq + base, chunk, cs)
                    out.append(accs[k] + jnp.sum(prod.reshape(chunk // SUBLANES, SUBLANES, LANES), axis=0))
                return tuple(out)

            accs = lax.fori_loop(0, nrc, dw_rows, tuple(jnp.zeros((SUBLANES, LANES), F32) for _ in range(TAPS_B)))
            for k in range(TAPS_B):
                dwacc[k * SUBLANES:(k + 1) * SUBLANES, cs] += accs[k]
        dbuf[rows:rows + HALO_B, :] = dbuf[0:HALO_B, :]

        @pl.when(i == nt - 1)
        def _():
            for k in range(HALO_B):
                dw_ref[k:k + 1, :] = jnp.sum(dwacc[k * SUBLANES:(k + 1) * SUBLANES, :], axis=0, keepdims=True)

    def rev(cols_blk):
        return pl.BlockSpec((rows, wb), lambda i, cb=cols_blk: (nt - 1 - i, cb))

    return pl.pallas_call(
        body, name="mixer_b_conv_bwd", grid=(nt,),
        in_specs=[ANY, rev(0), rev(0),
                  pl.BlockSpec((HALO_B, wb), lambda i: (jnp.maximum((nt - 1 - i) * per32 - 1, 0), 0)),
                  rev(sec_v), rev(sec_v + 1), pl.BlockSpec((HALO_B, wb), lambda i: (0, 0))],
        out_specs=[pl.BlockSpec((rows, 2 * wb), lambda i: (nt - 1 - i, sec_v // 2)),
                   pl.BlockSpec((HALO_B, wb), lambda i: (0, 0))],
        out_shape=[jax.ShapeDtypeStruct(dproj.shape, BF16), jax.ShapeDtypeStruct((HALO_B, wb), F32)],
        input_output_aliases={0: 0},
        scratch_shapes=[pltpu.VMEM((HALO_B + rows, wb), F32), pltpu.VMEM((rows + HALO_B, wb), F32),
                        pltpu.VMEM((8, HALO_B + rows, LANES), F32), pltpu.VMEM((8, rows + HALO_B, LANES), F32),
                        pltpu.VMEM((HALO_B * SUBLANES, wb), F32)],
        compiler_params=_params(("arbitrary",)),
    )(dproj, du, u0, u0, proj, proj, conv_w)


def _prenorm_bwd(x, r, dh, dx2, norm_g, scale, rows):
    s, d = x.shape

    def body(x_ref, r_ref, dh_ref, dx2_ref, g_ref, sc_ref, gx_ref, dsh_ref, dsc_ref, dg_ref):
        i = pl.program_id(0)
        rv = r_ref[...]
        xn = x_ref[...] * rv
        dhv = dh_ref[...]
        one_sc = 1.0 + sc_ref[...]
        dxn = dhv * one_sc * g_ref[...]
        gx_ref[...] = dx2_ref[...] + rv * (dxn - xn * jnp.mean(dxn * xn, axis=-1, keepdims=True))

        @pl.when(i == 0)
        def _():
            dsh_ref[...] = jnp.zeros_like(dsh_ref)
            dsc_ref[...] = jnp.zeros_like(dsc_ref)
            dg_ref[...] = jnp.zeros_like(dg_ref)

        dsh_ref[...] += jnp.sum(dhv, axis=0, keepdims=True)
        dsc_ref[...] += jnp.sum(dhv * (xn * g_ref[...]), axis=0, keepdims=True)
        dg_ref[...] += jnp.sum(dhv * one_sc * xn, axis=0, keepdims=True)

    blk = pl.BlockSpec((rows, d), lambda i: (i, 0))
    vec = pl.BlockSpec((1, d), lambda i: (0, 0))
    vshape = jax.ShapeDtypeStruct((1, d), F32)
    return pl.pallas_call(
        body, name="prenorm_bwd", grid=(s // rows,),
        in_specs=[blk, pl.BlockSpec((rows, 1), lambda i: (i, 0)), blk, blk, vec, vec],
        out_specs=[blk, vec, vec, vec],
        out_shape=[jax.ShapeDtypeStruct((s, d), F32), vshape, vshape, vshape],
        compiler_params=_params(("arbitrary",)),
    )(x, r, dh, dx2, norm_g, scale)


def _pad_rows(a, rows):
    return jnp.pad(a, ((0, rows - a.shape[0]), (0, 0)))


def _tile(n, want):
    t = min(n, want)
    while n % t:
        t -= 1
    return t


def kernel(x, c, norm_g, w_ada, b_ada, w_in, conv_a_w, conv_b_w, conv_b_b, ln_b_g, ln_b_b, w_out, final_g, loss_target, m_norm_g, m_w_ada, m_b_ada, m_w_in, m_conv_a_w, m_conv_b_w, m_conv_b_b, m_ln_b_g, m_ln_b_b, m_w_out, m_final_g, v_norm_g, v_w_ada, v_b_ada, v_w_in, v_conv_a_w, v_conv_b_w, v_conv_b_b, v_ln_b_g, v_ln_b_b, v_w_out, v_final_g):
    s, d = x.shape[1], x.shape[2]
    wa = conv_b_b.shape[-1]
    dmix = 2 * wa
    ns = w_in.shape[-1]
    din = N_CHIPS * ns
    r4 = w_out.shape[1]
    na = w_ada.shape[-1]
    wsh = conv_a_w.shape[-1]
    px, py, pc = _position()
    chip = 2 * px + py
    me = 4 * px + 2 * py + pc
    pos = jnp.stack([chip, pc]).astype(jnp.int32)
    x2d = x.reshape(s, d)
    target = loss_target.reshape(s, d)

    hc, ho, hrow = ns // 2, r4 // 2, d // 2
    _, cidx = _other_chips(px, py)
    hq = hc // 2
    win4 = _cast_quarters(w_in[0], _tile(d, 512), "cast_w_in")
    wout_bf = _cast_bf16(w_out[0], _tile(r4, 512), "cast_w_out")

    def gather_plan(b):
        x, y, cc = _position()
        xn, yn = (1 - x, y, cc), (x, 1 - y, cc)
        q0, q1 = b[0].at[2 * cc], b[0].at[2 * cc + 1]
        return [(q0, b[1], xn), (q1, b[4], yn), (q1, b[2], xn), (q0, b[3], yn)]

    def gather_sent(b):
        return [(src, src, dev) for src, _, dev in gather_plan(list(b) + [None] * 4)]

    def out_plan(b):
        x, y, cc = _position()
        chips, _ = _other_chips(x, y)
        return [(b[0].at[pl.ds(cc * ho, ho), :], b[1 + k], (cx, cy, cc)) for k, (cx, cy) in enumerate(chips)]

    def out_sent(b):
        return [(src, src, dev) for src, _, dev in out_plan(list(b) + [None] * 3)]

    def onward_plan(b):
        x, y, cc = _position()
        sib = (x, y, 1 - cc)
        return [(b[0], b[2], (x, 1 - y, cc)), (b[1], b[3], (1 - x, y, cc)), (b[0], b[4], sib), (b[1], b[5], sib)]

    pairs = lambda n: _to_sibling([lambda ref, cc: ref] * n)

    c8 = jnp.broadcast_to(c, (8, d))
    cw = jnp.concatenate([_pad_rows(conv_a_w[0], HALO_A), _pad_rows(conv_b_w[0], HALO_B)], axis=0)
    c_all, cw_all = _gather_cond(c8, cw)
    c_rows = c_all[:, 0, :]
    cw_full = jnp.transpose(cw_all, (1, 0, 2)).reshape(HALO_A + HALO_B, wa)
    conv_a_full, conv_b_full = cw_full[:HALO_A], cw_full[HALO_A:]

    b_ada_sh = lax.dynamic_slice(b_ada, (0, chip * na), (1, na))
    mod_part = _modulation(_pad_rows(c_rows, 2 * N_DEV), w_ada[0], b_ada_sh, _tile(na, 512), "modulation")[:N_DEV]
    mod_all = _exchange_mod(mod_part)
    mod = lax.dynamic_index_in_dim(mod_all, me, axis=1, keepdims=False).reshape(1, 3 * d)
    shift, scale, gate = mod[:, :d], mod[:, d:2 * d], mod[:, 2 * d:]

    def quarter():
        return lax.empty((d, hq), BF16)

    g_sems, g_bufs, g_tok = _start_copies(
        "gather_start", gather_plan, 4, [win4] + [quarter() for _ in range(4)], after=[mod_all])
    win4, (x0, x1, y0, y1) = g_bufs[0], g_bufs[1:5]

    h, r = _prenorm(x2d, norm_g, scale, shift + g_tok[0, 0], _tile(s, 256))
    bm = _tile(s, 1024)
    pieces = [None, None]

    def piece(slot, half, quarters, name, own_half=None):
        where = jnp.reshape(2 * slot + half, (1,)).astype(jnp.int32)
        pieces[:] = _proj_piece(where, h, quarters, pieces[0], pieces[1], din, 2 * N_CHIPS, _tile(s, 512), name,
                                own_half=own_half)
        return pieces[0]

    proj = piece(chip, 0, (win4, win4), "proj_own0", own_half=0)
    proj = piece(chip, 1, (win4, win4), "proj_own1", own_half=1)
    x0, y1 = _wait_copies("gather_wait_a", _landed, [x0, y1], g_sems[0:4], after=[proj], send=False)
    on_sems, (x0, y1, dg0, dg1, sx0, sy1), _ = _start_copies(
        "pass_on_a", onward_plan, 4, [x0, y1] + [quarter() for _ in range(4)])
    def two_quarters(half, quarters, places, name):
        where = jnp.stack([2 * cidx[0] + half, 2 * cidx[1] + half]).astype(jnp.int32)
        pieces[:] = _proj_quarter_pair(where, h, quarters, places, pieces[0], pieces[1], bm, name)
        return pieces[0]

    proj = two_quarters(pc, (x0, y1), (0, 1), "proj_first")
    x1, y0 = _wait_copies("gather_wait_b", _landed, [x1, y0], g_sems[4:8], after=[proj], send=False)
    pb_sems, (x1, sx1, y0, sy0), _ = _start_copies("pass_on_b", pairs(2), 2, [x1, quarter(), y0, quarter()])
    go_sems, go_bufs, _ = _start_copies("gather_out_start", out_plan, 3,
                                        [wout_bf] + [lax.empty((ho, d), BF16) for _ in range(3)], after=[x1])
    wout_bf, lo = go_bufs[0], go_bufs[1:4]
    proj = two_quarters(pc, (x1, y0), (1, 0), "proj_second")
    sx0, sy1 = _wait_copies("pass_wait_a", _landed, [sx0, sy1], on_sems[4:8], after=[proj], send=False)
    proj = two_quarters(1 - pc, (sx0, sy1), (0, 1), "proj_sibling_first")
    x1, sx1, y0, sy0 = _wait_copies("pass_wait_b", pairs(2), [x1, sx1, y0, sy0], pb_sems, after=[proj])
    proj = two_quarters(1 - pc, (sx1, sy0), (1, 0), "proj_sibling_second")
    x0, y1, dg0, dg1 = _wait_copies("diag_wait", lambda b: onward_plan(list(b) + [None, None])[:2],
                                    [x0, y1, dg0, dg1], on_sems[0:4], after=[proj])
    x0, y1 = _wait_copies("pass_sent_a", lambda b: [(b[0], b[0], (0, 0, 0)), (b[1], b[1], (0, 0, 0))],
                          [x0, y1], on_sems[4:8], after=[dg0], recv=False)
    pd_sems, (dg0, sd0, dg1, sd1), _ = _start_copies("pass_on_d", pairs(2), 2, [dg0, quarter(), dg1, quarter()],
                                                     after=[x0])
    proj = piece(cidx[2], pc, (dg0, dg1), "proj_da")
    dg0, sd0, dg1, sd1 = _wait_copies("pass_wait_d", pairs(2), [dg0, sd0, dg1, sd1], pd_sems, after=[proj])
    proj = piece(cidx[2], 1 - pc, (sd0, sd1), "proj_db")
    win_full = pieces[1]

    lo = _wait_copies("gather_wait_out", _landed, lo, go_sems, after=[proj], send=False)
    o_sems, o_bufs, o_tok = _start_copies(
        "pass_on_out", pairs(3), 3, [b for k in range(3) for b in (lo[k], lax.empty((ho, d), BF16))])
    win4, = _wait_copies("gather_wait_sent", gather_sent, [win4], g_sems, after=[o_tok], recv=False)
    wout_bf, = _wait_copies("gather_out_sent", out_sent, [wout_bf], go_sems, after=[win4], recv=False)

    def slot_index(k, chip_, cc, others):
        if k == 0:
            return pl.ds(2 * chip_, 2)
        return 2 * others[(k - 1) % 3] + (cc if k <= 3 else 1 - cc)

    y = _mixer_a_fwd(proj, conv_a_full, wa, _tile(s, 512), _tile(wa, 512))
    u0, u = _mixer_b_conv_fwd(proj, conv_b_full, conv_b_b, wa, _tile(s, 512), _tile(wa, 256), 64)
    y = _mixer_b_gate_fwd(y, u, proj, ln_b_g, ln_b_b, wa, _tile(s, 256))
    o_bufs = _wait_copies("pass_wait_out", pairs(3), o_bufs, o_sems, after=[y])
    wout_full = _assemble("assemble_w_out", [wout_bf.reshape(2, ho, d)] + o_bufs[0::2] + o_bufs[1::2],
                          jax.ShapeDtypeStruct((2 * N_CHIPS, ho, d), BF16), slot_index)
    wout2d = wout_full.reshape(dmix, d)
    bd = _tile(d, 1024)
    o = _matmul(
        y, wout2d, grid=(s // bm, d // bd, 1),
        a_spec=pl.BlockSpec((bm, dmix), lambda i, j, k: (i, 0)),
        b_spec=pl.BlockSpec((dmix, bd), lambda i, j, k: (0, j)),
        o_spec=pl.BlockSpec((bm, bd), lambda i, j, k: (i, j)),
        out_shape=jax.ShapeDtypeStruct((s, d), F32), dims=((1,), (0,)), name="out_proj")
    dx2, do, loss_p, gfg_p, dgate_p = _loss_head(x2d, o, target, gate, final_g.reshape(1, d), _tile(s, 128))

    be = _tile(dmix, 1024)
    g_wout = _matmul(
        y, do, grid=(dmix // be, d // bd, 1),
        a_spec=pl.BlockSpec((s, be), lambda i, j, k: (0, i)),
        b_spec=pl.BlockSpec((s, bd), lambda i, j, k: (0, j)),
        o_spec=pl.BlockSpec((be, bd), lambda i, j, k: (i, j)),
        out_shape=jax.ShapeDtypeStruct((dmix, d), F32), dims=((0,), (0,)), name="grad_w_out")
    swap_out = _to_sibling([lambda ref, cc: ref.at[:, pl.ds((1 - cc) * ho, ho), :]])
    so_sems, (g_wout3, ra_out), so_tok = _start_copies(
        "swap_out_start", swap_out, 1, [g_wout.reshape(N_CHIPS, r4, d), lax.empty((N_CHIPS, ho, d), F32)])
    dy = _matmul(
        do, wout2d, grid=(s // bm, dmix // be, 1),
        a_spec=pl.BlockSpec((bm, d), lambda i, j, k: (i, 0)),
        b_spec=pl.BlockSpec((be, d), lambda i, j, k: (j, 0)),
        o_spec=pl.BlockSpec((bm, be), lambda i, j, k: (i, j)),
        out_shape=jax.ShapeDtypeStruct((s, dmix), F32), dims=((1,), (1,)), name="dy", after=[so_tok])
    g_wout3, ra_out = _wait_copies("swap_out_wait", swap_out, [g_wout3, ra_out], so_sems, after=[dy])
    q_out = _chip_partial(pos, g_wout3, ra_out, _tile(ho, 256), "chip_partial_w_out")
    po_sems, po_bufs, po_tok = _start_copies(
        "send_out_start", _slots_to_chips, 3, [q_out] + [lax.empty((ho, d), BF16) for _ in range(3)])
    dproj, dwa_p = _mixer_a_bwd(proj, dy, conv_a_full + po_tok[0, 0], wa, din, _tile(s, 128))
    dproj, du, dlng_p, dlnb_p, dcb_p = _mixer_b_gate_bwd(dproj, dy, u, proj, ln_b_g, ln_b_b, wa, _tile(s, 128))
    dproj, dwb_p = _mixer_b_conv_bwd(dproj, du, u0, proj, conv_b_full, wa, _tile(s, 256), 64)

    slots = [cidx[0], cidx[1], cidx[2], chip]
    q, rb, snd = [None] * 3, [None] * 3, [None] * 3
    after = []
    for pair in ((0, 1), (2, 3)):
        given = {}
        for k in pair:
            theirs = _grad_slot(jnp.stack([slots[k], 1 - pc]).astype(jnp.int32), h, dproj, after, ns,
                                _tile(hrow, 512), hc, f"grad_w_in{k}a")
            sems, bufs, tok = _start_copies(f"swap_in_start{k}", pairs(1), 1, [theirs, lax.empty((hrow, ns), F32)])
            given[k] = (sems, bufs)
            after = [tok]
        for k in pair:
            sems, bufs = given[k]
            _, from_sibling = _wait_copies(f"swap_in_wait{k}", pairs(1), bufs, sems, after=after)
            mine = _grad_slot(jnp.stack([slots[k], pc]).astype(jnp.int32), h, dproj, [], ns, _tile(hrow, 512), hc,
                              f"grad_w_in{k}b", add=from_sibling, out_dtype=BF16 if k < 3 else F32)
            if k < 3:
                snd[k], (q[k], rb[k]), tok = _start_copies(f"send_in_start{k}", _to_chip(k), 1,
                                                           [mine, lax.empty((hrow, ns), BF16)])
                after = [tok]
            else:
                own_half, after = mine, [mine]
    dh = _matmul_by_pieces(dproj, win_full, bm, bd, "dh", after=after)
    grad_x, dshift_p, dscale_p, gng_p = _prenorm_bwd(x2d, r, dh, dx2, norm_g, scale, _tile(s, 128))

    def rows_of(v):
        return _pad_rows(v.reshape(-1, wa), 8 * ((v.size // wa + 7) // 8))

    dmod = jnp.concatenate([dshift_p, dscale_p, dgate_p], axis=1)
    parts = [gng_p, dmod, dwa_p, dwb_p, dcb_p, dlng_p, dlnb_p, gfg_p,
             jnp.broadcast_to(loss_p[:, :1], (1, wa))]
    starts, packed = [], []
    for p in parts:
        starts.append(sum(q.shape[0] for q in packed))
        packed.append(rows_of(p) if p.shape[0] == 1 else p)
    small_sum, small_all = _gather_small(jnp.concatenate(packed, axis=0))

    def summed(k, rows):
        return small_sum[starts[k]:starts[k] + rows]

    grad_norm_g = summed(0, d // wa).reshape(1, d)
    grad_b_ada = summed(1, 3 * d // wa).reshape(1, 3 * d)
    grad_conv_a_full = summed(2, TAPS_A)
    grad_conv_b_full = summed(3, TAPS_B)
    grad_conv_b_b = summed(4, 1)
    grad_ln_b_g = summed(5, 1)
    grad_ln_b_b = summed(6, 1)
    grad_final_g = summed(7, d // wa).reshape(d)
    loss = summed(8, 1)[0, 0]
    grad_conv_a_w = lax.dynamic_slice(grad_conv_a_full, (0, chip * wsh), (TAPS_A, wsh))
    grad_conv_b_w = lax.dynamic_slice(grad_conv_b_full, (0, chip * wsh), (TAPS_B, wsh))
    dmod_all = small_all[:, starts[1]:starts[1] + 3 * d // wa, :].reshape(N_DEV, 3 * d)
    dmod_sh = lax.dynamic_slice(dmod_all, (0, chip * na), (N_DEV, na))

    def pairs_to_chips(b):
        x, y, cc = _position()
        chips, _ = _other_chips(x, y)
        return [(b[2 * k], b[2 * k + 1], (cx, cy, cc)) for k, (cx, cy) in enumerate(chips)]

    po_bufs = _wait_copies("send_out_wait", _slots_to_chips, po_bufs, po_sems, after=[small_sum])
    gh_out = _final_half(pos, g_wout3, ra_out, po_bufs[1:], _tile(ho, 256), "final_half_w_out")
    in_bufs = _wait_copies("send_in_wait", pairs_to_chips, [b for k in range(3) for b in (q[k], rb[k])],
                           snd[0] + snd[1] + snd[2], after=[small_sum])
    gh_in = _final_sum(pos, own_half, in_bufs[1::2], _tile(hrow, 256), "final_half_w_in")
    sh_sems, sh_bufs, sh_tok = _start_copies("share_start", _halves_to_sibling, 2, [gh_in, gh_out])

    grad_w_ada, d_wada, nm_wada, nv_wada = _adam_ada(c_rows.T, dmod_sh + sh_tok[0, 0], w_ada[0], m_w_ada[0],
                                                     v_w_ada[0], _tile(d, 128), "adam_w_ada")
    gw_in, gw_out = _wait_copies("share_wait", _halves_to_sibling, sh_bufs, sh_sems, after=[d_wada])
    grad_w_in, d_win, nm_win, nv_win = _adam(w_in[0], gw_in, m_w_in[0], v_w_in[0], _tile(d, 128), "adam_w_in",
                                             return_grad=True)
    grad_w_out, d_wout, nm_wout, nv_wout = _adam(w_out[0], gw_out, m_w_out[0], v_w_out[0], _tile(r4, 128),
                                                 "adam_w_out", return_grad=True)

    def small_adam(w, g, m, v, name):
        shape = w.shape
        w2 = w.reshape(-1, shape[-1])
        out = _adam(w2, g.reshape(w2.shape), m.reshape(w2.shape), v.reshape(w2.shape), w2.shape[0], name)
        return [o_.reshape(shape) for o_ in out]

    small = {
        "norm_g": small_adam(norm_g, grad_norm_g, m_norm_g, v_norm_g, "adam_norm_g"),
        "b_ada": small_adam(b_ada, grad_b_ada, m_b_ada, v_b_ada, "adam_b_ada"),
        "conv_a_w": small_adam(conv_a_w, grad_conv_a_w, m_conv_a_w, v_conv_a_w, "adam_conv_a_w"),
        "conv_b_w": small_adam(conv_b_w, grad_conv_b_w, m_conv_b_w, v_conv_b_w, "adam_conv_b_w"),
        "conv_b_b": small_adam(conv_b_b, grad_conv_b_b, m_conv_b_b, v_conv_b_b, "adam_conv_b_b"),
        "ln_b_g": small_adam(ln_b_g, grad_ln_b_g, m_ln_b_g, v_ln_b_g, "adam_ln_b_g"),
        "ln_b_b": small_adam(ln_b_b, grad_ln_b_b, m_ln_b_b, v_ln_b_b, "adam_ln_b_b"),
        "final_g": small_adam(final_g.reshape(1, d), grad_final_g, m_final_g.reshape(1, d),
                              v_final_g.reshape(1, d), "adam_final_g"),
    }
    small["final_g"] = [o_.reshape(d) for o_ in small["final_g"]]
    big = {
        "w_ada": [a[None] for a in (d_wada, nm_wada, nv_wada)],
        "w_in": [a[None] for a in (d_win, nm_win, nv_win)],
        "w_out": [a[None] for a in (d_wout, nm_wout, nv_wout)],
    }
    upd = {**small, **big}
    order = ["norm_g", "w_ada", "b_ada", "w_in", "conv_a_w", "conv_b_w", "conv_b_b", "ln_b_g", "ln_b_b",
             "w_out", "final_g"]
    grads = {
        "norm_g": grad_norm_g, "w_ada": grad_w_ada[None], "b_ada": grad_b_ada, "w_in": grad_w_in[None],
        "conv_a_w": grad_conv_a_w[None], "conv_b_w": grad_conv_b_w[None], "conv_b_b": grad_conv_b_b,
        "ln_b_g": grad_ln_b_g, "ln_b_b": grad_ln_b_b, "w_out": grad_w_out[None], "final_g": grad_final_g,
    }
    return (loss, grad_x.reshape(1, s, d), *[grads[n] for n in order], *[upd[n][0] for n in order],
            *[upd[n][1] for n in order], *[upd[n][2] for n in order])
```

```python
import functools

import jax
import jax.numpy as jnp
from jax import lax
from jax.experimental import pallas as pl
from jax.experimental.pallas import tpu as pltpu

F32 = jnp.float32
BF16 = jnp.bfloat16
EPS = 1e-6
N_CHIPS = 4
N_DEV = 8
TAPS_A = 3
TAPS_B = 31
HALO_A = 8
HALO_IN = 16
HALO_B = 32
LANES = 128
SUBLANES = 8
ADAM_LR = 0.001
ADAM_B1 = 0.9
ADAM_B2 = 0.999
ADAM_EPS = 1e-08
ADAM_WD = 0.01
ADAM_STEP = 10
VMEM_LIMIT = 56 * 1024 * 1024
MESH = pl.DeviceIdType.MESH
ANY = pl.BlockSpec(memory_space=pl.ANY)
VMEM = pl.BlockSpec(memory_space=pltpu.VMEM)
HBM_SPEC = pl.BlockSpec(memory_space=pltpu.HBM)
SEM_SPEC = pl.BlockSpec(memory_space=pltpu.SEMAPHORE)
EFFECT = pltpu.SideEffectType.DATAFLOW_SIDE_EFFECTING


def _params(sem=None):
    return pltpu.CompilerParams(dimension_semantics=sem, vmem_limit_bytes=VMEM_LIMIT)


def _sigmoid(v):
    return jax.nn.sigmoid(v)


def _position():
    return lax.axis_index("x"), lax.axis_index("y"), lax.axis_index("c")


def _rcopy(src, dst, ssem, rsem, dev):
    return pltpu.make_async_remote_copy(src_ref=src, dst_ref=dst, send_sem=ssem, recv_sem=rsem,
                                        device_id=dev, device_id_type=MESH)


def _other_chips(x, y):
    chips = [(1 - x, y), (x, 1 - y), (1 - x, 1 - y)]
    return chips, [2 * cx + cy for cx, cy in chips]


def _cast_bf16(a, rows, name):
    m, n = a.shape

    def body(a_ref, o_ref):
        o_ref[...] = a_ref[...].astype(BF16)

    return pl.pallas_call(
        body, name=name, grid=(m // rows,),
        in_specs=[pl.BlockSpec((rows, n), lambda i: (i, 0))],
        out_specs=pl.BlockSpec((rows, n), lambda i: (i, 0)),
        out_shape=jax.ShapeDtypeStruct((m, n), BF16),
        compiler_params=_params(("parallel",)),
    )(a)


def _cast_quarters(a, rows, name):
    m, n = a.shape
    hq = n // 4

    def body(a_ref, o_ref):
        o_ref[...] = a_ref[...].astype(BF16)

    return pl.pallas_call(
        body, name=name, grid=(4, m // rows),
        in_specs=[pl.BlockSpec((rows, hq), lambda q, i: (i, q))],
        out_specs=pl.BlockSpec((None, rows, hq), lambda q, i: (q, i, 0)),
        out_shape=jax.ShapeDtypeStruct((4, m, hq), BF16),
        compiler_params=_params(("parallel", "parallel")),
    )(a)


def _proj_piece(where, h, quarters, proj, w_all, din, n_pieces, bm, name, own_half=None):
    s, d = h.shape
    hq = quarters[0].shape[-1]
    nm = s // bm
    if own_half is None:
        q_specs = [pl.BlockSpec((d, hq), lambda i, p: (0, 0), pipeline_mode=pl.Buffered(1))] * 2
    else:
        q_specs = [pl.BlockSpec((None, d, hq), lambda i, p, k=k: (2 * own_half + k, 0, 0),
                                pipeline_mode=pl.Buffered(1)) for k in range(2)]

    def body(p_ref, h_ref, q0_ref, q1_ref, *rest):
        o_ref, wall_ref, wbuf, sem = rest[-4:]
        i = pl.program_id(0)
        filed = pltpu.make_async_copy(wbuf, wall_ref.at[p_ref[0]], sem)

        @pl.when(i == 0)
        def _():
            wbuf[:, 0:hq] = q0_ref[...]
            wbuf[:, hq:2 * hq] = q1_ref[...]
            filed.start()

        o_ref[...] = jnp.dot(h_ref[...], wbuf[...], preferred_element_type=F32).astype(BF16)

        @pl.when(i == nm - 1)
        def _():
            filed.wait()

    args, extra, alias = [where, h, quarters[0], quarters[1]], [], {}
    if proj is not None:
        args, extra, alias = args + [proj, w_all], [ANY, ANY], {4: 0, 5: 1}
    return pl.pallas_call(
        body, name=name,
        grid_spec=pltpu.PrefetchScalarGridSpec(
            num_scalar_prefetch=1, grid=(nm,),
            in_specs=[pl.BlockSpec((bm, d), lambda i, p: (i, 0))] + q_specs + extra,
            out_specs=[pl.BlockSpec((bm, 2 * hq), lambda i, p: (i, p[0])), ANY],
            scratch_shapes=[pltpu.VMEM((d, 2 * hq), BF16), pltpu.SemaphoreType.DMA]),
        out_shape=[jax.ShapeDtypeStruct((s, din), BF16), jax.ShapeDtypeStruct((n_pieces, d, 2 * hq), BF16)],
        input_output_aliases=alias,
        compiler_params=_params(("arbitrary",)),
    )(*args)


def _proj_quarter_pair(where, h, quarters, places, proj, w_all, bm, name):
    s, d = h.shape
    hq = quarters[0].shape[-1]
    nm = s // bm
    q_spec = pl.BlockSpec((d, hq), lambda j, i, w: (0, 0), pipeline_mode=pl.Buffered(1))

    def body(w_ref, h_ref, qa_ref, qb_ref, proj_in, wall_in, o_ref, wall_ref, sem):
        j, i = pl.program_id(0), pl.program_id(1)
        for k, q_ref in enumerate((qa_ref, qb_ref)):
            @pl.when(j == k)
            def _(k=k, q_ref=q_ref):
                o_ref[...] = jnp.dot(h_ref[...], q_ref[...], preferred_element_type=F32).astype(BF16)
                filed = pltpu.make_async_copy(
                    q_ref, wall_ref.at[w_ref[k], :, pl.ds(places[k] * hq, hq)], sem.at[k])

                @pl.when(i == 0)
                def _():
                    filed.start()

                @pl.when(i == nm - 1)
                def _():
                    filed.wait()

    return pl.pallas_call(
        body, name=name,
        grid_spec=pltpu.PrefetchScalarGridSpec(
            num_scalar_prefetch=1, grid=(2, nm),
            in_specs=[pl.BlockSpec((bm, d), lambda j, i, w: (i, 0)), q_spec, q_spec, ANY, ANY],
            out_specs=[pl.BlockSpec((bm, hq),
                                    lambda j, i, w: (i, 2 * w[j] + places[0] + j * (places[1] - places[0]))), ANY],
            scratch_shapes=[pltpu.SemaphoreType.DMA((2,))]),
        out_shape=[jax.ShapeDtypeStruct(proj.shape, proj.dtype), jax.ShapeDtypeStruct(w_all.shape, w_all.dtype)],
        input_output_aliases={4: 0, 5: 1},
        compiler_params=_params(("arbitrary", "arbitrary")),
    )(where, h, quarters[0], quarters[1], proj, w_all)


def _grad_slot(where, h, dproj, after, ns, bd, bn, name, add=None, out_dtype=F32):
    s, d = h.shape
    nb = ns // bn
    ni = d // 2 // bd
    extra = [] if add is None else [add]

    def body(where_ref, h_ref, dp_ref, *rest):
        acc = lax.dot_general(h_ref[...], dp_ref[...], (((0,), (0,)), ((), ())), preferred_element_type=F32)
        if add is not None:
            acc = acc + rest[0][...]
        rest[-1][...] = acc.astype(out_dtype)

    return pl.pallas_call(
        body, name=name,
        grid_spec=pltpu.PrefetchScalarGridSpec(
            num_scalar_prefetch=1, grid=(nb, ni),
            in_specs=[pl.BlockSpec((s, bd), lambda j, i, w: (0, w[1] * ni + i)),
                      pl.BlockSpec((s, bn), lambda j, i, w: (0, w[0] * nb + j))]
            + [pl.BlockSpec((bd, bn), lambda j, i, w: (i, j))] * len(extra) + [ANY] * len(after),
            out_specs=pl.BlockSpec((bd, bn), lambda j, i, w: (i, j))),
        out_shape=jax.ShapeDtypeStruct((d // 2, ns), out_dtype),
        compiler_params=_params(("parallel", "parallel")),
    )(where, h, dproj, *extra, *after)


def _matmul(a, b, *, grid, a_spec, b_spec, o_spec, out_shape, dims, name, after=()):
    nk = grid[2]
    n_after = len(after)

    def body(a_ref, b_ref, *rest):
        o_ref, acc = rest[n_after], rest[n_after + 1:]
        p = lax.dot_general(a_ref[...], b_ref[...], (dims, ((), ())), preferred_element_type=F32)
        if nk == 1:
            o_ref[...] = p.astype(o_ref.dtype)
        else:
            acc_ref, = acc
            k = pl.program_id(2)

            @pl.when(k == 0)
            def _():
                acc_ref[...] = p

            @pl.when(k > 0)
            def _():
                acc_ref[...] += p

            @pl.when(k == nk - 1)
            def _():
                o_ref[...] = acc_ref[...].astype(o_ref.dtype)

    block = [d for d in o_spec.block_shape if d is not None]
    scratch = [pltpu.VMEM(tuple(block), F32)] if nk > 1 else []
    return pl.pallas_call(
        body, name=name, grid=grid, in_specs=[a_spec, b_spec] + [ANY] * n_after, out_specs=o_spec,
        out_shape=out_shape, scratch_shapes=scratch,
        compiler_params=_params(("parallel", "parallel", "arbitrary")),
    )(a, b, *after)


def _matmul_by_pieces(a, pieces, bm, bn, name, after=()):
    s = a.shape[0]
    n, rows, width = pieces.shape
    nk = n // 2
    n_after = len(after)

    def body(a_ref, b_ref, *rest):
        o_ref, acc_ref = rest[n_after], rest[n_after + 1]
        k = pl.program_id(2)
        nt = (((1,), (1,)), ((), ()))
        p = (lax.dot_general(a_ref[:, 0:width], b_ref[0], nt, preferred_element_type=F32)
             + lax.dot_general(a_ref[:, width:2 * width], b_ref[1], nt, preferred_element_type=F32))

        @pl.when(k == 0)
        def _():
            acc_ref[...] = p

        @pl.when(k > 0)
        def _():
            acc_ref[...] += p

        @pl.when(k == nk - 1)
        def _():
            o_ref[...] = acc_ref[...]

    return pl.pallas_call(
        body, name=name, grid=(s // bm, rows // bn, nk),
        in_specs=[pl.BlockSpec((bm, 2 * width), lambda i, j, k: (i, k)),
                  pl.BlockSpec((2, bn, width), lambda i, j, k: (k, j, 0))] + [ANY] * n_after,
        out_specs=pl.BlockSpec((bm, bn), lambda i, j, k: (i, j)),
        out_shape=jax.ShapeDtypeStruct((s, rows), F32), scratch_shapes=[pltpu.VMEM((bm, bn), F32)],
        compiler_params=_params(("parallel", "parallel", "arbitrary")),
    )(a, pieces, *after)


def _adam_math(w, g, m, v):
    m = ADAM_B1 * m + (1.0 - ADAM_B1) * g
    v = ADAM_B2 * v + (1.0 - ADAM_B2) * (g * g)
    m_hat = m / (1.0 - ADAM_B1 ** ADAM_STEP)
    v_hat = v / (1.0 - ADAM_B2 ** ADAM_STEP)
    delta = -ADAM_LR * (m_hat / (jnp.sqrt(v_hat) + ADAM_EPS) + ADAM_WD * w)
    return delta, m, v


def _adam(w, g, m, v, rows, name, return_grad=False):
    r, n = w.shape

    def body(w_ref, g_ref, m_ref, v_ref, *out):
        gv = g_ref[...]
        d, mo, vo = _adam_math(w_ref[...], gv, m_ref[...], v_ref[...])
        for o_ref, val in zip(out, ([gv] if return_grad else []) + [d, mo, vo]):
            o_ref[...] = val

    spec = pl.BlockSpec((rows, n), lambda i: (i, 0))
    shape = jax.ShapeDtypeStruct((r, n), F32)
    n_out = 4 if return_grad else 3
    return pl.pallas_call(
        body, name=name, grid=(r // rows,), in_specs=[spec] * 4, out_specs=[spec] * n_out,
        out_shape=[shape] * n_out, compiler_params=_params(("parallel",)),
    )(w, g, m, v)


def _adam_ada(c_cols, dmod, w, m, v, rows, name):
    r, n = w.shape

    def body(c_ref, dm_ref, w_ref, m_ref, v_ref, g_ref, d_ref, mo_ref, vo_ref):
        cv = c_ref[...]
        c_act = cv * _sigmoid(cv)
        g = c_act[:, 0:1] * dm_ref[0:1, :]
        for b in range(1, N_DEV):
            g = g + c_act[:, b:b + 1] * dm_ref[b:b + 1, :]
        d, mo, vo = _adam_math(w_ref[...], g, m_ref[...], v_ref[...])
        g_ref[...] = g
        d_ref[...] = d
        mo_ref[...] = mo
        vo_ref[...] = vo

    spec = pl.BlockSpec((rows, n), lambda i: (i, 0))
    shape = jax.ShapeDtypeStruct((r, n), F32)
    return pl.pallas_call(
        body, name=name, grid=(r // rows,),
        in_specs=[pl.BlockSpec((rows, N_DEV), lambda i: (i, 0)), pl.BlockSpec((N_DEV, n), lambda i: (0, 0)),
                  spec, spec, spec],
        out_specs=[spec] * 4, out_shape=[shape] * 4, compiler_params=_params(("parallel",)),
    )(c_cols, dmod, w, m, v)


def _start_copies(name, plan, n, bufs, after=()):
    nb, na = len(bufs), len(after)

    def body(*refs):
        sems = refs[nb + na:nb + na + 2 * n]
        for k, (src, dst, dev) in enumerate(plan(refs[:nb])):
            _rcopy(src, dst, sems[2 * k], sems[2 * k + 1], dev).start()
        refs[-1][...] = jnp.zeros((8, 128), F32)

    outs = pl.pallas_call(
        body, name=name,
        out_shape=[pltpu.SemaphoreType.DMA(())] * (2 * n) + [pltpu.HBM(a.shape, a.dtype) for a in bufs]
        + [jax.ShapeDtypeStruct((8, 128), F32)],
        in_specs=[HBM_SPEC] * nb + [ANY] * na, out_specs=[SEM_SPEC] * (2 * n) + [HBM_SPEC] * nb + [VMEM],
        input_output_aliases={i: 2 * n + i for i in range(nb)},
        compiler_params=pltpu.CompilerParams(has_side_effects=EFFECT),
    )(*[pltpu.with_memory_space_constraint(a, pltpu.HBM) for a in bufs], *after)
    return list(outs[:2 * n]), list(outs[2 * n:2 * n + nb]), outs[-1]


def _wait_copies(name, plan, bufs, sems, after=(), send=True, recv=True):
    nb, nsem = len(bufs), len(sems)

    def body(*refs):
        s = refs[nb:nb + nsem]
        for k, (src, dst, dev) in enumerate(plan(refs[:nb])):
            cp = _rcopy(src, dst, s[2 * k], s[2 * k + 1], dev)
            if send:
                cp.wait_send()
            if recv:
                cp.wait_recv()

    outs = pl.pallas_call(
        body, name=name, out_shape=[pltpu.HBM(a.shape, a.dtype) for a in bufs],
        in_specs=[HBM_SPEC] * nb + [SEM_SPEC] * nsem + [ANY] * len(after), out_specs=[HBM_SPEC] * nb,
        input_output_aliases={i: i for i in range(nb)},
        compiler_params=pltpu.CompilerParams(has_side_effects=EFFECT),
    )(*bufs, *sems, *after)
    return list(outs)


def _to_sibling(views):
    def plan(b):
        x, y, c = _position()
        return [(view(b[2 * k], c), b[2 * k + 1], (x, y, 1 - c)) for k, view in enumerate(views)]
    return plan


def _to_chip(k):
    def plan(b):
        x, y, c = _position()
        cx, cy = _other_chips(x, y)[0][k]
        return [(b[0], b[1], (cx, cy, c))]
    return plan


def _slots_to_chips(b):
    x, y, c = _position()
    chips, cidx = _other_chips(x, y)
    return [(b[0].at[cidx[k]], b[1 + k], (cx, cy, c)) for k, (cx, cy) in enumerate(chips)]


def _halves_to_sibling(b):
    x, y, c = _position()
    views = [r.at[pl.ds(c * (r.shape[0] // 2), r.shape[0] // 2), :] for r in b]
    return [(v, v, (x, y, 1 - c)) for v in views]


def _landed(b):
    x, y, c = _position()
    return [(ref, ref, (x, y, c)) for ref in b]


def _assemble(name, pieces, out_shape, index_of):
    n = len(pieces)

    def body(*refs):
        out_ref, sem = refs[n], refs[n + 1]
        x, y, c = _position()
        _, cidx = _other_chips(x, y)
        cps = [pltpu.make_async_copy(refs[k], out_ref.at[index_of(k, 2 * x + y, c, cidx)], sem.at[k]) for k in range(n)]
        for cp in cps:
            cp.start()
        for cp in cps:
            cp.wait()

    return pl.pallas_call(
        body, name=name, in_specs=[VMEM] * n, out_specs=ANY, out_shape=out_shape,
        scratch_shapes=[pltpu.SemaphoreType.DMA((n,))],
        compiler_params=pltpu.CompilerParams(vmem_limit_bytes=VMEM_LIMIT),
    )(*pieces)


def _gather_cond(c8, cw):
    def body(c8_ref, cw_ref, call_ref, cwall_ref, ssem, rsem, lsem):
        x, y, c = _position()
        chip = 2 * x + y
        me = 4 * x + 2 * y + c
        chips, cidx = _other_chips(x, y)
        own = [pltpu.make_async_copy(c8_ref, call_ref.at[me], lsem.at[0]),
               pltpu.make_async_copy(cw_ref, cwall_ref.at[chip], lsem.at[1])]
        for cp in own:
            cp.start()
        sends = [_rcopy(cw_ref, cwall_ref.at[chip], ssem.at[k], rsem.at[k], (cx, cy, c))
                 for k, (cx, cy) in enumerate(chips)]
        for mask in range(1, N_DEV):
            fx, fy, fc = (mask >> 2) & 1, (mask >> 1) & 1, mask & 1
            dev = (1 - x if fx else x, 1 - y if fy else y, 1 - c if fc else c)
            sends.append(_rcopy(c8_ref, call_ref.at[me], ssem.at[2 + mask], rsem.at[2 + mask], dev))
        for cp in sends:
            cp.start()
        for k in range(3):
            slot = cwall_ref.at[cidx[k]]
            _rcopy(slot, slot, ssem.at[k], rsem.at[k], (x, y, c)).wait_recv()
        for mask in range(1, N_DEV):
            slot = call_ref.at[jnp.bitwise_xor(me, mask)]
            _rcopy(slot, slot, ssem.at[2 + mask], rsem.at[2 + mask], (x, y, c)).wait_recv()
        for cp in sends:
            cp.wait_send()
        for cp in own:
            cp.wait()

    return pl.pallas_call(
        body, name="gather_cond", in_specs=[VMEM, VMEM], out_specs=[VMEM, VMEM],
        out_shape=[jax.ShapeDtypeStruct((N_DEV,) + c8.shape, F32), jax.ShapeDtypeStruct((N_CHIPS,) + cw.shape, F32)],
        scratch_shapes=[pltpu.SemaphoreType.DMA((10,)), pltpu.SemaphoreType.DMA((10,)), pltpu.SemaphoreType.DMA((2,))],
    )(c8, cw)


def _exchange_mod(mod_part):
    def body(mp_ref, out_ref, ssem, rsem, lsem):
        x, y, c = _position()
        chip = 2 * x + y
        chips, cidx = _other_chips(x, y)
        own = pltpu.make_async_copy(mp_ref, out_ref.at[chip], lsem)
        own.start()
        sends = [_rcopy(mp_ref, out_ref.at[chip], ssem.at[k], rsem.at[k], (cx, cy, c))
                 for k, (cx, cy) in enumerate(chips)]
        for cp in sends:
            cp.start()
        for k in range(3):
            slot = out_ref.at[cidx[k]]
            _rcopy(slot, slot, ssem.at[k], rsem.at[k], (x, y, c)).wait_recv()
        for cp in sends:
            cp.wait_send()
        own.wait()

    return pl.pallas_call(
        body, name="exchange_mod", in_specs=[VMEM], out_specs=VMEM,
        out_shape=jax.ShapeDtypeStruct((N_CHIPS,) + mod_part.shape, F32),
        scratch_shapes=[pltpu.SemaphoreType.DMA((3,)), pltpu.SemaphoreType.DMA((3,)), pltpu.SemaphoreType.DMA],
    )(mod_part)


def _gather_small(pack):
    rows, n = pack.shape

    def body(p_ref, sum_ref, all_ref, ssem, rsem, lsem):
        x, y, c = _position()
        me = 4 * x + 2 * y + c
        sib = (x, y, 1 - c)
        chips, cidx = _other_chips(x, y)
        own = pltpu.make_async_copy(p_ref, all_ref.at[me], lsem)
        own.start()
        sends = [_rcopy(p_ref, all_ref.at[me], ssem.at[0], rsem.at[0], sib)]
        sends += [_rcopy(p_ref, all_ref.at[me], ssem.at[1 + k], rsem.at[1 + k], (cx, cy, c))
                  for k, (cx, cy) in enumerate(chips)]
        for cp in sends:
            cp.start()
        for k in range(3):
            slot = all_ref.at[2 * cidx[k] + c]
            _rcopy(slot, slot, ssem.at[1 + k], rsem.at[1 + k], sib).wait_recv()
            fw = _rcopy(slot, slot, ssem.at[4 + k], rsem.at[4 + k], sib)
            fw.start()
            sends.append(fw)
        slot = all_ref.at[jnp.bitwise_xor(me, 1)]
        _rcopy(slot, slot, ssem.at[0], rsem.at[0], sib).wait_recv()
        for k in range(3):
            slot = all_ref.at[2 * cidx[k] + 1 - c]
            _rcopy(slot, slot, ssem.at[4 + k], rsem.at[4 + k], sib).wait_recv()
        for cp in sends:
            cp.wait_send()
        own.wait()
        acc = all_ref[0]
        for k in range(1, N_DEV):
            acc = acc + all_ref[k]
        sum_ref[...] = acc

    return pl.pallas_call(
        body, name="gather_small", in_specs=[VMEM], out_specs=[VMEM, VMEM],
        out_shape=[jax.ShapeDtypeStruct((rows, n), F32), jax.ShapeDtypeStruct((N_DEV, rows, n), F32)],
        scratch_shapes=[pltpu.SemaphoreType.DMA((7,)), pltpu.SemaphoreType.DMA((7,)), pltpu.SemaphoreType.DMA],
        compiler_params=pltpu.CompilerParams(vmem_limit_bytes=VMEM_LIMIT),
    )(pack)


def _chip_partial(pos, g, recv, rows, name):
    ns, full, n = g.shape
    h = full // 2
    nb = h // rows

    def body(pos_ref, g_ref, r_ref, o_ref):
        o_ref[...] = (g_ref[...] + r_ref[...]).astype(BF16)

    return pl.pallas_call(
        body, name=name,
        grid_spec=pltpu.PrefetchScalarGridSpec(
            num_scalar_prefetch=1, grid=(ns, nb),
            in_specs=[pl.BlockSpec((None, rows, n), lambda s, i, p: (s, p[1] * nb + i, 0)),
                      pl.BlockSpec((None, rows, n), lambda s, i, p: (s, i, 0))],
            out_specs=pl.BlockSpec((None, rows, n), lambda s, i, p: (s, i, 0))),
        out_shape=jax.ShapeDtypeStruct((ns, h, n), BF16),
        compiler_params=_params(("parallel", "parallel")),
    )(pos, g, recv)


def _final_sum(pos, first, parts, rows, name):
    h, n = first.shape
    nb = h // rows

    def body(pos_ref, f_ref, rb0_ref, rb1_ref, rb2_ref, o_ref):
        acc = f_ref[...]
        for rb_ref in (rb0_ref, rb1_ref, rb2_ref):
            acc = acc + rb_ref[...].astype(F32)
        o_ref[...] = acc

    part = pl.BlockSpec((rows, n), lambda i, p: (i, 0))
    return pl.pallas_call(
        body, name=name,
        grid_spec=pltpu.PrefetchScalarGridSpec(
            num_scalar_prefetch=1, grid=(nb,), in_specs=[part] * 4,
            out_specs=pl.BlockSpec((rows, n), lambda i, p: (p[1] * nb + i, 0))),
        out_shape=jax.ShapeDtypeStruct((2 * h, n), F32),
        compiler_params=_params(("parallel",)),
    )(pos, first, *parts)


def _final_half(pos, g, recv_a, recv_b, rows, name):
    ns, full, n = g.shape
    h = full // 2
    nb = h // rows

    def body(pos_ref, g_ref, ra_ref, rb0_ref, rb1_ref, rb2_ref, o_ref):
        acc = g_ref[...] + ra_ref[...]
        for rb_ref in (rb0_ref, rb1_ref, rb2_ref):
            acc = acc + rb_ref[...].astype(F32)
        o_ref[...] = acc

    part = pl.BlockSpec((rows, n), lambda i, p: (i, 0))
    return pl.pallas_call(
        body, name=name,
        grid_spec=pltpu.PrefetchScalarGridSpec(
            num_scalar_prefetch=1, grid=(nb,),
            in_specs=[pl.BlockSpec((None, rows, n), lambda i, p: (p[0], p[1] * nb + i, 0)),
                      pl.BlockSpec((None, rows, n), lambda i, p: (p[0], i, 0)), part, part, part],
            out_specs=pl.BlockSpec((rows, n), lambda i, p: (p[1] * nb + i, 0))),
        out_shape=jax.ShapeDtypeStruct((full, n), F32),
        compiler_params=_params(("parallel",)),
    )(pos, g, recv_a, *recv_b)


def _modulation(c_rows, w_ada, b_ada, cols, name):
    d, n = w_ada.shape
    rows = c_rows.shape[0]

    def body(c_ref, w_ref, b_ref, o_ref):
        cv = c_ref[...]
        c_act = (cv * _sigmoid(cv)).astype(BF16)
        o_ref[...] = jnp.dot(c_act, w_ref[...].astype(BF16), preferred_element_type=F32) + b_ref[...]

    return pl.pallas_call(
        body, name=name, grid=(n // cols,),
        in_specs=[pl.BlockSpec((rows, d), lambda j: (0, 0)), pl.BlockSpec((d, cols), lambda j: (0, j)),
                  pl.BlockSpec((1, cols), lambda j: (0, j))],
        out_specs=pl.BlockSpec((rows, cols), lambda j: (0, j)),
        out_shape=jax.ShapeDtypeStruct((rows, n), F32),
        compiler_params=_params(("parallel",)),
    )(c_rows, w_ada, b_ada)


def _prenorm(x, norm_g, scale, shift, rows):
    s, d = x.shape

    def body(x_ref, g_ref, sc_ref, sh_ref, h_ref, r_ref):
        xv = x_ref[...]
        r = lax.rsqrt(jnp.mean(xv * xv, axis=-1, keepdims=True) + EPS)
        h = (xv * r * g_ref[...]) * (1.0 + sc_ref[...]) + sh_ref[...]
        h_ref[...] = h.astype(BF16)
        r_ref[...] = r

    vec = pl.BlockSpec((1, d), lambda i: (0, 0))
    return pl.pallas_call(
        body, name="prenorm", grid=(s // rows,),
        in_specs=[pl.BlockSpec((rows, d), lambda i: (i, 0)), vec, vec, vec],
        out_specs=[pl.BlockSpec((rows, d), lambda i: (i, 0)), pl.BlockSpec((rows, 1), lambda i: (i, 0))],
        out_shape=[jax.ShapeDtypeStruct((s, d), BF16), jax.ShapeDtypeStruct((s, 1), F32)],
        compiler_params=_params(("parallel",)),
    )(x, norm_g, scale, shift)


def _mixer_a_fwd(proj, conv_w, wa, rows, cols):
    s = proj.shape[0]
    ncb = wa // cols

    def body(ab_ref, ac_ref, ax_ref, az_ref, w_ref, y_ref, qbuf):
        t = pl.program_id(1)

        @pl.when(t == 0)
        def _():
            qbuf[0:HALO_A, :] = jnp.zeros((HALO_A, cols), F32)

        q = ac_ref[...].astype(F32) * ax_ref[...].astype(F32)
        qbuf[HALO_A:HALO_A + rows, :] = q
        conv = w_ref[2:3, :] * q
        for k in range(TAPS_A - 1):
            off = HALO_A - (TAPS_A - 1) + k
            conv = conv + w_ref[k:k + 1, :] * qbuf[off:off + rows, :]
        zv = az_ref[...].astype(F32)
        y_ref[...] = (ab_ref[...].astype(F32) * conv * (zv * _sigmoid(zv))).astype(BF16)
        qbuf[0:HALO_A, :] = qbuf[rows:rows + HALO_A, :]

    def sec(k):
        return pl.BlockSpec((rows, cols), lambda cb, t, k=k: (t, k * ncb + cb))

    return pl.pallas_call(
        body, name="mixer_a_fwd", grid=(ncb, s // rows),
        in_specs=[sec(0), sec(1), sec(2), sec(3), pl.BlockSpec((HALO_A, cols), lambda cb, t: (0, cb))],
        out_specs=pl.BlockSpec((rows, cols), lambda cb, t: (t, cb)),
        out_shape=jax.ShapeDtypeStruct((s, 2 * wa), BF16),
        scratch_shapes=[pltpu.VMEM((HALO_A + rows, cols), F32)],
        compiler_params=_params(("parallel", "arbitrary")),
    )(proj, proj, proj, proj, conv_w)


def _shifted_back(dst, src, lo, hi, cs):
    for n in range(1, 8):
        dst[n, lo:hi, :] = src[lo - n:hi - n, cs]


def _shifted_fwd(dst, src, lo, hi, cs):
    for n in range(1, 8):
        dst[n, lo:hi, :] = src[lo + n:hi + n, cs]


def _shift_rows(shifted, plain, n, start, size, cs):
    return plain[pl.ds(start, size), cs] if n == 0 else shifted[n, pl.ds(start, size), :]


def _mixer_b_conv_fwd(proj, conv_w, conv_b, wa, rows, cols, chunk):
    s = proj.shape[0]
    wb = conv_w.shape[1]
    ncb = wb // cols
    sec0 = 4 * wa // cols

    def body(bv_ref, bg_ref, w_ref, b_ref, u0_ref, u_ref, ubuf, sh):
        t = pl.program_id(1)

        @pl.when(t == 0)
        def _():
            ubuf[0:HALO_B, :] = jnp.zeros((HALO_B, cols), F32)

        u0 = bv_ref[...].astype(F32) * _sigmoid(bg_ref[...].astype(F32))
        u0_ref[...] = u0
        ubuf[HALO_B:HALO_B + rows, :] = u0
        for lc in range(cols // LANES):
            cs = slice(lc * LANES, (lc + 1) * LANES)
            _shifted_back(sh, ubuf, 8, HALO_B + rows, cs)
            taps = [w_ref[k:k + 1, cs] for k in range(TAPS_B)]
            bias = b_ref[:, cs]

            def row_chunk(rc, carry, cs=cs, taps=taps, bias=bias):
                base = pl.multiple_of(rc * chunk, chunk)
                acc = jnp.zeros((chunk, LANES), F32)
                for k in range(TAPS_B):
                    mq, n = divmod(TAPS_B - 1 - k, 8)
                    acc = acc + taps[k] * _shift_rows(sh, ubuf, n, HALO_B - 8 * mq + base, chunk, cs)
                u_ref[pl.ds(base, chunk), cs] = acc + bias
                return carry

            lax.fori_loop(0, rows // chunk, row_chunk, 0)
        ubuf[0:HALO_B, :] = ubuf[rows:rows + HALO_B, :]

    return pl.pallas_call(
        body, name="mixer_b_conv_fwd", grid=(ncb, s // rows),
        in_specs=[pl.BlockSpec((rows, cols), lambda cb, t: (t, sec0 + cb)),
                  pl.BlockSpec((rows, cols), lambda cb, t: (t, sec0 + ncb + cb)),
                  pl.BlockSpec((HALO_B, cols), lambda cb, t: (0, cb)),
                  pl.BlockSpec((1, cols), lambda cb, t: (0, cb))],
        out_specs=[pl.BlockSpec((rows, cols), lambda cb, t: (t, cb))] * 2,
        out_shape=[jax.ShapeDtypeStruct((s, wb), F32)] * 2,
        scratch_shapes=[pltpu.VMEM((HALO_B + rows, cols), F32), pltpu.VMEM((8, HALO_B + rows, LANES), F32)],
        compiler_params=_params(("parallel", "arbitrary")),
    )(proj, proj, conv_w, conv_b)


def _layernorm_stats(u):
    mu = jnp.mean(u, axis=-1, keepdims=True)
    xc = u - mu
    var = jnp.mean(xc * xc, axis=-1, keepdims=True)
    return xc * lax.rsqrt(var + EPS), lax.rsqrt(var + EPS)


def _mixer_b_gate_fwd(y, u, proj, ln_g, ln_b, wa, rows):
    s, wb = u.shape
    sec_z = (4 * wa + 2 * wb) // wb

    def body(y_in, u_ref, bz_ref, g_ref, b_ref, y_ref):
        uh, _ = _layernorm_stats(u_ref[...])
        ln = uh * g_ref[...] + b_ref[...]
        zv = bz_ref[...].astype(F32)
        y_ref[...] = ((ln * _sigmoid(ln)) * (zv * _sigmoid(zv))).astype(BF16)

    vec = pl.BlockSpec((1, wb), lambda i: (0, 0))
    return pl.pallas_call(
        body, name="mixer_b_gate_fwd", grid=(s // rows,),
        in_specs=[ANY, pl.BlockSpec((rows, wb), lambda i: (i, 0)), pl.BlockSpec((rows, wb), lambda i: (i, sec_z)),
                  vec, vec],
        out_specs=pl.BlockSpec((rows, wb), lambda i: (i, wa // wb)),
        out_shape=jax.ShapeDtypeStruct(y.shape, BF16), input_output_aliases={0: 0},
        compiler_params=_params(("parallel",)),
    )(y, u, proj, ln_g, ln_b)


def _loss_head(x, o, target, gate, final_g, rows):
    s, d = x.shape

    def body(x_ref, o_ref, t_ref, gate_ref, fg_ref, dx2_ref, do_ref, loss_ref, gfg_ref, dgate_ref):
        i = pl.program_id(0)
        ov = o_ref[...]
        x2 = x_ref[...] + gate_ref[...] * ov
        r2 = lax.rsqrt(jnp.mean(x2 * x2, axis=-1, keepdims=True) + EPS)
        xn2 = x2 * r2
        diff = xn2 * fg_ref[...] - t_ref[...]
        dout = diff * (1.0 / d)
        dxn2 = dout * fg_ref[...]
        dx2 = r2 * (dxn2 - xn2 * jnp.mean(dxn2 * xn2, axis=-1, keepdims=True))
        dx2_ref[...] = dx2
        do_ref[...] = (gate_ref[...] * dx2).astype(BF16)
        loss_part = 0.5 * jnp.sum(jnp.mean(diff * diff, axis=-1, keepdims=True), axis=0, keepdims=True)
        gfg_part = jnp.sum(dout * xn2, axis=0, keepdims=True)
        dgate_part = jnp.sum(dx2 * ov, axis=0, keepdims=True)

        @pl.when(i == 0)
        def _():
            loss_ref[...] = jnp.zeros_like(loss_ref)
            gfg_ref[...] = jnp.zeros_like(gfg_ref)
            dgate_ref[...] = jnp.zeros_like(dgate_ref)

        loss_ref[...] += jnp.broadcast_to(loss_part, loss_ref.shape)
        gfg_ref[...] += gfg_part
        dgate_ref[...] += dgate_part

    blk = pl.BlockSpec((rows, d), lambda i: (i, 0))
    vec = pl.BlockSpec((1, d), lambda i: (0, 0))
    return pl.pallas_call(
        body, name="loss_head", grid=(s // rows,),
        in_specs=[blk, blk, blk, vec, vec],
        out_specs=[blk, blk, pl.BlockSpec((1, 128), lambda i: (0, 0)), vec, vec],
        out_shape=[jax.ShapeDtypeStruct((s, d), F32), jax.ShapeDtypeStruct((s, d), BF16),
                   jax.ShapeDtypeStruct((1, 128), F32), jax.ShapeDtypeStruct((1, d), F32),
                   jax.ShapeDtypeStruct((1, d), F32)],
        compiler_params=_params(("arbitrary",)),
    )(x, o, target, gate, final_g)


def _mixer_a_bwd(proj, dy, conv_w, wa, din, rows):
    s = proj.shape[0]
    nt = s // rows
    per_halo = rows // HALO_IN

    def body(ab_ref, ac_ref, ax_ref, az_ref, hc_ref, hx_ref, dy_ref, w_ref, dp_ref, dw_ref, qbuf, dbuf):
        i = pl.program_id(0)

        @pl.when(i == 0)
        def _():
            dbuf[rows:rows + HALO_A, :] = jnp.zeros((HALO_A, wa), F32)
            dw_ref[...] = jnp.zeros_like(dw_ref)

        keep = jnp.where(i == nt - 1, 0.0, 1.0)
        before = hc_ref[...].astype(F32) * hx_ref[...].astype(F32) * keep
        qbuf[0:HALO_A, :] = before[HALO_IN - HALO_A:HALO_IN, :]
        acv, axv = ac_ref[...].astype(F32), ax_ref[...].astype(F32)
        q = acv * axv
        qbuf[HALO_A:HALO_A + rows, :] = q
        conv = w_ref[2:3, :] * q
        for k in range(TAPS_A - 1):
            off = HALO_A - (TAPS_A - 1) + k
            conv = conv + w_ref[k:k + 1, :] * qbuf[off:off + rows, :]
        zv, abv, dyv = az_ref[...].astype(F32), ab_ref[...].astype(F32), dy_ref[...]
        sg = _sigmoid(zv)
        sz = zv * sg
        dp_ref[:, 0:wa] = (dyv * conv * sz).astype(BF16)
        dp_ref[:, 3 * wa:4 * wa] = (dyv * abv * conv * (sg * (1.0 + zv * (1.0 - sg)))).astype(BF16)
        dconv = dyv * abv * sz
        dbuf[0:rows, :] = dconv
        dq = w_ref[2:3, :] * dconv
        for k in range(TAPS_A - 1):
            off = TAPS_A - 1 - k
            dq = dq + w_ref[k:k + 1, :] * dbuf[off:off + rows, :]
        dp_ref[:, wa:2 * wa] = (dq * axv).astype(BF16)
        dp_ref[:, 2 * wa:3 * wa] = (dq * acv).astype(BF16)
        for k in range(TAPS_A):
            off = HALO_A - (TAPS_A - 1) + k
            dw_ref[k:k + 1, :] += jnp.sum(dconv * qbuf[off:off + rows, :], axis=0, keepdims=True)
        dbuf[rows:rows + HALO_A, :] = dbuf[0:HALO_A, :]

    def sec(k):
        return pl.BlockSpec((rows, wa), lambda i, k=k: (nt - 1 - i, k))

    def halo(k):
        return pl.BlockSpec((HALO_IN, wa), lambda i, k=k: (jnp.maximum((nt - 1 - i) * per_halo - 1, 0), k))

    return pl.pallas_call(
        body, name="mixer_a_bwd", grid=(nt,),
        in_specs=[sec(0), sec(1), sec(2), sec(3), halo(1), halo(2),
                  pl.BlockSpec((rows, wa), lambda i: (nt - 1 - i, 0)),
                  pl.BlockSpec((HALO_A, wa), lambda i: (0, 0))],
        out_specs=[pl.BlockSpec((rows, 4 * wa), lambda i: (nt - 1 - i, 0)),
                   pl.BlockSpec((HALO_A, wa), lambda i: (0, 0))],
        out_shape=[jax.ShapeDtypeStruct((s, din), BF16), jax.ShapeDtypeStruct((HALO_A, wa), F32)],
        scratch_shapes=[pltpu.VMEM((HALO_A + rows, wa), F32), pltpu.VMEM((rows + HALO_A, wa), F32)],
        compiler_params=_params(("arbitrary",)),
    )(proj, proj, proj, proj, proj, proj, dy, conv_w)


def _mixer_b_gate_bwd(dproj, dy, u, proj, ln_g, ln_b, wa, rows):
    s, wb = u.shape
    sec_z = (4 * wa + 2 * wb) // wb

    def body(dp_in, dy_ref, u_ref, bz_ref, g_ref, b_ref, dp_ref, du_ref, dg_ref, db_ref, dcb_ref):
        i = pl.program_id(0)
        uh, rs = _layernorm_stats(u_ref[...])
        ln = uh * g_ref[...] + b_ref[...]
        sl = _sigmoid(ln)
        zv = bz_ref[...].astype(F32)
        sg = _sigmoid(zv)
        dyv = dy_ref[...]
        dp_ref[...] = (dyv * (ln * sl) * (sg * (1.0 + zv * (1.0 - sg)))).astype(BF16)
        dln = dyv * (zv * sg) * (sl * (1.0 + ln * (1.0 - sl)))
        duh = dln * g_ref[...]
        du = rs * (duh - jnp.mean(duh, axis=-1, keepdims=True) - uh * jnp.mean(duh * uh, axis=-1, keepdims=True))
        du_ref[...] = du

        @pl.when(i == 0)
        def _():
            dg_ref[...] = jnp.zeros_like(dg_ref)
            db_ref[...] = jnp.zeros_like(db_ref)
            dcb_ref[...] = jnp.zeros_like(dcb_ref)

        dg_ref[...] += jnp.sum(dln * uh, axis=0, keepdims=True)
        db_ref[...] += jnp.sum(dln, axis=0, keepdims=True)
        dcb_ref[...] += jnp.sum(du, axis=0, keepdims=True)

    blk = pl.BlockSpec((rows, wb), lambda i: (i, 0))
    vec = pl.BlockSpec((1, wb), lambda i: (0, 0))
    vshape = jax.ShapeDtypeStruct((1, wb), F32)
    return pl.pallas_call(
        body, name="mixer_b_gate_bwd", grid=(s // rows,),
        in_specs=[ANY, pl.BlockSpec((rows, wb), lambda i: (i, wa // wb)), blk,
                  pl.BlockSpec((rows, wb), lambda i: (i, sec_z)), vec, vec],
        out_specs=[pl.BlockSpec((rows, wb), lambda i: (i, sec_z)), blk, vec, vec, vec],
        out_shape=[jax.ShapeDtypeStruct(dproj.shape, BF16), jax.ShapeDtypeStruct((s, wb), F32), vshape, vshape, vshape],
        input_output_aliases={0: 0},
        compiler_params=_params(("arbitrary",)),
    )(dproj, dy, u, proj, ln_g, ln_b)


def _mixer_b_conv_bwd(dproj, du, u0, proj, conv_w, wa, rows, chunk):
    s, wb = du.shape
    nt = s // rows
    per32 = rows // HALO_B
    sec_v = 4 * wa // wb
    nrc = rows // chunk

    def body(dp_in, du_ref, u0_ref, h0_ref, bv_ref, bg_ref, w_ref, dp_ref, dw_ref, ubuf, dbuf, sh, shf, dwacc):
        i = pl.program_id(0)

        @pl.when(i == 0)
        def _():
            dbuf[rows:rows + HALO_B, :] = jnp.zeros((HALO_B, wb), F32)
            dwacc[...] = jnp.zeros_like(dwacc)

        ubuf[0:HALO_B, :] = h0_ref[...] * jnp.where(i == nt - 1, 0.0, 1.0)
        ubuf[HALO_B:HALO_B + rows, :] = u0_ref[...]
        dbuf[0:rows, :] = du_ref[...]
        for lc in range(wb // LANES):
            cs = slice(lc * LANES, (lc + 1) * LANES)
            _shifted_back(sh, ubuf, 8, HALO_B + rows, cs)
            _shifted_fwd(shf, dbuf, 0, rows + HALO_B - 8, cs)
            taps = [w_ref[k:k + 1, cs] for k in range(TAPS_B)]

            def conv_rows(rc, c0, cs=cs, taps=taps, lc=lc):
                base = pl.multiple_of(rc * chunk, chunk)
                acc = jnp.zeros((chunk, LANES), F32)
                for k in range(TAPS_B):
                    mq, n = divmod(TAPS_B - 1 - k, 8)
                    acc = acc + taps[k] * _shift_rows(shf, dbuf, n, base + 8 * mq, chunk, cs)
                sg = _sigmoid(bg_ref[pl.ds(base, chunk), cs].astype(F32))
                bv = bv_ref[pl.ds(base, chunk), cs].astype(F32)
                dp_ref[pl.ds(base, chunk), cs] = (acc * sg).astype(BF16)
                dp_ref[pl.ds(base, chunk), wb + lc * LANES:wb + (lc + 1) * LANES] = (
                    acc * bv * sg * (1.0 - sg)).astype(BF16)
                return c0

            lax.fori_loop(0, nrc, conv_rows, 0)

            def dw_rows(rc, accs, cs=cs):
                base = pl.multiple_of(rc * chunk, chunk)
                du_c = dbuf[pl.ds(base, chunk), cs]
                out = []
                for k in range(TAPS_B):
                    mq, n = divmod(TAPS_B - 1 - k, 8)
                    prod = du_c * _shift_rows(sh, ubuf, n, HALO_B - 8 * mq + base, chunk, cs)
                    out.append(accs[k] + jnp.sum(prod.reshape(chunk // SUBLANES, SUBLANES, LANES), axis=0))
                return tuple(out)

            accs = lax.fori_loop(0, nrc, dw_rows, tuple(jnp.zeros((SUBLANES, LANES), F32) for _ in range(TAPS_B)))
            for k in range(TAPS_B):
                dwacc[k * SUBLANES:(k + 1) * SUBLANES, cs] += accs[k]
        dbuf[rows:rows + HALO_B, :] = dbuf[0:HALO_B, :]

        @pl.when(i == nt - 1)
        def _():
            for k in range(HALO_B):
                dw_ref[k:k + 1, :] = jnp.sum(dwacc[k * SUBLANES:(k + 1) * SUBLANES, :], axis=0, keepdims=True)

    def rev(cols_blk):
        return pl.BlockSpec((rows, wb), lambda i, cb=cols_blk: (nt - 1 - i, cb))

    return pl.pallas_call(
        body, name="mixer_b_conv_bwd", grid=(nt,),
        in_specs=[ANY, rev(0), rev(0),
                  pl.BlockSpec((HALO_B, wb), lambda i: (jnp.maximum((nt - 1 - i) * per32 - 1, 0), 0)),
                  rev(sec_v), rev(sec_v + 1), pl.BlockSpec((HALO_B, wb), lambda i: (0, 0))],
        out_specs=[pl.BlockSpec((rows, 2 * wb), lambda i: (nt - 1 - i, sec_v // 2)),
                   pl.BlockSpec((HALO_B, wb), lambda i: (0, 0))],
        out_shape=[jax.ShapeDtypeStruct(dproj.shape, BF16), jax.ShapeDtypeStruct((HALO_B, wb), F32)],
        input_output_aliases={0: 0},
        scratch_shapes=[pltpu.VMEM((HALO_B + rows, wb), F32), pltpu.VMEM((rows + HALO_B, wb), F32),
                        pltpu.VMEM((8, HALO_B + rows, LANES), F32), pltpu.VMEM((8, rows + HALO_B, LANES), F32),
                        pltpu.VMEM((HALO_B * SUBLANES, wb), F32)],
        compiler_params=_params(("arbitrary",)),
    )(dproj, du, u0, u0, proj, proj, conv_w)


def _prenorm_bwd(x, r, dh, dx2, norm_g, scale, rows):
    s, d = x.shape

    def body(x_ref, r_ref, dh_ref, dx2_ref, g_ref, sc_ref, gx_ref, dsh_ref, dsc_ref, dg_ref):
        i = pl.program_id(0)
        rv = r_ref[...]
        xn = x_ref[...] * rv
        dhv = dh_ref[...]
        one_sc = 1.0 + sc_ref[...]
        dxn = dhv * one_sc * g_ref[...]
        gx_ref[...] = dx2_ref[...] + rv * (dxn - xn * jnp.mean(dxn * xn, axis=-1, keepdims=True))

        @pl.when(i == 0)
        def _():
            dsh_ref[...] = jnp.zeros_like(dsh_ref)
            dsc_ref[...] = jnp.zeros_like(dsc_ref)
            dg_ref[...] = jnp.zeros_like(dg_ref)

        dsh_ref[...] += jnp.sum(dhv, axis=0, keepdims=True)
        dsc_ref[...] += jnp.sum(dhv * (xn * g_ref[...]), axis=0, keepdims=True)
        dg_ref[...] += jnp.sum(dhv * one_sc * xn, axis=0, keepdims=True)

    blk = pl.BlockSpec((rows, d), lambda i: (i, 0))
    vec = pl.BlockSpec((1, d), lambda i: (0, 0))
    vshape = jax.ShapeDtypeStruct((1, d), F32)
    return pl.pallas_call(
        body, name="prenorm_bwd", grid=(s // rows,),
        in_specs=[blk, pl.BlockSpec((rows, 1), lambda i: (i, 0)), blk, blk, vec, vec],
        out_specs=[blk, vec, vec, vec],
        out_shape=[jax.ShapeDtypeStruct((s, d), F32), vshape, vshape, vshape],
        compiler_params=_params(("arbitrary",)),
    )(x, r, dh, dx2, norm_g, scale)


def _pad_rows(a, rows):
    return jnp.pad(a, ((0, rows - a.shape[0]), (0, 0)))


def _tile(n, want):
    t = min(n, want)
    while n % t:
        t -= 1
    return t


def kernel(x, c, norm_g, w_ada, b_ada, w_in, conv_a_w, conv_b_w, conv_b_b, ln_b_g, ln_b_b, w_out, final_g, loss_target, m_norm_g, m_w_ada, m_b_ada, m_w_in, m_conv_a_w, m_conv_b_w, m_conv_b_b, m_ln_b_g, m_ln_b_b, m_w_out, m_final_g, v_norm_g, v_w_ada, v_b_ada, v_w_in, v_conv_a_w, v_conv_b_w, v_conv_b_b, v_ln_b_g, v_ln_b_b, v_w_out, v_final_g):
    s, d = x.shape[1], x.shape[2]
    wa = conv_b_b.shape[-1]
    dmix = 2 * wa
    ns = w_in.shape[-1]
    din = N_CHIPS * ns
    r4 = w_out.shape[1]
    na = w_ada.shape[-1]
    wsh = conv_a_w.shape[-1]
    px, py, pc = _position()
    chip = 2 * px + py
    me = 4 * px + 2 * py + pc
    pos = jnp.stack([chip, pc]).astype(jnp.int32)
    x2d = x.reshape(s, d)
    target = loss_target.reshape(s, d)

    hc, ho, hrow = ns // 2, r4 // 2, d // 2
    _, cidx = _other_chips(px, py)
    hq = hc // 2
    win4 = _cast_quarters(w_in[0], _tile(d, 512), "cast_w_in")
    wout_bf = _cast_bf16(w_out[0], _tile(r4, 512), "cast_w_out")

    def gather_plan(b):
        x, y, cc = _position()
        xn, yn = (1 - x, y, cc), (x, 1 - y, cc)
        q0, q1 = b[0].at[2 * cc], b[0].at[2 * cc + 1]
        return [(q0, b[1], xn), (q1, b[4], yn), (q1, b[2], xn), (q0, b[3], yn)]

    def gather_sent(b):
        return [(src, src, dev) for src, _, dev in gather_plan(list(b) + [None] * 4)]

    def out_plan(b):
        x, y, cc = _position()
        chips, _ = _other_chips(x, y)
        return [(b[0].at[pl.ds(cc * ho, ho), :], b[1 + k], (cx, cy, cc)) for k, (cx, cy) in enumerate(chips)]

    def out_sent(b):
        return [(src, src, dev) for src, _, dev in out_plan(list(b) + [None] * 3)]

    def onward_plan(b):
        x, y, cc = _position()
        sib = (x, y, 1 - cc)
        return [(b[0], b[2], (x, 1 - y, cc)), (b[1], b[3], (1 - x, y, cc)), (b[0], b[4], sib), (b[1], b[5], sib)]

    pairs = lambda n: _to_sibling([lambda ref, cc: ref] * n)

    c8 = jnp.broadcast_to(c, (8, d))
    cw = jnp.concatenate([_pad_rows(conv_a_w[0], HALO_A), _pad_rows(conv_b_w[0], HALO_B)], axis=0)
    c_all, cw_all = _gather_cond(c8, cw)
    c_rows = c_all[:, 0, :]
    cw_full = jnp.transpose(cw_all, (1, 0, 2)).reshape(HALO_A + HALO_B, wa)
    conv_a_full, conv_b_full = cw_full[:HALO_A], cw_full[HALO_A:]

    b_ada_sh = lax.dynamic_slice(b_ada, (0, chip * na), (1, na))
    mod_part = _modulation(_pad_rows(c_rows, 2 * N_DEV), w_ada[0], b_ada_sh, _tile(na, 512), "modulation")[:N_DEV]
    mod_all = _exchange_mod(mod_part)
    mod = lax.dynamic_index_in_dim(mod_all, me, axis=1, keepdims=False).reshape(1, 3 * d)
    shift, scale, gate = mod[:, :d], mod[:, d:2 * d], mod[:, 2 * d:]

    def quarter():
        return lax.empty((d, hq), BF16)

    g_sems, g_bufs, g_tok = _start_copies(
        "gather_start", gather_plan, 4, [win4] + [quarter() for _ in range(4)], after=[mod_all])
    win4, (x0, x1, y0, y1) = g_bufs[0], g_bufs[1:5]

    h, r = _prenorm(x2d, norm_g, scale, shift + g_tok[0, 0], _tile(s, 256))
    bm = _tile(s, 1024)
    pieces = [None, None]

    def piece(slot, half, quarters, name, own_half=None):
        where = jnp.reshape(2 * slot + half, (1,)).astype(jnp.int32)
        pieces[:] = _proj_piece(where, h, quarters, pieces[0], pieces[1], din, 2 * N_CHIPS, _tile(s, 512), name,
                                own_half=own_half)
        return pieces[0]

    proj = piece(chip, 0, (win4, win4), "proj_own0", own_half=0)
    proj = piece(chip, 1, (win4, win4), "proj_own1", own_half=1)
    x0, y1 = _wait_copies("gather_wait_a", _landed, [x0, y1], g_sems[0:4], after=[proj], send=False)
    on_sems, (x0, y1, dg0, dg1, sx0, sy1), _ = _start_copies(
        "pass_on_a", onward_plan, 4, [x0, y1] + [quarter() for _ in range(4)])
    def two_quarters(half, quarters, places, name):
        where = jnp.stack([2 * cidx[0] + half, 2 * cidx[1] + half]).astype(jnp.int32)
        pieces[:] = _proj_quarter_pair(where, h, quarters, places, pieces[0], pieces[1], bm, name)
        return pieces[0]

    proj = two_quarters(pc, (x0, y1), (0, 1), "proj_first")
    x1, y0 = _wait_copies("gather_wait_b", _landed, [x1, y0], g_sems[4:8], after=[proj], send=False)
    pb_sems, (x1, sx1, y0, sy0), _ = _start_copies("pass_on_b", pairs(2), 2, [x1, quarter(), y0, quarter()])
    go_sems, go_bufs, _ = _start_copies("gather_out_start", out_plan, 3,
                                        [wout_bf] + [lax.empty((ho, d), BF16) for _ in range(3)], after=[x1])
    wout_bf, lo = go_bufs[0], go_bufs[1:4]
    proj = two_quarters(pc, (x1, y0), (1, 0), "proj_second")
    sx0, sy1 = _wait_copies("pass_wait_a", _landed, [sx0, sy1], on_sems[4:8], after=[proj], send=False)
    x1, sx1, y0, sy0 = _wait_copies("pass_wait_b", pairs(2), [x1, sx1, y0, sy0], pb_sems, after=[sx0])
    proj = piece(cidx[0], 1 - pc, (sx0, sx1), "proj_xb")
    proj = piece(cidx[1], 1 - pc, (sy0, sy1), "proj_yb")
    x0, y1, dg0, dg1 = _wait_copies("diag_wait", lambda b: onward_plan(list(b) + [None, None])[:2],
                                    [x0, y1, dg0, dg1], on_sems[0:4], after=[proj])
    x0, y1 = _wait_copies("pass_sent_a", lambda b: [(b[0], b[0], (0, 0, 0)), (b[1], b[1], (0, 0, 0))],
                          [x0, y1], on_sems[4:8], after=[dg0], recv=False)
    pd_sems, (dg0, sd0, dg1, sd1), _ = _start_copies("pass_on_d", pairs(2), 2, [dg0, quarter(), dg1, quarter()],
                                                     after=[x0])
    proj = piece(cidx[2], pc, (dg0, dg1), "proj_da")
    dg0, sd0, dg1, sd1 = _wait_copies("pass_wait_d", pairs(2), [dg0, sd0, dg1, sd1], pd_sems, after=[proj])
    proj = piece(cidx[2], 1 - pc, (sd0, sd1), "proj_db")
    win_full = pieces[1]

    lo = _wait_copies("gather_wait_out", _landed, lo, go_sems, after=[proj], send=False)
    o_sems, o_bufs, o_tok = _start_copies(
        "pass_on_out", pairs(3), 3, [b for k in range(3) for b in (lo[k], lax.empty((ho, d), BF16))])
    win4, = _wait_copies("gather_wait_sent", gather_sent, [win4], g_sems, after=[o_tok], recv=False)
    wout_bf, = _wait_copies("gather_out_sent", out_sent, [wout_bf], go_sems, after=[win4], recv=False)

    def slot_index(k, chip_, cc, others):
        if k == 0:
            return pl.ds(2 * chip_, 2)
        return 2 * others[(k - 1) % 3] + (cc if k <= 3 else 1 - cc)

    y = _mixer_a_fwd(proj, conv_a_full, wa, _tile(s, 512), _tile(wa, 512))
    u0, u = _mixer_b_conv_fwd(proj, conv_b_full, conv_b_b, wa, _tile(s, 512), _tile(wa, 256), 64)
    y = _mixer_b_gate_fwd(y, u, proj, ln_b_g, ln_b_b, wa, _tile(s, 256))
    o_bufs = _wait_copies("pass_wait_out", pairs(3), o_bufs, o_sems, after=[y])
    wout_full = _assemble("assemble_w_out", [wout_bf.reshape(2, ho, d)] + o_bufs[0::2] + o_bufs[1::2],
                          jax.ShapeDtypeStruct((2 * N_CHIPS, ho, d), BF16), slot_index)
    wout2d = wout_full.reshape(dmix, d)
    bd = _tile(d, 1024)
    o = _matmul(
        y, wout2d, grid=(s // bm, d // bd, 1),
        a_spec=pl.BlockSpec((bm, dmix), lambda i, j, k: (i, 0)),
        b_spec=pl.BlockSpec((dmix, bd), lambda i, j, k: (0, j)),
        o_spec=pl.BlockSpec((bm, bd), lambda i, j, k: (i, j)),
        out_shape=jax.ShapeDtypeStruct((s, d), F32), dims=((1,), (0,)), name="out_proj")
    dx2, do, loss_p, gfg_p, dgate_p = _loss_head(x2d, o, target, gate, final_g.reshape(1, d), _tile(s, 128))

    be = _tile(dmix, 1024)
    g_wout = _matmul(
        y, do, grid=(dmix // be, d // bd, 1),
        a_spec=pl.BlockSpec((s, be), lambda i, j, k: (0, i)),
        b_spec=pl.BlockSpec((s, bd), lambda i, j, k: (0, j)),
        o_spec=pl.BlockSpec((be, bd), lambda i, j, k: (i, j)),
        out_shape=jax.ShapeDtypeStruct((dmix, d), F32), dims=((0,), (0,)), name="grad_w_out")
    swap_out = _to_sibling([lambda ref, cc: ref.at[:, pl.ds((1 - cc) * ho, ho), :]])
    so_sems, (g_wout3, ra_out), so_tok = _start_copies(
        "swap_out_start", swap_out, 1, [g_wout.reshape(N_CHIPS, r4, d), lax.empty((N_CHIPS, ho, d), F32)])
    dy = _matmul(
        do, wout2d, grid=(s // bm, dmix // be, 1),
        a_spec=pl.BlockSpec((bm, d), lambda i, j, k: (i, 0)),
        b_spec=pl.BlockSpec((be, d), lambda i, j, k: (j, 0)),
        o_spec=pl.BlockSpec((bm, be), lambda i, j, k: (i, j)),
        out_shape=jax.ShapeDtypeStruct((s, dmix), F32), dims=((1,), (1,)), name="dy", after=[so_tok])
    g_wout3, ra_out = _wait_copies("swap_out_wait", swap_out, [g_wout3, ra_out], so_sems, after=[dy])
    q_out = _chip_partial(pos, g_wout3, ra_out, _tile(ho, 256), "chip_partial_w_out")
    po_sems, po_bufs, po_tok = _start_copies(
        "send_out_start", _slots_to_chips, 3, [q_out] + [lax.empty((ho, d), BF16) for _ in range(3)])
    dproj, dwa_p = _mixer_a_bwd(proj, dy, conv_a_full + po_tok[0, 0], wa, din, _tile(s, 128))
    dproj, du, dlng_p, dlnb_p, dcb_p = _mixer_b_gate_bwd(dproj, dy, u, proj, ln_b_g, ln_b_b, wa, _tile(s, 128))
    dproj, dwb_p = _mixer_b_conv_bwd(dproj, du, u0, proj, conv_b_full, wa, _tile(s, 256), 64)

    slots = [cidx[0], cidx[1], cidx[2], chip]
    q, rb, snd = [None] * 3, [None] * 3, [None] * 3
    after = []
    for pair in ((0, 1), (2, 3)):
        given = {}
        for k in pair:
            theirs = _grad_slot(jnp.stack([slots[k], 1 - pc]).astype(jnp.int32), h, dproj, after, ns,
                                _tile(hrow, 512), hc, f"grad_w_in{k}a")
            sems, bufs, tok = _start_copies(f"swap_in_start{k}", pairs(1), 1, [theirs, lax.empty((hrow, ns), F32)])
            given[k] = (sems, bufs)
            after = [tok]
        for k in pair:
            sems, bufs = given[k]
            _, from_sibling = _wait_copies(f"swap_in_wait{k}", pairs(1), bufs, sems, after=after)
            mine = _grad_slot(jnp.stack([slots[k], pc]).astype(jnp.int32), h, dproj, [], ns, _tile(hrow, 512), hc,
                              f"grad_w_in{k}b", add=from_sibling, out_dtype=BF16 if k < 3 else F32)
            if k < 3:
                snd[k], (q[k], rb[k]), tok = _start_copies(f"send_in_start{k}", _to_chip(k), 1,
                                                           [mine, lax.empty((hrow, ns), BF16)])
                after = [tok]
            else:
                own_half, after = mine, [mine]
    dh = _matmul_by_pieces(dproj, win_full, bm, bd, "dh", after=after)
    grad_x, dshift_p, dscale_p, gng_p = _prenorm_bwd(x2d, r, dh, dx2, norm_g, scale, _tile(s, 128))

    def rows_of(v):
        return _pad_rows(v.reshape(-1, wa), 8 * ((v.size // wa + 7) // 8))

    dmod = jnp.concatenate([dshift_p, dscale_p, dgate_p], axis=1)
    parts = [gng_p, dmod, dwa_p, dwb_p, dcb_p, dlng_p, dlnb_p, gfg_p,
             jnp.broadcast_to(loss_p[:, :1], (1, wa))]
    starts, packed = [], []
    for p in parts:
        starts.append(sum(q.shape[0] for q in packed))
        packed.append(rows_of(p) if p.shape[0] == 1 else p)
    small_sum, small_all = _gather_small(jnp.concatenate(packed, axis=0))

    def summed(k, rows):
        return small_sum[starts[k]:starts[k] + rows]

    grad_norm_g = summed(0, d // wa).reshape(1, d)
    grad_b_ada = summed(1, 3 * d // wa).reshape(1, 3 * d)
    grad_conv_a_full = summed(2, TAPS_A)
    grad_conv_b_full = summed(3, TAPS_B)
    grad_conv_b_b = summed(4, 1)
    grad_ln_b_g = summed(5, 1)
    grad_ln_b_b = summed(6, 1)
    grad_final_g = summed(7, d // wa).reshape(d)
    loss = summed(8, 1)[0, 0]
    grad_conv_a_w = lax.dynamic_slice(grad_conv_a_full, (0, chip * wsh), (TAPS_A, wsh))
    grad_conv_b_w = lax.dynamic_slice(grad_conv_b_full, (0, chip * wsh), (TAPS_B, wsh))
    dmod_all = small_all[:, starts[1]:starts[1] + 3 * d // wa, :].reshape(N_DEV, 3 * d)
    dmod_sh = lax.dynamic_slice(dmod_all, (0, chip * na), (N_DEV, na))

    def pairs_to_chips(b):
        x, y, cc = _position()
        chips, _ = _other_chips(x, y)
        return [(b[2 * k], b[2 * k + 1], (cx, cy, cc)) for k, (cx, cy) in enumerate(chips)]

    po_bufs = _wait_copies("send_out_wait", _slots_to_chips, po_bufs, po_sems, after=[small_sum])
    gh_out = _final_half(pos, g_wout3, ra_out, po_bufs[1:], _tile(ho, 256), "final_half_w_out")
    in_bufs = _wait_copies("send_in_wait", pairs_to_chips, [b for k in range(3) for b in (q[k], rb[k])],
                           snd[0] + snd[1] + snd[2], after=[small_sum])
    gh_in = _final_sum(pos, own_half, in_bufs[1::2], _tile(hrow, 256), "final_half_w_in")
    sh_sems, sh_bufs, sh_tok = _start_copies("share_start", _halves_to_sibling, 2, [gh_in, gh_out])

    grad_w_ada, d_wada, nm_wada, nv_wada = _adam_ada(c_rows.T, dmod_sh + sh_tok[0, 0], w_ada[0], m_w_ada[0],
                                                     v_w_ada[0], _tile(d, 128), "adam_w_ada")
    gw_in, gw_out = _wait_copies("share_wait", _halves_to_sibling, sh_bufs, sh_sems, after=[d_wada])
    grad_w_in, d_win, nm_win, nv_win = _adam(w_in[0], gw_in, m_w_in[0], v_w_in[0], _tile(d, 128), "adam_w_in",
                                             return_grad=True)
    grad_w_out, d_wout, nm_wout, nv_wout = _adam(w_out[0], gw_out, m_w_out[0], v_w_out[0], _tile(r4, 128),
                                                 "adam_w_out", return_grad=True)

    def small_adam(w, g, m, v, name):
        shape = w.shape
        w2 = w.reshape(-1, shape[-1])
        out = _adam(w2, g.reshape(w2.shape), m.reshape(w2.shape), v.reshape(w2.shape), w2.shape[0], name)
        return [o_.reshape(shape) for o_ in out]

    small = {
        "norm_g": small_adam(norm_g, grad_norm_g, m_norm_g, v_norm_g, "adam_norm_g"),
        "b_ada": small_adam(b_ada, grad_b_ada, m_b_ada, v_b_ada, "adam_b_ada"),
        "conv_a_w": small_adam(conv_a_w, grad_conv_a_w, m_conv_a_w, v_conv_a_w, "adam_conv_a_w"),
        "conv_b_w": small_adam(conv_b_w, grad_conv_b_w, m_conv_b_w, v_conv_b_w, "adam_conv_b_w"),
        "conv_b_b": small_adam(conv_b_b, grad_conv_b_b, m_conv_b_b, v_conv_b_b, "adam_conv_b_b"),
        "ln_b_g": small_adam(ln_b_g, grad_ln_b_g, m_ln_b_g, v_ln_b_g, "adam_ln_b_g"),
        "ln_b_b": small_adam(ln_b_b, grad_ln_b_b, m_ln_b_b, v_ln_b_b, "adam_ln_b_b"),
        "final_g": small_adam(final_g.reshape(1, d), grad_final_g, m_final_g.reshape(1, d),
                              v_final_g.reshape(1, d), "adam_final_g"),
    }
    small["final_g"] = [o_.reshape(d) for o_ in small["final_g"]]
    big = {
        "w_ada": [a[None] for a in (d_wada, nm_wada, nv_wada)],
        "w_in": [a[None] for a in (d_win, nm_win, nv_win)],
        "w_out": [a[None] for a in (d_wout, nm_wout, nv_wout)],
    }
    upd = {**small, **big}
    order = ["norm_g", "w_ada", "b_ada", "w_in", "conv_a_w", "conv_b_w", "conv_b_b", "ln_b_g", "ln_b_b",
             "w_out", "final_g"]
    grads = {
        "norm_g": grad_norm_g, "w_ada": grad_w_ada[None], "b_ada": grad_b_ada, "w_in": grad_w_in[None],
        "conv_a_w": grad_conv_a_w[None], "conv_b_w": grad_conv_b_w[None], "conv_b_b": grad_conv_b_b,
        "ln_b_g": grad_ln_b_g, "ln_b_b": grad_ln_b_b, "w_out": grad_w_out[None], "final_g": grad_final_g,
    }
    return (loss, grad_x.reshape(1, s, d), *[grads[n] for n in order], *[upd[n][0] for n in order],
            *[upd[n][1] for n in order], *[upd[n][2] for n in order])
```

```python
import functools

import jax
import jax.numpy as jnp
from jax import lax
from jax.experimental import pallas as pl
from jax.experimental.pallas import tpu as pltpu

F32 = jnp.float32
BF16 = jnp.bfloat16
EPS = 1e-6
N_CHIPS = 4
N_DEV = 8
TAPS_A = 3
TAPS_B = 31
HALO_A = 8
HALO_IN = 16
HALO_B = 32
LANES = 128
SUBLANES = 8
ADAM_LR = 0.001
ADAM_B1 = 0.9
ADAM_B2 = 0.999
ADAM_EPS = 1e-08
ADAM_WD = 0.01
ADAM_STEP = 10
VMEM_LIMIT = 56 * 1024 * 1024
MESH = pl.DeviceIdType.MESH
ANY = pl.BlockSpec(memory_space=pl.ANY)
VMEM = pl.BlockSpec(memory_space=pltpu.VMEM)
HBM_SPEC = pl.BlockSpec(memory_space=pltpu.HBM)
SEM_SPEC = pl.BlockSpec(memory_space=pltpu.SEMAPHORE)
EFFECT = pltpu.SideEffectType.DATAFLOW_SIDE_EFFECTING


def _params(sem=None):
    return pltpu.CompilerParams(dimension_semantics=sem, vmem_limit_bytes=VMEM_LIMIT)


def _sigmoid(v):
    return jax.nn.sigmoid(v)


def _position():
    return lax.axis_index("x"), lax.axis_index("y"), lax.axis_index("c")


def _rcopy(src, dst, ssem, rsem, dev):
    return pltpu.make_async_remote_copy(src_ref=src, dst_ref=dst, send_sem=ssem, recv_sem=rsem,
                                        device_id=dev, device_id_type=MESH)


def _other_chips(x, y):
    chips = [(1 - x, y), (x, 1 - y), (1 - x, 1 - y)]
    return chips, [2 * cx + cy for cx, cy in chips]


def _cast_bf16(a, rows, name):
    m, n = a.shape

    def body(a_ref, o_ref):
        o_ref[...] = a_ref[...].astype(BF16)

    return pl.pallas_call(
        body, name=name, grid=(m // rows,),
        in_specs=[pl.BlockSpec((rows, n), lambda i: (i, 0))],
        out_specs=pl.BlockSpec((rows, n), lambda i: (i, 0)),
        out_shape=jax.ShapeDtypeStruct((m, n), BF16),
        compiler_params=_params(("parallel",)),
    )(a)


def _cast_quarters(a, rows, name):
    m, n = a.shape
    hq = n // 4

    def body(a_ref, o_ref):
        o_ref[...] = a_ref[...].astype(BF16)

    return pl.pallas_call(
        body, name=name, grid=(4, m // rows),
        in_specs=[pl.BlockSpec((rows, hq), lambda q, i: (i, q))],
        out_specs=pl.BlockSpec((None, rows, hq), lambda q, i: (q, i, 0)),
        out_shape=jax.ShapeDtypeStruct((4, m, hq), BF16),
        compiler_params=_params(("parallel", "parallel")),
    )(a)


def _proj_piece(where, h, quarters, proj, w_all, din, n_pieces, bm, name, own_half=None):
    s, d = h.shape
    hq = quarters[0].shape[-1]
    nm = s // bm
    if own_half is None:
        q_specs = [pl.BlockSpec((d, hq), lambda i, p: (0, 0), pipeline_mode=pl.Buffered(1))] * 2
    else:
        q_specs = [pl.BlockSpec((None, d, hq), lambda i, p, k=k: (2 * own_half + k, 0, 0),
                                pipeline_mode=pl.Buffered(1)) for k in range(2)]

    def body(p_ref, h_ref, q0_ref, q1_ref, *rest):
        o_ref, wall_ref, wbuf, sem = rest[-4:]
        i = pl.program_id(0)
        filed = pltpu.make_async_copy(wbuf, wall_ref.at[p_ref[0]], sem)

        @pl.when(i == 0)
        def _():
            wbuf[:, 0:hq] = q0_ref[...]
            wbuf[:, hq:2 * hq] = q1_ref[...]
            filed.start()

        o_ref[...] = jnp.dot(h_ref[...], wbuf[...], preferred_element_type=F32).astype(BF16)

        @pl.when(i == nm - 1)
        def _():
            filed.wait()

    args, extra, alias = [where, h, quarters[0], quarters[1]], [], {}
    if proj is not None:
        args, extra, alias = args + [proj, w_all], [ANY, ANY], {4: 0, 5: 1}
    return pl.pallas_call(
        body, name=name,
        grid_spec=pltpu.PrefetchScalarGridSpec(
            num_scalar_prefetch=1, grid=(nm,),
            in_specs=[pl.BlockSpec((bm, d), lambda i, p: (i, 0))] + q_specs + extra,
            out_specs=[pl.BlockSpec((bm, 2 * hq), lambda i, p: (i, p[0])), ANY],
            scratch_shapes=[pltpu.VMEM((d, 2 * hq), BF16), pltpu.SemaphoreType.DMA]),
        out_shape=[jax.ShapeDtypeStruct((s, din), BF16), jax.ShapeDtypeStruct((n_pieces, d, 2 * hq), BF16)],
        input_output_aliases=alias,
        compiler_params=_params(("arbitrary",)),
    )(*args)


def _proj_quarter_pair(where, h, quarters, places, proj, w_all, bm, name):
    s, d = h.shape
    hq = quarters[0].shape[-1]
    nm = s // bm
    q_spec = pl.BlockSpec((d, hq), lambda j, i, w: (0, 0), pipeline_mode=pl.Buffered(1))

    def body(w_ref, h_ref, qa_ref, qb_ref, proj_in, wall_in, o_ref, wall_ref, sem):
        j, i = pl.program_id(0), pl.program_id(1)
        for k, q_ref in enumerate((qa_ref, qb_ref)):
            @pl.when(j == k)
            def _(k=k, q_ref=q_ref):
                o_ref[...] = jnp.dot(h_ref[...], q_ref[...], preferred_element_type=F32).astype(BF16)
                filed = pltpu.make_async_copy(
                    q_ref, wall_ref.at[w_ref[k], :, pl.ds(places[k] * hq, hq)], sem.at[k])

                @pl.when(i == 0)
                def _():
                    filed.start()

                @pl.when(i == nm - 1)
                def _():
                    filed.wait()

    return pl.pallas_call(
        body, name=name,
        grid_spec=pltpu.PrefetchScalarGridSpec(
            num_scalar_prefetch=1, grid=(2, nm),
            in_specs=[pl.BlockSpec((bm, d), lambda j, i, w: (i, 0)), q_spec, q_spec, ANY, ANY],
            out_specs=[pl.BlockSpec((bm, hq),
                                    lambda j, i, w: (i, 2 * w[j] + places[0] + j * (places[1] - places[0]))), ANY],
            scratch_shapes=[pltpu.SemaphoreType.DMA((2,))]),
        out_shape=[jax.ShapeDtypeStruct(proj.shape, proj.dtype), jax.ShapeDtypeStruct(w_all.shape, w_all.dtype)],
        input_output_aliases={4: 0, 5: 1},
        compiler_params=_params(("arbitrary", "arbitrary")),
    )(where, h, quarters[0], quarters[1], proj, w_all)


def _grad_slot(where, h, dproj, after, ns, bd, bn, name, add=None, out_dtype=F32):
    s, d = h.shape
    nb = ns // bn
    ni = d // 2 // bd
    extra = [] if add is None else [add]

    def body(where_ref, h_ref, dp_ref, *rest):
        acc = lax.dot_general(h_ref[...], dp_ref[...], (((0,), (0,)), ((), ())), preferred_element_type=F32)
        if add is not None:
            acc = acc + rest[0][...]
        rest[-1][...] = acc.astype(out_dtype)

    return pl.pallas_call(
        body, name=name,
        grid_spec=pltpu.PrefetchScalarGridSpec(
            num_scalar_prefetch=1, grid=(nb, ni),
            in_specs=[pl.BlockSpec((s, bd), lambda j, i, w: (0, w[1] * ni + i)),
                      pl.BlockSpec((s, bn), lambda j, i, w: (0, w[0] * nb + j))]
            + [pl.BlockSpec((bd, bn), lambda j, i, w: (i, j))] * len(extra) + [ANY] * len(after),
            out_specs=pl.BlockSpec((bd, bn), lambda j, i, w: (i, j))),
        out_shape=jax.ShapeDtypeStruct((d // 2, ns), out_dtype),
        compiler_params=_params(("parallel", "parallel")),
    )(where, h, dproj, *extra, *after)


def _matmul(a, b, *, grid, a_spec, b_spec, o_spec, out_shape, dims, name, after=()):
    nk = grid[2]
    n_after = len(after)

    def body(a_ref, b_ref, *rest):
        o_ref, acc = rest[n_after], rest[n_after + 1:]
        p = lax.dot_general(a_ref[...], b_ref[...], (dims, ((), ())), preferred_element_type=F32)
        if nk == 1:
            o_ref[...] = p.astype(o_ref.dtype)
        else:
            acc_ref, = acc
            k = pl.program_id(2)

            @pl.when(k == 0)
            def _():
                acc_ref[...] = p

            @pl.when(k > 0)
            def _():
                acc_ref[...] += p

            @pl.when(k == nk - 1)
            def _():
                o_ref[...] = acc_ref[...].astype(o_ref.dtype)

    block = [d for d in o_spec.block_shape if d is not None]
    scratch = [pltpu.VMEM(tuple(block), F32)] if nk > 1 else []
    return pl.pallas_call(
        body, name=name, grid=grid, in_specs=[a_spec, b_spec] + [ANY] * n_after, out_specs=o_spec,
        out_shape=out_shape, scratch_shapes=scratch,
        compiler_params=_params(("parallel", "parallel", "arbitrary")),
    )(a, b, *after)


def _matmul_by_pieces(a, pieces, bm, bn, name, after=()):
    s = a.shape[0]
    n, rows, width = pieces.shape
    nk = n // 2
    n_after = len(after)

    def body(a_ref, b_ref, *rest):
        o_ref, acc_ref = rest[n_after], rest[n_after + 1]
        k = pl.program_id(2)
        nt = (((1,), (1,)), ((), ()))
        p = (lax.dot_general(a_ref[:, 0:width], b_ref[0], nt, preferred_element_type=F32)
             + lax.dot_general(a_ref[:, width:2 * width], b_ref[1], nt, preferred_element_type=F32))

        @pl.when(k == 0)
        def _():
            acc_ref[...] = p

        @pl.when(k > 0)
        def _():
            acc_ref[...] += p

        @pl.when(k == nk - 1)
        def _():
            o_ref[...] = acc_ref[...]

    return pl.pallas_call(
        body, name=name, grid=(s // bm, rows // bn, nk),
        in_specs=[pl.BlockSpec((bm, 2 * width), lambda i, j, k: (i, k)),
                  pl.BlockSpec((2, bn, width), lambda i, j, k: (k, j, 0))] + [ANY] * n_after,
        out_specs=pl.BlockSpec((bm, bn), lambda i, j, k: (i, j)),
        out_shape=jax.ShapeDtypeStruct((s, rows), F32), scratch_shapes=[pltpu.VMEM((bm, bn), F32)],
        compiler_params=_params(("parallel", "parallel", "arbitrary")),
    )(a, pieces, *after)


def _adam_math(w, g, m, v):
    m = ADAM_B1 * m + (1.0 - ADAM_B1) * g
    v = ADAM_B2 * v + (1.0 - ADAM_B2) * (g * g)
    m_hat = m / (1.0 - ADAM_B1 ** ADAM_STEP)
    v_hat = v / (1.0 - ADAM_B2 ** ADAM_STEP)
    delta = -ADAM_LR * (m_hat / (jnp.sqrt(v_hat) + ADAM_EPS) + ADAM_WD * w)
    return delta, m, v


def _adam(w, g, m, v, rows, name, return_grad=False):
    r, n = w.shape

    def body(w_ref, g_ref, m_ref, v_ref, *out):
        gv = g_ref[...]
        d, mo, vo = _adam_math(w_ref[...], gv, m_ref[...], v_ref[...])
        for o_ref, val in zip(out, ([gv] if return_grad else []) + [d, mo, vo]):
            o_ref[...] = val

    spec = pl.BlockSpec((rows, n), lambda i: (i, 0))
    shape = jax.ShapeDtypeStruct((r, n), F32)
    n_out = 4 if return_grad else 3
    return pl.pallas_call(
        body, name=name, grid=(r // rows,), in_specs=[spec] * 4, out_specs=[spec] * n_out,
        out_shape=[shape] * n_out, compiler_params=_params(("parallel",)),
    )(w, g, m, v)


def _adam_ada(c_cols, dmod, w, m, v, rows, name):
    r, n = w.shape

    def body(c_ref, dm_ref, w_ref, m_ref, v_ref, g_ref, d_ref, mo_ref, vo_ref):
        cv = c_ref[...]
        c_act = cv * _sigmoid(cv)
        g = c_act[:, 0:1] * dm_ref[0:1, :]
        for b in range(1, N_DEV):
            g = g + c_act[:, b:b + 1] * dm_ref[b:b + 1, :]
        d, mo, vo = _adam_math(w_ref[...], g, m_ref[...], v_ref[...])
        g_ref[...] = g
        d_ref[...] = d
        mo_ref[...] = mo
        vo_ref[...] = vo

    spec = pl.BlockSpec((rows, n), lambda i: (i, 0))
    shape = jax.ShapeDtypeStruct((r, n), F32)
    return pl.pallas_call(
        body, name=name, grid=(r // rows,),
        in_specs=[pl.BlockSpec((rows, N_DEV), lambda i: (i, 0)), pl.BlockSpec((N_DEV, n), lambda i: (0, 0)),
                  spec, spec, spec],
        out_specs=[spec] * 4, out_shape=[shape] * 4, compiler_params=_params(("parallel",)),
    )(c_cols, dmod, w, m, v)


def _start_copies(name, plan, n, bufs, after=()):
    nb, na = len(bufs), len(after)

    def body(*refs):
        sems = refs[nb + na:nb + na + 2 * n]
        for k, (src, dst, dev) in enumerate(plan(refs[:nb])):
            _rcopy(src, dst, sems[2 * k], sems[2 * k + 1], dev).start()
        refs[-1][...] = jnp.zeros((8, 128), F32)

    outs = pl.pallas_call(
        body, name=name,
        out_shape=[pltpu.SemaphoreType.DMA(())] * (2 * n) + [pltpu.HBM(a.shape, a.dtype) for a in bufs]
        + [jax.ShapeDtypeStruct((8, 128), F32)],
        in_specs=[HBM_SPEC] * nb + [ANY] * na, out_specs=[SEM_SPEC] * (2 * n) + [HBM_SPEC] * nb + [VMEM],
        input_output_aliases={i: 2 * n + i for i in range(nb)},
        compiler_params=pltpu.CompilerParams(has_side_effects=EFFECT),
    )(*[pltpu.with_memory_space_constraint(a, pltpu.HBM) for a in bufs], *after)
    return list(outs[:2 * n]), list(outs[2 * n:2 * n + nb]), outs[-1]


def _wait_copies(name, plan, bufs, sems, after=(), send=True, recv=True):
    nb, nsem = len(bufs), len(sems)

    def body(*refs):
        s = refs[nb:nb + nsem]
        for k, (src, dst, dev) in enumerate(plan(refs[:nb])):
            cp = _rcopy(src, dst, s[2 * k], s[2 * k + 1], dev)
            if send:
                cp.wait_send()
            if recv:
                cp.wait_recv()

    outs = pl.pallas_call(
        body, name=name, out_shape=[pltpu.HBM(a.shape, a.dtype) for a in bufs],
        in_specs=[HBM_SPEC] * nb + [SEM_SPEC] * nsem + [ANY] * len(after), out_specs=[HBM_SPEC] * nb,
        input_output_aliases={i: i for i in range(nb)},
        compiler_params=pltpu.CompilerParams(has_side_effects=EFFECT),
    )(*bufs, *sems, *after)
    return list(outs)


def _to_sibling(views):
    def plan(b):
        x, y, c = _position()
        return [(view(b[2 * k], c), b[2 * k + 1], (x, y, 1 - c)) for k, view in enumerate(views)]
    return plan


def _to_chip(k):
    def plan(b):
        x, y, c = _position()
        cx, cy = _other_chips(x, y)[0][k]
        return [(b[0], b[1], (cx, cy, c))]
    return plan


def _slots_to_chips(b):
    x, y, c = _position()
    chips, cidx = _other_chips(x, y)
    return [(b[0].at[cidx[k]], b[1 + k], (cx, cy, c)) for k, (cx, cy) in enumerate(chips)]


def _halves_to_sibling(b):
    x, y, c = _position()
    views = [r.at[pl.ds(c * (r.shape[0] // 2), r.shape[0] // 2), :] for r in b]
    return [(v, v, (x, y, 1 - c)) for v in views]


def _landed(b):
    x, y, c = _position()
    return [(ref, ref, (x, y, c)) for ref in b]


def _assemble(name, pieces, out_shape, index_of):
    n = len(pieces)

    def body(*refs):
        out_ref, sem = refs[n], refs[n + 1]
        x, y, c = _position()
        _, cidx = _other_chips(x, y)
        cps = [pltpu.make_async_copy(refs[k], out_ref.at[index_of(k, 2 * x + y, c, cidx)], sem.at[k]) for k in range(n)]
        for cp in cps:
            cp.start()
        for cp in cps:
            cp.wait()

    return pl.pallas_call(
        body, name=name, in_specs=[VMEM] * n, out_specs=ANY, out_shape=out_shape,
        scratch_shapes=[pltpu.SemaphoreType.DMA((n,))],
        compiler_params=pltpu.CompilerParams(vmem_limit_bytes=VMEM_LIMIT),
    )(*pieces)


def _cast_and_gather_cond(a, rows, c8, cw):
    m, n = a.shape
    hq = n // 4
    nr = m // rows

    def body(a_ref, c8_ref, cw_ref, o_ref, call_ref, cwall_ref, ssem, rsem, lsem):
        x, y, c = _position()
        chip = 2 * x + y
        me = 4 * x + 2 * y + c
        chips, cidx = _other_chips(x, y)
        step = pl.program_id(0) * nr + pl.program_id(1)
        own = [pltpu.make_async_copy(c8_ref, call_ref.at[me], lsem.at[0]),
               pltpu.make_async_copy(cw_ref, cwall_ref.at[chip], lsem.at[1])]
        sends = [_rcopy(cw_ref, cwall_ref.at[chip], ssem.at[k], rsem.at[k], (cx, cy, c))
                 for k, (cx, cy) in enumerate(chips)]
        for mask in range(1, N_DEV):
            fx, fy, fc = (mask >> 2) & 1, (mask >> 1) & 1, mask & 1
            dev = (1 - x if fx else x, 1 - y if fy else y, 1 - c if fc else c)
            sends.append(_rcopy(c8_ref, call_ref.at[me], ssem.at[2 + mask], rsem.at[2 + mask], dev))

        @pl.when(step == 0)
        def _():
            for cp in own + sends:
                cp.start()

        o_ref[...] = a_ref[...].astype(BF16)

        @pl.when(step == 4 * nr - 1)
        def _():
            for k in range(3):
                slot = cwall_ref.at[cidx[k]]
                _rcopy(slot, slot, ssem.at[k], rsem.at[k], (x, y, c)).wait_recv()
            for mask in range(1, N_DEV):
                slot = call_ref.at[jnp.bitwise_xor(me, mask)]
                _rcopy(slot, slot, ssem.at[2 + mask], rsem.at[2 + mask], (x, y, c)).wait_recv()
            for cp in sends:
                cp.wait_send()
            for cp in own:
                cp.wait()

    return pl.pallas_call(
        body, name="cast_w_in_gather_cond", grid=(4, nr),
        in_specs=[pl.BlockSpec((rows, hq), lambda q, i: (i, q)), VMEM, VMEM],
        out_specs=[pl.BlockSpec((None, rows, hq), lambda q, i: (q, i, 0)), ANY, ANY],
        out_shape=[jax.ShapeDtypeStruct((4, m, hq), BF16), jax.ShapeDtypeStruct((N_DEV,) + c8.shape, F32),
                   jax.ShapeDtypeStruct((N_CHIPS,) + cw.shape, F32)],
        scratch_shapes=[pltpu.SemaphoreType.DMA((10,)), pltpu.SemaphoreType.DMA((10,)), pltpu.SemaphoreType.DMA((2,))],
        compiler_params=_params(("arbitrary", "arbitrary")),
    )(a, c8, cw)


def _exchange_mod(mod_part):
    def body(mp_ref, out_ref, ssem, rsem, lsem):
        x, y, c = _position()
        chip = 2 * x + y
        chips, cidx = _other_chips(x, y)
        own = pltpu.make_async_copy(mp_ref, out_ref.at[chip], lsem)
        own.start()
        sends = [_rcopy(mp_ref, out_ref.at[chip], ssem.at[k], rsem.at[k], (cx, cy, c))
                 for k, (cx, cy) in enumerate(chips)]
        for cp in sends:
            cp.start()
        for k in range(3):
            slot = out_ref.at[cidx[k]]
            _rcopy(slot, slot, ssem.at[k], rsem.at[k], (x, y, c)).wait_recv()
        for cp in sends:
            cp.wait_send()
        own.wait()

    return pl.pallas_call(
        body, name="exchange_mod", in_specs=[VMEM], out_specs=VMEM,
        out_shape=jax.ShapeDtypeStruct((N_CHIPS,) + mod_part.shape, F32),
        scratch_shapes=[pltpu.SemaphoreType.DMA((3,)), pltpu.SemaphoreType.DMA((3,)), pltpu.SemaphoreType.DMA],
    )(mod_part)


def _gather_small(pack):
    rows, n = pack.shape

    def body(p_ref, sum_ref, all_ref, ssem, rsem, lsem):
        x, y, c = _position()
        me = 4 * x + 2 * y + c
        sib = (x, y, 1 - c)
        chips, cidx = _other_chips(x, y)
        own = pltpu.make_async_copy(p_ref, all_ref.at[me], lsem)
        own.start()
        sends = [_rcopy(p_ref, all_ref.at[me], ssem.at[0], rsem.at[0], sib)]
        sends += [_rcopy(p_ref, all_ref.at[me], ssem.at[1 + k], rsem.at[1 + k], (cx, cy, c))
                  for k, (cx, cy) in enumerate(chips)]
        for cp in sends:
            cp.start()
        for k in range(3):
            slot = all_ref.at[2 * cidx[k] + c]
            _rcopy(slot, slot, ssem.at[1 + k], rsem.at[1 + k], sib).wait_recv()
            fw = _rcopy(slot, slot, ssem.at[4 + k], rsem.at[4 + k], sib)
            fw.start()
            sends.append(fw)
        slot = all_ref.at[jnp.bitwise_xor(me, 1)]
        _rcopy(slot, slot, ssem.at[0], rsem.at[0], sib).wait_recv()
        for k in range(3):
            slot = all_ref.at[2 * cidx[k] + 1 - c]
            _rcopy(slot, slot, ssem.at[4 + k], rsem.at[4 + k], sib).wait_recv()
        for cp in sends:
            cp.wait_send()
        own.wait()
        acc = all_ref[0]
        for k in range(1, N_DEV):
            acc = acc + all_ref[k]
        sum_ref[...] = acc

    return pl.pallas_call(
        body, name="gather_small", in_specs=[VMEM], out_specs=[VMEM, VMEM],
        out_shape=[jax.ShapeDtypeStruct((rows, n), F32), jax.ShapeDtypeStruct((N_DEV, rows, n), F32)],
        scratch_shapes=[pltpu.SemaphoreType.DMA((7,)), pltpu.SemaphoreType.DMA((7,)), pltpu.SemaphoreType.DMA],
        compiler_params=pltpu.CompilerParams(vmem_limit_bytes=VMEM_LIMIT),
    )(pack)


def _chip_partial(pos, g, recv, rows, name):
    ns, full, n = g.shape
    h = full // 2
    nb = h // rows

    def body(pos_ref, g_ref, r_ref, o_ref):
        o_ref[...] = (g_ref[...] + r_ref[...]).astype(BF16)

    return pl.pallas_call(
        body, name=name,
        grid_spec=pltpu.PrefetchScalarGridSpec(
            num_scalar_prefetch=1, grid=(ns, nb),
            in_specs=[pl.BlockSpec((None, rows, n), lambda s, i, p: (s, p[1] * nb + i, 0)),
                      pl.BlockSpec((None, rows, n), lambda s, i, p: (s, i, 0))],
            out_specs=pl.BlockSpec((None, rows, n), lambda s, i, p: (s, i, 0))),
        out_shape=jax.ShapeDtypeStruct((ns, h, n), BF16),
        compiler_params=_params(("parallel", "parallel")),
    )(pos, g, recv)


def _final_sum(pos, first, parts, rows, name):
    h, n = first.shape
    nb = h // rows

    def body(pos_ref, f_ref, rb0_ref, rb1_ref, rb2_ref, o_ref):
        acc = f_ref[...]
        for rb_ref in (rb0_ref, rb1_ref, rb2_ref):
            acc = acc + rb_ref[...].astype(F32)
        o_ref[...] = acc

    part = pl.BlockSpec((rows, n), lambda i, p: (i, 0))
    return pl.pallas_call(
        body, name=name,
        grid_spec=pltpu.PrefetchScalarGridSpec(
            num_scalar_prefetch=1, grid=(nb,), in_specs=[part] * 4,
            out_specs=pl.BlockSpec((rows, n), lambda i, p: (p[1] * nb + i, 0))),
        out_shape=jax.ShapeDtypeStruct((2 * h, n), F32),
        compiler_params=_params(("parallel",)),
    )(pos, first, *parts)


def _final_half(pos, g, recv_a, recv_b, rows, name):
    ns, full, n = g.shape
    h = full // 2
    nb = h // rows

    def body(pos_ref, g_ref, ra_ref, rb0_ref, rb1_ref, rb2_ref, o_ref):
        acc = g_ref[...] + ra_ref[...]
        for rb_ref in (rb0_ref, rb1_ref, rb2_ref):
            acc = acc + rb_ref[...].astype(F32)
        o_ref[...] = acc

    part = pl.BlockSpec((rows, n), lambda i, p: (i, 0))
    return pl.pallas_call(
        body, name=name,
        grid_spec=pltpu.PrefetchScalarGridSpec(
            num_scalar_prefetch=1, grid=(nb,),
            in_specs=[pl.BlockSpec((None, rows, n), lambda i, p: (p[0], p[1] * nb + i, 0)),
                      pl.BlockSpec((None, rows, n), lambda i, p: (p[0], i, 0)), part, part, part],
            out_specs=pl.BlockSpec((rows, n), lambda i, p: (p[1] * nb + i, 0))),
        out_shape=jax.ShapeDtypeStruct((full, n), F32),
        compiler_params=_params(("parallel",)),
    )(pos, g, recv_a, *recv_b)


def _modulation(c_rows, w_ada, b_ada, cols, name):
    d, n = w_ada.shape
    rows = c_rows.shape[0]

    def body(c_ref, w_ref, b_ref, o_ref):
        cv = c_ref[...]
        c_act = (cv * _sigmoid(cv)).astype(BF16)
        o_ref[...] = jnp.dot(c_act, w_ref[...].astype(BF16), preferred_element_type=F32) + b_ref[...]

    return pl.pallas_call(
        body, name=name, grid=(n // cols,),
        in_specs=[pl.BlockSpec((rows, d), lambda j: (0, 0)), pl.BlockSpec((d, cols), lambda j: (0, j)),
                  pl.BlockSpec((1, cols), lambda j: (0, j))],
        out_specs=pl.BlockSpec((rows, cols), lambda j: (0, j)),
        out_shape=jax.ShapeDtypeStruct((rows, n), F32),
        compiler_params=_params(("parallel",)),
    )(c_rows, w_ada, b_ada)


def _prenorm(x, norm_g, scale, shift, rows):
    s, d = x.shape

    def body(x_ref, g_ref, sc_ref, sh_ref, h_ref, r_ref):
        xv = x_ref[...]
        r = lax.rsqrt(jnp.mean(xv * xv, axis=-1, keepdims=True) + EPS)
        h = (xv * r * g_ref[...]) * (1.0 + sc_ref[...]) + sh_ref[...]
        h_ref[...] = h.astype(BF16)
        r_ref[...] = r

    vec = pl.BlockSpec((1, d), lambda i: (0, 0))
    return pl.pallas_call(
        body, name="prenorm", grid=(s // rows,),
        in_specs=[pl.BlockSpec((rows, d), lambda i: (i, 0)), vec, vec, vec],
        out_specs=[pl.BlockSpec((rows, d), lambda i: (i, 0)), pl.BlockSpec((rows, 1), lambda i: (i, 0))],
        out_shape=[jax.ShapeDtypeStruct((s, d), BF16), jax.ShapeDtypeStruct((s, 1), F32)],
        compiler_params=_params(("parallel",)),
    )(x, norm_g, scale, shift)


def _mixer_a_fwd(proj, conv_w, wa, rows, cols):
    s = proj.shape[0]
    ncb = wa // cols

    def body(ab_ref, ac_ref, ax_ref, az_ref, w_ref, y_ref, qbuf):
        t = pl.program_id(1)

        @pl.when(t == 0)
        def _():
            qbuf[0:HALO_A, :] = jnp.zeros((HALO_A, cols), F32)

        q = ac_ref[...].astype(F32) * ax_ref[...].astype(F32)
        qbuf[HALO_A:HALO_A + rows, :] = q
        conv = w_ref[2:3, :] * q
        for k in range(TAPS_A - 1):
            off = HALO_A - (TAPS_A - 1) + k
            conv = conv + w_ref[k:k + 1, :] * qbuf[off:off + rows, :]
        zv = az_ref[...].astype(F32)
        y_ref[...] = (ab_ref[...].astype(F32) * conv * (zv * _sigmoid(zv))).astype(BF16)
        qbuf[0:HALO_A, :] = qbuf[rows:rows + HALO_A, :]

    def sec(k):
        return pl.BlockSpec((rows, cols), lambda cb, t, k=k: (t, k * ncb + cb))

    return pl.pallas_call(
        body, name="mixer_a_fwd", grid=(ncb, s // rows),
        in_specs=[sec(0), sec(1), sec(2), sec(3), pl.BlockSpec((HALO_A, cols), lambda cb, t: (0, cb))],
        out_specs=pl.BlockSpec((rows, cols), lambda cb, t: (t, cb)),
        out_shape=jax.ShapeDtypeStruct((s, 2 * wa), BF16),
        scratch_shapes=[pltpu.VMEM((HALO_A + rows, cols), F32)],
        compiler_params=_params(("parallel", "arbitrary")),
    )(proj, proj, proj, proj, conv_w)


def _shifted_back(dst, src, lo, hi, cs):
    for n in range(1, 8):
        dst[n, lo:hi, :] = src[lo - n:hi - n, cs]


def _shifted_fwd(dst, src, lo, hi, cs):
    for n in range(1, 8):
        dst[n, lo:hi, :] = src[lo + n:hi + n, cs]


def _shift_rows(shifted, plain, n, start, size, cs):
    return plain[pl.ds(start, size), cs] if n == 0 else shifted[n, pl.ds(start, size), :]


def _mixer_b_conv_fwd(proj, conv_w, conv_b, wa, rows, cols, chunk):
    s = proj.shape[0]
    wb = conv_w.shape[1]
    ncb = wb // cols
    sec0 = 4 * wa // cols

    def body(bv_ref, bg_ref, w_ref, b_ref, u0_ref, u_ref, ubuf, sh):
        t = pl.program_id(1)

        @pl.when(t == 0)
        def _():
            ubuf[0:HALO_B, :] = jnp.zeros((HALO_B, cols), F32)

        u0 = bv_ref[...].astype(F32) * _sigmoid(bg_ref[...].astype(F32))
        u0_ref[...] = u0
        ubuf[HALO_B:HALO_B + rows, :] = u0
        for lc in range(cols // LANES):
            cs = slice(lc * LANES, (lc + 1) * LANES)
            _shifted_back(sh, ubuf, 8, HALO_B + rows, cs)
            taps = [w_ref[k:k + 1, cs] for k in range(TAPS_B)]
            bias = b_ref[:, cs]

            def row_chunk(rc, carry, cs=cs, taps=taps, bias=bias):
                base = pl.multiple_of(rc * chunk, chunk)
                acc = jnp.zeros((chunk, LANES), F32)
                for k in range(TAPS_B):
                    mq, n = divmod(TAPS_B - 1 - k, 8)
                    acc = acc + taps[k] * _shift_rows(sh, ubuf, n, HALO_B - 8 * mq + base, chunk, cs)
                u_ref[pl.ds(base, chunk), cs] = acc + bias
                return carry

            lax.fori_loop(0, rows // chunk, row_chunk, 0)
        ubuf[0:HALO_B, :] = ubuf[rows:rows + HALO_B, :]

    return pl.pallas_call(
        body, name="mixer_b_conv_fwd", grid=(ncb, s // rows),
        in_specs=[pl.BlockSpec((rows, cols), lambda cb, t: (t, sec0 + cb)),
                  pl.BlockSpec((rows, cols), lambda cb, t: (t, sec0 + ncb + cb)),
                  pl.BlockSpec((HALO_B, cols), lambda cb, t: (0, cb)),
                  pl.BlockSpec((1, cols), lambda cb, t: (0, cb))],
        out_specs=[pl.BlockSpec((rows, cols), lambda cb, t: (t, cb))] * 2,
        out_shape=[jax.ShapeDtypeStruct((s, wb), F32)] * 2,
        scratch_shapes=[pltpu.VMEM((HALO_B + rows, cols), F32), pltpu.VMEM((8, HALO_B + rows, LANES), F32)],
        compiler_params=_params(("parallel", "arbitrary")),
    )(proj, proj, conv_w, conv_b)


def _layernorm_stats(u):
    mu = jnp.mean(u, axis=-1, keepdims=True)
    xc = u - mu
    var = jnp.mean(xc * xc, axis=-1, keepdims=True)
    return xc * lax.rsqrt(var + EPS), lax.rsqrt(var + EPS)


def _mixer_b_gate_fwd(y, u, proj, ln_g, ln_b, wa, rows):
    s, wb = u.shape
    sec_z = (4 * wa + 2 * wb) // wb

    def body(y_in, u_ref, bz_ref, g_ref, b_ref, y_ref):
        uh, _ = _layernorm_stats(u_ref[...])
        ln = uh * g_ref[...] + b_ref[...]
        zv = bz_ref[...].astype(F32)
        y_ref[...] = ((ln * _sigmoid(ln)) * (zv * _sigmoid(zv))).astype(BF16)

    vec = pl.BlockSpec((1, wb), lambda i: (0, 0))
    return pl.pallas_call(
        body, name="mixer_b_gate_fwd", grid=(s // rows,),
        in_specs=[ANY, pl.BlockSpec((rows, wb), lambda i: (i, 0)), pl.BlockSpec((rows, wb), lambda i: (i, sec_z)),
                  vec, vec],
        out_specs=pl.BlockSpec((rows, wb), lambda i: (i, wa // wb)),
        out_shape=jax.ShapeDtypeStruct(y.shape, BF16), input_output_aliases={0: 0},
        compiler_params=_params(("parallel",)),
    )(y, u, proj, ln_g, ln_b)


def _loss_head(x, o, target, gate, final_g, rows):
    s, d = x.shape

    def body(x_ref, o_ref, t_ref, gate_ref, fg_ref, dx2_ref, do_ref, loss_ref, gfg_ref, dgate_ref):
        i = pl.program_id(0)
        ov = o_ref[...]
        x2 = x_ref[...] + gate_ref[...] * ov
        r2 = lax.rsqrt(jnp.mean(x2 * x2, axis=-1, keepdims=True) + EPS)
        xn2 = x2 * r2
        diff = xn2 * fg_ref[...] - t_ref[...]
        dout = diff * (1.0 / d)
        dxn2 = dout * fg_ref[...]
        dx2 = r2 * (dxn2 - xn2 * jnp.mean(dxn2 * xn2, axis=-1, keepdims=True))
        dx2_ref[...] = dx2
        do_ref[...] = (gate_ref[...] * dx2).astype(BF16)
        loss_part = 0.5 * jnp.sum(jnp.mean(diff * diff, axis=-1, keepdims=True), axis=0, keepdims=True)
        gfg_part = jnp.sum(dout * xn2, axis=0, keepdims=True)
        dgate_part = jnp.sum(dx2 * ov, axis=0, keepdims=True)

        @pl.when(i == 0)
        def _():
            loss_ref[...] = jnp.zeros_like(loss_ref)
            gfg_ref[...] = jnp.zeros_like(gfg_ref)
            dgate_ref[...] = jnp.zeros_like(dgate_ref)

        loss_ref[...] += jnp.broadcast_to(loss_part, loss_ref.shape)
        gfg_ref[...] += gfg_part
        dgate_ref[...] += dgate_part

    blk = pl.BlockSpec((rows, d), lambda i: (i, 0))
    vec = pl.BlockSpec((1, d), lambda i: (0, 0))
    return pl.pallas_call(
        body, name="loss_head", grid=(s // rows,),
        in_specs=[blk, blk, blk, vec, vec],
        out_specs=[blk, blk, pl.BlockSpec((1, 128), lambda i: (0, 0)), vec, vec],
        out_shape=[jax.ShapeDtypeStruct((s, d), F32), jax.ShapeDtypeStruct((s, d), BF16),
                   jax.ShapeDtypeStruct((1, 128), F32), jax.ShapeDtypeStruct((1, d), F32),
                   jax.ShapeDtypeStruct((1, d), F32)],
        compiler_params=_params(("arbitrary",)),
    )(x, o, target, gate, final_g)


def _mixer_a_bwd(proj, dy, conv_w, wa, din, rows):
    s = proj.shape[0]
    nt = s // rows
    per_halo = rows // HALO_IN

    def body(ab_ref, ac_ref, ax_ref, az_ref, hc_ref, hx_ref, dy_ref, w_ref, dp_ref, dw_ref, qbuf, dbuf):
        i = pl.program_id(0)

        @pl.when(i == 0)
        def _():
            dbuf[rows:rows + HALO_A, :] = jnp.zeros((HALO_A, wa), F32)
            dw_ref[...] = jnp.zeros_like(dw_ref)

        keep = jnp.where(i == nt - 1, 0.0, 1.0)
        before = hc_ref[...].astype(F32) * hx_ref[...].astype(F32) * keep
        qbuf[0:HALO_A, :] = before[HALO_IN - HALO_A:HALO_IN, :]
        acv, axv = ac_ref[...].astype(F32), ax_ref[...].astype(F32)
        q = acv * axv
        qbuf[HALO_A:HALO_A + rows, :] = q
        conv = w_ref[2:3, :] * q
        for k in range(TAPS_A - 1):
            off = HALO_A - (TAPS_A - 1) + k
            conv = conv + w_ref[k:k + 1, :] * qbuf[off:off + rows, :]
        zv, abv, dyv = az_ref[...].astype(F32), ab_ref[...].astype(F32), dy_ref[...]
        sg = _sigmoid(zv)
        sz = zv * sg
        dp_ref[:, 0:wa] = (dyv * conv * sz).astype(BF16)
        dp_ref[:, 3 * wa:4 * wa] = (dyv * abv * conv * (sg * (1.0 + zv * (1.0 - sg)))).astype(BF16)
        dconv = dyv * abv * sz
        dbuf[0:rows, :] = dconv
        dq = w_ref[2:3, :] * dconv
        for k in range(TAPS_A - 1):
            off = TAPS_A - 1 - k
            dq = dq + w_ref[k:k + 1, :] * dbuf[off:off + rows, :]
        dp_ref[:, wa:2 * wa] = (dq * axv).astype(BF16)
        dp_ref[:, 2 * wa:3 * wa] = (dq * acv).astype(BF16)
        for k in range(TAPS_A):
            off = HALO_A - (TAPS_A - 1) + k
            dw_ref[k:k + 1, :] += jnp.sum(dconv * qbuf[off:off + rows, :], axis=0, keepdims=True)
        dbuf[rows:rows + HALO_A, :] = dbuf[0:HALO_A, :]

    def sec(k):
        return pl.BlockSpec((rows, wa), lambda i, k=k: (nt - 1 - i, k))

    def halo(k):
        return pl.BlockSpec((HALO_IN, wa), lambda i, k=k: (jnp.maximum((nt - 1 - i) * per_halo - 1, 0), k))

    return pl.pallas_call(
        body, name="mixer_a_bwd", grid=(nt,),
        in_specs=[sec(0), sec(1), sec(2), sec(3), halo(1), halo(2),
                  pl.BlockSpec((rows, wa), lambda i: (nt - 1 - i, 0)),
                  pl.BlockSpec((HALO_A, wa), lambda i: (0, 0))],
        out_specs=[pl.BlockSpec((rows, 4 * wa), lambda i: (nt - 1 - i, 0)),
                   pl.BlockSpec((HALO_A, wa), lambda i: (0, 0))],
        out_shape=[jax.ShapeDtypeStruct((s, din), BF16), jax.ShapeDtypeStruct((HALO_A, wa), F32)],
        scratch_shapes=[pltpu.VMEM((HALO_A + rows, wa), F32), pltpu.VMEM((rows + HALO_A, wa), F32)],
        compiler_params=_params(("arbitrary",)),
    )(proj, proj, proj, proj, proj, proj, dy, conv_w)


def _mixer_b_gate_bwd(dproj, dy, u, proj, ln_g, ln_b, wa, rows):
    s, wb = u.shape
    sec_z = (4 * wa + 2 * wb) // wb

    def body(dp_in, dy_ref, u_ref, bz_ref, g_ref, b_ref, dp_ref, du_ref, dg_ref, db_ref, dcb_ref):
        i = pl.program_id(0)
        uh, rs = _layernorm_stats(u_ref[...])
        ln = uh * g_ref[...] + b_ref[...]
        sl = _sigmoid(ln)
        zv = bz_ref[...].astype(F32)
        sg = _sigmoid(zv)
        dyv = dy_ref[...]
        dp_ref[...] = (dyv * (ln * sl) * (sg * (1.0 + zv * (1.0 - sg)))).astype(BF16)
        dln = dyv * (zv * sg) * (sl * (1.0 + ln * (1.0 - sl)))
        duh = dln * g_ref[...]
        du = rs * (duh - jnp.mean(duh, axis=-1, keepdims=True) - uh * jnp.mean(duh * uh, axis=-1, keepdims=True))
        du_ref[...] = du

        @pl.when(i == 0)
        def _():
            dg_ref[...] = jnp.zeros_like(dg_ref)
            db_ref[...] = jnp.zeros_like(db_ref)
            dcb_ref[...] = jnp.zeros_like(dcb_ref)

        dg_ref[...] += jnp.sum(dln * uh, axis=0, keepdims=True)
        db_ref[...] += jnp.sum(dln, axis=0, keepdims=True)
        dcb_ref[...] += jnp.sum(du, axis=0, keepdims=True)

    blk = pl.BlockSpec((rows, wb), lambda i: (i, 0))
    vec = pl.BlockSpec((1, wb), lambda i: (0, 0))
    vshape = jax.ShapeDtypeStruct((1, wb), F32)
    return pl.pallas_call(
        body, name="mixer_b_gate_bwd", grid=(s // rows,),
        in_specs=[ANY, pl.BlockSpec((rows, wb), lambda i: (i, wa // wb)), blk,
                  pl.BlockSpec((rows, wb), lambda i: (i, sec_z)), vec, vec],
        out_specs=[pl.BlockSpec((rows, wb), lambda i: (i, sec_z)), blk, vec, vec, vec],
        out_shape=[jax.ShapeDtypeStruct(dproj.shape, BF16), jax.ShapeDtypeStruct((s, wb), F32), vshape, vshape, vshape],
        input_output_aliases={0: 0},
        compiler_params=_params(("arbitrary",)),
    )(dproj, dy, u, proj, ln_g, ln_b)


def _mixer_b_conv_bwd(dproj, du, u0, proj, conv_w, wa, rows, chunk):
    s, wb = du.shape
    nt = s // rows
    per32 = rows // HALO_B
    sec_v = 4 * wa // wb
    nrc = rows // chunk

    def body(dp_in, du_ref, u0_ref, h0_ref, bv_ref, bg_ref, w_ref, dp_ref, dw_ref, ubuf, dbuf, sh, shf, dwacc):
        i = pl.program_id(0)

        @pl.when(i == 0)
        def _():
            dbuf[rows:rows + HALO_B, :] = jnp.zeros((HALO_B, wb), F32)
            dwacc[...] = jnp.zeros_like(dwacc)

        ubuf[0:HALO_B, :] = h0_ref[...] * jnp.where(i == nt - 1, 0.0, 1.0)
        ubuf[HALO_B:HALO_B + rows, :] = u0_ref[...]
        dbuf[0:rows, :] = du_ref[...]
        for lc in range(wb // LANES):
            cs = slice(lc * LANES, (lc + 1) * LANES)
            _shifted_back(sh, ubuf, 8, HALO_B + rows, cs)
            _shifted_fwd(shf, dbuf, 0, rows + HALO_B - 8, cs)
            taps = [w_ref[k:k + 1, cs] for k in range(TAPS_B)]

            def conv_rows(rc, c0, cs=cs, taps=taps, lc=lc):
                base = pl.multiple_of(rc * chunk, chunk)
                acc = jnp.zeros((chunk, LANES), F32)
                for k in range(TAPS_B):
                    mq, n = divmod(TAPS_B - 1 - k, 8)
                    acc = acc + taps[k] * _shift_rows(shf, dbuf, n, base + 8 * mq, chunk, cs)
                sg = _sigmoid(bg_ref[pl.ds(base, chunk), cs].astype(F32))
                bv = bv_ref[pl.ds(base, chunk), cs].astype(F32)
                dp_ref[pl.ds(base, chunk), cs] = (acc * sg).astype(BF16)
                dp_ref[pl.ds(base, chunk), wb + lc * LANES:wb + (lc + 1) * LANES] = (
                    acc * bv * sg * (1.0 - sg)).astype(BF16)
                return c0

            lax.fori_loop(0, nrc, conv_rows, 0)

            def dw_rows(rc, accs, cs=cs):
                base = pl.multiple_of(rc * chunk, chunk)
                du_c = dbuf[pl.ds(base, chunk), cs]
                out = []
                for k in range(TAPS_B):
                    mq, n = divmod(TAPS_B - 1 - k, 8)
                    prod = du_c * _shift_rows(sh, ubuf, n, HALO_B - 8 * mq + base, chunk, cs)
                    out.append(accs[k] + jnp.sum(prod.reshape(chunk // SUBLANES, SUBLANES, LANES), axis=0))
                return tuple(out)

            accs = lax.fori_loop(0, nrc, dw_rows, tuple(jnp.zeros((SUBLANES, LANES), F32) for _ in range(TAPS_B)))
            for k in range(TAPS_B):
                dwacc[k * SUBLANES:(k + 1) * SUBLANES, cs] += accs[k]
        dbuf[rows:rows + HALO_B, :] = dbuf[0:HALO_B, :]

        @pl.when(i == nt - 1)
        def _():
            for k in range(HALO_B):
                dw_ref[k:k + 1, :] = jnp.sum(dwacc[k * SUBLANES:(k + 1) * SUBLANES, :], axis=0, keepdims=True)

    def rev(cols_blk):
        return pl.BlockSpec((rows, wb), lambda i, cb=cols_blk: (nt - 1 - i, cb))

    return pl.pallas_call(
        body, name="mixer_b_conv_bwd", grid=(nt,),
        in_specs=[ANY, rev(0), rev(0),
                  pl.BlockSpec((HALO_B, wb), lambda i: (jnp.maximum((nt - 1 - i) * per32 - 1, 0), 0)),
                  rev(sec_v), rev(sec_v + 1), pl.BlockSpec((HALO_B, wb), lambda i: (0, 0))],
        out_specs=[pl.BlockSpec((rows, 2 * wb), lambda i: (nt - 1 - i, sec_v // 2)),
                   pl.BlockSpec((HALO_B, wb), lambda i: (0, 0))],
        out_shape=[jax.ShapeDtypeStruct(dproj.shape, BF16), jax.ShapeDtypeStruct((HALO_B, wb), F32)],
        input_output_aliases={0: 0},
        scratch_shapes=[pltpu.VMEM((HALO_B + rows, wb), F32), pltpu.VMEM((rows + HALO_B, wb), F32),
                        pltpu.VMEM((8, HALO_B + rows, LANES), F32), pltpu.VMEM((8, rows + HALO_B, LANES), F32),
                        pltpu.VMEM((HALO_B * SUBLANES, wb), F32)],
        compiler_params=_params(("arbitrary",)),
    )(dproj, du, u0, u0, proj, proj, conv_w)


def _prenorm_bwd(x, r, dh, dx2, norm_g, scale, rows):
    s, d = x.shape

    def body(x_ref, r_ref, dh_ref, dx2_ref, g_ref, sc_ref, gx_ref, dsh_ref, dsc_ref, dg_ref):
        i = pl.program_id(0)
        rv = r_ref[...]
        xn = x_ref[...] * rv
        dhv = dh_ref[...]
        one_sc = 1.0 + sc_ref[...]
        dxn = dhv * one_sc * g_ref[...]
        gx_ref[...] = dx2_ref[...] + rv * (dxn - xn * jnp.mean(dxn * xn, axis=-1, keepdims=True))

        @pl.when(i == 0)
        def _():
            dsh_ref[...] = jnp.zeros_like(dsh_ref)
            dsc_ref[...] = jnp.zeros_like(dsc_ref)
            dg_ref[...] = jnp.zeros_like(dg_ref)

        dsh_ref[...] += jnp.sum(dhv, axis=0, keepdims=True)
        dsc_ref[...] += jnp.sum(dhv * (xn * g_ref[...]), axis=0, keepdims=True)
        dg_ref[...] += jnp.sum(dhv * one_sc * xn, axis=0, keepdims=True)

    blk = pl.BlockSpec((rows, d), lambda i: (i, 0))
    vec = pl.BlockSpec((1, d), lambda i: (0, 0))
    vshape = jax.ShapeDtypeStruct((1, d), F32)
    return pl.pallas_call(
        body, name="prenorm_bwd", grid=(s // rows,),
        in_specs=[blk, pl.BlockSpec((rows, 1), lambda i: (i, 0)), blk, blk, vec, vec],
        out_specs=[blk, vec, vec, vec],
        out_shape=[jax.ShapeDtypeStruct((s, d), F32), vshape, vshape, vshape],
        compiler_params=_params(("arbitrary",)),
    )(x, r, dh, dx2, norm_g, scale)


def _pad_rows(a, rows):
    return jnp.pad(a, ((0, rows - a.shape[0]), (0, 0)))


def _tile(n, want):
    t = min(n, want)
    while n % t:
        t -= 1
    return t


def kernel(x, c, norm_g, w_ada, b_ada, w_in, conv_a_w, conv_b_w, conv_b_b, ln_b_g, ln_b_b, w_out, final_g, loss_target, m_norm_g, m_w_ada, m_b_ada, m_w_in, m_conv_a_w, m_conv_b_w, m_conv_b_b, m_ln_b_g, m_ln_b_b, m_w_out, m_final_g, v_norm_g, v_w_ada, v_b_ada, v_w_in, v_conv_a_w, v_conv_b_w, v_conv_b_b, v_ln_b_g, v_ln_b_b, v_w_out, v_final_g):
    s, d = x.shape[1], x.shape[2]
    wa = conv_b_b.shape[-1]
    dmix = 2 * wa
    ns = w_in.shape[-1]
    din = N_CHIPS * ns
    r4 = w_out.shape[1]
    na = w_ada.shape[-1]
    wsh = conv_a_w.shape[-1]
    px, py, pc = _position()
    chip = 2 * px + py
    me = 4 * px + 2 * py + pc
    pos = jnp.stack([chip, pc]).astype(jnp.int32)
    x2d = x.reshape(s, d)
    target = loss_target.reshape(s, d)

    hc, ho, hrow = ns // 2, r4 // 2, d // 2
    _, cidx = _other_chips(px, py)
    hq = hc // 2
    wout_bf = _cast_bf16(w_out[0], _tile(r4, 512), "cast_w_out")

    def gather_plan(b):
        x, y, cc = _position()
        xn, yn = (1 - x, y, cc), (x, 1 - y, cc)
        q0, q1 = b[0].at[2 * cc], b[0].at[2 * cc + 1]
        return [(q0, b[1], xn), (q1, b[4], yn), (q1, b[2], xn), (q0, b[3], yn)]

    def gather_sent(b):
        return [(src, src, dev) for src, _, dev in gather_plan(list(b) + [None] * 4)]

    def out_plan(b):
        x, y, cc = _position()
        chips, _ = _other_chips(x, y)
        return [(b[0].at[pl.ds(cc * ho, ho), :], b[1 + k], (cx, cy, cc)) for k, (cx, cy) in enumerate(chips)]

    def out_sent(b):
        return [(src, src, dev) for src, _, dev in out_plan(list(b) + [None] * 3)]

    def onward_plan(b):
        x, y, cc = _position()
        sib = (x, y, 1 - cc)
        return [(b[0], b[2], (x, 1 - y, cc)), (b[1], b[3], (1 - x, y, cc)), (b[0], b[4], sib), (b[1], b[5], sib)]

    pairs = lambda n: _to_sibling([lambda ref, cc: ref] * n)

    c8 = jnp.broadcast_to(c, (8, d))
    cw = jnp.concatenate([_pad_rows(conv_a_w[0], HALO_A), _pad_rows(conv_b_w[0], HALO_B)], axis=0)
    win4, c_all, cw_all = _cast_and_gather_cond(w_in[0], _tile(d, 512), c8, cw)
    c_rows = c_all[:, 0, :]
    cw_full = jnp.transpose(cw_all, (1, 0, 2)).reshape(HALO_A + HALO_B, wa)
    conv_a_full, conv_b_full = cw_full[:HALO_A], cw_full[HALO_A:]

    b_ada_sh = lax.dynamic_slice(b_ada, (0, chip * na), (1, na))
    mod_part = _modulation(_pad_rows(c_rows, 2 * N_DEV), w_ada[0], b_ada_sh, _tile(na, 512), "modulation")[:N_DEV]
    mod_all = _exchange_mod(mod_part)
    mod = lax.dynamic_index_in_dim(mod_all, me, axis=1, keepdims=False).reshape(1, 3 * d)
    shift, scale, gate = mod[:, :d], mod[:, d:2 * d], mod[:, 2 * d:]

    def quarter():
        return lax.empty((d, hq), BF16)

    g_sems, g_bufs, g_tok = _start_copies(
        "gather_start", gather_plan, 4, [win4] + [quarter() for _ in range(4)], after=[mod_all])
    win4, (x0, x1, y0, y1) = g_bufs[0], g_bufs[1:5]

    h, r = _prenorm(x2d, norm_g, scale, shift + g_tok[0, 0], _tile(s, 256))
    bm = _tile(s, 1024)
    pieces = [None, None]

    def piece(slot, half, quarters, name, own_half=None):
        where = jnp.reshape(2 * slot + half, (1,)).astype(jnp.int32)
        pieces[:] = _proj_piece(where, h, quarters, pieces[0], pieces[1], din, 2 * N_CHIPS, _tile(s, 512), name,
                                own_half=own_half)
        return pieces[0]

    proj = piece(chip, 0, (win4, win4), "proj_own0", own_half=0)
    proj = piece(chip, 1, (win4, win4), "proj_own1", own_half=1)
    x0, y1 = _wait_copies("gather_wait_a", _landed, [x0, y1], g_sems[0:4], after=[proj], send=False)
    on_sems, (x0, y1, dg0, dg1, sx0, sy1), _ = _start_copies(
        "pass_on_a", onward_plan, 4, [x0, y1] + [quarter() for _ in range(4)])
    def two_quarters(half, quarters, places, name):
        where = jnp.stack([2 * cidx[0] + half, 2 * cidx[1] + half]).astype(jnp.int32)
        pieces[:] = _proj_quarter_pair(where, h, quarters, places, pieces[0], pieces[1], bm, name)
        return pieces[0]

    proj = two_quarters(pc, (x0, y1), (0, 1), "proj_first")
    x1, y0 = _wait_copies("gather_wait_b", _landed, [x1, y0], g_sems[4:8], after=[proj], send=False)
    pb_sems, (x1, sx1, y0, sy0), _ = _start_copies("pass_on_b", pairs(2), 2, [x1, quarter(), y0, quarter()])
    go_sems, go_bufs, _ = _start_copies("gather_out_start", out_plan, 3,
                                        [wout_bf] + [lax.empty((ho, d), BF16) for _ in range(3)], after=[x1])
    wout_bf, lo = go_bufs[0], go_bufs[1:4]
    proj = two_quarters(pc, (x1, y0), (1, 0), "proj_second")
    sx0, sy1 = _wait_copies("pass_wait_a", _landed, [sx0, sy1], on_sems[4:8], after=[proj], send=False)
    x1, sx1, y0, sy0 = _wait_copies("pass_wait_b", pairs(2), [x1, sx1, y0, sy0], pb_sems, after=[sx0])
    proj = piece(cidx[0], 1 - pc, (sx0, sx1), "proj_xb")
    proj = piece(cidx[1], 1 - pc, (sy0, sy1), "proj_yb")
    x0, y1, dg0, dg1 = _wait_copies("diag_wait", lambda b: onward_plan(list(b) + [None, None])[:2],
                                    [x0, y1, dg0, dg1], on_sems[0:4], after=[proj])
    x0, y1 = _wait_copies("pass_sent_a", lambda b: [(b[0], b[0], (0, 0, 0)), (b[1], b[1], (0, 0, 0))],
                          [x0, y1], on_sems[4:8], after=[dg0], recv=False)
    pd_sems, (dg0, sd0, dg1, sd1), _ = _start_copies("pass_on_d", pairs(2), 2, [dg0, quarter(), dg1, quarter()],
                                                     after=[x0])
    proj = piece(cidx[2], pc, (dg0, dg1), "proj_da")
    dg0, sd0, dg1, sd1 = _wait_copies("pass_wait_d", pairs(2), [dg0, sd0, dg1, sd1], pd_sems, after=[proj])
    proj = piece(cidx[2], 1 - pc, (sd0, sd1), "proj_db")
    win_full = pieces[1]

    lo = _wait_copies("gather_wait_out", _landed, lo, go_sems, after=[proj], send=False)
    o_sems, o_bufs, o_tok = _start_copies(
        "pass_on_out", pairs(3), 3, [b for k in range(3) for b in (lo[k], lax.empty((ho, d), BF16))])
    win4, = _wait_copies("gather_wait_sent", gather_sent, [win4], g_sems, after=[o_tok], recv=False)
    wout_bf, = _wait_copies("gather_out_sent", out_sent, [wout_bf], go_sems, after=[win4], recv=False)

    def slot_index(k, chip_, cc, others):
        if k == 0:
            return pl.ds(2 * chip_, 2)
        return 2 * others[(k - 1) % 3] + (cc if k <= 3 else 1 - cc)

    y = _mixer_a_fwd(proj, conv_a_full, wa, _tile(s, 512), _tile(wa, 512))
    u0, u = _mixer_b_conv_fwd(proj, conv_b_full, conv_b_b, wa, _tile(s, 512), _tile(wa, 256), 64)
    y = _mixer_b_gate_fwd(y, u, proj, ln_b_g, ln_b_b, wa, _tile(s, 256))
    o_bufs = _wait_copies("pass_wait_out", pairs(3), o_bufs, o_sems, after=[y])
    wout_full = _assemble("assemble_w_out", [wout_bf.reshape(2, ho, d)] + o_bufs[0::2] + o_bufs[1::2],
                          jax.ShapeDtypeStruct((2 * N_CHIPS, ho, d), BF16), slot_index)
    wout2d = wout_full.reshape(dmix, d)
    bd = _tile(d, 1024)
    o = _matmul(
        y, wout2d, grid=(s // bm, d // bd, 1),
        a_spec=pl.BlockSpec((bm, dmix), lambda i, j, k: (i, 0)),
        b_spec=pl.BlockSpec((dmix, bd), lambda i, j, k: (0, j)),
        o_spec=pl.BlockSpec((bm, bd), lambda i, j, k: (i, j)),
        out_shape=jax.ShapeDtypeStruct((s, d), F32), dims=((1,), (0,)), name="out_proj")
    dx2, do, loss_p, gfg_p, dgate_p = _loss_head(x2d, o, target, gate, final_g.reshape(1, d), _tile(s, 128))

    be = _tile(dmix, 1024)
    g_wout = _matmul(
        y, do, grid=(dmix // be, d // bd, 1),
        a_spec=pl.BlockSpec((s, be), lambda i, j, k: (0, i)),
        b_spec=pl.BlockSpec((s, bd), lambda i, j, k: (0, j)),
        o_spec=pl.BlockSpec((be, bd), lambda i, j, k: (i, j)),
        out_shape=jax.ShapeDtypeStruct((dmix, d), F32), dims=((0,), (0,)), name="grad_w_out")
    swap_out = _to_sibling([lambda ref, cc: ref.at[:, pl.ds((1 - cc) * ho, ho), :]])
    so_sems, (g_wout3, ra_out), so_tok = _start_copies(
        "swap_out_start", swap_out, 1, [g_wout.reshape(N_CHIPS, r4, d), lax.empty((N_CHIPS, ho, d), F32)])
    dy = _matmul(
        do, wout2d, grid=(s // bm, dmix // be, 1),
        a_spec=pl.BlockSpec((bm, d), lambda i, j, k: (i, 0)),
        b_spec=pl.BlockSpec((be, d), lambda i, j, k: (j, 0)),
        o_spec=pl.BlockSpec((bm, be), lambda i, j, k: (i, j)),
        out_shape=jax.ShapeDtypeStruct((s, dmix), F32), dims=((1,), (1,)), name="dy", after=[so_tok])
    g_wout3, ra_out = _wait_copies("swap_out_wait", swap_out, [g_wout3, ra_out], so_sems, after=[dy])
    q_out = _chip_partial(pos, g_wout3, ra_out, _tile(ho, 256), "chip_partial_w_out")
    po_sems, po_bufs, po_tok = _start_copies(
        "send_out_start", _slots_to_chips, 3, [q_out] + [lax.empty((ho, d), BF16) for _ in range(3)])
    dproj, dwa_p = _mixer_a_bwd(proj, dy, conv_a_full + po_tok[0, 0], wa, din, _tile(s, 128))
    dproj, du, dlng_p, dlnb_p, dcb_p = _mixer_b_gate_bwd(dproj, dy, u, proj, ln_b_g, ln_b_b, wa, _tile(s, 128))
    dproj, dwb_p = _mixer_b_conv_bwd(dproj, du, u0, proj, conv_b_full, wa, _tile(s, 256), 64)

    slots = [cidx[0], cidx[1], cidx[2], chip]
    q, rb, snd = [None] * 3, [None] * 3, [None] * 3
    after = []
    for pair in ((0, 1), (2, 3)):
        given = {}
        for k in pair:
            theirs = _grad_slot(jnp.stack([slots[k], 1 - pc]).astype(jnp.int32), h, dproj, after, ns,
                                _tile(hrow, 512), hc, f"grad_w_in{k}a")
            sems, bufs, tok = _start_copies(f"swap_in_start{k}", pairs(1), 1, [theirs, lax.empty((hrow, ns), F32)])
            given[k] = (sems, bufs)
            after = [tok]
        for k in pair:
            sems, bufs = given[k]
            _, from_sibling = _wait_copies(f"swap_in_wait{k}", pairs(1), bufs, sems, after=after)
            mine = _grad_slot(jnp.stack([slots[k], pc]).astype(jnp.int32), h, dproj, [], ns, _tile(hrow, 512), hc,
                              f"grad_w_in{k}b", add=from_sibling, out_dtype=BF16 if k < 3 else F32)
            if k < 3:
                snd[k], (q[k], rb[k]), tok = _start_copies(f"send_in_start{k}", _to_chip(k), 1,
                                                           [mine, lax.empty((hrow, ns), BF16)])
                after = [tok]
            else:
                own_half, after = mine, [mine]
    dh = _matmul_by_pieces(dproj, win_full, bm, bd, "dh", after=after)
    grad_x, dshift_p, dscale_p, gng_p = _prenorm_bwd(x2d, r, dh, dx2, norm_g, scale, _tile(s, 128))

    def rows_of(v):
        return _pad_rows(v.reshape(-1, wa), 8 * ((v.size // wa + 7) // 8))

    dmod = jnp.concatenate([dshift_p, dscale_p, dgate_p], axis=1)
    parts = [gng_p, dmod, dwa_p, dwb_p, dcb_p, dlng_p, dlnb_p, gfg_p,
             jnp.broadcast_to(loss_p[:, :1], (1, wa))]
    starts, packed = [], []
    for p in parts:
        starts.append(sum(q.shape[0] for q in packed))
        packed.append(rows_of(p) if p.shape[0] == 1 else p)
    small_sum, small_all = _gather_small(jnp.concatenate(packed, axis=0))

    def summed(k, rows):
        return small_sum[starts[k]:starts[k] + rows]

    grad_norm_g = summed(0, d // wa).reshape(1, d)
    grad_b_ada = summed(1, 3 * d // wa).reshape(1, 3 * d)
    grad_conv_a_full = summed(2, TAPS_A)
    grad_conv_b_full = summed(3, TAPS_B)
    grad_conv_b_b = summed(4, 1)
    grad_ln_b_g = summed(5, 1)
    grad_ln_b_b = summed(6, 1)
    grad_final_g = summed(7, d // wa).reshape(d)
    loss = summed(8, 1)[0, 0]
    grad_conv_a_w = lax.dynamic_slice(grad_conv_a_full, (0, chip * wsh), (TAPS_A, wsh))
    grad_conv_b_w = lax.dynamic_slice(grad_conv_b_full, (0, chip * wsh), (TAPS_B, wsh))
    dmod_all = small_all[:, starts[1]:starts[1] + 3 * d // wa, :].reshape(N_DEV, 3 * d)
    dmod_sh = lax.dynamic_slice(dmod_all, (0, chip * na), (N_DEV, na))

    def pairs_to_chips(b):
        x, y, cc = _position()
        chips, _ = _other_chips(x, y)
        return [(b[2 * k], b[2 * k + 1], (cx, cy, cc)) for k, (cx, cy) in enumerate(chips)]

    po_bufs = _wait_copies("send_out_wait", _slots_to_chips, po_bufs, po_sems, after=[small_sum])
    gh_out = _final_half(pos, g_wout3, ra_out, po_bufs[1:], _tile(ho, 256), "final_half_w_out")
    in_bufs = _wait_copies("send_in_wait", pairs_to_chips, [b for k in range(3) for b in (q[k], rb[k])],
                           snd[0] + snd[1] + snd[2], after=[small_sum])
    gh_in = _final_sum(pos, own_half, in_bufs[1::2], _tile(hrow, 256), "final_half_w_in")
    sh_sems, sh_bufs, sh_tok = _start_copies("share_start", _halves_to_sibling, 2, [gh_in, gh_out])

    grad_w_ada, d_wada, nm_wada, nv_wada = _adam_ada(c_rows.T, dmod_sh + sh_tok[0, 0], w_ada[0], m_w_ada[0],
                                                     v_w_ada[0], _tile(d, 128), "adam_w_ada")
    gw_in, gw_out = _wait_copies("share_wait", _halves_to_sibling, sh_bufs, sh_sems, after=[d_wada])
    grad_w_in, d_win, nm_win, nv_win = _adam(w_in[0], gw_in, m_w_in[0], v_w_in[0], _tile(d, 128), "adam_w_in",
                                             return_grad=True)
    grad_w_out, d_wout, nm_wout, nv_wout = _adam(w_out[0], gw_out, m_w_out[0], v_w_out[0], _tile(r4, 128),
                                                 "adam_w_out", return_grad=True)

    def small_adam(w, g, m, v, name):
        shape = w.shape
        w2 = w.reshape(-1, shape[-1])
        out = _adam(w2, g.reshape(w2.shape), m.reshape(w2.shape), v.reshape(w2.shape), w2.shape[0], name)
        return [o_.reshape(shape) for o_ in out]

    small = {
        "norm_g": small_adam(norm_g, grad_norm_g, m_norm_g, v_norm_g, "adam_norm_g"),
        "b_ada": small_adam(b_ada, grad_b_ada, m_b_ada, v_b_ada, "adam_b_ada"),
        "conv_a_w": small_adam(conv_a_w, grad_conv_a_w, m_conv_a_w, v_conv_a_w, "adam_conv_a_w"),
        "conv_b_w": small_adam(conv_b_w, grad_conv_b_w, m_conv_b_w, v_conv_b_w, "adam_conv_b_w"),
        "conv_b_b": small_adam(conv_b_b, grad_conv_b_b, m_conv_b_b, v_conv_b_b, "adam_conv_b_b"),
        "ln_b_g": small_adam(ln_b_g, grad_ln_b_g, m_ln_b_g, v_ln_b_g, "adam_ln_b_g"),
        "ln_b_b": small_adam(ln_b_b, grad_ln_b_b, m_ln_b_b, v_ln_b_b, "adam_ln_b_b"),
        "final_g": small_adam(final_g.reshape(1, d), grad_final_g, m_final_g.reshape(1, d),
                              v_final_g.reshape(1, d), "adam_final_g"),
    }
    small["final_g"] = [o_.reshape(d) for o_ in small["final_g"]]
    big = {
        "w_ada": [a[None] for a in (d_wada, nm_wada, nv_wada)],
        "w_in": [a[None] for a in (d_win, nm_win, nv_win)],
        "w_out": [a[None] for a in (d_wout, nm_wout, nv_wout)],
    }
    upd = {**small, **big}
    order = ["norm_g", "w_ada", "b_ada", "w_in", "conv_a_w", "conv_b_w", "conv_b_b", "ln_b_g", "ln_b_b",
             "w_out", "final_g"]
    grads = {
        "norm_g": grad_norm_g, "w_ada": grad_w_ada[None], "b_ada": grad_b_ada, "w_in": grad_w_in[None],
        "conv_a_w": grad_conv_a_w[None], "conv_b_w": grad_conv_b_w[None], "conv_b_b": grad_conv_b_b,
        "ln_b_g": grad_ln_b_g, "ln_b_b": grad_ln_b_b, "w_out": grad_w_out[None], "final_g": grad_final_g,
    }
    return (loss, grad_x.reshape(1, s, d), *[grads[n] for n in order], *[upd[n][0] for n in order],
            *[upd[n][1] for n in order], *[upd[n][2] for n in order])
```
